```python
import math
import jax, jax.numpy as jnp
from jax import lax
import numpy as np

D_MODEL = 2048
BATCH = 4
SEQ = 4096
DEPTH = 4

GRID_W = 64
CTX_LEN = 256
EPS = 1e-6
CONV_W = 4
ROPE_BASE = 10000.0
N_BRANCH = 3
LRU_WIDTH = 1024
LRU_BLOCKS = 16
LRU_BLOCK = LRU_WIDTH // LRU_BLOCKS
LRU_C = 8.0
SSD_INNER = 1024
SSD_HEADDIM = 64
SSD_HEADS = SSD_INNER // SSD_HEADDIM
SSD_GROUPS = 4
SSD_STATE = 128
SSD_CHUNK = 128
SSD_CONV_DIM = SSD_INNER + 2 * SSD_GROUPS * SSD_STATE
NA_HEADS = 16
NA_HEADDIM = 64
NA_WIDTH = NA_HEADS * NA_HEADDIM
NA_WIN_R = 8
NA_WIN_C = 16
N_EXPERTS = 16
EXPERT_FF = 1024
CAPACITY_FACTOR = 2
PROJ_SIZES = (LRU_WIDTH, LRU_WIDTH, SSD_INNER, SSD_CONV_DIM, 2 * SSD_HEADS, NA_WIDTH, NA_WIDTH, NA_WIDTH, N_BRANCH * D_MODEL)
PROJ_WIDTH = sum(PROJ_SIZES)

kernel_name = "hybrid_lru_ssd_natten_ecmoe_dit"


def rmsnorm(x, g):
    xf = x.astype(jnp.float32)
    y = xf * lax.rsqrt(jnp.mean(xf * xf, axis=-1, keepdims=True) + EPS)
    return (y * g.astype(jnp.float32)).astype(x.dtype)


def split_proj(p):
    idx = tuple(int(i) for i in np.cumsum(PROJ_SIZES)[:-1])
    return jnp.split(p, idx, axis=-1)


def dwconv_centred(x, w, b):
    L = x.shape[1]
    lo = CONV_W // 2
    xp = jnp.pad(x, ((0, 0), (lo, CONV_W - 1 - lo), (0, 0)))
    out = b + w[0] * xp[:, 0:L]
    for k in range(1, CONV_W):
        out = out + w[k] * xp[:, k:k + L]
    return out


def flip_dir(t, d):
    return jnp.flip(t, axis=1) if d else t


def axial_rope(L, dim, dtype):
    pos = jnp.arange(L)
    row = (pos // GRID_W).astype(jnp.float32)
    col = (pos % GRID_W).astype(jnp.float32)
    n_freq = dim // 4
    freqs = ROPE_BASE ** (-jnp.arange(n_freq, dtype=jnp.float32) / n_freq)
    ang = jnp.concatenate([row[:, None] * freqs, col[:, None] * freqs], axis=-1)
    return jnp.cos(ang).astype(dtype), jnp.sin(ang).astype(dtype)


def apply_rope(x, cos, sin):
    x1, x2 = jnp.split(x, 2, axis=-1)
    c, s = cos[:, None, :], sin[:, None, :]
    return jnp.concatenate([x1 * c - x2 * s, x1 * s + x2 * c], axis=-1)


def linear_scan(a, b, h0):
    def combine(l, r):
        al, bl = l
        ar, br = r
        return al * ar, ar * bl + br
    a_cum, h = lax.associative_scan(combine, (a, b), axis=1)
    return h + a_cum * h0[:, None]


def rglru_inputs(x, w_r, b_r, w_i, b_i, lam):
    xb = x.reshape(*x.shape[:-1], LRU_BLOCKS, LRU_BLOCK)
    r = jax.nn.sigmoid(jnp.einsum('blnk,nkj->blnj', xb, w_r).reshape(x.shape) + b_r)
    i = jax.nn.sigmoid(jnp.einsum('blnk,nkj->blnj', xb, w_i).reshape(x.shape) + b_i)
    log_a = -LRU_C * r * jax.nn.softplus(-lam)
    a = jnp.exp(log_a)
    return a, jnp.sqrt(-jnp.expm1(2.0 * log_a)) * (i * x)


def rglru_branch(ax_c, ag_c, ax_l, ag_l, conv_w, conv_b, w_r, b_r, w_i, b_i, lam, ctx_out):
    xc = dwconv_centred(ax_c, conv_w, conv_b)
    xl = dwconv_centred(ax_l, conv_w, conv_b)
    y_c, y_l = None, None
    for d in range(2):
        a_c, u_c = rglru_inputs(flip_dir(xc, d), w_r[d], b_r[d], w_i[d], b_i[d], lam[d])
        h_c = linear_scan(a_c, u_c, jnp.zeros_like(u_c[:, 0]))
        a_l, u_l = rglru_inputs(flip_dir(xl, d), w_r[d], b_r[d], w_i[d], b_i[d], lam[d])
        h_l = flip_dir(linear_scan(a_l, u_l, h_c[:, -1]), d)
        y_l = h_l if y_l is None else y_l + h_l
        if ctx_out:
            h_c = flip_dir(h_c, d)
            y_c = h_c if y_c is None else y_c + h_c
    y_l = y_l * jax.nn.gelu(ag_l)
    if ctx_out:
        y_c = y_c * jax.nn.gelu(ag_c)
    return y_c, y_l


def ssd_chunked(xs, dt, a, bm, cm, h0, need_y):
    bsz, L, H, P = xs.shape
    G, N = bm.shape[2], bm.shape[3]
    R = H // G
    Q = SSD_CHUNK
    nc = L // Q
    f32 = jnp.float32
    x = xs.reshape(bsz, nc, Q, G, R, P).astype(f32)
    dtc = dt.reshape(bsz, nc, Q, G, R)
    bc = bm.reshape(bsz, nc, Q, G, N).astype(f32)
    cc = cm.reshape(bsz, nc, Q, G, N).astype(f32)
    xdt = x * dtc[..., None]
    at = jnp.moveaxis(jnp.cumsum(dtc * a.reshape(G, R), axis=2), 2, -1)
    decay_end = jnp.exp(at[..., -1:] - at)
    states = jnp.einsum('bcjgn,bcgrj,bcjgrp->bcgrpn', bc, decay_end, xdt)
    chunk_decay = jnp.exp(at[..., -1])

    def step(h, inp):
        dec, st = inp
        return dec[..., None, None] * h + st, h

    h_final, h_prev = lax.scan(step, h0.reshape(bsz, G, R, P, N),
                               (jnp.moveaxis(chunk_decay, 1, 0), jnp.moveaxis(states, 1, 0)))
    h_final = h_final.reshape(bsz, H, P, N)
    if not need_y:
        return None, h_final
    h_prev = jnp.moveaxis(h_prev, 0, 1)
    lower = np.tril(np.ones((Q, Q), dtype=bool))
    seg = jnp.where(lower, at[..., :, None] - at[..., None, :], -jnp.inf)
    cb = jnp.einsum('bcign,bcjgn->bcgij', cc, bc)
    m = cb[:, :, :, None] * jnp.exp(seg)
    y_diag = jnp.einsum('bcgrij,bcjgrp->bcigrp', m, xdt)
    y_off = jnp.einsum('bcign,bcgri,bcgrpn->bcigrp', cc, jnp.exp(at), h_prev)
    y = (y_diag + y_off).reshape(bsz, L, H, P).astype(xs.dtype)
    return y, h_final


def ssd_branch(z_c, xbc_c, dt_c, z_l, xbc_l, dt_l, cos, sin, conv_w, conv_b, a_log, dt_bias, d_skip, norm_g, ctx_out):
    gn = SSD_GROUPS * SSD_STATE

    def prep(xbc, dt_raw, rotate):
        xbc = jax.nn.silu(dwconv_centred(xbc, conv_w, conv_b))
        xs, bm, cm = jnp.split(xbc, [SSD_INNER, SSD_INNER + gn], axis=-1)
        bsz, L = xs.shape[:2]
        bm = bm.reshape(bsz, L, SSD_GROUPS, SSD_STATE)
        cm = cm.reshape(bsz, L, SSD_GROUPS, SSD_STATE)
        if rotate:
            bm = apply_rope(bm, cos, sin)
            cm = apply_rope(cm, cos, sin)
        xs = xs.reshape(bsz, L, SSD_HEADS, SSD_HEADDIM)
        dt = jax.nn.softplus(dt_raw.reshape(bsz, L, 2, SSD_HEADS).astype(jnp.float32) + dt_bias.astype(jnp.float32))
        return xs, bm, cm, dt

    xs_c, b_c, c_c, dtc = prep(xbc_c, dt_c, False)
    xs_l, b_l, c_l, dtl = prep(xbc_l, dt_l, True)
    A = -jnp.exp(a_log.astype(jnp.float32))
    bsz = xs_l.shape[0]
    y_c, y_l = None, None
    for d in range(2):
        h0 = jnp.zeros((bsz, SSD_HEADS, SSD_HEADDIM, SSD_STATE), jnp.float32)
        yc_d, h_ctx = ssd_chunked(flip_dir(xs_c, d), flip_dir(dtc[:, :, d], d), A[d],
                                  flip_dir(b_c, d), flip_dir(c_c, d), h0, ctx_out)
        yl_d, _ = ssd_chunked(flip_dir(xs_l, d), flip_dir(dtl[:, :, d], d), A[d],
                              flip_dir(b_l, d), flip_dir(c_l, d), h_ctx, True)
        yl_d = flip_dir(yl_d, d)
        y_l = yl_d if y_l is None else y_l + yl_d
        if ctx_out:
            yc_d = flip_dir(yc_d, d)
            y_c = yc_d if y_c is None else y_c + yc_d

    def finish(y, xs, z):
        y = y + d_skip[:, None] * xs
        y = y.reshape(*y.shape[:2], SSD_INNER)
        return rmsnorm(y * jax.nn.silu(z), norm_g)

    y_l = finish(y_l, xs_l, z_l)
    if ctx_out:
        y_c = finish(y_c, xs_c, z_c)
    return y_c, y_l


def na_branch(q_c, k_c, v_c, q_l, k_l, v_l, rpb, ctx_out):
    bsz, L, _ = q_l.shape
    n_ctx = k_c.shape[1]
    rows = L // GRID_W
    wr = min(NA_WIN_R, rows)
    scale = NA_HEADDIM ** -0.5
    kc = k_c.reshape(bsz, n_ctx, NA_HEADS, NA_HEADDIM)
    vc = v_c.reshape(bsz, n_ctx, NA_HEADS, NA_HEADDIM)
    qg = q_l.reshape(bsz, rows, GRID_W, NA_HEADS, NA_HEADDIM)
    kg = k_l.reshape(bsz, rows, GRID_W, NA_HEADS, NA_HEADDIM)
    vg = v_l.reshape(bsz, rows, GRID_W, NA_HEADS, NA_HEADDIM)
    cols = np.arange(GRID_W)
    col_start = np.clip(cols - NA_WIN_C // 2, 0, GRID_W - NA_WIN_C)
    col_idx = col_start[:, None] + np.arange(NA_WIN_C)[None, :]
    col_off = col_idx - cols[:, None] + (NA_WIN_C - 1)
    rpb_cols = rpb[:, :, col_off]
    n_win = wr * NA_WIN_C

    def row_step(r):
        rs = jnp.clip(r - NA_WIN_R // 2, 0, rows - wr)
        q_row = lax.dynamic_index_in_dim(qg, r, axis=1, keepdims=False)
        k_band = lax.dynamic_slice_in_dim(kg, rs, wr, axis=1)
        v_band = lax.dynamic_slice_in_dim(vg, rs, wr, axis=1)
        k_win = k_band[:, :, col_idx]
        v_win = v_band[:, :, col_idx]
        roff = rs + jnp.arange(wr) - r + (NA_WIN_R - 1)
        bias = jnp.take(rpb_cols, roff, axis=1).transpose(0, 2, 1, 3)
        s_win = jnp.einsum('bjhd,brjwhd->bhjrw', q_row, k_win).astype(jnp.float32) * scale + bias
        s_ctx = jnp.einsum('bjhd,bkhd->bhjk', q_row, kc).astype(jnp.float32) * scale
        s = jnp.concatenate([s_win.reshape(bsz, NA_HEADS, GRID_W, n_win), s_ctx], axis=-1)
        p = jax.nn.softmax(s, axis=-1).astype(v_l.dtype)
        p_win = p[..., :n_win].reshape(bsz, NA_HEADS, GRID_W, wr, NA_WIN_C)
        return (jnp.einsum('bhjrw,brjwhd->bjhd', p_win, v_win)
                + jnp.einsum('bhjk,bkhd->bjhd', p[..., n_win:], vc))

    o = lax.map(row_step, jnp.arange(rows))
    y_l = jnp.moveaxis(o, 0, 1).reshape(bsz, L, NA_WIDTH)
    y_c = None
    if ctx_out:
        qc = q_c.reshape(bsz, n_ctx, NA_HEADS, NA_HEADDIM)
        s = jnp.einsum('bqhd,bkhd->bhqk', qc, kc).astype(jnp.float32) * scale
        p = jax.nn.softmax(s, axis=-1).astype(v_c.dtype)
        y_c = jnp.einsum('bhqk,bkhd->bqhd', p, vc).reshape(bsz, n_ctx, NA_WIDTH)
    return y_c, y_l


def hybrid_mixer(u_c, u_l, cos, sin, w_in, lru_conv_w, lru_conv_b, lru_w_r, lru_b_r, lru_w_i, lru_b_i, lru_lambda,
                 ssd_conv_w, ssd_conv_b, ssd_a_log, ssd_dt_bias, ssd_d, ssd_norm, na_rpb,
                 w_branch_lru, w_branch_ssd, w_branch_na, w_out, ctx_out):
    ax_c, ag_c, z_c, xbc_c, dt_c, q_c, k_c, v_c, g_c = split_proj(u_c @ w_in)
    ax_l, ag_l, z_l, xbc_l, dt_l, q_l, k_l, v_l, g_l = split_proj(u_l @ w_in)
    ya_c, ya_l = rglru_branch(ax_c, ag_c, ax_l, ag_l, lru_conv_w, lru_conv_b, lru_w_r, lru_b_r,
                              lru_w_i, lru_b_i, lru_lambda, ctx_out)
    yb_c, yb_l = ssd_branch(z_c, xbc_c, dt_c, z_l, xbc_l, dt_l, cos, sin, ssd_conv_w, ssd_conv_b,
                            ssd_a_log, ssd_dt_bias, ssd_d, ssd_norm, ctx_out)
    yc_c, yc_l = na_branch(q_c, k_c, v_c, q_l, k_l, v_l, na_rpb, ctx_out)

    def merge(ya, yb, yc, g):
        ga, gb, gc = jnp.split(jax.nn.sigmoid(g), N_BRANCH, axis=-1)
        y = ga * (ya @ w_branch_lru) + gb * (yb @ w_branch_ssd) + gc * (yc @ w_branch_na)
        return y @ w_out

    y_l = merge(ya_l, yb_l, yc_l, g_l)
    y_c = merge(ya_c, yb_c, yc_c, g_c) if ctx_out else None
    return y_c, y_l


def expert_choice_moe(v, w_router, w1, w3, w2):
    bsz, L, _ = v.shape
    cap = CAPACITY_FACTOR * L // N_EXPERTS
    aff = jax.nn.softmax((v @ w_router).astype(jnp.float32), axis=-1)
    g, idx = lax.top_k(jnp.swapaxes(aff, 1, 2), cap)
    bidx = jnp.arange(bsz)[:, None, None]
    xg = v[bidx, idx]
    hdn = jax.nn.silu(jnp.einsum('becd,edf->becf', xg, w1)) * jnp.einsum('becd,edf->becf', xg, w3)
    ye = jnp.einsum('becf,efd->becd', hdn, w2) * g[..., None].astype(v.dtype)
    return jnp.zeros_like(v).at[bidx, idx].add(ye)


def setup_inputs(seed: int = 0) -> dict:
    key = jax.random.key(seed)
    ks = jax.random.split(key, 40)
    f32 = jnp.float32
    D = D_MODEL

    def nrm(k, shape, s):
        return jax.random.normal(k, shape, f32) * s

    def gain(k, shape):
        return 1.0 + nrm(k, shape, 0.02)

    u = jax.random.uniform(ks[20], (DEPTH, 2, LRU_WIDTH), f32, 0.9, 0.999)
    sa = u ** (1.0 / LRU_C)
    dt0 = jnp.exp(jax.random.uniform(ks[22], (DEPTH, 2, SSD_HEADS), f32, math.log(1e-3), math.log(1e-1)))
    return {
        "x": nrm(ks[0], (BATCH, SEQ, D), 1.0),
        "c": nrm(ks[1], (BATCH, D), 1.0),
        "ctx": nrm(ks[2], (BATCH, CTX_LEN, D), 1.0),
        "c_ctx": nrm(ks[3], (D,), 1.0),
        "w_ada": nrm(ks[4], (DEPTH, D, 6 * D), 0.5 * D ** -0.5),
        "b_ada": nrm(ks[5], (DEPTH, 6 * D), 0.02),
        "norm_mix": gain(ks[6], (DEPTH, D)),
        "norm_ffn": gain(ks[7], (DEPTH, D)),
        "w_in": nrm(ks[8], (DEPTH, D, PROJ_WIDTH), D ** -0.5),
        "lru_conv_w": nrm(ks[9], (DEPTH, CONV_W, LRU_WIDTH), 0.5),
        "lru_conv_b": nrm(ks[10], (DEPTH, LRU_WIDTH), 0.02),
        "lru_w_r": nrm(ks[11], (DEPTH, 2, LRU_BLOCKS, LRU_BLOCK, LRU_BLOCK), LRU_BLOCK ** -0.5),
        "lru_b_r": nrm(ks[12], (DEPTH, 2, LRU_WIDTH), 0.02),
        "lru_w_i": nrm(ks[13], (DEPTH, 2, LRU_BLOCKS, LRU_BLOCK, LRU_BLOCK), LRU_BLOCK ** -0.5),
        "lru_b_i": nrm(ks[14], (DEPTH, 2, LRU_WIDTH), 0.02),
        "lru_lambda": jnp.log(sa) - jnp.log1p(-sa),
        "ssd_conv_w": nrm(ks[15], (DEPTH, CONV_W, SSD_CONV_DIM), 0.5),
        "ssd_conv_b": nrm(ks[16], (DEPTH, SSD_CONV_DIM), 0.02),
        "ssd_a_log": jnp.log(jax.random.uniform(ks[21], (DEPTH, 2, SSD_HEADS), f32, 1.0, 16.0)),
        "ssd_dt_bias": dt0 + jnp.log(-jnp.expm1(-dt0)),
        "ssd_d": 1.0 + nrm(ks[17], (DEPTH, SSD_HEADS), 0.1),
        "ssd_norm": gain(ks[18], (DEPTH, SSD_INNER)),
        "na_rpb": nrm(ks[19], (DEPTH, NA_HEADS, 2 * NA_WIN_R - 1, 2 * NA_WIN_C - 1), 0.1),
        "w_branch_lru": nrm(ks[23], (DEPTH, LRU_WIDTH, D), LRU_WIDTH ** -0.5),
        "w_branch_ssd": nrm(ks[24], (DEPTH, SSD_INNER, D), SSD_INNER ** -0.5),
        "w_branch_na": nrm(ks[25], (DEPTH, NA_WIDTH, D), NA_WIDTH ** -0.5),
        "w_out": nrm(ks[26], (DEPTH, D, D), D ** -0.5),
        "w_router": nrm(ks[27], (DEPTH, D, N_EXPERTS), D ** -0.5),
        "w1": nrm(ks[28], (DEPTH, N_EXPERTS, D, EXPERT_FF), D ** -0.5),
        "w3": nrm(ks[29], (DEPTH, N_EXPERTS, D, EXPERT_FF), D ** -0.5),
        "w2": nrm(ks[30], (DEPTH, N_EXPERTS, EXPERT_FF, D), EXPERT_FF ** -0.5),
        "norm_final": gain(ks[31], (D,)),
    }


def reference(x, c, ctx, c_ctx, w_ada, b_ada, norm_mix, norm_ffn, w_in, lru_conv_w, lru_conv_b, lru_w_r, lru_b_r,
              lru_w_i, lru_b_i, lru_lambda, ssd_conv_w, ssd_conv_b, ssd_a_log, ssd_dt_bias, ssd_d, ssd_norm, na_rpb,
              w_branch_lru, w_branch_ssd, w_branch_na, w_out, w_router, w1, w3, w2, norm_final):
    L = x.shape[1]
    cos, sin = axial_rope(L, SSD_STATE, x.dtype)
    silu_c = jax.nn.silu(c)
    silu_cc = jax.nn.silu(c_ctx)
    h_l, h_c = x, ctx
    for l in range(DEPTH):
        ctx_out = l < DEPTH - 1
        mod_l = (silu_c @ w_ada[l] + b_ada[l])[:, None, :]
        mod_c = silu_cc @ w_ada[l] + b_ada[l]
        sh1_l, sc1_l, g1_l, sh2_l, sc2_l, g2_l = jnp.split(mod_l, 6, axis=-1)
        sh1_c, sc1_c, g1_c, sh2_c, sc2_c, g2_c = jnp.split(mod_c, 6, axis=-1)
        u_l = rmsnorm(h_l, norm_mix[l]) * (1.0 + sc1_l) + sh1_l
        u_c = rmsnorm(h_c, norm_mix[l]) * (1.0 + sc1_c) + sh1_c
        y_c, y_l = hybrid_mixer(u_c, u_l, cos, sin, w_in[l], lru_conv_w[l], lru_conv_b[l], lru_w_r[l], lru_b_r[l],
                                lru_w_i[l], lru_b_i[l], lru_lambda[l], ssd_conv_w[l], ssd_conv_b[l], ssd_a_log[l],
                                ssd_dt_bias[l], ssd_d[l], ssd_norm[l], na_rpb[l], w_branch_lru[l], w_branch_ssd[l],
                                w_branch_na[l], w_out[l], ctx_out)
        h_l = h_l + g1_l * y_l
        v_l = rmsnorm(h_l, norm_ffn[l]) * (1.0 + sc2_l) + sh2_l
        h_l = h_l + g2_l * expert_choice_moe(v_l, w_router[l], w1[l], w3[l], w2[l])
        if ctx_out:
            h_c = h_c + g1_c * y_c
            v_c = rmsnorm(h_c, norm_ffn[l]) * (1.0 + sc2_c) + sh2_c
            h_c = h_c + g2_c * expert_choice_moe(v_c, w_router[l], w1[l], w3[l], w2[l])
    return rmsnorm(h_l, norm_final)
```

```python
import functools
import math

import jax
import jax.numpy as jnp
import numpy as np
from jax import lax
from jax.experimental import pallas as pl
from jax.experimental.pallas import tpu as pltpu

D_MODEL = 2048
DEPTH = 4
GRID_W = 64
EPS = 1e-6
CONV_W = 4
ROPE_BASE = 10000.0
N_BRANCH = 3
LRU_WIDTH = 1024
LRU_BLOCKS = 16
LRU_BLOCK = LRU_WIDTH // LRU_BLOCKS
LRU_C = 8.0
SSD_INNER = 1024
SSD_HEADDIM = 64
SSD_HEADS = SSD_INNER // SSD_HEADDIM
SSD_GROUPS = 4
SSD_STATE = 128
SSD_CHUNK = 128
SSD_CONV_DIM = SSD_INNER + 2 * SSD_GROUPS * SSD_STATE
NA_HEADS = 16
NA_HEADDIM = 64
NA_WIDTH = NA_HEADS * NA_HEADDIM
NA_WIN_R = 8
NA_WIN_C = 16
N_EXPERTS = 16
EXPERT_FF = 1024
CAPACITY_FACTOR = 2
PROJ_SIZES = (LRU_WIDTH, LRU_WIDTH, SSD_INNER, SSD_CONV_DIM, 2 * SSD_HEADS, NA_WIDTH, NA_WIDTH, NA_WIDTH,
              N_BRANCH * D_MODEL)

V7X_VMEM_LIMIT = 48 * 1024 * 1024


def _mm_kernel(x_ref, w_ref, o_ref):
    o_ref[...] = jnp.dot(x_ref[...].astype(jnp.bfloat16), w_ref[...].astype(jnp.bfloat16),
                         preferred_element_type=jnp.float32)


def _pick(n, cands):
    for c in cands:
        if n % c == 0:
            return c
    raise ValueError(n)


def matmul(x, w):
    m, k = x.shape
    n = w.shape[1]
    tm = _pick(m, (512, 256, 128, 64, 32, 16, 8))
    tn = _pick(n, (512, 256, 128))
    return pl.pallas_call(
        _mm_kernel,
        grid=(m // tm, n // tn),
        in_specs=[pl.BlockSpec((tm, k), lambda i, j: (i, 0)),
                  pl.BlockSpec((k, tn), lambda i, j: (0, j))],
        out_specs=pl.BlockSpec((tm, tn), lambda i, j: (i, j)),
        out_shape=jax.ShapeDtypeStruct((m, n), jnp.float32),
        compiler_params=pltpu.CompilerParams(vmem_limit_bytes=V7X_VMEM_LIMIT),
    )(x, w)


def mm_tokens(x, w):
    lead = x.shape[:-1]
    n = w.shape[1]
    npad = (-n) % 512
    if npad:
        w = jnp.pad(w, ((0, 0), (0, npad)))
    out = matmul(x.reshape(-1, x.shape[-1]), w)
    return out[:, :n].reshape(*lead, n)


def rmsnorm(x, g):
    xf = x.astype(jnp.float32)
    y = xf * lax.rsqrt(jnp.mean(xf * xf, axis=-1, keepdims=True) + EPS)
    return (y * g.astype(jnp.float32)).astype(x.dtype)


def split_proj(p):
    idx = tuple(int(i) for i in np.cumsum(PROJ_SIZES)[:-1])
    return jnp.split(p, idx, axis=-1)


def dwconv_centred(x, w, b):
    L = x.shape[1]
    lo = CONV_W // 2
    xp = jnp.pad(x, ((0, 0), (lo, CONV_W - 1 - lo), (0, 0)))
    out = b + w[0] * xp[:, 0:L]
    for k in range(1, CONV_W):
        out = out + w[k] * xp[:, k:k + L]
    return out


def flip_dir(t, d):
    return jnp.flip(t, axis=1) if d else t


def axial_rope(L, dim, dtype):
    pos = jnp.arange(L)
    row = (pos // GRID_W).astype(jnp.float32)
    col = (pos % GRID_W).astype(jnp.float32)
    n_freq = dim // 4
    freqs = ROPE_BASE ** (-jnp.arange(n_freq, dtype=jnp.float32) / n_freq)
    ang = jnp.concatenate([row[:, None] * freqs, col[:, None] * freqs], axis=-1)
    return jnp.cos(ang).astype(dtype), jnp.sin(ang).astype(dtype)


def apply_rope(x, cos, sin):
    x1, x2 = jnp.split(x, 2, axis=-1)
    c, s = cos[:, None, :], sin[:, None, :]
    return jnp.concatenate([x1 * c - x2 * s, x1 * s + x2 * c], axis=-1)


def linear_scan(a, b, h0):
    def combine(l, r):
        al, bl = l
        ar, br = r
        return al * ar, ar * bl + br
    a_cum, h = lax.associative_scan(combine, (a, b), axis=1)
    return h + a_cum * h0[:, None]


def rglru_inputs(x, w_r, b_r, w_i, b_i, lam):
    xb = x.reshape(*x.shape[:-1], LRU_BLOCKS, LRU_BLOCK)
    r = jax.nn.sigmoid(jnp.einsum('blnk,nkj->blnj', xb, w_r).reshape(x.shape) + b_r)
    i = jax.nn.sigmoid(jnp.einsum('blnk,nkj->blnj', xb, w_i).reshape(x.shape) + b_i)
    log_a = -LRU_C * r * jax.nn.softplus(-lam)
    a = jnp.exp(log_a)
    return a, jnp.sqrt(-jnp.expm1(2.0 * log_a)) * (i * x)


def rglru_branch(ax_c, ag_c, ax_l, ag_l, conv_w, conv_b, w_r, b_r, w_i, b_i, lam, ctx_out):
    xc = dwconv_centred(ax_c, conv_w, conv_b)
    xl = dwconv_centred(ax_l, conv_w, conv_b)
    y_c, y_l = None, None
    for d in range(2):
        a_c, u_c = rglru_inputs(flip_dir(xc, d), w_r[d], b_r[d], w_i[d], b_i[d], lam[d])
        h_c = linear_scan(a_c, u_c, jnp.zeros_like(u_c[:, 0]))
        a_l, u_l = rglru_inputs(flip_dir(xl, d), w_r[d], b_r[d], w_i[d], b_i[d], lam[d])
        h_l = flip_dir(linear_scan(a_l, u_l, h_c[:, -1]), d)
        y_l = h_l if y_l is None else y_l + h_l
        if ctx_out:
            h_c = flip_dir(h_c, d)
            y_c = h_c if y_c is None else y_c + h_c
    y_l = y_l * jax.nn.gelu(ag_l)
    if ctx_out:
        y_c = y_c * jax.nn.gelu(ag_c)
    return y_c, y_l


def ssd_chunked(xs, dt, a, bm, cm, h0, need_y):
    bsz, L, H, P = xs.shape
    G, N = bm.shape[2], bm.shape[3]
    R = H // G
    Q = SSD_CHUNK
    nc = L // Q
    f32 = jnp.float32
    x = xs.reshape(bsz, nc, Q, G, R, P).astype(f32)
    dtc = dt.reshape(bsz, nc, Q, G, R)
    bc = bm.reshape(bsz, nc, Q, G, N).astype(f32)
    cc = cm.reshape(bsz, nc, Q, G, N).astype(f32)
    xdt = x * dtc[..., None]
    at = jnp.moveaxis(jnp.cumsum(dtc * a.reshape(G, R), axis=2), 2, -1)
    decay_end = jnp.exp(at[..., -1:] - at)
    states = jnp.einsum('bcjgn,bcgrj,bcjgrp->bcgrpn', bc, decay_end, xdt)
    chunk_decay = jnp.exp(at[..., -1])

    def step(h, inp):
        dec, st = inp
        return dec[..., None, None] * h + st, h

    h_final, h_prev = lax.scan(step, h0.reshape(bsz, G, R, P, N),
                               (jnp.moveaxis(chunk_decay, 1, 0), jnp.moveaxis(states, 1, 0)))
    h_final = h_final.reshape(bsz, H, P, N)
    if not need_y:
        return None, h_final
    h_prev = jnp.moveaxis(h_prev, 0, 1)
    lower = np.tril(np.ones((Q, Q), dtype=bool))
    seg = jnp.where(lower, at[..., :, None] - at[..., None, :], -jnp.inf)
    cb = jnp.einsum('bcign,bcjgn->bcgij', cc, bc)
    m = cb[:, :, :, None] * jnp.exp(seg)
    y_diag = jnp.einsum('bcgrij,bcjgrp->bcigrp', m, xdt)
    y_off = jnp.einsum('bcign,bcgri,bcgrpn->bcigrp', cc, jnp.exp(at), h_prev)
    y = (y_diag + y_off).reshape(bsz, L, H, P).astype(xs.dtype)
    return y, h_final


def ssd_branch(z_c, xbc_c, dt_c, z_l, xbc_l, dt_l, cos, sin, conv_w, conv_b, a_log, dt_bias, d_skip, norm_g, ctx_out):
    gn = SSD_GROUPS * SSD_STATE

    def prep(xbc, dt_raw, rotate):
        xbc = jax.nn.silu(dwconv_centred(xbc, conv_w, conv_b))
        xs, bm, cm = jnp.split(xbc, [SSD_INNER, SSD_INNER + gn], axis=-1)
        bsz, L = xs.shape[:2]
        bm = bm.reshape(bsz, L, SSD_GROUPS, SSD_STATE)
        cm = cm.reshape(bsz, L, SSD_GROUPS, SSD_STATE)
        if rotate:
            bm = apply_rope(bm, cos, sin)
            cm = apply_rope(cm, cos, sin)
        xs = xs.reshape(bsz, L, SSD_HEADS, SSD_HEADDIM)
        dt = jax.nn.softplus(dt_raw.reshape(bsz, L, 2, SSD_HEADS).astype(jnp.float32) + dt_bias.astype(jnp.float32))
        return xs, bm, cm, dt

    xs_c, b_c, c_c, dtc = prep(xbc_c, dt_c, False)
    xs_l, b_l, c_l, dtl = prep(xbc_l, dt_l, True)
    A = -jnp.exp(a_log.astype(jnp.float32))
    bsz = xs_l.shape[0]
    y_c, y_l = None, None
    for d in range(2):
        h0 = jnp.zeros((bsz, SSD_HEADS, SSD_HEADDIM, SSD_STATE), jnp.float32)
        yc_d, h_ctx = ssd_chunked(flip_dir(xs_c, d), flip_dir(dtc[:, :, d], d), A[d],
                                  flip_dir(b_c, d), flip_dir(c_c, d), h0, ctx_out)
        yl_d, _ = ssd_chunked(flip_dir(xs_l, d), flip_dir(dtl[:, :, d], d), A[d],
                              flip_dir(b_l, d), flip_dir(c_l, d), h_ctx, True)
        yl_d = flip_dir(yl_d, d)
        y_l = yl_d if y_l is None else y_l + yl_d
        if ctx_out:
            yc_d = flip_dir(yc_d, d)
            y_c = yc_d if y_c is None else y_c + yc_d

    def finish(y, xs, z):
        y = y + d_skip[:, None] * xs
        y = y.reshape(*y.shape[:2], SSD_INNER)
        return rmsnorm(y * jax.nn.silu(z), norm_g)

    y_l = finish(y_l, xs_l, z_l)
    if ctx_out:
        y_c = finish(y_c, xs_c, z_c)
    return y_c, y_l


def na_branch(q_c, k_c, v_c, q_l, k_l, v_l, rpb, ctx_out):
    bsz, L, _ = q_l.shape
    n_ctx = k_c.shape[1]
    rows = L // GRID_W
    wr = min(NA_WIN_R, rows)
    scale = NA_HEADDIM ** -0.5
    kc = k_c.reshape(bsz, n_ctx, NA_HEADS, NA_HEADDIM)
    vc = v_c.reshape(bsz, n_ctx, NA_HEADS, NA_HEADDIM)
    qg = q_l.reshape(bsz, rows, GRID_W, NA_HEADS, NA_HEADDIM)
    kg = k_l.reshape(bsz, rows, GRID_W, NA_HEADS, NA_HEADDIM)
    vg = v_l.reshape(bsz, rows, GRID_W, NA_HEADS, NA_HEADDIM)
    cols = np.arange(GRID_W)
    col_start = np.clip(cols - NA_WIN_C // 2, 0, GRID_W - NA_WIN_C)
    col_idx = col_start[:, None] + np.arange(NA_WIN_C)[None, :]
    col_off = col_idx - cols[:, None] + (NA_WIN_C - 1)
    rpb_cols = rpb[:, :, col_off]
    n_win = wr * NA_WIN_C

    def row_step(r):
        rs = jnp.clip(r - NA_WIN_R // 2, 0, rows - wr)
        q_row = lax.dynamic_index_in_dim(qg, r, axis=1, keepdims=False)
        k_band = lax.dynamic_slice_in_dim(kg, rs, wr, axis=1)
        v_band = lax.dynamic_slice_in_dim(vg, rs, wr, axis=1)
        k_win = k_band[:, :, col_idx]
        v_win = v_band[:, :, col_idx]
        roff = rs + jnp.arange(wr) - r + (NA_WIN_R - 1)
        bias = jnp.take(rpb_cols, roff, axis=1).transpose(0, 2, 1, 3)
        s_win = jnp.einsum('bjhd,brjwhd->bhjrw', q_row, k_win).astype(jnp.float32) * scale + bias
        s_ctx = jnp.einsum('bjhd,bkhd->bhjk', q_row, kc).astype(jnp.float32) * scale
        s = jnp.concatenate([s_win.reshape(bsz, NA_HEADS, GRID_W, n_win), s_ctx], axis=-1)
        p = jax.nn.softmax(s, axis=-1).astype(v_l.dtype)
        p_win = p[..., :n_win].reshape(bsz, NA_HEADS, GRID_W, wr, NA_WIN_C)
        return (jnp.einsum('bhjrw,brjwhd->bjhd', p_win, v_win)
                + jnp.einsum('bhjk,bkhd->bjhd', p[..., n_win:], vc))

    o = lax.map(row_step, jnp.arange(rows))
    y_l = jnp.moveaxis(o, 0, 1).reshape(bsz, L, NA_WIDTH)
    y_c = None
    if ctx_out:
        qc = q_c.reshape(bsz, n_ctx, NA_HEADS, NA_HEADDIM)
        s = jnp.einsum('bqhd,bkhd->bhqk', qc, kc).astype(jnp.float32) * scale
        p = jax.nn.softmax(s, axis=-1).astype(v_c.dtype)
        y_c = jnp.einsum('bhqk,bkhd->bqhd', p, vc).reshape(bsz, n_ctx, NA_WIDTH)
    return y_c, y_l


def hybrid_mixer(u_c, u_l, cos, sin, w_in, lru_conv_w, lru_conv_b, lru_w_r, lru_b_r, lru_w_i, lru_b_i, lru_lambda,
                 ssd_conv_w, ssd_conv_b, ssd_a_log, ssd_dt_bias, ssd_d, ssd_norm, na_rpb,
                 w_branch_lru, w_branch_ssd, w_branch_na, w_out, ctx_out):
    ax_c, ag_c, z_c, xbc_c, dt_c, q_c, k_c, v_c, g_c = split_proj(mm_tokens(u_c, w_in))
    ax_l, ag_l, z_l, xbc_l, dt_l, q_l, k_l, v_l, g_l = split_proj(mm_tokens(u_l, w_in))
    ya_c, ya_l = rglru_branch(ax_c, ag_c, ax_l, ag_l, lru_conv_w, lru_conv_b, lru_w_r, lru_b_r,
                              lru_w_i, lru_b_i, lru_lambda, ctx_out)
    yb_c, yb_l = ssd_branch(z_c, xbc_c, dt_c, z_l, xbc_l, dt_l, cos, sin, ssd_conv_w, ssd_conv_b,
                            ssd_a_log, ssd_dt_bias, ssd_d, ssd_norm, ctx_out)
    yc_c, yc_l = na_branch(q_c, k_c, v_c, q_l, k_l, v_l, na_rpb, ctx_out)

    def merge(ya, yb, yc, g):
        ga, gb, gc = jnp.split(jax.nn.sigmoid(g), N_BRANCH, axis=-1)
        y = (ga * mm_tokens(ya, w_branch_lru) + gb * mm_tokens(yb, w_branch_ssd)
             + gc * mm_tokens(yc, w_branch_na))
        return mm_tokens(y, w_out)

    y_l = merge(ya_l, yb_l, yc_l, g_l)
    y_c = merge(ya_c, yb_c, yc_c, g_c) if ctx_out else None
    return y_c, y_l


def expert_choice_moe(v, w_router, w1, w3, w2):
    bsz, L, _ = v.shape
    cap = CAPACITY_FACTOR * L // N_EXPERTS
    aff = jax.nn.softmax((v @ w_router).astype(jnp.float32), axis=-1)
    g, idx = lax.top_k(jnp.swapaxes(aff, 1, 2), cap)
    bidx = jnp.arange(bsz)[:, None, None]
    xg = v[bidx, idx]
    hdn = jax.nn.silu(jnp.einsum('becd,edf->becf', xg, w1)) * jnp.einsum('becd,edf->becf', xg, w3)
    ye = jnp.einsum('becf,efd->becd', hdn, w2) * g[..., None].astype(v.dtype)
    return jnp.zeros_like(v).at[bidx, idx].add(ye)


def kernel(x, c, ctx, c_ctx, w_ada, b_ada, norm_mix, norm_ffn, w_in, lru_conv_w, lru_conv_b, lru_w_r, lru_b_r,
           lru_w_i, lru_b_i, lru_lambda, ssd_conv_w, ssd_conv_b, ssd_a_log, ssd_dt_bias, ssd_d, ssd_norm, na_rpb,
           w_branch_lru, w_branch_ssd, w_branch_na, w_out, w_router, w1, w3, w2, norm_final):
    L = x.shape[1]
    cos, sin = axial_rope(L, SSD_STATE, x.dtype)
    silu_c = jax.nn.silu(c)
    silu_cc = jax.nn.silu(c_ctx)
    h_l, h_c = x, ctx
    for l in range(DEPTH):
        ctx_out = l < DEPTH - 1
        mod_l = (silu_c @ w_ada[l] + b_ada[l])[:, None, :]
        mod_c = silu_cc @ w_ada[l] + b_ada[l]
        sh1_l, sc1_l, g1_l, sh2_l, sc2_l, g2_l = jnp.split(mod_l, 6, axis=-1)
        sh1_c, sc1_c, g1_c, sh2_c, sc2_c, g2_c = jnp.split(mod_c, 6, axis=-1)
        u_l = rmsnorm(h_l, norm_mix[l]) * (1.0 + sc1_l) + sh1_l
        u_c = rmsnorm(h_c, norm_mix[l]) * (1.0 + sc1_c) + sh1_c
        y_c, y_l = hybrid_mixer(u_c, u_l, cos, sin, w_in[l], lru_conv_w[l], lru_conv_b[l], lru_w_r[l], lru_b_r[l],
                                lru_w_i[l], lru_b_i[l], lru_lambda[l], ssd_conv_w[l], ssd_conv_b[l], ssd_a_log[l],
                                ssd_dt_bias[l], ssd_d[l], ssd_norm[l], na_rpb[l], w_branch_lru[l], w_branch_ssd[l],
                                w_branch_na[l], w_out[l], ctx_out)
        h_l = h_l + g1_l * y_l
        v_l = rmsnorm(h_l, norm_ffn[l]) * (1.0 + sc2_l) + sh2_l
        h_l = h_l + g2_l * expert_choice_moe(v_l, w_router[l], w1[l], w3[l], w2[l])
        if ctx_out:
            h_c = h_c + g1_c * y_c
            v_c = rmsnorm(h_c, norm_ffn[l]) * (1.0 + sc2_c) + sh2_c
            h_c = h_c + g2_c * expert_choice_moe(v_c, w_router[l], w1[l], w3[l], w2[l])
    return rmsnorm(h_l, norm_final)
```

```python
import functools

import jax
import jax.numpy as jnp
import numpy as np
from jax import lax
from jax.experimental import pallas as pl
from jax.experimental.pallas import tpu as pltpu

D_MODEL = 2048
SEQ = 4096
CTX_LEN = 256
DEPTH = 4
GRID_W = 64
EPS = 1e-6
ROPE_BASE = 10000.0
LRU_WIDTH = 1024
LRU_BLOCKS = 16
LRU_BLOCK = LRU_WIDTH // LRU_BLOCKS
LRU_C = 8.0
SSD_INNER = 1024
SSD_HEADDIM = 64
SSD_HEADS = SSD_INNER // SSD_HEADDIM
SSD_GROUPS = 4
SSD_STATE = 128
SSD_CHUNK = 128
SSD_CONV_DIM = SSD_INNER + 2 * SSD_GROUPS * SSD_STATE
NA_HEADS = 16
NA_HEADDIM = 64
NA_WIDTH = NA_HEADS * NA_HEADDIM
NA_WIN_R = 8
NA_WIN_C = 16
N_EXPERTS = 16
EXPERT_FF = 1024
CAPACITY_FACTOR = 2

V7X_VMEM_LIMIT = 56 * 1024 * 1024
TM = 1024
TQ = 256
HALO = 8
LAT_BLOCKS = SEQ // TQ
NEG = -1e30
HI = lax.Precision.HIGHEST
bf16 = jnp.bfloat16
f32 = jnp.float32

COL_AX, COL_AG, COL_Z = 2, 3, 4
COL_DT = 40
COL_G = 6144
GW = SSD_INNER // SSD_GROUPS

NA_GROUP_ROWS = 2
NA_BAND_ROWS = 10
NA_NCFG = 5


def _params():
    return pltpu.CompilerParams(vmem_limit_bytes=V7X_VMEM_LIMIT)


def _ada_kernel(x_ref, w_ref, b_ref, o_ref):
    c = x_ref[...]
    x = (c * jax.nn.sigmoid(c)).astype(bf16)
    o_ref[...] = jnp.dot(x, w_ref[...].astype(bf16), preferred_element_type=f32) + b_ref[...]


def ada_modulation(cond, w_ada, b_ada):
    depth, d, n = w_ada.shape
    tn = 1024
    return pl.pallas_call(
        _ada_kernel,
        grid=(depth, n // tn),
        in_specs=[pl.BlockSpec((8, d), lambda l, j: (0, 0)),
                  pl.BlockSpec((None, d, tn), lambda l, j: (l, 0, j)),
                  pl.BlockSpec((None, 1, tn), lambda l, j: (l, 0, j))],
        out_specs=pl.BlockSpec((None, 8, tn), lambda l, j: (l, 0, j)),
        out_shape=jax.ShapeDtypeStruct((depth, 8, n), f32),
        compiler_params=_params(),
        name="ada_modulation",
    )(cond, w_ada, b_ada.reshape(depth, 1, n))


def _norm_mod_kernel(h_ref, g_ref, sc_ref, sh_ref, o_ref):
    x = h_ref[...]
    y = x * lax.rsqrt(jnp.mean(x * x, axis=-1, keepdims=True) + EPS)
    o_ref[...] = ((y * g_ref[...]) * (1.0 + sc_ref[...]) + sh_ref[...]).astype(o_ref.dtype)


def norm_modulate(h, g, sc_t, sh_t, out_dtype):
    t, d = h.shape
    tm = min(512, TM)
    per = TM // tm
    return pl.pallas_call(
        _norm_mod_kernel,
        grid=(t // tm,),
        in_specs=[pl.BlockSpec((tm, d), lambda i: (i, 0)),
                  pl.BlockSpec((1, d), lambda i: (0, 0)),
                  pl.BlockSpec((None, 1, d), lambda i: (i // per, 0, 0)),
                  pl.BlockSpec((None, 1, d), lambda i: (i // per, 0, 0))],
        out_specs=pl.BlockSpec((tm, d), lambda i: (i, 0)),
        out_shape=jax.ShapeDtypeStruct((t, d), out_dtype),
        compiler_params=_params(),
        name="norm_modulate",
    )(h, g.reshape(1, d), sc_t, sh_t)


def _mm_kernel(x_ref, w_ref, o_ref):
    o_ref[...] = jnp.dot(x_ref[...], w_ref[...], preferred_element_type=f32).astype(o_ref.dtype)


def matmul_bf16(x, w, out_dtype, tn=512):
    m, k = x.shape
    n = w.shape[1]
    return pl.pallas_call(
        _mm_kernel,
        grid=(m // TM, n // tn),
        in_specs=[pl.BlockSpec((TM, k), lambda i, j: (i, 0)),
                  pl.BlockSpec((k, tn), lambda i, j: (0, j))],
        out_specs=pl.BlockSpec((TM, tn), lambda i, j: (i, j)),
        out_shape=jax.ShapeDtypeStruct((m, n), out_dtype),
        compiler_params=_params(),
        name="matmul_bf16",
    )(x, w)


def _merge_kernel(ya_ref, yb_ref, yc_ref, ga_ref, gb_ref, gc_ref, pa_ref, pb_ref, pc_ref, o_ref):
    acc = jax.nn.sigmoid(ga_ref[...]) * jnp.dot(ya_ref[...], pa_ref[...], preferred_element_type=f32)
    acc = acc + jax.nn.sigmoid(gb_ref[...]) * jnp.dot(yb_ref[...], pb_ref[...], preferred_element_type=f32)
    acc = acc + jax.nn.sigmoid(gc_ref[...]) * jnp.dot(yc_ref[...], pc_ref[...], preferred_element_type=f32)
    o_ref[...] = acc.astype(o_ref.dtype)


def branch_merge(ya, yb, yc, proj, pa, pb, pc, g_col0):
    t, k = ya.shape
    n = pa.shape[1]
    tn = 512
    gb0 = g_col0 // tn
    nj = n // tn
    xs = pl.BlockSpec((TM, k), lambda i, j: (i, 0))
    ws = pl.BlockSpec((k, tn), lambda i, j: (0, j))
    gs = lambda q: pl.BlockSpec((TM, tn), lambda i, j: (i, gb0 + q * nj + j))
    return pl.pallas_call(
        _merge_kernel,
        grid=(t // TM, nj),
        in_specs=[xs, xs, xs, gs(0), gs(1), gs(2), ws, ws, ws],
        out_specs=pl.BlockSpec((TM, tn), lambda i, j: (i, j)),
        out_shape=jax.ShapeDtypeStruct((t, n), bf16),
        compiler_params=_params(),
        name="branch_merge",
    )(ya, yb, yc, proj, proj, proj, pa, pb, pc)


def _resid_mm_kernel(y_ref, w_ref, h_ref, g_ref, o_ref):
    o_ref[...] = h_ref[...] + g_ref[...] * jnp.dot(y_ref[...], w_ref[...], preferred_element_type=f32)


def residual_matmul(y, w, h, gate_t):
    m, k = y.shape
    n = w.shape[1]
    tn = 512
    return pl.pallas_call(
        _resid_mm_kernel,
        grid=(m // TM, n // tn),
        in_specs=[pl.BlockSpec((TM, k), lambda i, j: (i, 0)),
                  pl.BlockSpec((k, tn), lambda i, j: (0, j)),
                  pl.BlockSpec((TM, tn), lambda i, j: (i, j)),
                  pl.BlockSpec((None, 1, tn), lambda i, j: (i, 0, j))],
        out_specs=pl.BlockSpec((TM, tn), lambda i, j: (i, j)),
        out_shape=jax.ShapeDtypeStruct((m, n), f32),
        compiler_params=_params(),
        name="residual_matmul",
    )(y, w, h, gate_t)


def _router_kernel(v_ref, w_ref, o_ref):
    s = jnp.dot(v_ref[...], w_ref[...], preferred_element_type=f32)
    lane = lax.broadcasted_iota(jnp.int32, s.shape, 1)
    s = jnp.where(lane < N_EXPERTS, s, NEG)
    e = jnp.exp(s - s.max(axis=-1, keepdims=True))
    o_ref[...] = e / e.sum(axis=-1, keepdims=True)


def router_affinity(v, w_router):
    t, d = v.shape
    w = jnp.pad(w_router, ((0, 0), (0, 128 - N_EXPERTS))).astype(bf16)
    return pl.pallas_call(
        _router_kernel,
        grid=(t // TM,),
        in_specs=[pl.BlockSpec((TM, d), lambda i: (i, 0)), pl.BlockSpec((d, 128), lambda i: (0, 0))],
        out_specs=pl.BlockSpec((TM, 128), lambda i: (i, 0)),
        out_shape=jax.ShapeDtypeStruct((t, 128), f32),
        compiler_params=_params(),
        name="router_affinity",
    )(v, w)


def _expert_kernel(x_ref, g_ref, w1_ref, w3_ref, w2_ref, o_ref):
    x = x_ref[...]
    a = jnp.dot(x, w1_ref[...], preferred_element_type=f32)
    b = jnp.dot(x, w3_ref[...], preferred_element_type=f32)
    hdn = ((a * jax.nn.sigmoid(a)) * b).astype(bf16)
    o_ref[...] = jnp.dot(hdn, w2_ref[...], preferred_element_type=f32) * g_ref[...]


def expert_ffn(xg, gates, w1, w3, w2, tm):
    e, r, d = xg.shape
    f = w1.shape[-1]
    return pl.pallas_call(
        _expert_kernel,
        grid=(e, r // tm),
        in_specs=[pl.BlockSpec((None, tm, d), lambda k, i: (k, i, 0)),
                  pl.BlockSpec((None, tm, 1), lambda k, i: (k, i, 0)),
                  pl.BlockSpec((None, d, f), lambda k, i: (k, 0, 0)),
                  pl.BlockSpec((None, d, f), lambda k, i: (k, 0, 0)),
                  pl.BlockSpec((None, f, d), lambda k, i: (k, 0, 0))],
        out_specs=pl.BlockSpec((None, tm, d), lambda k, i: (k, i, 0)),
        out_shape=jax.ShapeDtypeStruct((e, r, d), f32),
        compiler_params=_params(),
        name="expert_ffn",
    )(xg, gates, w1, w3, w2)


def na_bias_table(rpb):
    rows = SEQ // GRID_W
    r0s = (0, 2, 4, 60, 62)
    nq, nk = NA_GROUP_ROWS * GRID_W, NA_BAND_ROWS * GRID_W
    dr = np.zeros((NA_NCFG, nq, nk), np.int32)
    dc = np.zeros((NA_NCFG, nq, nk), np.int32)
    ok = np.zeros((NA_NCFG, nq, nk), bool)
    qr, qc = np.divmod(np.arange(nq), GRID_W)
    kr, kc = np.divmod(np.arange(nk), GRID_W)
    for c, r0 in enumerate(r0s):
        bs = int(np.clip(r0 - NA_WIN_R // 2, 0, rows - NA_BAND_ROWS))
        r = r0 + qr[:, None]
        rs = np.clip(r - NA_WIN_R // 2, 0, rows - NA_WIN_R)
        krow = bs + kr[None, :]
        cs = np.clip(qc - NA_WIN_C // 2, 0, GRID_W - NA_WIN_C)[:, None]
        valid = (krow >= rs) & (krow < rs + NA_WIN_R) & (kc[None, :] >= cs) & (kc[None, :] < cs + NA_WIN_C)
        ok[c] = valid
        dr[c] = np.where(valid, krow - r + (NA_WIN_R - 1), 0)
        dc[c] = np.where(valid, kc[None, :] - qc[:, None] + (NA_WIN_C - 1), 0)
    return jnp.where(ok[None], rpb[:, dr, dc], NEG).astype(f32)


def _dot_nt(a, b):
    return lax.dot_general(a, b, (((1,), (1,)), ((), ())), preferred_element_type=f32)


def _softmax_av(parts):
    m = parts[0][0].max(axis=-1, keepdims=True)
    for s, _ in parts[1:]:
        m = jnp.maximum(m, s.max(axis=-1, keepdims=True))
    l = None
    o = None
    for s, v in parts:
        p = jnp.exp(s - m)
        ls = p.sum(axis=-1, keepdims=True)
        os_ = jnp.dot(p.astype(bf16), v, preferred_element_type=f32)
        l = ls if l is None else l + ls
        o = os_ if o is None else o + os_
    return o / l


def _na_kernel(q_ref, k_ref, v_ref, kc_ref, vc_ref, tbl_ref, o_ref):
    j = pl.program_id(2)
    scale = NA_HEADDIM ** -0.5
    lane = lax.broadcasted_iota(jnp.int32, (1, 2 * NA_HEADDIM), 1)
    first = lane < NA_HEADDIM
    kc = kc_ref[...]
    vc = vc_ref[...]
    gq = NA_GROUP_ROWS * GRID_W

    def head_q(q, hh):
        keep = first if hh == 0 else jnp.logical_not(first)
        return jnp.where(keep, q, jnp.zeros_like(q))

    @pl.when(j < LAT_BLOCKS)
    def _():
        for g in range(TQ // gq):
            r0 = j * (TQ // GRID_W) + NA_GROUP_ROWS * g
            bs = jnp.clip(r0 - NA_WIN_R // 2, 0, SEQ // GRID_W - NA_BAND_ROWS)
            cfg = jnp.where(r0 == 0, 0, jnp.where(r0 == 2, 1, jnp.where(r0 == 60, 3, jnp.where(r0 == 62, 4, 2))))
            start = pl.multiple_of(bs * GRID_W, GRID_W)
            kb = k_ref[pl.ds(start, NA_BAND_ROWS * GRID_W), :]
            vb = v_ref[pl.ds(start, NA_BAND_ROWS * GRID_W), :]
            q = q_ref[pl.ds(g * gq, gq), :]
            outs = []
            for hh in range(2):
                qm = head_q(q, hh)
                s = _dot_nt(qm, kb) * scale + tbl_ref[hh, cfg]
                sc = _dot_nt(qm, kc) * scale
                outs.append(_softmax_av([(s, vb), (sc, vc)]))
            o_ref[pl.ds(g * gq, gq), :] = jnp.where(first, outs[0], outs[1]).astype(o_ref.dtype)

    @pl.when(j == LAT_BLOCKS)
    def _():
        q = q_ref[...]
        outs = []
        for hh in range(2):
            sc = _dot_nt(head_q(q, hh), kc) * scale
            outs.append(_softmax_av([(sc, vc)]))
        o_ref[...] = jnp.where(first, outs[0], outs[1]).astype(o_ref.dtype)


def na_attention(qkv, tbl, bsz):
    t = qkv.shape[0]
    npair = NA_HEADS // 2
    ctx0 = bsz * LAT_BLOCKS

    def qrow(b, p, j):
        return jnp.where(j < LAT_BLOCKS, b * LAT_BLOCKS + j, ctx0 + b)

    return pl.pallas_call(
        _na_kernel,
        grid=(bsz, npair, LAT_BLOCKS + 1),
        in_specs=[
            pl.BlockSpec((TQ, 128), lambda b, p, j: (qrow(b, p, j), p)),
            pl.BlockSpec((SEQ, 128), lambda b, p, j: (b, npair + p)),
            pl.BlockSpec((SEQ, 128), lambda b, p, j: (b, 2 * npair + p)),
            pl.BlockSpec((CTX_LEN, 128), lambda b, p, j: (ctx0 + b, npair + p)),
            pl.BlockSpec((CTX_LEN, 128), lambda b, p, j: (ctx0 + b, 2 * npair + p)),
            pl.BlockSpec((2, NA_NCFG, NA_GROUP_ROWS * GRID_W, NA_BAND_ROWS * GRID_W), lambda b, p, j: (p, 0, 0, 0)),
        ],
        out_specs=pl.BlockSpec((TQ, 128), lambda b, p, j: (qrow(b, p, j), p)),
        out_shape=jax.ShapeDtypeStruct((t, NA_WIDTH), bf16),
        compiler_params=_params(),
        name="na_attention",
    )(qkv, qkv, qkv, qkv, qkv, tbl)


def _seq_block(d, bsz, lat_blocks):
    ctx0 = bsz * lat_blocks

    def jj(s):
        return (s - 1) if d == 0 else (lat_blocks - s)

    def blk(b, s):
        return jnp.where(s == 0, ctx0 + b, b * lat_blocks + jj(s))

    return jj, blk


def _halo_specs(width, col, blk, t):
    nb8 = TQ // HALO
    return [
        pl.BlockSpec((TQ, width), lambda b, s: (blk(b, s), col)),
        pl.BlockSpec((HALO, width), lambda b, s: (jnp.maximum(blk(b, s) * nb8 - 1, 0), col)),
        pl.BlockSpec((HALO, width), lambda b, s: (jnp.minimum(blk(b, s) * nb8 + nb8, t // HALO - 1), col)),
    ]


def _centred_conv(xe, x_ref, xp_ref, xn_ref, cw_ref, cb_ref, has_prev, has_next):
    xe[0:HALO, :] = jnp.where(has_prev, xp_ref[...], 0.0)
    xe[HALO:HALO + TQ, :] = x_ref[...]
    xe[HALO + TQ:2 * HALO + TQ, :] = jnp.where(has_next, xn_ref[...], 0.0)
    out = cb_ref[...] + cw_ref[0:1, :] * xe[HALO - 2:HALO - 2 + TQ, :]
    for k in range(1, 4):
        out = out + cw_ref[k:k + 1, :] * xe[HALO - 2 + k:HALO - 2 + k + TQ, :]
    return out


def lru_gate_weights(w_r, w_i):
    def pair(w):
        w = w.reshape(2, LRU_BLOCKS // 2, 2, LRU_BLOCK, LRU_BLOCK)
        z = jnp.zeros_like(w[:, :, 0])
        top = jnp.concatenate([w[:, :, 0], z], axis=-1)
        bot = jnp.concatenate([z, w[:, :, 1]], axis=-1)
        return jnp.concatenate([top, bot], axis=-2)
    return jnp.concatenate([pair(w_r), pair(w_i)], axis=-1).astype(bf16)


def _gelu_tanh(x):
    return 0.5 * x * (1.0 + jnp.tanh(float(np.sqrt(2.0 / np.pi)) * (x + 0.044715 * (x * x * x))))


def _lru_kernel(d, lat_blocks, *refs):
    if d == 0:
        (x_ref, xp_ref, xn_ref, cw_ref, cb_ref, wg_ref, br_ref, bi_ref, lam_ref, o_ref,
         xe, a_scr, u_scr, h_scr) = refs
    else:
        (x_ref, xp_ref, xn_ref, ag_ref, hf_ref, cw_ref, cb_ref, wg_ref, br_ref, bi_ref, lam_ref, o_ref,
         xe, a_scr, u_scr, h_scr, hs_scr) = refs
    s = pl.program_id(1)
    j = (s - 1) if d == 0 else (lat_blocks - s)
    has_prev = jnp.logical_and(s > 0, j > 0)
    has_next = jnp.logical_and(s > 0, j < lat_blocks - 1)

    @pl.when(s == 0)
    def _():
        h_scr[...] = jnp.zeros_like(h_scr)

    xc = _centred_conv(xe, x_ref, xp_ref, xn_ref, cw_ref, cb_ref, has_prev, has_next)
    xcb = xc.astype(bf16)
    sp = jax.nn.softplus(-lam_ref[...])
    for p in range(LRU_WIDTH // 128):
        sl = slice(128 * p, 128 * (p + 1))
        g = jnp.dot(xcb[:, sl], wg_ref[p], preferred_element_type=f32)
        r = jax.nn.sigmoid(g[:, :128] + br_ref[:, sl])
        i = jax.nn.sigmoid(g[:, 128:] + bi_ref[:, sl])
        log_a = -LRU_C * r * sp[:, sl]
        a = jnp.exp(log_a)
        a_scr[:, sl] = a
        u_scr[:, sl] = jnp.sqrt(-jnp.tanh(log_a) * (a * a + 1.0)) * (i * xc[:, sl])

    out = o_ref if d == 0 else hs_scr
    ngrp = TQ // 8

    def body(gi, h):
        base = pl.multiple_of((gi if d == 0 else ngrp - 1 - gi) * 8, 8)
        for t in (range(8) if d == 0 else range(7, -1, -1)):
            h = a_scr[pl.ds(base + t, 1), :] * h + u_scr[pl.ds(base + t, 1), :]
            out[pl.ds(base + t, 1), :] = h
        return h

    h = lax.fori_loop(0, ngrp, body, h_scr[0:1, :])
    h_scr[0:1, :] = h
    if d == 1:
        y = (hf_ref[...] + hs_scr[...]) * _gelu_tanh(ag_ref[...])
        o_ref[...] = y.astype(o_ref.dtype)


def lru_pass(d, proj, hf, conv_w, conv_b, wg, b_r, b_i, lam, bsz, lat_blocks):
    t = proj.shape[0]
    _, blk = _seq_block(d, bsz, lat_blocks)
    row = lambda c: (lambda b, s: (blk(b, s), c))
    in_specs = _halo_specs(LRU_WIDTH, COL_AX, blk, t)
    args = [proj, proj, proj]
    if d == 1:
        in_specs += [pl.BlockSpec((TQ, LRU_WIDTH), row(COL_AG)), pl.BlockSpec((TQ, LRU_WIDTH), row(0))]
        args += [proj, hf]
    const = lambda shape: pl.BlockSpec(shape, lambda b, s: (0,) * len(shape))
    in_specs += [const((4, LRU_WIDTH)), const((1, LRU_WIDTH)), const((LRU_WIDTH // 128, 128, 256)),
                 const((1, LRU_WIDTH)), const((1, LRU_WIDTH)), const((1, LRU_WIDTH))]
    args += [conv_w, conv_b.reshape(1, -1), wg[d], b_r[d].reshape(1, -1), b_i[d].reshape(1, -1), lam[d].reshape(1, -1)]
    scratch = [pltpu.VMEM((TQ + 2 * HALO, LRU_WIDTH), f32), pltpu.VMEM((TQ, LRU_WIDTH), f32),
               pltpu.VMEM((TQ, LRU_WIDTH), f32), pltpu.VMEM((8, LRU_WIDTH), f32)]
    if d == 1:
        scratch.append(pltpu.VMEM((TQ, LRU_WIDTH), f32))
    return pl.pallas_call(
        functools.partial(_lru_kernel, d, lat_blocks),
        grid=(bsz, lat_blocks + 1),
        in_specs=in_specs,
        out_specs=pl.BlockSpec((TQ, LRU_WIDTH), row(0)),
        out_shape=jax.ShapeDtypeStruct((t, LRU_WIDTH), f32 if d == 0 else bf16),
        scratch_shapes=scratch,
        compiler_params=_params(),
        name=f"lru_pass{d}",
    )(*args)


def lru_branch(proj, conv_w, conv_b, w_r, b_r, w_i, b_i, lam, bsz, lat_blocks):
    wg = lru_gate_weights(w_r, w_i)
    hf = lru_pass(0, proj, None, conv_w, conv_b, wg, b_r, b_i, lam, bsz, lat_blocks)
    return lru_pass(1, proj, hf, conv_w, conv_b, wg, b_r, b_i, lam, bsz, lat_blocks)


def rope_tables(seq):
    pos = jnp.arange(seq)
    row = (pos // GRID_W).astype(f32)
    col = (pos % GRID_W).astype(f32)
    n_freq = SSD_STATE // 4
    freqs = ROPE_BASE ** (-jnp.arange(n_freq, dtype=f32) / n_freq)
    ang = jnp.concatenate([row[:, None] * freqs, col[:, None] * freqs], axis=-1)
    cos, sin = jnp.cos(ang), jnp.sin(ang)
    cosf = jnp.concatenate([cos, cos], axis=-1)
    sinf = jnp.concatenate([-sin, sin], axis=-1)
    cosf = jnp.concatenate([cosf, jnp.ones((TQ, SSD_STATE), f32)], axis=0)
    sinf = jnp.concatenate([sinf, jnp.zeros((TQ, SSD_STATE), f32)], axis=0)
    return cosf, sinf


def head_expand_matrix(d):
    e = np.zeros((128, SSD_INNER), np.float32)
    for h in range(SSD_HEADS):
        e[SSD_HEADS * d + h, h * SSD_HEADDIM:(h + 1) * SSD_HEADDIM] = 1.0
    return jnp.asarray(e)


def _ssd_kernel(d, lat_blocks, *refs):
    if d == 0:
        (x_ref, xp_ref, xn_ref, dt_ref, cos_ref, sin_ref, cw_ref, cb_ref, dtb_ref, alog_ref, e_ref,
         o_ref, xe, h_scr) = refs
    else:
        (x_ref, xp_ref, xn_ref, dt_ref, cos_ref, sin_ref, z_ref, y0_ref, cw_ref, cb_ref, dtb_ref, alog_ref, e_ref,
         dsk_ref, ng_ref, o_ref, xe, h_scr, y_scr) = refs
    s = pl.program_id(1)
    j = (s - 1) if d == 0 else (lat_blocks - s)
    has_prev = jnp.logical_and(s > 0, j > 0)
    has_next = jnp.logical_and(s > 0, j < lat_blocks - 1)
    q = SSD_CHUNK

    @pl.when(s == 0)
    def _():
        h_scr[...] = jnp.zeros_like(h_scr)

    xbc = _centred_conv(xe, x_ref, xp_ref, xn_ref, cw_ref, cb_ref, has_prev, has_next)
    xbc = xbc * jax.nn.sigmoid(xbc)
    xs = xbc[:, :SSD_INNER]
    cosf, sinf = cos_ref[...], sin_ref[...]

    def rope(g, off):
        v = xbc[:, off + g * SSD_STATE: off + (g + 1) * SSD_STATE]
        return (v * cosf + pltpu.roll(v, SSD_STATE // 2, 1) * sinf).astype(bf16)

    bm = [rope(g, SSD_INNER) for g in range(SSD_GROUPS)]
    cm = [rope(g, SSD_INNER + SSD_GROUPS * SSD_STATE) for g in range(SSD_GROUPS)]
    dt = jax.nn.softplus(dt_ref[...] + dtb_ref[...])
    delta = dt * (-jnp.exp(alog_ref[...]))
    ri = lax.broadcasted_iota(jnp.int32, (q, q), 0)
    ci = lax.broadcasted_iota(jnp.int32, (q, q), 1)
    keep = (ci <= ri) if d == 0 else (ci >= ri)
    tri = jnp.where(keep, 1.0, 0.0).astype(f32)
    lane = lax.broadcasted_iota(jnp.int32, (1, 2 * SSD_HEADDIM), 1)
    halves = (lane < SSD_HEADDIM, lane >= SSD_HEADDIM)
    e = e_ref[...]
    last = q - 1 if d == 0 else 0
    out = o_ref if d == 0 else y_scr

    for c in (range(TQ // q) if d == 0 else range(TQ // q - 1, -1, -1)):
        rows = slice(c * q, (c + 1) * q)
        at = jnp.dot(tri, delta[rows], precision=HI, preferred_element_type=f32)
        at_exp = jnp.dot(at, e, precision=HI, preferred_element_type=f32)
        dt_exp = jnp.dot(dt[rows], e, precision=HI, preferred_element_type=f32)
        tot_exp = at_exp[last:last + 1, :]
        xdt = xs[rows] * dt_exp
        xd = (xdt * jnp.exp(tot_exp - at_exp)).astype(bf16)
        eat = jnp.exp(at_exp)
        cdec = jnp.exp(tot_exp)
        at_row = at.T
        ys = []
        for g in range(SSD_GROUPS):
            bg, cg = bm[g][rows], cm[g][rows]
            cb = _dot_nt(cg, bg)
            ht = h_scr[g]
            yoff = jnp.dot(cg, ht.astype(bf16), preferred_element_type=f32) * eat[:, g * GW:(g + 1) * GW]
            for pp in range(2):
                xpair = xdt[:, g * GW + pp * 128: g * GW + (pp + 1) * 128]
                acc = yoff[:, pp * 128:(pp + 1) * 128]
                for hh in range(2):
                    li = SSD_HEADS * d + 4 * g + 2 * pp + hh
                    seg = at[:, li:li + 1] - at_row[li:li + 1, :]
                    m = (cb * jnp.exp(jnp.where(keep, seg, NEG))).astype(bf16)
                    xm = jnp.where(halves[hh], xpair, 0.0).astype(bf16)
                    acc = acc + jnp.dot(m, xm, preferred_element_type=f32)
                ys.append(acc)
            upd = lax.dot_general(bg, xd[:, g * GW:(g + 1) * GW], (((0,), (0,)), ((), ())),
                                  preferred_element_type=f32)
            h_scr[g] = cdec[:, g * GW:(g + 1) * GW] * ht + upd
        out[rows, :] = jnp.concatenate(ys, axis=-1)

    if d == 1:
        y = y0_ref[...] + y_scr[...] + dsk_ref[...] * xs
        z = z_ref[...]
        y = y * (z * jax.nn.sigmoid(z))
        y = y * lax.rsqrt(jnp.mean(y * y, axis=-1, keepdims=True) + EPS)
        o_ref[...] = (y * ng_ref[...]).astype(o_ref.dtype)


def ssd_pass(d, proj, y0, cosf, sinf, conv_w, conv_b, a_log, dt_bias, d_skip, norm_g, bsz, lat_blocks):
    t = proj.shape[0]
    jj, blk = _seq_block(d, bsz, lat_blocks)
    row = lambda c: (lambda b, s: (blk(b, s), c))
    tbl = lambda b, s: (jnp.where(s == 0, lat_blocks, jj(s)), 0)
    const = lambda shape: pl.BlockSpec(shape, lambda b, s: (0,) * len(shape))
    pad128 = lambda v: jnp.pad(v.reshape(1, -1).astype(f32), ((0, 0), (0, 128 - v.size)))
    in_specs = _halo_specs(SSD_CONV_DIM, 0, blk, t) + [
        pl.BlockSpec((TQ, 128), row(COL_DT)),
        pl.BlockSpec((TQ, SSD_STATE), tbl),
        pl.BlockSpec((TQ, SSD_STATE), tbl),
    ]
    args = [proj, proj, proj, proj, cosf, sinf]
    if d == 1:
        in_specs += [pl.BlockSpec((TQ, SSD_INNER), row(COL_Z)), pl.BlockSpec((TQ, SSD_INNER), row(0))]
        args += [proj, y0]
    in_specs += [const((4, SSD_CONV_DIM)), const((1, SSD_CONV_DIM)), const((1, 128)), const((1, 128)),
                 const((128, SSD_INNER))]
    args += [conv_w, conv_b.reshape(1, -1), pad128(dt_bias), pad128(a_log), head_expand_matrix(d)]
    scratch = [pltpu.VMEM((TQ + 2 * HALO, SSD_CONV_DIM), f32), pltpu.VMEM((SSD_GROUPS, SSD_STATE, GW), f32)]
    if d == 1:
        in_specs += [const((1, SSD_INNER)), const((1, SSD_INNER))]
        args += [jnp.repeat(d_skip, SSD_HEADDIM).reshape(1, -1), norm_g.reshape(1, -1)]
        scratch.append(pltpu.VMEM((TQ, SSD_INNER), f32))
    return pl.pallas_call(
        functools.partial(_ssd_kernel, d, lat_blocks),
        grid=(bsz, lat_blocks + 1),
        in_specs=in_specs,
        out_specs=pl.BlockSpec((TQ, SSD_INNER), row(0)),
        out_shape=jax.ShapeDtypeStruct((t, SSD_INNER), f32 if d == 0 else bf16),
        scratch_shapes=scratch,
        compiler_params=_params(),
        name=f"ssd_pass{d}",
    )(*args)


def ssd_branch(proj, cosf, sinf, conv_w, conv_b, a_log, dt_bias, d_skip, norm_g, bsz, lat_blocks):
    y0 = ssd_pass(0, proj, None, cosf, sinf, conv_w, conv_b, a_log, dt_bias, d_skip, norm_g, bsz, lat_blocks)
    return ssd_pass(1, proj, y0, cosf, sinf, conv_w, conv_b, a_log, dt_bias, d_skip, norm_g, bsz, lat_blocks)


def _tile_rows(mod_rows, bsz):
    idx = np.concatenate([np.repeat(np.arange(bsz), SEQ // TM), np.full(bsz * CTX_LEN // TM, bsz)])
    return mod_rows[idx][:, None, :]


def _expert_choice_moe(v, w_router, w1, w3, w2, bsz, with_ctx):
    t, d = v.shape
    aff = router_affinity(v, w_router)[:, :N_EXPERTS]
    nl = bsz * SEQ

    def choose(a, length, row0):
        cap = CAPACITY_FACTOR * length // N_EXPERTS
        g, idx = lax.top_k(jnp.swapaxes(a.reshape(bsz, length, N_EXPERTS), 1, 2), cap)
        rows = idx + (row0 + jnp.arange(bsz) * length)[:, None, None]
        to_e = lambda x: jnp.swapaxes(x, 0, 1).reshape(N_EXPERTS, bsz * cap)
        return to_e(g), to_e(rows)

    g, rows = choose(aff[:nl], SEQ, 0)
    if with_ctx:
        gc, rc = choose(aff[nl:], CTX_LEN, nl)
        g = jnp.concatenate([g, gc], axis=1)
        rows = jnp.concatenate([rows, rc], axis=1)
    r = rows.shape[1]
    xg = jnp.take(v, rows.reshape(-1), axis=0).reshape(N_EXPERTS, r, d)
    ye = expert_ffn(xg, g[..., None], w1, w3, w2, r // 4)
    return jnp.zeros((t, d), f32).at[rows.reshape(-1)].add(ye.reshape(-1, d))


def kernel(x, c, ctx, c_ctx, w_ada, b_ada, norm_mix, norm_ffn, w_in, lru_conv_w, lru_conv_b, lru_w_r, lru_b_r,
           lru_w_i, lru_b_i, lru_lambda, ssd_conv_w, ssd_conv_b, ssd_a_log, ssd_dt_bias, ssd_d, ssd_norm, na_rpb,
           w_branch_lru, w_branch_ssd, w_branch_na, w_out, w_router, w1, w3, w2, norm_final):
    bsz = x.shape[0]
    assert x.shape[1:] == (SEQ, D_MODEL) and ctx.shape[1:] == (CTX_LEN, D_MODEL) and bsz * CTX_LEN == TM
    d = D_MODEL
    h = jnp.concatenate([x.reshape(bsz * SEQ, d), ctx.reshape(bsz * CTX_LEN, d)], axis=0)
    t = h.shape[0]

    cond = jnp.concatenate([c, c_ctx[None, :], jnp.zeros((8 - bsz - 1, d), f32)], axis=0)
    mod = ada_modulation(cond, w_ada, b_ada)
    cosf, sinf = rope_tables(SEQ)

    sizes = np.cumsum((LRU_WIDTH, LRU_WIDTH, SSD_INNER, SSD_CONV_DIM, 2 * SSD_HEADS, NA_WIDTH, NA_WIDTH, NA_WIDTH))
    ax_w, ag_w, z_w, xbc_w, dt_w, q_w, k_w, v_w, g_w = jnp.split(w_in, [int(i) for i in sizes], axis=-1)
    pad_w = jnp.zeros((w_in.shape[0], d, 1024 - 2 * SSD_HEADS), f32)
    w_f32part = jnp.concatenate([xbc_w, ax_w, ag_w, z_w, dt_w, pad_w, g_w], axis=-1).astype(bf16)
    w_qkv = jnp.concatenate([q_w, k_w, v_w], axis=-1).astype(bf16)
    pa, pb, pc, wo = (w.astype(bf16) for w in (w_branch_lru, w_branch_ssd, w_branch_na, w_out))
    w1b, w3b, w2b = (w.astype(bf16) for w in (w1, w3, w2))

    for l in range(DEPTH):
        sh1, sc1, g1, sh2, sc2, g2 = (_tile_rows(m, bsz) for m in jnp.split(mod[l], 6, axis=-1))
        u = norm_modulate(h, norm_mix[l], sc1, sh1, bf16)
        proj = matmul_bf16(u, w_f32part[l], f32)
        qkv = matmul_bf16(u, w_qkv[l], bf16)
        ya = lru_branch(proj, lru_conv_w[l], lru_conv_b[l], lru_w_r[l], lru_b_r[l], lru_w_i[l], lru_b_i[l],
                        lru_lambda[l], bsz, LAT_BLOCKS)
        yb = ssd_branch(proj, cosf, sinf, ssd_conv_w[l], ssd_conv_b[l], ssd_a_log[l], ssd_dt_bias[l], ssd_d[l],
                        ssd_norm[l], bsz, LAT_BLOCKS)
        yc = na_attention(qkv, na_bias_table(na_rpb[l]), bsz)
        y = branch_merge(ya, yb, yc, proj, pa[l], pb[l], pc[l], COL_G)
        h = residual_matmul(y, wo[l], h, g1)
        v = norm_modulate(h, norm_ffn[l], sc2, sh2, bf16)
        moe = _expert_choice_moe(v, w_router[l], w1b[l], w3b[l], w2b[l], bsz, with_ctx=l < DEPTH - 1)
        h = (h.reshape(-1, TM, d) + g2 * moe.reshape(-1, TM, d)).reshape(t, d)

    zeros = jnp.zeros((t // TM, 1, d), f32)
    out = norm_modulate(h, norm_final, zeros, zeros, f32)
    return out[:bsz * SEQ].reshape(bsz, SEQ, d)
```

```python
import functools

import jax
import jax.numpy as jnp
import numpy as np
from jax import lax
from jax.experimental import pallas as pl
from jax.experimental.pallas import tpu as pltpu

D_MODEL = 2048
SEQ = 4096
CTX_LEN = 256
DEPTH = 4
GRID_W = 64
EPS = 1e-6
ROPE_BASE = 10000.0
LRU_WIDTH = 1024
LRU_BLOCKS = 16
LRU_BLOCK = LRU_WIDTH // LRU_BLOCKS
LRU_C = 8.0
SSD_INNER = 1024
SSD_HEADDIM = 64
SSD_HEADS = SSD_INNER // SSD_HEADDIM
SSD_GROUPS = 4
SSD_STATE = 128
SSD_CHUNK = 128
SSD_CONV_DIM = SSD_INNER + 2 * SSD_GROUPS * SSD_STATE
NA_HEADS = 16
NA_HEADDIM = 64
NA_WIDTH = NA_HEADS * NA_HEADDIM
NA_WIN_R = 8
NA_WIN_C = 16
N_EXPERTS = 16
EXPERT_FF = 1024
CAPACITY_FACTOR = 2

V7X_VMEM_LIMIT = 56 * 1024 * 1024
TM = 1024
TQ = 256
HALO = 8
LAT_BLOCKS = SEQ // TQ
NEG = -1e30
HI = lax.Precision.HIGHEST
bf16 = jnp.bfloat16
f32 = jnp.float32

TN = 512
DT_PAD = 480
COL_AX, COL_AG, COL_Z, COL_XS, COL_BC = 0, 1, 2, 3, 4
COL_DT = 40
W_QKV0 = 5632
COL_G = 5632
N_F32 = COL_G + 3 * D_MODEL
GW = SSD_INNER // SSD_GROUPS

NA_GROUP_ROWS = 2
NA_BAND_ROWS = 10
NA_NCFG = 5


def _params():
    return pltpu.CompilerParams(vmem_limit_bytes=V7X_VMEM_LIMIT)


def _ada_kernel(x_ref, w_ref, b_ref, o_ref):
    c = x_ref[...]
    x = (c * jax.nn.sigmoid(c)).astype(bf16)
    o_ref[...] = jnp.dot(x, w_ref[...].astype(bf16), preferred_element_type=f32) + b_ref[...]


def ada_modulation(cond, w_ada, b_ada):
    depth, d, n = w_ada.shape
    tn = 1024
    return pl.pallas_call(
        _ada_kernel,
        grid=(depth, n // tn),
        in_specs=[pl.BlockSpec((8, d), lambda l, j: (0, 0)),
                  pl.BlockSpec((None, d, tn), lambda l, j: (l, 0, j)),
                  pl.BlockSpec((None, 1, tn), lambda l, j: (l, 0, j))],
        out_specs=pl.BlockSpec((None, 8, tn), lambda l, j: (l, 0, j)),
        out_shape=jax.ShapeDtypeStruct((depth, 8, n), f32),
        compiler_params=_params(),
        name="ada_modulation",
    )(cond, w_ada, b_ada.reshape(depth, 1, n))


def _norm_mod_kernel(h_ref, g_ref, sc_ref, sh_ref, o_ref):
    x = h_ref[...]
    y = x * lax.rsqrt(jnp.mean(x * x, axis=-1, keepdims=True) + EPS)
    o_ref[...] = ((y * g_ref[...]) * (1.0 + sc_ref[...]) + sh_ref[...]).astype(o_ref.dtype)


def norm_modulate(h, g, sc_t, sh_t, out_dtype):
    t, d = h.shape
    tm = min(512, TM)
    per = TM // tm
    return pl.pallas_call(
        _norm_mod_kernel,
        grid=(t // tm,),
        in_specs=[pl.BlockSpec((tm, d), lambda i: (i, 0)),
                  pl.BlockSpec((1, d), lambda i: (0, 0)),
                  pl.BlockSpec((None, 1, d), lambda i: (i // per, 0, 0)),
                  pl.BlockSpec((None, 1, d), lambda i: (i // per, 0, 0))],
        out_specs=pl.BlockSpec((tm, d), lambda i: (i, 0)),
        out_shape=jax.ShapeDtypeStruct((t, d), out_dtype),
        compiler_params=_params(),
        name="norm_modulate",
    )(h, g.reshape(1, d), sc_t, sh_t)


def _mm_kernel(x_ref, w_ref, o_ref):
    o_ref[...] = jnp.dot(x_ref[...], w_ref[...], preferred_element_type=f32).astype(o_ref.dtype)


def matmul_bf16(x, w, n_out, w_block, out_dtype):
    m, k = x.shape
    return pl.pallas_call(
        _mm_kernel,
        grid=(m // TM, n_out // TN),
        in_specs=[pl.BlockSpec((TM, k), lambda i, j: (i, 0)),
                  pl.BlockSpec((k, TN), lambda i, j: (0, w_block(j)))],
        out_specs=pl.BlockSpec((TM, TN), lambda i, j: (i, j)),
        out_shape=jax.ShapeDtypeStruct((m, n_out), out_dtype),
        compiler_params=_params(),
        name="matmul_bf16",
    )(x, w)


def _merge_kernel(ya_ref, yb_ref, yc_ref, ga_ref, gb_ref, gc_ref, pa_ref, pb_ref, pc_ref, o_ref):
    acc = jax.nn.sigmoid(ga_ref[...]) * jnp.dot(ya_ref[...], pa_ref[...], preferred_element_type=f32)
    acc = acc + jax.nn.sigmoid(gb_ref[...]) * jnp.dot(yb_ref[...], pb_ref[...], preferred_element_type=f32)
    acc = acc + jax.nn.sigmoid(gc_ref[...]) * jnp.dot(yc_ref[...], pc_ref[...], preferred_element_type=f32)
    o_ref[...] = acc.astype(o_ref.dtype)


def branch_merge(ya, yb, yc, proj, pa, pb, pc, g_col0):
    t, k = ya.shape
    n = pa.shape[1]
    tn = 512
    gb0 = g_col0 // tn
    nj = n // tn
    xs = pl.BlockSpec((TM, k), lambda i, j: (i, 0))
    ws = pl.BlockSpec((k, tn), lambda i, j: (0, j))
    gs = lambda q: pl.BlockSpec((TM, tn), lambda i, j: (i, gb0 + q * nj + j))
    return pl.pallas_call(
        _merge_kernel,
        grid=(t // TM, nj),
        in_specs=[xs, xs, xs, gs(0), gs(1), gs(2), ws, ws, ws],
        out_specs=pl.BlockSpec((TM, tn), lambda i, j: (i, j)),
        out_shape=jax.ShapeDtypeStruct((t, n), bf16),
        compiler_params=_params(),
        name="branch_merge",
    )(ya, yb, yc, proj, proj, proj, pa, pb, pc)


def _resid_mm_kernel(y_ref, w_ref, h_ref, g_ref, o_ref):
    o_ref[...] = h_ref[...] + g_ref[...] * jnp.dot(y_ref[...], w_ref[...], preferred_element_type=f32)


def residual_matmul(y, w, h, gate_t):
    m, k = y.shape
    n = w.shape[1]
    tn = 512
    return pl.pallas_call(
        _resid_mm_kernel,
        grid=(m // TM, n // tn),
        in_specs=[pl.BlockSpec((TM, k), lambda i, j: (i, 0)),
                  pl.BlockSpec((k, tn), lambda i, j: (0, j)),
                  pl.BlockSpec((TM, tn), lambda i, j: (i, j)),
                  pl.BlockSpec((None, 1, tn), lambda i, j: (i, 0, j))],
        out_specs=pl.BlockSpec((TM, tn), lambda i, j: (i, j)),
        out_shape=jax.ShapeDtypeStruct((m, n), f32),
        compiler_params=_params(),
        name="residual_matmul",
    )(y, w, h, gate_t)


def _router_kernel(v_ref, w_ref, o_ref):
    s = jnp.dot(v_ref[...], w_ref[...], preferred_element_type=f32)
    lane = lax.broadcasted_iota(jnp.int32, s.shape, 1)
    s = jnp.where(lane < N_EXPERTS, s, NEG)
    e = jnp.exp(s - s.max(axis=-1, keepdims=True))
    o_ref[...] = e / e.sum(axis=-1, keepdims=True)


def router_affinity(v, w_router):
    t, d = v.shape
    w = jnp.pad(w_router, ((0, 0), (0, 128 - N_EXPERTS))).astype(bf16)
    return pl.pallas_call(
        _router_kernel,
        grid=(t // TM,),
        in_specs=[pl.BlockSpec((TM, d), lambda i: (i, 0)), pl.BlockSpec((d, 128), lambda i: (0, 0))],
        out_specs=pl.BlockSpec((TM, 128), lambda i: (i, 0)),
        out_shape=jax.ShapeDtypeStruct((t, 128), f32),
        compiler_params=_params(),
        name="router_affinity",
    )(v, w)


def _expert_kernel(x_ref, g_ref, w1_ref, w3_ref, w2_ref, o_ref):
    x = x_ref[...]
    a = jnp.dot(x, w1_ref[...], preferred_element_type=f32)
    b = jnp.dot(x, w3_ref[...], preferred_element_type=f32)
    hdn = ((a * jax.nn.sigmoid(a)) * b).astype(bf16)
    o_ref[...] = jnp.dot(hdn, w2_ref[...], preferred_element_type=f32) * g_ref[...]


def expert_ffn(xg, gates, w1, w3, w2, tm):
    e, r, d = xg.shape
    f = w1.shape[-1]
    return pl.pallas_call(
        _expert_kernel,
        grid=(e, r // tm),
        in_specs=[pl.BlockSpec((None, tm, d), lambda k, i: (k, i, 0)),
                  pl.BlockSpec((None, tm, 1), lambda k, i: (k, i, 0)),
                  pl.BlockSpec((None, d, f), lambda k, i: (k, 0, 0)),
                  pl.BlockSpec((None, d, f), lambda k, i: (k, 0, 0)),
                  pl.BlockSpec((None, f, d), lambda k, i: (k, 0, 0))],
        out_specs=pl.BlockSpec((None, tm, d), lambda k, i: (k, i, 0)),
        out_shape=jax.ShapeDtypeStruct((e, r, d), f32),
        compiler_params=_params(),
        name="expert_ffn",
    )(xg, gates, w1, w3, w2)


def na_bias_table(rpb):
    rows = SEQ // GRID_W
    n_dr, n_dc = 2 * NA_WIN_R - 1, 2 * NA_WIN_C - 1
    qc, kc = np.arange(GRID_W)[:, None], np.arange(GRID_W)[None, :]
    cs = np.clip(qc - NA_WIN_C // 2, 0, GRID_W - NA_WIN_C)
    col_ok = (kc >= cs) & (kc < cs + NA_WIN_C)
    pick = (np.arange(n_dc)[:, None, None] == (kc - qc + NA_WIN_C - 1)[None]) & col_ok[None]
    blocks = jnp.einsum('...rd,dqk->...rqk', rpb, jnp.asarray(pick, f32), precision=HI)
    blocks = jnp.where(col_ok, blocks, NEG)
    blocks = jnp.concatenate([blocks, jnp.full(blocks.shape[:-3] + (1, GRID_W, GRID_W), NEG, f32)], axis=-3)
    which = np.full((NA_NCFG, NA_GROUP_ROWS, NA_BAND_ROWS), n_dr, np.int32)
    for c, r0 in enumerate((0, 2, 4, 60, 62)):
        bs = int(np.clip(r0 - NA_WIN_R // 2, 0, rows - NA_BAND_ROWS))
        for qr in range(NA_GROUP_ROWS):
            r = r0 + qr
            rs = int(np.clip(r - NA_WIN_R // 2, 0, rows - NA_WIN_R))
            for kr in range(NA_BAND_ROWS):
                if rs <= bs + kr < rs + NA_WIN_R:
                    which[c, qr, kr] = bs + kr - r + NA_WIN_R - 1
    tbl = jnp.take(blocks, which.reshape(-1), axis=-3)
    lead = tbl.shape[:-3]
    tbl = tbl.reshape(lead + (NA_NCFG, NA_GROUP_ROWS, NA_BAND_ROWS, GRID_W, GRID_W))
    tbl = jnp.swapaxes(tbl, -3, -2)
    return tbl.reshape(lead + (NA_NCFG, NA_GROUP_ROWS * GRID_W, NA_BAND_ROWS * GRID_W))


def _dot_nt(a, b):
    return lax.dot_general(a, b, (((1,), (1,)), ((), ())), preferred_element_type=f32)


def _softmax_av(parts):
    m = parts[0][0].max(axis=-1, keepdims=True)
    for s, _ in parts[1:]:
        m = jnp.maximum(m, s.max(axis=-1, keepdims=True))
    l = None
    o = None
    for s, v in parts:
        p = jnp.exp(s - m)
        ls = p.sum(axis=-1, keepdims=True)
        os_ = jnp.dot(p.astype(bf16), v, preferred_element_type=f32)
        l = ls if l is None else l + ls
        o = os_ if o is None else o + os_
    return o / l


def _na_kernel(q_ref, k_ref, v_ref, kc_ref, vc_ref, tbl_ref, o_ref):
    j = pl.program_id(2)
    scale = NA_HEADDIM ** -0.5
    lane = lax.broadcasted_iota(jnp.int32, (1, 2 * NA_HEADDIM), 1)
    first = lane < NA_HEADDIM
    kc = kc_ref[...]
    vc = vc_ref[...]
    gq = NA_GROUP_ROWS * GRID_W

    def head_q(q, hh):
        keep = first if hh == 0 else jnp.logical_not(first)
        return jnp.where(keep, q, jnp.zeros_like(q))

    @pl.when(j < LAT_BLOCKS)
    def _():
        for g in range(TQ // gq):
            r0 = j * (TQ // GRID_W) + NA_GROUP_ROWS * g
            bs = jnp.clip(r0 - NA_WIN_R // 2, 0, SEQ // GRID_W - NA_BAND_ROWS)
            cfg = jnp.where(r0 == 0, 0, jnp.where(r0 == 2, 1, jnp.where(r0 == 60, 3, jnp.where(r0 == 62, 4, 2))))
            start = pl.multiple_of(bs * GRID_W, GRID_W)
            kb = k_ref[pl.ds(start, NA_BAND_ROWS * GRID_W), :]
            vb = v_ref[pl.ds(start, NA_BAND_ROWS * GRID_W), :]
            q = q_ref[pl.ds(g * gq, gq), :]
            outs = []
            for hh in range(2):
                qm = head_q(q, hh)
                s = _dot_nt(qm, kb) * scale + tbl_ref[hh, cfg]
                sc = _dot_nt(qm, kc) * scale
                outs.append(_softmax_av([(s, vb), (sc, vc)]))
            o_ref[pl.ds(g * gq, gq), :] = jnp.where(first, outs[0], outs[1]).astype(o_ref.dtype)

    @pl.when(j == LAT_BLOCKS)
    def _():
        q = q_ref[...]
        outs = []
        for hh in range(2):
            sc = _dot_nt(head_q(q, hh), kc) * scale
            outs.append(_softmax_av([(sc, vc)]))
        o_ref[...] = jnp.where(first, outs[0], outs[1]).astype(o_ref.dtype)


def na_attention(qkv, tbl, bsz):
    t = qkv.shape[0]
    npair = NA_HEADS // 2
    ctx0 = bsz * LAT_BLOCKS

    def qrow(b, p, j):
        return jnp.where(j < LAT_BLOCKS, b * LAT_BLOCKS + j, ctx0 + b)

    return pl.pallas_call(
        _na_kernel,
        grid=(bsz, npair, LAT_BLOCKS + 1),
        in_specs=[
            pl.BlockSpec((TQ, 128), lambda b, p, j: (qrow(b, p, j), p)),
            pl.BlockSpec((SEQ, 128), lambda b, p, j: (b, npair + p)),
            pl.BlockSpec((SEQ, 128), lambda b, p, j: (b, 2 * npair + p)),
            pl.BlockSpec((CTX_LEN, 128), lambda b, p, j: (ctx0 + b, npair + p)),
            pl.BlockSpec((CTX_LEN, 128), lambda b, p, j: (ctx0 + b, 2 * npair + p)),
            pl.BlockSpec((2, NA_NCFG, NA_GROUP_ROWS * GRID_W, NA_BAND_ROWS * GRID_W), lambda b, p, j: (p, 0, 0, 0)),
        ],
        out_specs=pl.BlockSpec((TQ, 128), lambda b, p, j: (qrow(b, p, j), p)),
        out_shape=jax.ShapeDtypeStruct((t, NA_WIDTH), bf16),
        compiler_params=_params(),
        name="na_attention",
    )(qkv, qkv, qkv, qkv, qkv, tbl)


def _seq_block(d, bsz, lat_blocks):
    ctx0 = bsz * lat_blocks

    def jj(s):
        return (s - 1) if d == 0 else (lat_blocks - s)

    def blk(b, s):
        return jnp.where(s == 0, ctx0 + b, b * lat_blocks + jj(s))

    return jj, blk


def _halo_specs(width, col, blk, t):
    nb8 = TQ // HALO
    return [
        pl.BlockSpec((TQ, width), lambda b, s: (blk(b, s), col)),
        pl.BlockSpec((HALO, width), lambda b, s: (jnp.maximum(blk(b, s) * nb8 - 1, 0), col)),
        pl.BlockSpec((HALO, width), lambda b, s: (jnp.minimum(blk(b, s) * nb8 + nb8, t // HALO - 1), col)),
    ]


def _centred_conv(xe, parts, cw_ref, cb_ref, has_prev, has_next):
    c0 = 0
    for x_ref, xp_ref, xn_ref in parts:
        cols = slice(c0, c0 + x_ref.shape[1])
        xe[0:HALO, cols] = jnp.where(has_prev, xp_ref[...], 0.0)
        xe[HALO:HALO + TQ, cols] = x_ref[...]
        xe[HALO + TQ:2 * HALO + TQ, cols] = jnp.where(has_next, xn_ref[...], 0.0)
        c0 += x_ref.shape[1]
    out = cb_ref[...] + cw_ref[0:1, :] * xe[HALO - 2:HALO - 2 + TQ, :]
    for k in range(1, 4):
        out = out + cw_ref[k:k + 1, :] * xe[HALO - 2 + k:HALO - 2 + k + TQ, :]
    return out


def lru_gate_weights(w_r, w_i):
    def pair(w):
        w = w.reshape(2, LRU_BLOCKS // 2, 2, LRU_BLOCK, LRU_BLOCK)
        z = jnp.zeros_like(w[:, :, 0])
        top = jnp.concatenate([w[:, :, 0], z], axis=-1)
        bot = jnp.concatenate([z, w[:, :, 1]], axis=-1)
        return jnp.concatenate([top, bot], axis=-2)
    return jnp.concatenate([pair(w_r), pair(w_i)], axis=-1).astype(bf16)


def _gelu_tanh(x):
    return 0.5 * x * (1.0 + jnp.tanh(float(np.sqrt(2.0 / np.pi)) * (x + 0.044715 * (x * x * x))))


def _lru_kernel(d, lat_blocks, *refs):
    if d == 0:
        (x_ref, xp_ref, xn_ref, cw_ref, cb_ref, wg_ref, br_ref, bi_ref, lam_ref, o_ref,
         xe, a_scr, u_scr, h_scr) = refs
    else:
        (x_ref, xp_ref, xn_ref, ag_ref, hf_ref, cw_ref, cb_ref, wg_ref, br_ref, bi_ref, lam_ref, o_ref,
         xe, a_scr, u_scr, h_scr, hs_scr) = refs
    s = pl.program_id(1)
    j = (s - 1) if d == 0 else (lat_blocks - s)
    has_prev = jnp.logical_and(s > 0, j > 0)
    has_next = jnp.logical_and(s > 0, j < lat_blocks - 1)

    @pl.when(s == 0)
    def _():
        h_scr[...] = jnp.zeros_like(h_scr)

    xc = _centred_conv(xe, [(x_ref, xp_ref, xn_ref)], cw_ref, cb_ref, has_prev, has_next)
    xcb = xc.astype(bf16)
    sp = jax.nn.softplus(-lam_ref[...])
    for p in range(LRU_WIDTH // 128):
        sl = slice(128 * p, 128 * (p + 1))
        g = jnp.dot(xcb[:, sl], wg_ref[p], preferred_element_type=f32)
        r = jax.nn.sigmoid(g[:, :128] + br_ref[:, sl])
        i = jax.nn.sigmoid(g[:, 128:] + bi_ref[:, sl])
        log_a = -LRU_C * r * sp[:, sl]
        a = jnp.exp(log_a)
        a_scr[:, sl] = a
        u_scr[:, sl] = jnp.sqrt(-jnp.tanh(log_a) * (a * a + 1.0)) * (i * xc[:, sl])

    out = o_ref if d == 0 else hs_scr
    ngrp = TQ // 8

    def body(gi, h):
        base = pl.multiple_of((gi if d == 0 else ngrp - 1 - gi) * 8, 8)
        for t in (range(8) if d == 0 else range(7, -1, -1)):
            h = a_scr[pl.ds(base + t, 1), :] * h + u_scr[pl.ds(base + t, 1), :]
            out[pl.ds(base + t, 1), :] = h
        return h

    h = lax.fori_loop(0, ngrp, body, h_scr[0:1, :])
    h_scr[0:1, :] = h
    if d == 1:
        y = (hf_ref[...] + hs_scr[...]) * _gelu_tanh(ag_ref[...])
        o_ref[...] = y.astype(o_ref.dtype)


def lru_pass(d, proj, hf, conv_w, conv_b, wg, b_r, b_i, lam, bsz, lat_blocks):
    t = proj.shape[0]
    _, blk = _seq_block(d, bsz, lat_blocks)
    row = lambda c: (lambda b, s: (blk(b, s), c))
    in_specs = _halo_specs(LRU_WIDTH, COL_AX, blk, t)
    args = [proj, proj, proj]
    if d == 1:
        in_specs += [pl.BlockSpec((TQ, LRU_WIDTH), row(COL_AG)), pl.BlockSpec((TQ, LRU_WIDTH), row(0))]
        args += [proj, hf]
    const = lambda shape: pl.BlockSpec(shape, lambda b, s: (0,) * len(shape))
    in_specs += [const((4, LRU_WIDTH)), const((1, LRU_WIDTH)), const((LRU_WIDTH // 128, 128, 256)),
                 const((1, LRU_WIDTH)), const((1, LRU_WIDTH)), const((1, LRU_WIDTH))]
    args += [conv_w, conv_b.reshape(1, -1), wg[d], b_r[d].reshape(1, -1), b_i[d].reshape(1, -1), lam[d].reshape(1, -1)]
    scratch = [pltpu.VMEM((TQ + 2 * HALO, LRU_WIDTH), f32), pltpu.VMEM((TQ, LRU_WIDTH), f32),
               pltpu.VMEM((TQ, LRU_WIDTH), f32), pltpu.VMEM((8, LRU_WIDTH), f32)]
    if d == 1:
        scratch.append(pltpu.VMEM((TQ, LRU_WIDTH), f32))
    return pl.pallas_call(
        functools.partial(_lru_kernel, d, lat_blocks),
        grid=(bsz, lat_blocks + 1),
        in_specs=in_specs,
        out_specs=pl.BlockSpec((TQ, LRU_WIDTH), row(0)),
        out_shape=jax.ShapeDtypeStruct((t, LRU_WIDTH), f32 if d == 0 else bf16),
        scratch_shapes=scratch,
        compiler_params=_params(),
        name=f"lru_pass{d}",
    )(*args)


def lru_branch(proj, conv_w, conv_b, w_r, b_r, w_i, b_i, lam, bsz, lat_blocks):
    wg = lru_gate_weights(w_r, w_i)
    hf = lru_pass(0, proj, None, conv_w, conv_b, wg, b_r, b_i, lam, bsz, lat_blocks)
    return lru_pass(1, proj, hf, conv_w, conv_b, wg, b_r, b_i, lam, bsz, lat_blocks)


def rope_tables(seq):
    pos = jnp.arange(seq)
    row = (pos // GRID_W).astype(f32)
    col = (pos % GRID_W).astype(f32)
    n_freq = SSD_STATE // 4
    freqs = ROPE_BASE ** (-jnp.arange(n_freq, dtype=f32) / n_freq)
    ang = jnp.concatenate([row[:, None] * freqs, col[:, None] * freqs], axis=-1)
    cos, sin = jnp.cos(ang), jnp.sin(ang)
    cosf = jnp.concatenate([cos, cos], axis=-1)
    sinf = jnp.concatenate([-sin, sin], axis=-1)
    cosf = jnp.concatenate([cosf, jnp.ones((TQ, SSD_STATE), f32)], axis=0)
    sinf = jnp.concatenate([sinf, jnp.zeros((TQ, SSD_STATE), f32)], axis=0)
    return cosf, sinf


def head_expand_matrix(d):
    e = np.zeros((128, SSD_INNER), np.float32)
    for h in range(SSD_HEADS):
        e[SSD_HEADS * d + h, h * SSD_HEADDIM:(h + 1) * SSD_HEADDIM] = 1.0
    return jnp.asarray(e)


def _ssd_kernel(d, lat_blocks, *refs):
    xs_refs, bc_refs, refs = refs[0:3], refs[3:6], refs[6:]
    if d == 0:
        (dt_ref, cos_ref, sin_ref, cw_ref, cb_ref, dtb_ref, alog_ref, e_ref,
         o_ref, xe, h_scr) = refs
    else:
        (dt_ref, cos_ref, sin_ref, z_ref, y0_ref, cw_ref, cb_ref, dtb_ref, alog_ref, e_ref,
         dsk_ref, ng_ref, o_ref, xe, h_scr, y_scr) = refs
    s = pl.program_id(1)
    j = (s - 1) if d == 0 else (lat_blocks - s)
    has_prev = jnp.logical_and(s > 0, j > 0)
    has_next = jnp.logical_and(s > 0, j < lat_blocks - 1)
    q = SSD_CHUNK

    @pl.when(s == 0)
    def _():
        h_scr[...] = jnp.zeros_like(h_scr)

    xbc = _centred_conv(xe, [xs_refs, bc_refs], cw_ref, cb_ref, has_prev, has_next)
    xbc = xbc * jax.nn.sigmoid(xbc)
    xs = xbc[:, :SSD_INNER]
    cosf, sinf = cos_ref[...], sin_ref[...]

    def rope(g, off):
        v = xbc[:, off + g * SSD_STATE: off + (g + 1) * SSD_STATE]
        return (v * cosf + pltpu.roll(v, SSD_STATE // 2, 1) * sinf).astype(bf16)

    bm = [rope(g, SSD_INNER) for g in range(SSD_GROUPS)]
    cm = [rope(g, SSD_INNER + SSD_GROUPS * SSD_STATE) for g in range(SSD_GROUPS)]
    dt = jax.nn.softplus(dt_ref[...] + dtb_ref[...])
    delta = dt * (-jnp.exp(alog_ref[...]))
    ri = lax.broadcasted_iota(jnp.int32, (q, q), 0)
    ci = lax.broadcasted_iota(jnp.int32, (q, q), 1)
    keep = (ci <= ri) if d == 0 else (ci >= ri)
    tri = jnp.where(keep, 1.0, 0.0).astype(f32)
    lane = lax.broadcasted_iota(jnp.int32, (1, 2 * SSD_HEADDIM), 1)
    halves = (lane < SSD_HEADDIM, lane >= SSD_HEADDIM)
    e = e_ref[...]
    last = q - 1 if d == 0 else 0
    out = o_ref if d == 0 else y_scr

    for c in (range(TQ // q) if d == 0 else range(TQ // q - 1, -1, -1)):
        rows = slice(c * q, (c + 1) * q)
        at = jnp.dot(tri, delta[rows], precision=HI, preferred_element_type=f32)
        at_exp = jnp.dot(at, e, precision=HI, preferred_element_type=f32)
        dt_exp = jnp.dot(dt[rows], e, precision=HI, preferred_element_type=f32)
        tot_exp = at_exp[last:last + 1, :]
        xdt = xs[rows] * dt_exp
        xd = (xdt * jnp.exp(tot_exp - at_exp)).astype(bf16)
        eat = jnp.exp(at_exp)
        cdec = jnp.exp(tot_exp)
        at_row = at.T
        ys = []
        for g in range(SSD_GROUPS):
            bg, cg = bm[g][rows], cm[g][rows]
            cb = _dot_nt(cg, bg)
            ht = h_scr[g]
            yoff = jnp.dot(cg, ht.astype(bf16), preferred_element_type=f32) * eat[:, g * GW:(g + 1) * GW]
            for pp in range(2):
                xpair = xdt[:, g * GW + pp * 128: g * GW + (pp + 1) * 128]
                acc = yoff[:, pp * 128:(pp + 1) * 128]
                for hh in range(2):
                    li = SSD_HEADS * d + 4 * g + 2 * pp + hh
                    seg = at[:, li:li + 1] - at_row[li:li + 1, :]
                    m = (cb * jnp.exp(jnp.where(keep, seg, NEG))).astype(bf16)
                    xm = jnp.where(halves[hh], xpair, 0.0).astype(bf16)
                    acc = acc + jnp.dot(m, xm, preferred_element_type=f32)
                ys.append(acc)
            upd = lax.dot_general(bg, xd[:, g * GW:(g + 1) * GW], (((0,), (0,)), ((), ())),
                                  preferred_element_type=f32)
            h_scr[g] = cdec[:, g * GW:(g + 1) * GW] * ht + upd
        out[rows, :] = jnp.concatenate(ys, axis=-1)

    if d == 1:
        y = y0_ref[...] + y_scr[...] + dsk_ref[...] * xs
        z = z_ref[...]
        y = y * (z * jax.nn.sigmoid(z))
        y = y * lax.rsqrt(jnp.mean(y * y, axis=-1, keepdims=True) + EPS)
        o_ref[...] = (y * ng_ref[...]).astype(o_ref.dtype)


def ssd_pass(d, proj, y0, cosf, sinf, conv_w, conv_b, a_log, dt_bias, d_skip, norm_g, bsz, lat_blocks):
    t = proj.shape[0]
    jj, blk = _seq_block(d, bsz, lat_blocks)
    row = lambda c: (lambda b, s: (blk(b, s), c))
    tbl = lambda b, s: (jnp.where(s == 0, lat_blocks, jj(s)), 0)
    const = lambda shape: pl.BlockSpec(shape, lambda b, s: (0,) * len(shape))
    pad128 = lambda v: jnp.pad(v.reshape(1, -1).astype(f32), ((0, 0), (0, 128 - v.size)))
    in_specs = _halo_specs(SSD_INNER, COL_XS, blk, t) + _halo_specs(SSD_CONV_DIM - SSD_INNER, COL_BC, blk, t) + [
        pl.BlockSpec((TQ, 128), row(COL_DT)),
        pl.BlockSpec((TQ, SSD_STATE), tbl),
        pl.BlockSpec((TQ, SSD_STATE), tbl),
    ]
    args = [proj] * 7 + [cosf, sinf]
    if d == 1:
        in_specs += [pl.BlockSpec((TQ, SSD_INNER), row(COL_Z)), pl.BlockSpec((TQ, SSD_INNER), row(0))]
        args += [proj, y0]
    in_specs += [const((4, SSD_CONV_DIM)), const((1, SSD_CONV_DIM)), const((1, 128)), const((1, 128)),
                 const((128, SSD_INNER))]
    args += [conv_w, conv_b.reshape(1, -1), pad128(dt_bias), pad128(a_log), head_expand_matrix(d)]
    scratch = [pltpu.VMEM((TQ + 2 * HALO, SSD_CONV_DIM), f32), pltpu.VMEM((SSD_GROUPS, SSD_STATE, GW), f32)]
    if d == 1:
        in_specs += [const((1, SSD_INNER)), const((1, SSD_INNER))]
        args += [jnp.repeat(d_skip, SSD_HEADDIM).reshape(1, -1), norm_g.reshape(1, -1)]
        scratch.append(pltpu.VMEM((TQ, SSD_INNER), f32))
    return pl.pallas_call(
        functools.partial(_ssd_kernel, d, lat_blocks),
        grid=(bsz, lat_blocks + 1),
        in_specs=in_specs,
        out_specs=pl.BlockSpec((TQ, SSD_INNER), row(0)),
        out_shape=jax.ShapeDtypeStruct((t, SSD_INNER), f32 if d == 0 else bf16),
        scratch_shapes=scratch,
        compiler_params=_params(),
        name=f"ssd_pass{d}",
    )(*args)


def ssd_branch(proj, cosf, sinf, conv_w, conv_b, a_log, dt_bias, d_skip, norm_g, bsz, lat_blocks):
    y0 = ssd_pass(0, proj, None, cosf, sinf, conv_w, conv_b, a_log, dt_bias, d_skip, norm_g, bsz, lat_blocks)
    return ssd_pass(1, proj, y0, cosf, sinf, conv_w, conv_b, a_log, dt_bias, d_skip, norm_g, bsz, lat_blocks)


def _tile_rows(mod_rows, bsz):
    idx = np.concatenate([np.repeat(np.arange(bsz), SEQ // TM), np.full(bsz * CTX_LEN // TM, bsz)])
    return mod_rows[idx][:, None, :]


def _expert_choice_moe(v, w_router, w1, w3, w2, bsz, with_ctx):
    t, d = v.shape
    aff = router_affinity(v, w_router)[:, :N_EXPERTS]
    nl = bsz * SEQ

    def choose(a, length, row0):
        cap = CAPACITY_FACTOR * length // N_EXPERTS
        g, idx = lax.top_k(jnp.swapaxes(a.reshape(bsz, length, N_EXPERTS), 1, 2), cap)
        rows = idx + (row0 + jnp.arange(bsz) * length)[:, None, None]
        to_e = lambda x: jnp.swapaxes(x, 0, 1).reshape(N_EXPERTS, bsz * cap)
        return to_e(g), to_e(rows)

    g, rows = choose(aff[:nl], SEQ, 0)
    if with_ctx:
        gc, rc = choose(aff[nl:], CTX_LEN, nl)
        g = jnp.concatenate([g, gc], axis=1)
        rows = jnp.concatenate([rows, rc], axis=1)
    r = rows.shape[1]
    xg = jnp.take(v, rows.reshape(-1), axis=0).reshape(N_EXPERTS, r, d)
    ye = expert_ffn(xg, g[..., None], w1, w3, w2, r // 4)
    return jnp.zeros((t, d), f32).at[rows.reshape(-1)].add(ye.reshape(-1, d))


def kernel(x, c, ctx, c_ctx, w_ada, b_ada, norm_mix, norm_ffn, w_in, lru_conv_w, lru_conv_b, lru_w_r, lru_b_r,
           lru_w_i, lru_b_i, lru_lambda, ssd_conv_w, ssd_conv_b, ssd_a_log, ssd_dt_bias, ssd_d, ssd_norm, na_rpb,
           w_branch_lru, w_branch_ssd, w_branch_na, w_out, w_router, w1, w3, w2, norm_final):
    bsz = x.shape[0]
    assert x.shape[1:] == (SEQ, D_MODEL) and ctx.shape[1:] == (CTX_LEN, D_MODEL) and bsz * CTX_LEN == TM
    d = D_MODEL
    h = jnp.concatenate([x.reshape(bsz * SEQ, d), ctx.reshape(bsz * CTX_LEN, d)], axis=0)
    t = h.shape[0]

    cond = jnp.concatenate([c, c_ctx[None, :], jnp.zeros((8 - bsz - 1, d), f32)], axis=0)
    mod = ada_modulation(cond, w_ada, b_ada)
    cosf, sinf = rope_tables(SEQ)

    dt_end = 2 * LRU_WIDTH + SSD_INNER + SSD_CONV_DIM + 2 * SSD_HEADS
    assert dt_end + DT_PAD == W_QKV0 and w_in.shape[-1] == dt_end + 3 * NA_WIDTH + 3 * d
    w_proj = jnp.concatenate([w_in[..., :dt_end].astype(bf16), jnp.zeros((w_in.shape[0], d, DT_PAD), bf16),
                              w_in[..., dt_end:].astype(bf16)], axis=-1)
    qkv_blocks = 3 * NA_WIDTH // TN
    pa, pb, pc, wo = (w.astype(bf16) for w in (w_branch_lru, w_branch_ssd, w_branch_na, w_out))
    w1b, w3b, w2b = (w.astype(bf16) for w in (w1, w3, w2))
    na_tbl = na_bias_table(na_rpb)

    for l in range(DEPTH):
        sh1, sc1, g1, sh2, sc2, g2 = (_tile_rows(m, bsz) for m in jnp.split(mod[l], 6, axis=-1))
        u = norm_modulate(h, norm_mix[l], sc1, sh1, bf16)
        proj = matmul_bf16(u, w_proj[l], N_F32, lambda j: jnp.where(j < W_QKV0 // TN, j, j + qkv_blocks), f32)
        qkv = matmul_bf16(u, w_proj[l], 3 * NA_WIDTH, lambda j: j + W_QKV0 // TN, bf16)
        ya = lru_branch(proj, lru_conv_w[l], lru_conv_b[l], lru_w_r[l], lru_b_r[l], lru_w_i[l], lru_b_i[l],
                        lru_lambda[l], bsz, LAT_BLOCKS)
        yb = ssd_branch(proj, cosf, sinf, ssd_conv_w[l], ssd_conv_b[l], ssd_a_log[l], ssd_dt_bias[l], ssd_d[l],
                        ssd_norm[l], bsz, LAT_BLOCKS)
        yc = na_attention(qkv, na_tbl[l], bsz)
        y = branch_merge(ya, yb, yc, proj, pa[l], pb[l], pc[l], COL_G)
        h = residual_matmul(y, wo[l], h, g1)
        v = norm_modulate(h, norm_ffn[l], sc2, sh2, bf16)
        moe = _expert_choice_moe(v, w_router[l], w1b[l], w3b[l], w2b[l], bsz, with_ctx=l < DEPTH - 1)
        h = (h.reshape(-1, TM, d) + g2 * moe.reshape(-1, TM, d)).reshape(t, d)

    zeros = jnp.zeros((t // TM, 1, d), f32)
    out = norm_modulate(h, norm_final, zeros, zeros, f32)
    return out[:bsz * SEQ].reshape(bsz, SEQ, d)
```

```python
import functools

import jax
import jax.numpy as jnp
import numpy as np
from jax import lax
from jax.experimental import pallas as pl
from jax.experimental.pallas import tpu as pltpu

D_MODEL = 2048
SEQ = 4096
CTX_LEN = 256
DEPTH = 4
GRID_W = 64
EPS = 1e-6
ROPE_BASE = 10000.0
LRU_WIDTH = 1024
LRU_BLOCKS = 16
LRU_BLOCK = LRU_WIDTH // LRU_BLOCKS
LRU_C = 8.0
SSD_INNER = 1024
SSD_HEADDIM = 64
SSD_HEADS = SSD_INNER // SSD_HEADDIM
SSD_GROUPS = 4
SSD_STATE = 128
SSD_CHUNK = 128
SSD_CONV_DIM = SSD_INNER + 2 * SSD_GROUPS * SSD_STATE
NA_HEADS = 16
NA_HEADDIM = 64
NA_WIDTH = NA_HEADS * NA_HEADDIM
NA_WIN_R = 8
NA_WIN_C = 16
N_EXPERTS = 16
EXPERT_FF = 1024
CAPACITY_FACTOR = 2

V7X_VMEM_LIMIT = 56 * 1024 * 1024
TM = 1024
TQ = 256
HALO = 8
LAT_BLOCKS = SEQ // TQ
NEG = -1e30
HI = lax.Precision.HIGHEST
bf16 = jnp.bfloat16
f32 = jnp.float32

TN = 512
DT_PAD = 480
COL_AX, COL_AG, COL_Z, COL_XS, COL_BC = 0, 1, 2, 3, 4
COL_DT = 40
W_QKV0 = 5632
COL_G = 5632
N_F32 = COL_G + 3 * D_MODEL
GW = SSD_INNER // SSD_GROUPS

NA_GROUP_ROWS = 2
NA_BAND_ROWS = 10
NA_NCFG = 5


def _params():
    return pltpu.CompilerParams(vmem_limit_bytes=V7X_VMEM_LIMIT)


def _ada_kernel(x_ref, w_ref, b_ref, o_ref):
    c = x_ref[...]
    x = (c * jax.nn.sigmoid(c)).astype(bf16)
    o_ref[...] = jnp.dot(x, w_ref[...].astype(bf16), preferred_element_type=f32) + b_ref[...]


def ada_modulation(cond, w_ada, b_ada):
    depth, d, n = w_ada.shape
    tn = 1024
    return pl.pallas_call(
        _ada_kernel,
        grid=(depth, n // tn),
        in_specs=[pl.BlockSpec((8, d), lambda l, j: (0, 0)),
                  pl.BlockSpec((None, d, tn), lambda l, j: (l, 0, j)),
                  pl.BlockSpec((None, 1, tn), lambda l, j: (l, 0, j))],
        out_specs=pl.BlockSpec((None, 8, tn), lambda l, j: (l, 0, j)),
        out_shape=jax.ShapeDtypeStruct((depth, 8, n), f32),
        compiler_params=_params(),
        name="ada_modulation",
    )(cond, w_ada, b_ada.reshape(depth, 1, n))


def _norm_mod_kernel(h_ref, g_ref, sc_ref, sh_ref, o_ref):
    x = h_ref[...]
    y = x * lax.rsqrt(jnp.mean(x * x, axis=-1, keepdims=True) + EPS)
    o_ref[...] = ((y * g_ref[...]) * (1.0 + sc_ref[...]) + sh_ref[...]).astype(o_ref.dtype)


def norm_modulate(h, g, sc_t, sh_t, out_dtype):
    t, d = h.shape
    tm = min(512, TM)
    per = TM // tm
    return pl.pallas_call(
        _norm_mod_kernel,
        grid=(t // tm,),
        in_specs=[pl.BlockSpec((tm, d), lambda i: (i, 0)),
                  pl.BlockSpec((1, d), lambda i: (0, 0)),
                  pl.BlockSpec((None, 1, d), lambda i: (i // per, 0, 0)),
                  pl.BlockSpec((None, 1, d), lambda i: (i // per, 0, 0))],
        out_specs=pl.BlockSpec((tm, d), lambda i: (i, 0)),
        out_shape=jax.ShapeDtypeStruct((t, d), out_dtype),
        compiler_params=_params(),
        name="norm_modulate",
    )(h, g.reshape(1, d), sc_t, sh_t)


def _mm_kernel(x_ref, w_ref, o_ref):
    o_ref[...] = jnp.dot(x_ref[...], w_ref[...], preferred_element_type=f32).astype(o_ref.dtype)


def matmul_bf16(x, w, n_out, w_block, out_dtype):
    m, k = x.shape
    return pl.pallas_call(
        _mm_kernel,
        grid=(m // TM, n_out // TN),
        in_specs=[pl.BlockSpec((TM, k), lambda i, j: (i, 0)),
                  pl.BlockSpec((k, TN), lambda i, j: (0, w_block(j)))],
        out_specs=pl.BlockSpec((TM, TN), lambda i, j: (i, j)),
        out_shape=jax.ShapeDtypeStruct((m, n_out), out_dtype),
        compiler_params=_params(),
        name="matmul_bf16",
    )(x, w)


def _merge_kernel(ya_ref, yb_ref, yc_ref, ga_ref, gb_ref, gc_ref, pa_ref, pb_ref, pc_ref, o_ref):
    acc = jax.nn.sigmoid(ga_ref[...]) * jnp.dot(ya_ref[...], pa_ref[...], preferred_element_type=f32)
    acc = acc + jax.nn.sigmoid(gb_ref[...]) * jnp.dot(yb_ref[...], pb_ref[...], preferred_element_type=f32)
    acc = acc + jax.nn.sigmoid(gc_ref[...]) * jnp.dot(yc_ref[...], pc_ref[...], preferred_element_type=f32)
    o_ref[...] = acc.astype(o_ref.dtype)


def branch_merge(ya, yb, yc, proj, pa, pb, pc, g_col0):
    t, k = ya.shape
    n = pa.shape[1]
    tn = 512
    gb0 = g_col0 // tn
    nj = n // tn
    xs = pl.BlockSpec((TM, k), lambda i, j: (i, 0))
    ws = pl.BlockSpec((k, tn), lambda i, j: (0, j))
    gs = lambda q: pl.BlockSpec((TM, tn), lambda i, j: (i, gb0 + q * nj + j))
    return pl.pallas_call(
        _merge_kernel,
        grid=(t // TM, nj),
        in_specs=[xs, xs, xs, gs(0), gs(1), gs(2), ws, ws, ws],
        out_specs=pl.BlockSpec((TM, tn), lambda i, j: (i, j)),
        out_shape=jax.ShapeDtypeStruct((t, n), bf16),
        compiler_params=_params(),
        name="branch_merge",
    )(ya, yb, yc, proj, proj, proj, pa, pb, pc)


def _resid_mm_kernel(y_ref, w_ref, h_ref, g_ref, o_ref):
    o_ref[...] = h_ref[...] + g_ref[...] * jnp.dot(y_ref[...], w_ref[...], preferred_element_type=f32)


def residual_matmul(y, w, h, gate_t):
    m, k = y.shape
    n = w.shape[1]
    tn = 512
    return pl.pallas_call(
        _resid_mm_kernel,
        grid=(m // TM, n // tn),
        in_specs=[pl.BlockSpec((TM, k), lambda i, j: (i, 0)),
                  pl.BlockSpec((k, tn), lambda i, j: (0, j)),
                  pl.BlockSpec((TM, tn), lambda i, j: (i, j)),
                  pl.BlockSpec((None, 1, tn), lambda i, j: (i, 0, j))],
        out_specs=pl.BlockSpec((TM, tn), lambda i, j: (i, j)),
        out_shape=jax.ShapeDtypeStruct((m, n), f32),
        compiler_params=_params(),
        name="residual_matmul",
    )(y, w, h, gate_t)


def _router_kernel(v_ref, w_ref, o_ref):
    s = jnp.dot(v_ref[...], w_ref[...], preferred_element_type=f32)
    lane = lax.broadcasted_iota(jnp.int32, s.shape, 1)
    s = jnp.where(lane < N_EXPERTS, s, NEG)
    e = jnp.exp(s - s.max(axis=-1, keepdims=True))
    o_ref[...] = e / e.sum(axis=-1, keepdims=True)


def router_affinity(v, w_router):
    t, d = v.shape
    w = jnp.pad(w_router, ((0, 0), (0, 128 - N_EXPERTS))).astype(bf16)
    return pl.pallas_call(
        _router_kernel,
        grid=(t // TM,),
        in_specs=[pl.BlockSpec((TM, d), lambda i: (i, 0)), pl.BlockSpec((d, 128), lambda i: (0, 0))],
        out_specs=pl.BlockSpec((TM, 128), lambda i: (i, 0)),
        out_shape=jax.ShapeDtypeStruct((t, 128), f32),
        compiler_params=_params(),
        name="router_affinity",
    )(v, w)


def _expert_hidden_kernel(x_ref, w1_ref, w3_ref, o_ref):
    x = x_ref[...]
    a = jnp.dot(x, w1_ref[...], preferred_element_type=f32)
    b = jnp.dot(x, w3_ref[...], preferred_element_type=f32)
    o_ref[...] = ((a * jax.nn.sigmoid(a)) * b).astype(o_ref.dtype)


def expert_hidden(xg, w1, w3, tm):
    e, r, d = xg.shape
    f = w1.shape[-1]
    return pl.pallas_call(
        _expert_hidden_kernel,
        grid=(e, r // tm),
        in_specs=[pl.BlockSpec((None, tm, d), lambda k, i: (k, i, 0)),
                  pl.BlockSpec((None, d, f), lambda k, i: (k, 0, 0)),
                  pl.BlockSpec((None, d, f), lambda k, i: (k, 0, 0))],
        out_specs=pl.BlockSpec((None, tm, f), lambda k, i: (k, i, 0)),
        out_shape=jax.ShapeDtypeStruct((e, r, f), bf16),
        compiler_params=_params(),
        name="expert_hidden",
    )(xg, w1, w3)


SCATTER_UNROLL = 8


def _combine_kernel(nb, slots, idx_ref, hdn_ref, w2_ref, g_ref, h_ref, g2_ref, *rest):
    o_ref, ye_scr = rest[-2], rest[-1]
    b, e = pl.program_id(0), pl.program_id(2)

    @pl.when(e == 0)
    def _():
        o_ref[...] = jnp.zeros_like(o_ref)

    ye_scr[...] = jnp.dot(hdn_ref[...], w2_ref[...], preferred_element_type=f32) * g_ref[...]
    base0 = (e * nb + b) * slots

    def body(i, carry):
        s0 = pl.multiple_of(i * SCATTER_UNROLL, SCATTER_UNROLL)
        rows = [idx_ref[base0 + s0 + k] for k in range(SCATTER_UNROLL)]
        vals = [o_ref[pl.ds(rows[k], 1), :] + ye_scr[pl.ds(s0 + k, 1), :] for k in range(SCATTER_UNROLL)]
        for k in range(SCATTER_UNROLL):
            o_ref[pl.ds(rows[k], 1), :] = vals[k]
        return carry

    lax.fori_loop(0, slots // SCATTER_UNROLL, body, 0)

    @pl.when(e == pl.num_programs(2) - 1)
    def _():
        o_ref[...] = h_ref[...] + g2_ref[...] * o_ref[...]


def moe_combine(idx, hdn, slot_blk0, slots, gates, w2, h, g2_rows, g2_row0, row_blk0, rows_blk, dq, nb, prev=None):
    e, _, f = hdn.shape
    t, d = h.shape
    in_specs = [
        pl.BlockSpec((None, slots, f), lambda b, q, k, idx: (k, slot_blk0 + b, 0)),
        pl.BlockSpec((None, f, dq), lambda b, q, k, idx: (k, 0, q)),
        pl.BlockSpec((None, slots, 1), lambda b, q, k, idx: (k, slot_blk0 + b, 0)),
        pl.BlockSpec((rows_blk, dq), lambda b, q, k, idx: (row_blk0 + b, q)),
        pl.BlockSpec((None, 1, dq), lambda b, q, k, idx: (g2_row0 + b, 0, q)),
    ]
    args = [idx, hdn, w2, gates, h, g2_rows]
    aliases = {}
    if prev is not None:
        in_specs.append(pl.BlockSpec(memory_space=pl.ANY))
        args.append(prev)
        aliases = {len(args) - 1: 0}
    return pl.pallas_call(
        functools.partial(_combine_kernel, nb, slots),
        grid_spec=pltpu.PrefetchScalarGridSpec(
            num_scalar_prefetch=1,
            grid=(nb, d // dq, e),
            in_specs=in_specs,
            out_specs=pl.BlockSpec((rows_blk, dq), lambda b, q, k, idx: (row_blk0 + b, q)),
            scratch_shapes=[pltpu.VMEM((slots, dq), f32)],
        ),
        out_shape=jax.ShapeDtypeStruct((t, d), f32),
        input_output_aliases=aliases,
        compiler_params=_params(),
        name="moe_combine",
    )(*args)


def na_bias_table(rpb):
    rows = SEQ // GRID_W
    n_dr, n_dc = 2 * NA_WIN_R - 1, 2 * NA_WIN_C - 1
    qc, kc = np.arange(GRID_W)[:, None], np.arange(GRID_W)[None, :]
    cs = np.clip(qc - NA_WIN_C // 2, 0, GRID_W - NA_WIN_C)
    col_ok = (kc >= cs) & (kc < cs + NA_WIN_C)
    pick = (np.arange(n_dc)[:, None, None] == (kc - qc + NA_WIN_C - 1)[None]) & col_ok[None]
    blocks = jnp.einsum('...rd,dqk->...rqk', rpb, jnp.asarray(pick, f32), precision=HI)
    blocks = jnp.where(col_ok, blocks, NEG)
    blocks = jnp.concatenate([blocks, jnp.full(blocks.shape[:-3] + (1, GRID_W, GRID_W), NEG, f32)], axis=-3)
    which = np.full((NA_NCFG, NA_GROUP_ROWS, NA_BAND_ROWS), n_dr, np.int32)
    for c, r0 in enumerate((0, 2, 4, 60, 62)):
        bs = int(np.clip(r0 - NA_WIN_R // 2, 0, rows - NA_BAND_ROWS))
        for qr in range(NA_GROUP_ROWS):
            r = r0 + qr
            rs = int(np.clip(r - NA_WIN_R // 2, 0, rows - NA_WIN_R))
            for kr in range(NA_BAND_ROWS):
                if rs <= bs + kr < rs + NA_WIN_R:
                    which[c, qr, kr] = bs + kr - r + NA_WIN_R - 1
    tbl = jnp.take(blocks, which.reshape(-1), axis=-3)
    lead = tbl.shape[:-3]
    tbl = tbl.reshape(lead + (NA_NCFG, NA_GROUP_ROWS, NA_BAND_ROWS, GRID_W, GRID_W))
    tbl = jnp.swapaxes(tbl, -3, -2)
    return tbl.reshape(lead + (NA_NCFG, NA_GROUP_ROWS * GRID_W, NA_BAND_ROWS * GRID_W))


def _dot_nt(a, b):
    return lax.dot_general(a, b, (((1,), (1,)), ((), ())), preferred_element_type=f32)


def _softmax_av(parts):
    m = parts[0][0].max(axis=-1, keepdims=True)
    for s, _ in parts[1:]:
        m = jnp.maximum(m, s.max(axis=-1, keepdims=True))
    l = None
    o = None
    for s, v in parts:
        p = jnp.exp(s - m)
        ls = p.sum(axis=-1, keepdims=True)
        os_ = jnp.dot(p.astype(bf16), v, preferred_element_type=f32)
        l = ls if l is None else l + ls
        o = os_ if o is None else o + os_
    return o / l


def _na_kernel(q_ref, k_ref, v_ref, kc_ref, vc_ref, tbl_ref, o_ref):
    j = pl.program_id(2)
    scale = NA_HEADDIM ** -0.5
    lane = lax.broadcasted_iota(jnp.int32, (1, 2 * NA_HEADDIM), 1)
    first = lane < NA_HEADDIM
    kc = kc_ref[...]
    vc = vc_ref[...]
    gq = NA_GROUP_ROWS * GRID_W

    def head_q(q, hh):
        keep = first if hh == 0 else jnp.logical_not(first)
        return jnp.where(keep, q, jnp.zeros_like(q))

    @pl.when(j < LAT_BLOCKS)
    def _():
        for g in range(TQ // gq):
            r0 = j * (TQ // GRID_W) + NA_GROUP_ROWS * g
            bs = jnp.clip(r0 - NA_WIN_R // 2, 0, SEQ // GRID_W - NA_BAND_ROWS)
            cfg = jnp.where(r0 == 0, 0, jnp.where(r0 == 2, 1, jnp.where(r0 == 60, 3, jnp.where(r0 == 62, 4, 2))))
            start = pl.multiple_of(bs * GRID_W, GRID_W)
            kb = k_ref[pl.ds(start, NA_BAND_ROWS * GRID_W), :]
            vb = v_ref[pl.ds(start, NA_BAND_ROWS * GRID_W), :]
            q = q_ref[pl.ds(g * gq, gq), :]
            outs = []
            for hh in range(2):
                qm = head_q(q, hh)
                s = _dot_nt(qm, kb) * scale + tbl_ref[hh, cfg]
                sc = _dot_nt(qm, kc) * scale
                outs.append(_softmax_av([(s, vb), (sc, vc)]))
            o_ref[pl.ds(g * gq, gq), :] = jnp.where(first, outs[0], outs[1]).astype(o_ref.dtype)

    @pl.when(j == LAT_BLOCKS)
    def _():
        q = q_ref[...]
        outs = []
        for hh in range(2):
            sc = _dot_nt(head_q(q, hh), kc) * scale
            outs.append(_softmax_av([(sc, vc)]))
        o_ref[...] = jnp.where(first, outs[0], outs[1]).astype(o_ref.dtype)


def na_attention(qkv, tbl, bsz):
    t = qkv.shape[0]
    npair = NA_HEADS // 2
    ctx0 = bsz * LAT_BLOCKS

    def qrow(b, p, j):
        return jnp.where(j < LAT_BLOCKS, b * LAT_BLOCKS + j, ctx0 + b)

    return pl.pallas_call(
        _na_kernel,
        grid=(bsz, npair, LAT_BLOCKS + 1),
        in_specs=[
            pl.BlockSpec((TQ, 128), lambda b, p, j: (qrow(b, p, j), p)),
            pl.BlockSpec((SEQ, 128), lambda b, p, j: (b, npair + p)),
            pl.BlockSpec((SEQ, 128), lambda b, p, j: (b, 2 * npair + p)),
            pl.BlockSpec((CTX_LEN, 128), lambda b, p, j: (ctx0 + b, npair + p)),
            pl.BlockSpec((CTX_LEN, 128), lambda b, p, j: (ctx0 + b, 2 * npair + p)),
            pl.BlockSpec((2, NA_NCFG, NA_GROUP_ROWS * GRID_W, NA_BAND_ROWS * GRID_W), lambda b, p, j: (p, 0, 0, 0)),
        ],
        out_specs=pl.BlockSpec((TQ, 128), lambda b, p, j: (qrow(b, p, j), p)),
        out_shape=jax.ShapeDtypeStruct((t, NA_WIDTH), bf16),
        compiler_params=_params(),
        name="na_attention",
    )(qkv, qkv, qkv, qkv, qkv, tbl)


def _seq_block(d, bsz, lat_blocks):
    ctx0 = bsz * lat_blocks

    def jj(s):
        return (s - 1) if d == 0 else (lat_blocks - s)

    def blk(b, s):
        return jnp.where(s == 0, ctx0 + b, b * lat_blocks + jj(s))

    return jj, blk


def _halo_specs(width, col, blk, t):
    nb8 = TQ // HALO
    return [
        pl.BlockSpec((TQ, width), lambda b, s: (blk(b, s), col)),
        pl.BlockSpec((HALO, width), lambda b, s: (jnp.maximum(blk(b, s) * nb8 - 1, 0), col)),
        pl.BlockSpec((HALO, width), lambda b, s: (jnp.minimum(blk(b, s) * nb8 + nb8, t // HALO - 1), col)),
    ]


def _centred_conv(xe, parts, cw_ref, cb_ref, has_prev, has_next):
    c0 = 0
    for x_ref, xp_ref, xn_ref in parts:
        cols = slice(c0, c0 + x_ref.shape[1])
        xe[0:HALO, cols] = jnp.where(has_prev, xp_ref[...], 0.0)
        xe[HALO:HALO + TQ, cols] = x_ref[...]
        xe[HALO + TQ:2 * HALO + TQ, cols] = jnp.where(has_next, xn_ref[...], 0.0)
        c0 += x_ref.shape[1]
    out = cb_ref[...] + cw_ref[0:1, :] * xe[HALO - 2:HALO - 2 + TQ, :]
    for k in range(1, 4):
        out = out + cw_ref[k:k + 1, :] * xe[HALO - 2 + k:HALO - 2 + k + TQ, :]
    return out


def lru_gate_weights(w_r, w_i):
    def pair(w):
        w = w.reshape(2, LRU_BLOCKS // 2, 2, LRU_BLOCK, LRU_BLOCK)
        z = jnp.zeros_like(w[:, :, 0])
        top = jnp.concatenate([w[:, :, 0], z], axis=-1)
        bot = jnp.concatenate([z, w[:, :, 1]], axis=-1)
        return jnp.concatenate([top, bot], axis=-2)
    return jnp.concatenate([pair(w_r), pair(w_i)], axis=-1).astype(bf16)


def _gelu_tanh(x):
    return 0.5 * x * (1.0 + jnp.tanh(float(np.sqrt(2.0 / np.pi)) * (x + 0.044715 * (x * x * x))))


def _lru_kernel(d, lat_blocks, *refs):
    if d == 0:
        (x_ref, xp_ref, xn_ref, cw_ref, cb_ref, wg_ref, br_ref, bi_ref, lam_ref, o_ref,
         xe, a_scr, u_scr, h_scr) = refs
    else:
        (x_ref, xp_ref, xn_ref, ag_ref, hf_ref, cw_ref, cb_ref, wg_ref, br_ref, bi_ref, lam_ref, o_ref,
         xe, a_scr, u_scr, h_scr, hs_scr) = refs
    s = pl.program_id(1)
    j = (s - 1) if d == 0 else (lat_blocks - s)
    has_prev = jnp.logical_and(s > 0, j > 0)
    has_next = jnp.logical_and(s > 0, j < lat_blocks - 1)

    @pl.when(s == 0)
    def _():
        h_scr[...] = jnp.zeros_like(h_scr)

    xc = _centred_conv(xe, [(x_ref, xp_ref, xn_ref)], cw_ref, cb_ref, has_prev, has_next)
    xcb = xc.astype(bf16)
    sp = jax.nn.softplus(-lam_ref[...])
    for p in range(LRU_WIDTH // 128):
        sl = slice(128 * p, 128 * (p + 1))
        g = jnp.dot(xcb[:, sl], wg_ref[p], preferred_element_type=f32)
        r = jax.nn.sigmoid(g[:, :128] + br_ref[:, sl])
        i = jax.nn.sigmoid(g[:, 128:] + bi_ref[:, sl])
        log_a = -LRU_C * r * sp[:, sl]
        a = jnp.exp(log_a)
        a_scr[:, sl] = a
        u_scr[:, sl] = jnp.sqrt(-jnp.tanh(log_a) * (a * a + 1.0)) * (i * xc[:, sl])

    out = o_ref if d == 0 else hs_scr
    ngrp = TQ // 8

    def body(gi, h):
        base = pl.multiple_of((gi if d == 0 else ngrp - 1 - gi) * 8, 8)
        for t in (range(8) if d == 0 else range(7, -1, -1)):
            h = a_scr[pl.ds(base + t, 1), :] * h + u_scr[pl.ds(base + t, 1), :]
            out[pl.ds(base + t, 1), :] = h
        return h

    h = lax.fori_loop(0, ngrp, body, h_scr[0:1, :])
    h_scr[0:1, :] = h
    if d == 1:
        y = (hf_ref[...] + hs_scr[...]) * _gelu_tanh(ag_ref[...])
        o_ref[...] = y.astype(o_ref.dtype)


def lru_pass(d, proj, hf, conv_w, conv_b, wg, b_r, b_i, lam, bsz, lat_blocks):
    t = proj.shape[0]
    _, blk = _seq_block(d, bsz, lat_blocks)
    row = lambda c: (lambda b, s: (blk(b, s), c))
    in_specs = _halo_specs(LRU_WIDTH, COL_AX, blk, t)
    args = [proj, proj, proj]
    if d == 1:
        in_specs += [pl.BlockSpec((TQ, LRU_WIDTH), row(COL_AG)), pl.BlockSpec((TQ, LRU_WIDTH), row(0))]
        args += [proj, hf]
    const = lambda shape: pl.BlockSpec(shape, lambda b, s: (0,) * len(shape))
    in_specs += [const((4, LRU_WIDTH)), const((1, LRU_WIDTH)), const((LRU_WIDTH // 128, 128, 256)),
                 const((1, LRU_WIDTH)), const((1, LRU_WIDTH)), const((1, LRU_WIDTH))]
    args += [conv_w, conv_b.reshape(1, -1), wg[d], b_r[d].reshape(1, -1), b_i[d].reshape(1, -1), lam[d].reshape(1, -1)]
    scratch = [pltpu.VMEM((TQ + 2 * HALO, LRU_WIDTH), f32), pltpu.VMEM((TQ, LRU_WIDTH), f32),
               pltpu.VMEM((TQ, LRU_WIDTH), f32), pltpu.VMEM((8, LRU_WIDTH), f32)]
    if d == 1:
        scratch.append(pltpu.VMEM((TQ, LRU_WIDTH), f32))
    return pl.pallas_call(
        functools.partial(_lru_kernel, d, lat_blocks),
        grid=(bsz, lat_blocks + 1),
        in_specs=in_specs,
        out_specs=pl.BlockSpec((TQ, LRU_WIDTH), row(0)),
        out_shape=jax.ShapeDtypeStruct((t, LRU_WIDTH), f32 if d == 0 else bf16),
        scratch_shapes=scratch,
        compiler_params=_params(),
        name=f"lru_pass{d}",
    )(*args)


def lru_branch(proj, conv_w, conv_b, w_r, b_r, w_i, b_i, lam, bsz, lat_blocks):
    wg = lru_gate_weights(w_r, w_i)
    hf = lru_pass(0, proj, None, conv_w, conv_b, wg, b_r, b_i, lam, bsz, lat_blocks)
    return lru_pass(1, proj, hf, conv_w, conv_b, wg, b_r, b_i, lam, bsz, lat_blocks)


def rope_tables(seq):
    pos = jnp.arange(seq)
    row = (pos // GRID_W).astype(f32)
    col = (pos % GRID_W).astype(f32)
    n_freq = SSD_STATE // 4
    freqs = ROPE_BASE ** (-jnp.arange(n_freq, dtype=f32) / n_freq)
    ang = jnp.concatenate([row[:, None] * freqs, col[:, None] * freqs], axis=-1)
    cos, sin = jnp.cos(ang), jnp.sin(ang)
    cosf = jnp.concatenate([cos, cos], axis=-1)
    sinf = jnp.concatenate([-sin, sin], axis=-1)
    cosf = jnp.concatenate([cosf, jnp.ones((TQ, SSD_STATE), f32)], axis=0)
    sinf = jnp.concatenate([sinf, jnp.zeros((TQ, SSD_STATE), f32)], axis=0)
    return cosf, sinf


def head_expand_matrix(d):
    e = np.zeros((128, SSD_INNER), np.float32)
    for h in range(SSD_HEADS):
        e[SSD_HEADS * d + h, h * SSD_HEADDIM:(h + 1) * SSD_HEADDIM] = 1.0
    return jnp.asarray(e)


def _ssd_kernel(d, lat_blocks, *refs):
    xs_refs, bc_refs, refs = refs[0:3], refs[3:6], refs[6:]
    if d == 0:
        (dt_ref, cos_ref, sin_ref, cw_ref, cb_ref, dtb_ref, alog_ref, e_ref,
         o_ref, xe, h_scr) = refs
    else:
        (dt_ref, cos_ref, sin_ref, z_ref, y0_ref, cw_ref, cb_ref, dtb_ref, alog_ref, e_ref,
         dsk_ref, ng_ref, o_ref, xe, h_scr, y_scr) = refs
    s = pl.program_id(1)
    j = (s - 1) if d == 0 else (lat_blocks - s)
    has_prev = jnp.logical_and(s > 0, j > 0)
    has_next = jnp.logical_and(s > 0, j < lat_blocks - 1)
    q = SSD_CHUNK

    @pl.when(s == 0)
    def _():
        h_scr[...] = jnp.zeros_like(h_scr)

    xbc = _centred_conv(xe, [xs_refs, bc_refs], cw_ref, cb_ref, has_prev, has_next)
    xbc = xbc * jax.nn.sigmoid(xbc)
    xs = xbc[:, :SSD_INNER]
    cosf, sinf = cos_ref[...], sin_ref[...]

    def rope(g, off):
        v = xbc[:, off + g * SSD_STATE: off + (g + 1) * SSD_STATE]
        return (v * cosf + pltpu.roll(v, SSD_STATE // 2, 1) * sinf).astype(bf16)

    bm = [rope(g, SSD_INNER) for g in range(SSD_GROUPS)]
    cm = [rope(g, SSD_INNER + SSD_GROUPS * SSD_STATE) for g in range(SSD_GROUPS)]
    dt = jax.nn.softplus(dt_ref[...] + dtb_ref[...])
    delta = dt * (-jnp.exp(alog_ref[...]))
    ri = lax.broadcasted_iota(jnp.int32, (q, q), 0)
    ci = lax.broadcasted_iota(jnp.int32, (q, q), 1)
    keep = (ci <= ri) if d == 0 else (ci >= ri)
    tri = jnp.where(keep, 1.0, 0.0).astype(f32)
    lane = lax.broadcasted_iota(jnp.int32, (1, 2 * SSD_HEADDIM), 1)
    halves = (lane < SSD_HEADDIM, lane >= SSD_HEADDIM)
    e = e_ref[...]
    last = q - 1 if d == 0 else 0
    out = o_ref if d == 0 else y_scr

    for c in (range(TQ // q) if d == 0 else range(TQ // q - 1, -1, -1)):
        rows = slice(c * q, (c + 1) * q)
        at = jnp.dot(tri, delta[rows], precision=HI, preferred_element_type=f32)
        at_exp = jnp.dot(at, e, precision=HI, preferred_element_type=f32)
        dt_exp = jnp.dot(dt[rows], e, precision=HI, preferred_element_type=f32)
        tot_exp = at_exp[last:last + 1, :]
        xdt = xs[rows] * dt_exp
        xd = (xdt * jnp.exp(tot_exp - at_exp)).astype(bf16)
        eat = jnp.exp(at_exp)
        cdec = jnp.exp(tot_exp)
        at_row = at.T
        ys = []
        for g in range(SSD_GROUPS):
            bg, cg = bm[g][rows], cm[g][rows]
            cb = _dot_nt(cg, bg)
            ht = h_scr[g]
            yoff = jnp.dot(cg, ht.astype(bf16), preferred_element_type=f32) * eat[:, g * GW:(g + 1) * GW]
            for pp in range(2):
                xpair = xdt[:, g * GW + pp * 128: g * GW + (pp + 1) * 128]
                acc = yoff[:, pp * 128:(pp + 1) * 128]
                for hh in range(2):
                    li = SSD_HEADS * d + 4 * g + 2 * pp + hh
                    seg = at[:, li:li + 1] - at_row[li:li + 1, :]
                    m = (cb * jnp.exp(jnp.where(keep, seg, NEG))).astype(bf16)
                    xm = jnp.where(halves[hh], xpair, 0.0).astype(bf16)
                    acc = acc + jnp.dot(m, xm, preferred_element_type=f32)
                ys.append(acc)
            upd = lax.dot_general(bg, xd[:, g * GW:(g + 1) * GW], (((0,), (0,)), ((), ())),
                                  preferred_element_type=f32)
            h_scr[g] = cdec[:, g * GW:(g + 1) * GW] * ht + upd
        out[rows, :] = jnp.concatenate(ys, axis=-1)

    if d == 1:
        y = y0_ref[...] + y_scr[...] + dsk_ref[...] * xs
        z = z_ref[...]
        y = y * (z * jax.nn.sigmoid(z))
        y = y * lax.rsqrt(jnp.mean(y * y, axis=-1, keepdims=True) + EPS)
        o_ref[...] = (y * ng_ref[...]).astype(o_ref.dtype)


def ssd_pass(d, proj, y0, cosf, sinf, conv_w, conv_b, a_log, dt_bias, d_skip, norm_g, bsz, lat_blocks):
    t = proj.shape[0]
    jj, blk = _seq_block(d, bsz, lat_blocks)
    row = lambda c: (lambda b, s: (blk(b, s), c))
    tbl = lambda b, s: (jnp.where(s == 0, lat_blocks, jj(s)), 0)
    const = lambda shape: pl.BlockSpec(shape, lambda b, s: (0,) * len(shape))
    pad128 = lambda v: jnp.pad(v.reshape(1, -1).astype(f32), ((0, 0), (0, 128 - v.size)))
    in_specs = _halo_specs(SSD_INNER, COL_XS, blk, t) + _halo_specs(SSD_CONV_DIM - SSD_INNER, COL_BC, blk, t) + [
        pl.BlockSpec((TQ, 128), row(COL_DT)),
        pl.BlockSpec((TQ, SSD_STATE), tbl),
        pl.BlockSpec((TQ, SSD_STATE), tbl),
    ]
    args = [proj] * 7 + [cosf, sinf]
    if d == 1:
        in_specs += [pl.BlockSpec((TQ, SSD_INNER), row(COL_Z)), pl.BlockSpec((TQ, SSD_INNER), row(0))]
        args += [proj, y0]
    in_specs += [const((4, SSD_CONV_DIM)), const((1, SSD_CONV_DIM)), const((1, 128)), const((1, 128)),
                 const((128, SSD_INNER))]
    args += [conv_w, conv_b.reshape(1, -1), pad128(dt_bias), pad128(a_log), head_expand_matrix(d)]
    scratch = [pltpu.VMEM((TQ + 2 * HALO, SSD_CONV_DIM), f32), pltpu.VMEM((SSD_GROUPS, SSD_STATE, GW), f32)]
    if d == 1:
        in_specs += [const((1, SSD_INNER)), const((1, SSD_INNER))]
        args += [jnp.repeat(d_skip, SSD_HEADDIM).reshape(1, -1), norm_g.reshape(1, -1)]
        scratch.append(pltpu.VMEM((TQ, SSD_INNER), f32))
    return pl.pallas_call(
        functools.partial(_ssd_kernel, d, lat_blocks),
        grid=(bsz, lat_blocks + 1),
        in_specs=in_specs,
        out_specs=pl.BlockSpec((TQ, SSD_INNER), row(0)),
        out_shape=jax.ShapeDtypeStruct((t, SSD_INNER), f32 if d == 0 else bf16),
        scratch_shapes=scratch,
        compiler_params=_params(),
        name=f"ssd_pass{d}",
    )(*args)


def ssd_branch(proj, cosf, sinf, conv_w, conv_b, a_log, dt_bias, d_skip, norm_g, bsz, lat_blocks):
    y0 = ssd_pass(0, proj, None, cosf, sinf, conv_w, conv_b, a_log, dt_bias, d_skip, norm_g, bsz, lat_blocks)
    return ssd_pass(1, proj, y0, cosf, sinf, conv_w, conv_b, a_log, dt_bias, d_skip, norm_g, bsz, lat_blocks)


def _tile_rows(mod_rows, bsz):
    idx = np.concatenate([np.repeat(np.arange(bsz), SEQ // TM), np.full(bsz * CTX_LEN // TM, bsz)])
    return mod_rows[idx][:, None, :]


def _expert_choice_moe(h, v, w_router, w1, w3, w2, g2_rows, bsz):
    t, d = v.shape
    aff = router_affinity(v, w_router)[:, :N_EXPERTS]
    nl = bsz * SEQ

    def choose(a, length):
        cap = CAPACITY_FACTOR * length // N_EXPERTS
        g, idx = lax.top_k(jnp.swapaxes(a.reshape(bsz, length, N_EXPERTS), 1, 2), cap)
        return jnp.swapaxes(g, 0, 1), jnp.swapaxes(idx, 0, 1), cap

    g_l, i_l, cap_l = choose(aff[:nl], SEQ)
    g_c, i_c, cap_c = choose(aff[nl:], CTX_LEN)
    boff = jnp.arange(bsz)[None, :, None]
    ctx_rows = (i_c + boff * CTX_LEN).reshape(N_EXPERTS, -1)
    rows = jnp.concatenate([(i_l + boff * SEQ).reshape(N_EXPERTS, -1), ctx_rows + nl], axis=1)
    gates = jnp.concatenate([g_l.reshape(N_EXPERTS, -1), g_c.reshape(N_EXPERTS, -1)], axis=1)[..., None]
    r = rows.shape[1]
    xg = jnp.take(v, rows.reshape(-1), axis=0).reshape(N_EXPERTS, r, d)
    hdn = expert_hidden(xg, w1, w3, r // 4)
    ctx_slots, ctx_tile = bsz * cap_c, bsz * CTX_LEN
    assert (bsz * cap_l) % ctx_slots == 0 and nl % ctx_tile == 0
    out = moe_combine(i_l.reshape(-1), hdn, 0, cap_l, gates, w2, h, g2_rows, 0, 0, SEQ, TN, bsz)
    return moe_combine(ctx_rows.reshape(-1), hdn, bsz * cap_l // ctx_slots, ctx_slots, gates, w2, h, g2_rows, bsz,
                       nl // ctx_tile, ctx_tile, d, 1, prev=out)


def kernel(x, c, ctx, c_ctx, w_ada, b_ada, norm_mix, norm_ffn, w_in, lru_conv_w, lru_conv_b, lru_w_r, lru_b_r,
           lru_w_i, lru_b_i, lru_lambda, ssd_conv_w, ssd_conv_b, ssd_a_log, ssd_dt_bias, ssd_d, ssd_norm, na_rpb,
           w_branch_lru, w_branch_ssd, w_branch_na, w_out, w_router, w1, w3, w2, norm_final):
    bsz = x.shape[0]
    assert x.shape[1:] == (SEQ, D_MODEL) and ctx.shape[1:] == (CTX_LEN, D_MODEL) and bsz * CTX_LEN == TM
    d = D_MODEL
    h = jnp.concatenate([x.reshape(bsz * SEQ, d), ctx.reshape(bsz * CTX_LEN, d)], axis=0)
    t = h.shape[0]

    cond = jnp.concatenate([c, c_ctx[None, :], jnp.zeros((8 - bsz - 1, d), f32)], axis=0)
    mod = ada_modulation(cond, w_ada, b_ada)
    cosf, sinf = rope_tables(SEQ)

    dt_end = 2 * LRU_WIDTH + SSD_INNER + SSD_CONV_DIM + 2 * SSD_HEADS
    assert dt_end + DT_PAD == W_QKV0 and w_in.shape[-1] == dt_end + 3 * NA_WIDTH + 3 * d
    w_proj = jnp.concatenate([w_in[..., :dt_end].astype(bf16), jnp.zeros((w_in.shape[0], d, DT_PAD), bf16),
                              w_in[..., dt_end:].astype(bf16)], axis=-1)
    qkv_blocks = 3 * NA_WIDTH // TN
    pa, pb, pc, wo = (w.astype(bf16) for w in (w_branch_lru, w_branch_ssd, w_branch_na, w_out))
    w1b, w3b, w2b = (w.astype(bf16) for w in (w1, w3, w2))
    na_tbl = na_bias_table(na_rpb)

    for l in range(DEPTH):
        mods = jnp.split(mod[l], 6, axis=-1)
        sh1, sc1, g1, sh2, sc2 = (_tile_rows(m, bsz) for m in mods[:5])
        u = norm_modulate(h, norm_mix[l], sc1, sh1, bf16)
        proj = matmul_bf16(u, w_proj[l], N_F32, lambda j: jnp.where(j < W_QKV0 // TN, j, j + qkv_blocks), f32)
        qkv = matmul_bf16(u, w_proj[l], 3 * NA_WIDTH, lambda j: j + W_QKV0 // TN, bf16)
        ya = lru_branch(proj, lru_conv_w[l], lru_conv_b[l], lru_w_r[l], lru_b_r[l], lru_w_i[l], lru_b_i[l],
                        lru_lambda[l], bsz, LAT_BLOCKS)
        yb = ssd_branch(proj, cosf, sinf, ssd_conv_w[l], ssd_conv_b[l], ssd_a_log[l], ssd_dt_bias[l], ssd_d[l],
                        ssd_norm[l], bsz, LAT_BLOCKS)
        yc = na_attention(qkv, na_tbl[l], bsz)
        y = branch_merge(ya, yb, yc, proj, pa[l], pb[l], pc[l], COL_G)
        h = residual_matmul(y, wo[l], h, g1)
        v = norm_modulate(h, norm_ffn[l], sc2, sh2, bf16)
        h = _expert_choice_moe(h, v, w_router[l], w1b[l], w3b[l], w2b[l], mods[5][:, None, :], bsz)

    zeros = jnp.zeros((t // TM, 1, d), f32)
    out = norm_modulate(h, norm_final, zeros, zeros, f32)
    return out[:bsz * SEQ].reshape(bsz, SEQ, d)
```

```python
import functools

import jax
import jax.numpy as jnp
import numpy as np
from jax import lax
from jax.experimental import pallas as pl
from jax.experimental.pallas import tpu as pltpu

D_MODEL = 2048
SEQ = 4096
CTX_LEN = 256
DEPTH = 4
GRID_W = 64
EPS = 1e-6
ROPE_BASE = 10000.0
LRU_WIDTH = 1024
LRU_BLOCKS = 16
LRU_BLOCK = LRU_WIDTH // LRU_BLOCKS
LRU_C = 8.0
SSD_INNER = 1024
SSD_HEADDIM = 64
SSD_HEADS = SSD_INNER // SSD_HEADDIM
SSD_GROUPS = 4
SSD_STATE = 128
SSD_CHUNK = 128
SSD_CONV_DIM = SSD_INNER + 2 * SSD_GROUPS * SSD_STATE
NA_HEADS = 16
NA_HEADDIM = 64
NA_WIDTH = NA_HEADS * NA_HEADDIM
NA_WIN_R = 8
NA_WIN_C = 16
N_EXPERTS = 16
EXPERT_FF = 1024
CAPACITY_FACTOR = 2

V7X_VMEM_LIMIT = 56 * 1024 * 1024
TM = 1024
TQ = 256
HALO = 8
LAT_BLOCKS = SEQ // TQ
NEG = -1e30
HI = lax.Precision.HIGHEST
bf16 = jnp.bfloat16
f32 = jnp.float32

TN = 512
DT_END = 2 * LRU_WIDTH + SSD_INNER + SSD_CONV_DIM + 2 * SSD_HEADS
DT_PAD = 480
COL_AX, COL_AG, COL_Z, COL_XS, COL_BC = 0, 1, 2, 3, 4
COL_DT = 40
W_QKV0 = 5632
COL_G = 5632
N_F32 = COL_G + 3 * D_MODEL
GW = SSD_INNER // SSD_GROUPS

NA_GROUP_ROWS = 2
NA_BAND_ROWS = 10
NA_NCFG = 5


def _params():
    return pltpu.CompilerParams(vmem_limit_bytes=V7X_VMEM_LIMIT)


def _ada_kernel(x_ref, w_ref, b_ref, o_ref):
    c = x_ref[...]
    x = (c * jax.nn.sigmoid(c)).astype(bf16)
    o_ref[...] = jnp.dot(x, w_ref[...].astype(bf16), preferred_element_type=f32) + b_ref[...]


def ada_modulation(cond, w_ada, b_ada):
    depth, d, n = w_ada.shape
    tn = 1024
    return pl.pallas_call(
        _ada_kernel,
        grid=(depth, n // tn),
        in_specs=[pl.BlockSpec((8, d), lambda l, j: (0, 0)),
                  pl.BlockSpec((None, d, tn), lambda l, j: (l, 0, j)),
                  pl.BlockSpec((None, 1, tn), lambda l, j: (l, 0, j))],
        out_specs=pl.BlockSpec((None, 8, tn), lambda l, j: (l, 0, j)),
        out_shape=jax.ShapeDtypeStruct((depth, 8, n), f32),
        compiler_params=_params(),
        name="ada_modulation",
    )(cond, w_ada, b_ada.reshape(depth, 1, n))


def _norm_mod_kernel(h_ref, g_ref, sc_ref, sh_ref, o_ref):
    x = h_ref[...]
    y = x * lax.rsqrt(jnp.mean(x * x, axis=-1, keepdims=True) + EPS)
    o_ref[...] = ((y * g_ref[...]) * (1.0 + sc_ref[...]) + sh_ref[...]).astype(o_ref.dtype)


def norm_modulate(h, g, sc_t, sh_t, out_dtype):
    t, d = h.shape
    tm = min(512, TM)
    per = TM // tm
    return pl.pallas_call(
        _norm_mod_kernel,
        grid=(t // tm,),
        in_specs=[pl.BlockSpec((tm, d), lambda i: (i, 0)),
                  pl.BlockSpec((1, d), lambda i: (0, 0)),
                  pl.BlockSpec((None, 1, d), lambda i: (i // per, 0, 0)),
                  pl.BlockSpec((None, 1, d), lambda i: (i // per, 0, 0))],
        out_specs=pl.BlockSpec((tm, d), lambda i: (i, 0)),
        out_shape=jax.ShapeDtypeStruct((t, d), out_dtype),
        compiler_params=_params(),
        name="norm_modulate",
    )(h, g.reshape(1, d), sc_t, sh_t)


def _relayout_kernel(shift, a_ref, b_ref, o_ref):
    j = pl.program_id(2)
    split = DT_END // TN

    @pl.when(j < split)
    def _():
        o_ref[...] = a_ref[...].astype(o_ref.dtype)

    @pl.when(j == split)
    def _():
        lane = lax.broadcasted_iota(jnp.int32, a_ref.shape, 1)
        o_ref[...] = jnp.where(lane < shift, a_ref[...], 0.0).astype(o_ref.dtype)

    @pl.when(j > split)
    def _():
        cat = jnp.concatenate([a_ref[...], b_ref[...]], axis=1)
        o_ref[...] = cat[:, shift:shift + TN].astype(o_ref.dtype)


def relayout_proj_weights(w_in):
    depth, d, n = w_in.shape
    shift = DT_END % TN
    assert shift + DT_PAD == TN and shift <= 128
    split = DT_END // TN
    tr = 512
    return pl.pallas_call(
        functools.partial(_relayout_kernel, shift),
        grid=(depth, d // tr, (n + DT_PAD) // TN),
        in_specs=[pl.BlockSpec((None, tr, TN), lambda l, i, j: (l, i, jnp.where(j <= split, j, j - 1))),
                  pl.BlockSpec((None, tr, 128), lambda l, i, j: (l, i, jnp.where(j <= split, 0, (TN // 128) * j)))],
        out_specs=pl.BlockSpec((None, tr, TN), lambda l, i, j: (l, i, j)),
        out_shape=jax.ShapeDtypeStruct((depth, d, n + DT_PAD), bf16),
        compiler_params=_params(),
        name="relayout_proj_weights",
    )(w_in, w_in)


def _mm_kernel(x_ref, w_ref, o_ref):
    o_ref[...] = jnp.dot(x_ref[...], w_ref[...], preferred_element_type=f32).astype(o_ref.dtype)


def matmul_bf16(x, w, n_out, w_block, out_dtype):
    m, k = x.shape
    return pl.pallas_call(
        _mm_kernel,
        grid=(m // TM, n_out // TN),
        in_specs=[pl.BlockSpec((TM, k), lambda i, j: (i, 0)),
                  pl.BlockSpec((k, TN), lambda i, j: (0, w_block(j)))],
        out_specs=pl.BlockSpec((TM, TN), lambda i, j: (i, j)),
        out_shape=jax.ShapeDtypeStruct((m, n_out), out_dtype),
        compiler_params=_params(),
        name="matmul_bf16",
    )(x, w)


def _merge_kernel(ya_ref, yb_ref, yc_ref, ga_ref, gb_ref, gc_ref, pa_ref, pb_ref, pc_ref, o_ref):
    acc = jax.nn.sigmoid(ga_ref[...]) * jnp.dot(ya_ref[...], pa_ref[...], preferred_element_type=f32)
    acc = acc + jax.nn.sigmoid(gb_ref[...]) * jnp.dot(yb_ref[...], pb_ref[...], preferred_element_type=f32)
    acc = acc + jax.nn.sigmoid(gc_ref[...]) * jnp.dot(yc_ref[...], pc_ref[...], preferred_element_type=f32)
    o_ref[...] = acc.astype(o_ref.dtype)


def branch_merge(ya, yb, yc, proj, pa, pb, pc, g_col0):
    t, k = ya.shape
    n = pa.shape[1]
    tn = 512
    gb0 = g_col0 // tn
    nj = n // tn
    xs = pl.BlockSpec((TM, k), lambda i, j: (i, 0))
    ws = pl.BlockSpec((k, tn), lambda i, j: (0, j))
    gs = lambda q: pl.BlockSpec((TM, tn), lambda i, j: (i, gb0 + q * nj + j))
    return pl.pallas_call(
        _merge_kernel,
        grid=(t // TM, nj),
        in_specs=[xs, xs, xs, gs(0), gs(1), gs(2), ws, ws, ws],
        out_specs=pl.BlockSpec((TM, tn), lambda i, j: (i, j)),
        out_shape=jax.ShapeDtypeStruct((t, n), bf16),
        compiler_params=_params(),
        name="branch_merge",
    )(ya, yb, yc, proj, proj, proj, pa, pb, pc)


def _resid_mm_kernel(y_ref, w_ref, h_ref, g_ref, o_ref):
    o_ref[...] = h_ref[...] + g_ref[...] * jnp.dot(y_ref[...], w_ref[...], preferred_element_type=f32)


def residual_matmul(y, w, h, gate_t):
    m, k = y.shape
    n = w.shape[1]
    tn = 512
    return pl.pallas_call(
        _resid_mm_kernel,
        grid=(m // TM, n // tn),
        in_specs=[pl.BlockSpec((TM, k), lambda i, j: (i, 0)),
                  pl.BlockSpec((k, tn), lambda i, j: (0, j)),
                  pl.BlockSpec((TM, tn), lambda i, j: (i, j)),
                  pl.BlockSpec((None, 1, tn), lambda i, j: (i, 0, j))],
        out_specs=pl.BlockSpec((TM, tn), lambda i, j: (i, j)),
        out_shape=jax.ShapeDtypeStruct((m, n), f32),
        compiler_params=_params(),
        name="residual_matmul",
    )(y, w, h, gate_t)


def _router_kernel(v_ref, w_ref, o_ref):
    s = jnp.dot(v_ref[...], w_ref[...], preferred_element_type=f32)
    lane = lax.broadcasted_iota(jnp.int32, s.shape, 1)
    s = jnp.where(lane < N_EXPERTS, s, NEG)
    e = jnp.exp(s - s.max(axis=-1, keepdims=True))
    o_ref[...] = e / e.sum(axis=-1, keepdims=True)


def router_affinity(v, w_router):
    t, d = v.shape
    w = jnp.pad(w_router, ((0, 0), (0, 128 - N_EXPERTS))).astype(bf16)
    return pl.pallas_call(
        _router_kernel,
        grid=(t // TM,),
        in_specs=[pl.BlockSpec((TM, d), lambda i: (i, 0)), pl.BlockSpec((d, 128), lambda i: (0, 0))],
        out_specs=pl.BlockSpec((TM, 128), lambda i: (i, 0)),
        out_shape=jax.ShapeDtypeStruct((t, 128), f32),
        compiler_params=_params(),
        name="router_affinity",
    )(v, w)


def _expert_hidden_kernel(x_ref, w1_ref, w3_ref, o_ref, w1b, w3b):
    @pl.when(pl.program_id(1) == 0)
    def _():
        w1b[...] = w1_ref[...].astype(bf16)
        w3b[...] = w3_ref[...].astype(bf16)

    x = x_ref[...]
    a = jnp.dot(x, w1b[...], preferred_element_type=f32)
    b = jnp.dot(x, w3b[...], preferred_element_type=f32)
    o_ref[...] = ((a * jax.nn.sigmoid(a)) * b).astype(o_ref.dtype)


def expert_hidden(xg, w1, w3, tm):
    e, r, d = xg.shape
    f = w1.shape[-1]
    return pl.pallas_call(
        _expert_hidden_kernel,
        grid=(e, r // tm),
        in_specs=[pl.BlockSpec((None, tm, d), lambda k, i: (k, i, 0)),
                  pl.BlockSpec((None, d, f), lambda k, i: (k, 0, 0)),
                  pl.BlockSpec((None, d, f), lambda k, i: (k, 0, 0))],
        out_specs=pl.BlockSpec((None, tm, f), lambda k, i: (k, i, 0)),
        out_shape=jax.ShapeDtypeStruct((e, r, f), bf16),
        scratch_shapes=[pltpu.VMEM((d, f), bf16), pltpu.VMEM((d, f), bf16)],
        compiler_params=_params(),
        name="expert_hidden",
    )(xg, w1, w3)


SCATTER_UNROLL = 8


def _combine_kernel(nb, slots, idx_ref, hdn_ref, w2_ref, g_ref, h_ref, g2_ref, *rest):
    o_ref, ye_scr = rest[-2], rest[-1]
    b, e = pl.program_id(0), pl.program_id(2)

    @pl.when(e == 0)
    def _():
        o_ref[...] = jnp.zeros_like(o_ref)

    ye_scr[...] = jnp.dot(hdn_ref[...], w2_ref[...], preferred_element_type=f32) * g_ref[...]
    base0 = (e * nb + b) * slots

    def body(i, carry):
        s0 = pl.multiple_of(i * SCATTER_UNROLL, SCATTER_UNROLL)
        rows = [idx_ref[base0 + s0 + k] for k in range(SCATTER_UNROLL)]
        vals = [o_ref[pl.ds(rows[k], 1), :] + ye_scr[pl.ds(s0 + k, 1), :] for k in range(SCATTER_UNROLL)]
        for k in range(SCATTER_UNROLL):
            o_ref[pl.ds(rows[k], 1), :] = vals[k]
        return carry

    lax.fori_loop(0, slots // SCATTER_UNROLL, body, 0)

    @pl.when(e == pl.num_programs(2) - 1)
    def _():
        o_ref[...] = h_ref[...] + g2_ref[...] * o_ref[...]


def moe_combine(idx, hdn, slot_blk0, slots, gates, w2, h, g2_rows, g2_row0, row_blk0, rows_blk, dq, nb, prev=None):
    e, _, f = hdn.shape
    t, d = h.shape
    in_specs = [
        pl.BlockSpec((None, slots, f), lambda b, q, k, idx: (k, slot_blk0 + b, 0)),
        pl.BlockSpec((None, f, dq), lambda b, q, k, idx: (k, 0, q)),
        pl.BlockSpec((None, slots, 1), lambda b, q, k, idx: (k, slot_blk0 + b, 0)),
        pl.BlockSpec((rows_blk, dq), lambda b, q, k, idx: (row_blk0 + b, q)),
        pl.BlockSpec((None, 1, dq), lambda b, q, k, idx: (g2_row0 + b, 0, q)),
    ]
    args = [idx, hdn, w2, gates, h, g2_rows]
    aliases = {}
    if prev is not None:
        in_specs.append(pl.BlockSpec(memory_space=pl.ANY))
        args.append(prev)
        aliases = {len(args) - 1: 0}
    return pl.pallas_call(
        functools.partial(_combine_kernel, nb, slots),
        grid_spec=pltpu.PrefetchScalarGridSpec(
            num_scalar_prefetch=1,
            grid=(nb, d // dq, e),
            in_specs=in_specs,
            out_specs=pl.BlockSpec((rows_blk, dq), lambda b, q, k, idx: (row_blk0 + b, q)),
            scratch_shapes=[pltpu.VMEM((slots, dq), f32)],
        ),
        out_shape=jax.ShapeDtypeStruct((t, d), f32),
        input_output_aliases=aliases,
        compiler_params=_params(),
        name="moe_combine",
    )(*args)


def na_bias_blocks(rpb):
    rows = SEQ // GRID_W
    n_dr, n_dc = 2 * NA_WIN_R - 1, 2 * NA_WIN_C - 1
    qc, kc = np.arange(GRID_W)[:, None], np.arange(GRID_W)[None, :]
    cs = np.clip(qc - NA_WIN_C // 2, 0, GRID_W - NA_WIN_C)
    col_ok = (kc >= cs) & (kc < cs + NA_WIN_C)
    pick = (np.arange(n_dc)[:, None, None] == (kc - qc + NA_WIN_C - 1)[None]) & col_ok[None]
    blocks = jnp.einsum('...rd,dqk->...rqk', rpb, jnp.asarray(pick, f32), precision=HI)
    blocks = jnp.where(col_ok, blocks, NEG)
    blocks = jnp.concatenate([blocks, jnp.full(blocks.shape[:-3] + (1, GRID_W, GRID_W), NEG, f32)], axis=-3)
    which = np.full((NA_NCFG, NA_GROUP_ROWS, NA_BAND_ROWS), n_dr, np.int32)
    for c, r0 in enumerate((0, 2, 4, 60, 62)):
        bs = int(np.clip(r0 - NA_WIN_R // 2, 0, rows - NA_BAND_ROWS))
        for qr in range(NA_GROUP_ROWS):
            r = r0 + qr
            rs = int(np.clip(r - NA_WIN_R // 2, 0, rows - NA_WIN_R))
            for kr in range(NA_BAND_ROWS):
                if rs <= bs + kr < rs + NA_WIN_R:
                    which[c, qr, kr] = bs + kr - r + NA_WIN_R - 1
    pairs = sorted({(int(which[c, qr, 2 * m]), int(which[c, qr, 2 * m + 1]))
                    for c in range(NA_NCFG) for qr in range(NA_GROUP_ROWS) for m in range(NA_BAND_ROWS // 2)})
    pair_of = np.array([[[pairs.index((int(which[c, qr, 2 * m]), int(which[c, qr, 2 * m + 1])))
                          for m in range(NA_BAND_ROWS // 2)] for qr in range(NA_GROUP_ROWS)]
                        for c in range(NA_NCFG)], np.int32)
    left = jnp.take(blocks, np.array([p[0] for p in pairs]), axis=-3)
    right = jnp.take(blocks, np.array([p[1] for p in pairs]), axis=-3)
    return jnp.concatenate([left, right], axis=-1), pair_of.reshape(-1)


def _dot_nt(a, b):
    return lax.dot_general(a, b, (((1,), (1,)), ((), ())), preferred_element_type=f32)


def _softmax_av(parts):
    m = parts[0][0].max(axis=-1, keepdims=True)
    for s, _ in parts[1:]:
        m = jnp.maximum(m, s.max(axis=-1, keepdims=True))
    l = None
    o = None
    for s, v in parts:
        p = jnp.exp(s - m)
        ls = p.sum(axis=-1, keepdims=True)
        os_ = jnp.dot(p.astype(bf16), v, preferred_element_type=f32)
        l = ls if l is None else l + ls
        o = os_ if o is None else o + os_
    return o / l


def _na_kernel(pair_ref, q_ref, k_ref, v_ref, kc_ref, vc_ref, tbl_ref, o_ref):
    j = pl.program_id(2)
    npairs = NA_BAND_ROWS // 2

    def bias(hh, cfg):
        rows = []
        for qr in range(NA_GROUP_ROWS):
            base = (cfg * NA_GROUP_ROWS + qr) * npairs
            rows.append(jnp.concatenate([tbl_ref[hh, pair_ref[base + m]] for m in range(npairs)], axis=1))
        return jnp.concatenate(rows, axis=0)

    scale = NA_HEADDIM ** -0.5
    lane = lax.broadcasted_iota(jnp.int32, (1, 2 * NA_HEADDIM), 1)
    first = lane < NA_HEADDIM
    kc = kc_ref[...]
    vc = vc_ref[...]
    gq = NA_GROUP_ROWS * GRID_W

    def head_q(q, hh):
        keep = first if hh == 0 else jnp.logical_not(first)
        return jnp.where(keep, q, jnp.zeros_like(q))

    @pl.when(j < LAT_BLOCKS)
    def _():
        for g in range(TQ // gq):
            r0 = j * (TQ // GRID_W) + NA_GROUP_ROWS * g
            bs = jnp.clip(r0 - NA_WIN_R // 2, 0, SEQ // GRID_W - NA_BAND_ROWS)
            cfg = jnp.where(r0 == 0, 0, jnp.where(r0 == 2, 1, jnp.where(r0 == 60, 3, jnp.where(r0 == 62, 4, 2))))
            start = pl.multiple_of(bs * GRID_W, GRID_W)
            kb = k_ref[pl.ds(start, NA_BAND_ROWS * GRID_W), :]
            vb = v_ref[pl.ds(start, NA_BAND_ROWS * GRID_W), :]
            q = q_ref[pl.ds(g * gq, gq), :]
            outs = []
            for hh in range(2):
                qm = head_q(q, hh)
                s = _dot_nt(qm, kb) * scale + bias(hh, cfg)
                sc = _dot_nt(qm, kc) * scale
                outs.append(_softmax_av([(s, vb), (sc, vc)]))
            o_ref[pl.ds(g * gq, gq), :] = jnp.where(first, outs[0], outs[1]).astype(o_ref.dtype)

    @pl.when(j == LAT_BLOCKS)
    def _():
        q = q_ref[...]
        outs = []
        for hh in range(2):
            sc = _dot_nt(head_q(q, hh), kc) * scale
            outs.append(_softmax_av([(sc, vc)]))
        o_ref[...] = jnp.where(first, outs[0], outs[1]).astype(o_ref.dtype)


def na_attention(qkv, tbl, pair_of, bsz):
    t = qkv.shape[0]
    npair = NA_HEADS // 2
    ctx0 = bsz * LAT_BLOCKS

    def qrow(b, j):
        return jnp.where(j < LAT_BLOCKS, b * LAT_BLOCKS + j, ctx0 + b)

    return pl.pallas_call(
        _na_kernel,
        grid_spec=pltpu.PrefetchScalarGridSpec(
            num_scalar_prefetch=1,
            grid=(bsz, npair, LAT_BLOCKS + 1),
            in_specs=[
                pl.BlockSpec((TQ, 128), lambda b, p, j, po: (qrow(b, j), p)),
                pl.BlockSpec((SEQ, 128), lambda b, p, j, po: (b, npair + p)),
                pl.BlockSpec((SEQ, 128), lambda b, p, j, po: (b, 2 * npair + p)),
                pl.BlockSpec((CTX_LEN, 128), lambda b, p, j, po: (ctx0 + b, npair + p)),
                pl.BlockSpec((CTX_LEN, 128), lambda b, p, j, po: (ctx0 + b, 2 * npair + p)),
                pl.BlockSpec((2,) + tbl.shape[1:], lambda b, p, j, po: (p, 0, 0, 0)),
            ],
            out_specs=pl.BlockSpec((TQ, 128), lambda b, p, j, po: (qrow(b, j), p)),
        ),
        out_shape=jax.ShapeDtypeStruct((t, NA_WIDTH), bf16),
        compiler_params=_params(),
        name="na_attention",
    )(jnp.asarray(pair_of), qkv, qkv, qkv, qkv, qkv, tbl)


def _seq_block(d, bsz, lat_blocks):
    ctx0 = bsz * lat_blocks

    def jj(s):
        return (s - 1) if d == 0 else (lat_blocks - s)

    def blk(b, s):
        return jnp.where(s == 0, ctx0 + b, b * lat_blocks + jj(s))

    return jj, blk


def _halo_specs(width, col, blk, t):
    nb8 = TQ // HALO
    return [
        pl.BlockSpec((TQ, width), lambda b, s: (blk(b, s), col)),
        pl.BlockSpec((HALO, width), lambda b, s: (jnp.maximum(blk(b, s) * nb8 - 1, 0), col)),
        pl.BlockSpec((HALO, width), lambda b, s: (jnp.minimum(blk(b, s) * nb8 + nb8, t // HALO - 1), col)),
    ]


def _centred_conv(xe, parts, cw_ref, cb_ref, has_prev, has_next):
    c0 = 0
    for x_ref, xp_ref, xn_ref in parts:
        cols = slice(c0, c0 + x_ref.shape[1])
        xe[0:HALO, cols] = jnp.where(has_prev, xp_ref[...], 0.0)
        xe[HALO:HALO + TQ, cols] = x_ref[...]
        xe[HALO + TQ:2 * HALO + TQ, cols] = jnp.where(has_next, xn_ref[...], 0.0)
        c0 += x_ref.shape[1]
    out = cb_ref[...] + cw_ref[0:1, :] * xe[HALO - 2:HALO - 2 + TQ, :]
    for k in range(1, 4):
        out = out + cw_ref[k:k + 1, :] * xe[HALO - 2 + k:HALO - 2 + k + TQ, :]
    return out


def lru_gate_weights(w_r, w_i):
    def pair(w):
        w = w.reshape(2, LRU_BLOCKS // 2, 2, LRU_BLOCK, LRU_BLOCK)
        z = jnp.zeros_like(w[:, :, 0])
        top = jnp.concatenate([w[:, :, 0], z], axis=-1)
        bot = jnp.concatenate([z, w[:, :, 1]], axis=-1)
        return jnp.concatenate([top, bot], axis=-2)
    return jnp.concatenate([pair(w_r), pair(w_i)], axis=-1).astype(bf16)


def _gelu_tanh(x):
    return 0.5 * x * (1.0 + jnp.tanh(float(np.sqrt(2.0 / np.pi)) * (x + 0.044715 * (x * x * x))))


def _lru_kernel(d, lat_blocks, *refs):
    if d == 0:
        (x_ref, xp_ref, xn_ref, cw_ref, cb_ref, wg_ref, br_ref, bi_ref, lam_ref, o_ref,
         xe, a_scr, u_scr, h_scr) = refs
    else:
        (x_ref, xp_ref, xn_ref, ag_ref, hf_ref, cw_ref, cb_ref, wg_ref, br_ref, bi_ref, lam_ref, o_ref,
         xe, a_scr, u_scr, h_scr, hs_scr) = refs
    s = pl.program_id(1)
    j = (s - 1) if d == 0 else (lat_blocks - s)
    has_prev = jnp.logical_and(s > 0, j > 0)
    has_next = jnp.logical_and(s > 0, j < lat_blocks - 1)

    @pl.when(s == 0)
    def _():
        h_scr[...] = jnp.zeros_like(h_scr)

    xc = _centred_conv(xe, [(x_ref, xp_ref, xn_ref)], cw_ref, cb_ref, has_prev, has_next)
    xcb = xc.astype(bf16)
    sp = jax.nn.softplus(-lam_ref[...])
    for p in range(LRU_WIDTH // 128):
        sl = slice(128 * p, 128 * (p + 1))
        g = jnp.dot(xcb[:, sl], wg_ref[p], preferred_element_type=f32)
        r = jax.nn.sigmoid(g[:, :128] + br_ref[:, sl])
        i = jax.nn.sigmoid(g[:, 128:] + bi_ref[:, sl])
        log_a = -LRU_C * r * sp[:, sl]
        a = jnp.exp(log_a)
        a_scr[:, sl] = a
        u_scr[:, sl] = jnp.sqrt(-jnp.tanh(log_a) * (a * a + 1.0)) * (i * xc[:, sl])

    out = o_ref if d == 0 else hs_scr
    ngrp = TQ // 8

    def body(gi, h):
        base = pl.multiple_of((gi if d == 0 else ngrp - 1 - gi) * 8, 8)
        for t in (range(8) if d == 0 else range(7, -1, -1)):
            h = a_scr[pl.ds(base + t, 1), :] * h + u_scr[pl.ds(base + t, 1), :]
            out[pl.ds(base + t, 1), :] = h
        return h

    h = lax.fori_loop(0, ngrp, body, h_scr[0:1, :])
    h_scr[0:1, :] = h
    if d == 1:
        y = (hf_ref[...] + hs_scr[...]) * _gelu_tanh(ag_ref[...])
        o_ref[...] = y.astype(o_ref.dtype)


def lru_pass(d, proj, hf, conv_w, conv_b, wg, b_r, b_i, lam, bsz, lat_blocks):
    t = proj.shape[0]
    _, blk = _seq_block(d, bsz, lat_blocks)
    row = lambda c: (lambda b, s: (blk(b, s), c))
    in_specs = _halo_specs(LRU_WIDTH, COL_AX, blk, t)
    args = [proj, proj, proj]
    if d == 1:
        in_specs += [pl.BlockSpec((TQ, LRU_WIDTH), row(COL_AG)), pl.BlockSpec((TQ, LRU_WIDTH), row(0))]
        args += [proj, hf]
    const = lambda shape: pl.BlockSpec(shape, lambda b, s: (0,) * len(shape))
    in_specs += [const((4, LRU_WIDTH)), const((1, LRU_WIDTH)), const((LRU_WIDTH // 128, 128, 256)),
                 const((1, LRU_WIDTH)), const((1, LRU_WIDTH)), const((1, LRU_WIDTH))]
    args += [conv_w, conv_b.reshape(1, -1), wg[d], b_r[d].reshape(1, -1), b_i[d].reshape(1, -1), lam[d].reshape(1, -1)]
    scratch = [pltpu.VMEM((TQ + 2 * HALO, LRU_WIDTH), f32), pltpu.VMEM((TQ, LRU_WIDTH), f32),
               pltpu.VMEM((TQ, LRU_WIDTH), f32), pltpu.VMEM((8, LRU_WIDTH), f32)]
    if d == 1:
        scratch.append(pltpu.VMEM((TQ, LRU_WIDTH), f32))
    return pl.pallas_call(
        functools.partial(_lru_kernel, d, lat_blocks),
        grid=(bsz, lat_blocks + 1),
        in_specs=in_specs,
        out_specs=pl.BlockSpec((TQ, LRU_WIDTH), row(0)),
        out_shape=jax.ShapeDtypeStruct((t, LRU_WIDTH), f32 if d == 0 else bf16),
        scratch_shapes=scratch,
        compiler_params=_params(),
        name=f"lru_pass{d}",
    )(*args)


def lru_branch(proj, conv_w, conv_b, w_r, b_r, w_i, b_i, lam, bsz, lat_blocks):
    wg = lru_gate_weights(w_r, w_i)
    hf = lru_pass(0, proj, None, conv_w, conv_b, wg, b_r, b_i, lam, bsz, lat_blocks)
    return lru_pass(1, proj, hf, conv_w, conv_b, wg, b_r, b_i, lam, bsz, lat_blocks)


def rope_tables(seq):
    pos = jnp.arange(seq)
    row = (pos // GRID_W).astype(f32)
    col = (pos % GRID_W).astype(f32)
    n_freq = SSD_STATE // 4
    freqs = ROPE_BASE ** (-jnp.arange(n_freq, dtype=f32) / n_freq)
    ang = jnp.concatenate([row[:, None] * freqs, col[:, None] * freqs], axis=-1)
    cos, sin = jnp.cos(ang), jnp.sin(ang)
    cosf = jnp.concatenate([cos, cos], axis=-1)
    sinf = jnp.concatenate([-sin, sin], axis=-1)
    cosf = jnp.concatenate([cosf, jnp.ones((TQ, SSD_STATE), f32)], axis=0)
    sinf = jnp.concatenate([sinf, jnp.zeros((TQ, SSD_STATE), f32)], axis=0)
    return cosf, sinf


def head_expand_matrix(d):
    e = np.zeros((128, SSD_INNER), np.float32)
    for h in range(SSD_HEADS):
        e[SSD_HEADS * d + h, h * SSD_HEADDIM:(h + 1) * SSD_HEADDIM] = 1.0
    return jnp.asarray(e, bf16)


def _split3(a):
    hi = a.astype(bf16)
    r = a - hi.astype(f32)
    mid = r.astype(bf16)
    return hi, mid, (r - mid.astype(f32)).astype(bf16)


def _dot_exact_rhs01(a, m01):
    return sum(jnp.dot(p, m01, preferred_element_type=f32) for p in _split3(a))


def _dot_exact_lhs01(m01, a):
    return sum(jnp.dot(m01, p, preferred_element_type=f32) for p in _split3(a))


def _ssd_kernel(d, lat_blocks, *refs):
    xs_refs, bc_refs, refs = refs[0:3], refs[3:6], refs[6:]
    if d == 0:
        (dt_ref, cos_ref, sin_ref, cw_ref, cb_ref, dtb_ref, alog_ref, e_ref,
         o_ref, xe, h_scr) = refs
    else:
        (dt_ref, cos_ref, sin_ref, z_ref, y0_ref, cw_ref, cb_ref, dtb_ref, alog_ref, e_ref,
         dsk_ref, ng_ref, o_ref, xe, h_scr, y_scr) = refs
    s = pl.program_id(1)
    j = (s - 1) if d == 0 else (lat_blocks - s)
    has_prev = jnp.logical_and(s > 0, j > 0)
    has_next = jnp.logical_and(s > 0, j < lat_blocks - 1)
    q = SSD_CHUNK

    @pl.when(s == 0)
    def _():
        h_scr[...] = jnp.zeros_like(h_scr)

    xbc = _centred_conv(xe, [xs_refs, bc_refs], cw_ref, cb_ref, has_prev, has_next)
    xbc = xbc * jax.nn.sigmoid(xbc)
    xs = xbc[:, :SSD_INNER]
    cosf, sinf = cos_ref[...], sin_ref[...]

    def rope(g, off):
        v = xbc[:, off + g * SSD_STATE: off + (g + 1) * SSD_STATE]
        return (v * cosf + pltpu.roll(v, SSD_STATE // 2, 1) * sinf).astype(bf16)

    bm = [rope(g, SSD_INNER) for g in range(SSD_GROUPS)]
    cm = [rope(g, SSD_INNER + SSD_GROUPS * SSD_STATE) for g in range(SSD_GROUPS)]
    dt = jax.nn.softplus(dt_ref[...] + dtb_ref[...])
    delta = dt * (-jnp.exp(alog_ref[...]))
    ri = lax.broadcasted_iota(jnp.int32, (q, q), 0)
    ci = lax.broadcasted_iota(jnp.int32, (q, q), 1)
    keep = (ci <= ri) if d == 0 else (ci >= ri)
    tri = jnp.where(keep, 1.0, 0.0).astype(bf16)
    lane = lax.broadcasted_iota(jnp.int32, (1, 2 * SSD_HEADDIM), 1)
    halves = (lane < SSD_HEADDIM, lane >= SSD_HEADDIM)
    e = e_ref[...]
    last = q - 1 if d == 0 else 0
    out = o_ref if d == 0 else y_scr

    for c in (range(TQ // q) if d == 0 else range(TQ // q - 1, -1, -1)):
        rows = slice(c * q, (c + 1) * q)
        at = _dot_exact_lhs01(tri, delta[rows])
        at_exp = _dot_exact_rhs01(at, e)
        dt_exp = _dot_exact_rhs01(dt[rows], e)
        tot_exp = at_exp[last:last + 1, :]
        xdt = xs[rows] * dt_exp
        xd = (xdt * jnp.exp(tot_exp - at_exp)).astype(bf16)
        eat = jnp.exp(at_exp)
        cdec = jnp.exp(tot_exp)
        at_row = at.T
        ys = []
        for g in range(SSD_GROUPS):
            bg, cg = bm[g][rows], cm[g][rows]
            cb = _dot_nt(cg, bg)
            ht = h_scr[g]
            yoff = jnp.dot(cg, ht.astype(bf16), preferred_element_type=f32) * eat[:, g * GW:(g + 1) * GW]
            for pp in range(2):
                xpair = xdt[:, g * GW + pp * 128: g * GW + (pp + 1) * 128]
                acc = yoff[:, pp * 128:(pp + 1) * 128]
                for hh in range(2):
                    li = SSD_HEADS * d + 4 * g + 2 * pp + hh
                    seg = at[:, li:li + 1] - at_row[li:li + 1, :]
                    m = (cb * jnp.exp(jnp.where(keep, seg, NEG))).astype(bf16)
                    xm = jnp.where(halves[hh], xpair, 0.0).astype(bf16)
                    acc = acc + jnp.dot(m, xm, preferred_element_type=f32)
                ys.append(acc)
            upd = lax.dot_general(bg, xd[:, g * GW:(g + 1) * GW], (((0,), (0,)), ((), ())),
                                  preferred_element_type=f32)
            h_scr[g] = cdec[:, g * GW:(g + 1) * GW] * ht + upd
        out[rows, :] = jnp.concatenate(ys, axis=-1)

    if d == 1:
        y = y0_ref[...] + y_scr[...] + dsk_ref[...] * xs
        z = z_ref[...]
        y = y * (z * jax.nn.sigmoid(z))
        y = y * lax.rsqrt(jnp.mean(y * y, axis=-1, keepdims=True) + EPS)
        o_ref[...] = (y * ng_ref[...]).astype(o_ref.dtype)


def ssd_pass(d, proj, y0, cosf, sinf, conv_w, conv_b, a_log, dt_bias, d_skip, norm_g, bsz, lat_blocks):
    t = proj.shape[0]
    jj, blk = _seq_block(d, bsz, lat_blocks)
    row = lambda c: (lambda b, s: (blk(b, s), c))
    tbl = lambda b, s: (jnp.where(s == 0, lat_blocks, jj(s)), 0)
    const = lambda shape: pl.BlockSpec(shape, lambda b, s: (0,) * len(shape))
    pad128 = lambda v: jnp.pad(v.reshape(1, -1).astype(f32), ((0, 0), (0, 128 - v.size)))
    in_specs = _halo_specs(SSD_INNER, COL_XS, blk, t) + _halo_specs(SSD_CONV_DIM - SSD_INNER, COL_BC, blk, t) + [
        pl.BlockSpec((TQ, 128), row(COL_DT)),
        pl.BlockSpec((TQ, SSD_STATE), tbl),
        pl.BlockSpec((TQ, SSD_STATE), tbl),
    ]
    args = [proj] * 7 + [cosf, sinf]
    if d == 1:
        in_specs += [pl.BlockSpec((TQ, SSD_INNER), row(COL_Z)), pl.BlockSpec((TQ, SSD_INNER), row(0))]
        args += [proj, y0]
    in_specs += [const((4, SSD_CONV_DIM)), const((1, SSD_CONV_DIM)), const((1, 128)), const((1, 128)),
                 const((128, SSD_INNER))]
    args += [conv_w, conv_b.reshape(1, -1), pad128(dt_bias), pad128(a_log), head_expand_matrix(d)]
    scratch = [pltpu.VMEM((TQ + 2 * HALO, SSD_CONV_DIM), f32), pltpu.VMEM((SSD_GROUPS, SSD_STATE, GW), f32)]
    if d == 1:
        in_specs += [const((1, SSD_INNER)), const((1, SSD_INNER))]
        args += [jnp.repeat(d_skip, SSD_HEADDIM).reshape(1, -1), norm_g.reshape(1, -1)]
        scratch.append(pltpu.VMEM((TQ, SSD_INNER), f32))
    return pl.pallas_call(
        functools.partial(_ssd_kernel, d, lat_blocks),
        grid=(bsz, lat_blocks + 1),
        in_specs=in_specs,
        out_specs=pl.BlockSpec((TQ, SSD_INNER), row(0)),
        out_shape=jax.ShapeDtypeStruct((t, SSD_INNER), f32 if d == 0 else bf16),
        scratch_shapes=scratch,
        compiler_params=_params(),
        name=f"ssd_pass{d}",
    )(*args)


def ssd_branch(proj, cosf, sinf, conv_w, conv_b, a_log, dt_bias, d_skip, norm_g, bsz, lat_blocks):
    y0 = ssd_pass(0, proj, None, cosf, sinf, conv_w, conv_b, a_log, dt_bias, d_skip, norm_g, bsz, lat_blocks)
    return ssd_pass(1, proj, y0, cosf, sinf, conv_w, conv_b, a_log, dt_bias, d_skip, norm_g, bsz, lat_blocks)


def _tile_rows(mod_rows, bsz):
    idx = np.concatenate([np.repeat(np.arange(bsz), SEQ // TM), np.full(bsz * CTX_LEN // TM, bsz)])
    return mod_rows[idx][:, None, :]


def _expert_choice_moe(h, v, w_router, w1, w3, w2, g2_rows, bsz):
    t, d = v.shape
    aff = router_affinity(v, w_router)[:, :N_EXPERTS]
    nl = bsz * SEQ

    def choose(a, length):
        cap = CAPACITY_FACTOR * length // N_EXPERTS
        g, idx = lax.top_k(jnp.swapaxes(a.reshape(bsz, length, N_EXPERTS), 1, 2), cap)
        return jnp.swapaxes(g, 0, 1), jnp.swapaxes(idx, 0, 1), cap

    g_l, i_l, cap_l = choose(aff[:nl], SEQ)
    g_c, i_c, cap_c = choose(aff[nl:], CTX_LEN)
    boff = jnp.arange(bsz)[None, :, None]
    ctx_rows = (i_c + boff * CTX_LEN).reshape(N_EXPERTS, -1)
    rows = jnp.concatenate([(i_l + boff * SEQ).reshape(N_EXPERTS, -1), ctx_rows + nl], axis=1)
    gates = jnp.concatenate([g_l.reshape(N_EXPERTS, -1), g_c.reshape(N_EXPERTS, -1)], axis=1)[..., None]
    r = rows.shape[1]
    xg = jnp.take(v, rows.reshape(-1), axis=0).reshape(N_EXPERTS, r, d)
    hdn = expert_hidden(xg, w1, w3, r // 4)
    ctx_slots, ctx_tile = bsz * cap_c, bsz * CTX_LEN
    assert (bsz * cap_l) % ctx_slots == 0 and nl % ctx_tile == 0
    out = moe_combine(i_l.reshape(-1), hdn, 0, cap_l, gates, w2, h, g2_rows, 0, 0, SEQ, TN, bsz)
    return moe_combine(ctx_rows.reshape(-1), hdn, bsz * cap_l // ctx_slots, ctx_slots, gates, w2, h, g2_rows, bsz,
                       nl // ctx_tile, ctx_tile, d, 1, prev=out)


def kernel(x, c, ctx, c_ctx, w_ada, b_ada, norm_mix, norm_ffn, w_in, lru_conv_w, lru_conv_b, lru_w_r, lru_b_r,
           lru_w_i, lru_b_i, lru_lambda, ssd_conv_w, ssd_conv_b, ssd_a_log, ssd_dt_bias, ssd_d, ssd_norm, na_rpb,
           w_branch_lru, w_branch_ssd, w_branch_na, w_out, w_router, w1, w3, w2, norm_final):
    bsz = x.shape[0]
    assert x.shape[1:] == (SEQ, D_MODEL) and ctx.shape[1:] == (CTX_LEN, D_MODEL) and bsz * CTX_LEN == TM
    d = D_MODEL
    h = jnp.concatenate([x.reshape(bsz * SEQ, d), ctx.reshape(bsz * CTX_LEN, d)], axis=0)
    t = h.shape[0]

    cond = jnp.concatenate([c, c_ctx[None, :], jnp.zeros((8 - bsz - 1, d), f32)], axis=0)
    mod = ada_modulation(cond, w_ada, b_ada)
    cosf, sinf = rope_tables(SEQ)

    assert DT_END + DT_PAD == W_QKV0 and w_in.shape[-1] == DT_END + 3 * NA_WIDTH + 3 * d
    w_proj = relayout_proj_weights(w_in)
    qkv_blocks = 3 * NA_WIDTH // TN
    pa, pb, pc, wo, w2b = (w.astype(bf16) for w in (w_branch_lru, w_branch_ssd, w_branch_na, w_out, w2))
    na_tbl, na_pair_of = na_bias_blocks(na_rpb)

    for l in range(DEPTH):
        mods = jnp.split(mod[l], 6, axis=-1)
        sh1, sc1, g1, sh2, sc2 = (_tile_rows(m, bsz) for m in mods[:5])
        u = norm_modulate(h, norm_mix[l], sc1, sh1, bf16)
        proj = matmul_bf16(u, w_proj[l], N_F32, lambda j: jnp.where(j < W_QKV0 // TN, j, j + qkv_blocks), f32)
        qkv = matmul_bf16(u, w_proj[l], 3 * NA_WIDTH, lambda j: j + W_QKV0 // TN, bf16)
        ya = lru_branch(proj, lru_conv_w[l], lru_conv_b[l], lru_w_r[l], lru_b_r[l], lru_w_i[l], lru_b_i[l],
                        lru_lambda[l], bsz, LAT_BLOCKS)
        yb = ssd_branch(proj, cosf, sinf, ssd_conv_w[l], ssd_conv_b[l], ssd_a_log[l], ssd_dt_bias[l], ssd_d[l],
                        ssd_norm[l], bsz, LAT_BLOCKS)
        yc = na_attention(qkv, na_tbl[l], na_pair_of, bsz)
        y = branch_merge(ya, yb, yc, proj, pa[l], pb[l], pc[l], COL_G)
        h = residual_matmul(y, wo[l], h, g1)
        v = norm_modulate(h, norm_ffn[l], sc2, sh2, bf16)
        h = _expert_choice_moe(h, v, w_router[l], w1[l], w3[l], w2b[l], mods[5][:, None, :], bsz)

    zeros = jnp.zeros((t // TM, 1, d), f32)
    out = norm_modulate(h, norm_final, zeros, zeros, f32)
    return out[:bsz * SEQ].reshape(bsz, SEQ, d)
```

```python
import functools

import jax
import jax.numpy as jnp
import numpy as np
from jax import lax
from jax.experimental import pallas as pl
from jax.experimental.pallas import tpu as pltpu

D_MODEL = 2048
SEQ = 4096
CTX_LEN = 256
DEPTH = 4
GRID_W = 64
EPS = 1e-6
ROPE_BASE = 10000.0
LRU_WIDTH = 1024
LRU_BLOCKS = 16
LRU_BLOCK = LRU_WIDTH // LRU_BLOCKS
LRU_C = 8.0
SSD_INNER = 1024
SSD_HEADDIM = 64
SSD_HEADS = SSD_INNER // SSD_HEADDIM
SSD_GROUPS = 4
SSD_STATE = 128
SSD_CHUNK = 128
SSD_CONV_DIM = SSD_INNER + 2 * SSD_GROUPS * SSD_STATE
NA_HEADS = 16
NA_HEADDIM = 64
NA_WIDTH = NA_HEADS * NA_HEADDIM
NA_WIN_R = 8
NA_WIN_C = 16
N_EXPERTS = 16
EXPERT_FF = 1024
CAPACITY_FACTOR = 2

V7X_VMEM_LIMIT = 56 * 1024 * 1024
TM = 1024
TQ = 256
HALO = 8
LAT_BLOCKS = SEQ // TQ
NEG = -1e30
HI = lax.Precision.HIGHEST
bf16 = jnp.bfloat16
f32 = jnp.float32

TN = 512
DT_END = 2 * LRU_WIDTH + SSD_INNER + SSD_CONV_DIM + 2 * SSD_HEADS
DT_PAD = 480
COL_AX, COL_AG, COL_Z, COL_XS, COL_BC = 0, 1, 2, 3, 4
COL_DT = 40
W_QKV0 = 5632
COL_G = 5632
N_F32 = COL_G + 3 * D_MODEL
GW = SSD_INNER // SSD_GROUPS

NA_GROUP_ROWS = 2
NA_BAND_ROWS = 10
NA_NCFG = 5


def _params():
    return pltpu.CompilerParams(vmem_limit_bytes=V7X_VMEM_LIMIT)


def _ada_kernel(x_ref, w_ref, b_ref, o_ref):
    c = x_ref[...]
    x = (c * jax.nn.sigmoid(c)).astype(bf16)
    o_ref[...] = jnp.dot(x, w_ref[...].astype(bf16), preferred_element_type=f32) + b_ref[...]


def ada_modulation(cond, w_ada, b_ada):
    depth, d, n = w_ada.shape
    tn = 1024
    return pl.pallas_call(
        _ada_kernel,
        grid=(depth, n // tn),
        in_specs=[pl.BlockSpec((8, d), lambda l, j: (0, 0)),
                  pl.BlockSpec((None, d, tn), lambda l, j: (l, 0, j)),
                  pl.BlockSpec((None, 1, tn), lambda l, j: (l, 0, j))],
        out_specs=pl.BlockSpec((None, 8, tn), lambda l, j: (l, 0, j)),
        out_shape=jax.ShapeDtypeStruct((depth, 8, n), f32),
        compiler_params=_params(),
        name="ada_modulation",
    )(cond, w_ada, b_ada.reshape(depth, 1, n))


def _norm_mod_kernel(h_ref, g_ref, sc_ref, sh_ref, o_ref):
    x = h_ref[...]
    y = x * lax.rsqrt(jnp.mean(x * x, axis=-1, keepdims=True) + EPS)
    o_ref[...] = ((y * g_ref[...]) * (1.0 + sc_ref[...]) + sh_ref[...]).astype(o_ref.dtype)


def norm_modulate(h, g, sc_t, sh_t, out_dtype):
    t, d = h.shape
    tm = min(512, TM)
    per = TM // tm
    return pl.pallas_call(
        _norm_mod_kernel,
        grid=(t // tm,),
        in_specs=[pl.BlockSpec((tm, d), lambda i: (i, 0)),
                  pl.BlockSpec((1, d), lambda i: (0, 0)),
                  pl.BlockSpec((None, 1, d), lambda i: (i // per, 0, 0)),
                  pl.BlockSpec((None, 1, d), lambda i: (i // per, 0, 0))],
        out_specs=pl.BlockSpec((tm, d), lambda i: (i, 0)),
        out_shape=jax.ShapeDtypeStruct((t, d), out_dtype),
        compiler_params=_params(),
        name="norm_modulate",
    )(h, g.reshape(1, d), sc_t, sh_t)


def _relayout_kernel(shift, a_ref, b_ref, o_ref):
    j = pl.program_id(2)
    split = DT_END // TN

    @pl.when(j < split)
    def _():
        o_ref[...] = a_ref[...].astype(o_ref.dtype)

    @pl.when(j == split)
    def _():
        lane = lax.broadcasted_iota(jnp.int32, a_ref.shape, 1)
        o_ref[...] = jnp.where(lane < shift, a_ref[...], 0.0).astype(o_ref.dtype)

    @pl.when(j > split)
    def _():
        cat = jnp.concatenate([a_ref[...], b_ref[...]], axis=1)
        o_ref[...] = cat[:, shift:shift + TN].astype(o_ref.dtype)


def relayout_proj_weights(w_in):
    depth, d, n = w_in.shape
    shift = DT_END % TN
    assert shift + DT_PAD == TN and shift <= 128
    split = DT_END // TN
    tr = d
    return pl.pallas_call(
        functools.partial(_relayout_kernel, shift),
        grid=(depth, d // tr, (n + DT_PAD) // TN),
        in_specs=[pl.BlockSpec((None, tr, TN), lambda l, i, j: (l, i, jnp.where(j <= split, j, j - 1))),
                  pl.BlockSpec((None, tr, 128), lambda l, i, j: (l, i, jnp.where(j <= split, 0, (TN // 128) * j)))],
        out_specs=pl.BlockSpec((None, tr, TN), lambda l, i, j: (l, i, j)),
        out_shape=jax.ShapeDtypeStruct((depth, d, n + DT_PAD), bf16),
        compiler_params=_params(),
        name="relayout_proj_weights",
    )(w_in, w_in)


def _mm_kernel(x_ref, w_ref, o_ref):
    o_ref[...] = jnp.dot(x_ref[...], w_ref[...], preferred_element_type=f32).astype(o_ref.dtype)


def matmul_bf16(x, w, layer, n_out, w_block, out_dtype):
    m, k = x.shape
    return pl.pallas_call(
        _mm_kernel,
        grid=(m // TM, n_out // TN),
        in_specs=[pl.BlockSpec((TM, k), lambda i, j: (i, 0)),
                  pl.BlockSpec((None, k, TN), lambda i, j: (layer, 0, w_block(j)))],
        out_specs=pl.BlockSpec((TM, TN), lambda i, j: (i, j)),
        out_shape=jax.ShapeDtypeStruct((m, n_out), out_dtype),
        compiler_params=_params(),
        name="matmul_bf16",
    )(x, w)


def _merge_kernel(ya_ref, yb_ref, yc_ref, ga_ref, gb_ref, gc_ref, pa_ref, pb_ref, pc_ref, o_ref):
    acc = jax.nn.sigmoid(ga_ref[...]) * jnp.dot(ya_ref[...], pa_ref[...], preferred_element_type=f32)
    acc = acc + jax.nn.sigmoid(gb_ref[...]) * jnp.dot(yb_ref[...], pb_ref[...], preferred_element_type=f32)
    acc = acc + jax.nn.sigmoid(gc_ref[...]) * jnp.dot(yc_ref[...], pc_ref[...], preferred_element_type=f32)
    o_ref[...] = acc.astype(o_ref.dtype)


def branch_merge(ya, yb, yc, proj, pa, pb, pc, layer, g_col0):
    t, k = ya.shape
    n = pa.shape[-1]
    tn = 512
    gb0 = g_col0 // tn
    nj = n // tn
    xs = pl.BlockSpec((TM, k), lambda i, j: (i, 0))
    ws = pl.BlockSpec((None, k, tn), lambda i, j: (layer, 0, j))
    gs = lambda q: pl.BlockSpec((TM, tn), lambda i, j: (i, gb0 + q * nj + j))
    return pl.pallas_call(
        _merge_kernel,
        grid=(t // TM, nj),
        in_specs=[xs, xs, xs, gs(0), gs(1), gs(2), ws, ws, ws],
        out_specs=pl.BlockSpec((TM, tn), lambda i, j: (i, j)),
        out_shape=jax.ShapeDtypeStruct((t, n), bf16),
        compiler_params=_params(),
        name="branch_merge",
    )(ya, yb, yc, proj, proj, proj, pa, pb, pc)


def _resid_mm_kernel(y_ref, w_ref, h_ref, g_ref, o_ref):
    o_ref[...] = h_ref[...] + g_ref[...] * jnp.dot(y_ref[...], w_ref[...], preferred_element_type=f32)


def residual_matmul(y, w, layer, h, gate_t):
    m, k = y.shape
    n = w.shape[-1]
    tn = 512
    return pl.pallas_call(
        _resid_mm_kernel,
        grid=(m // TM, n // tn),
        in_specs=[pl.BlockSpec((TM, k), lambda i, j: (i, 0)),
                  pl.BlockSpec((None, k, tn), lambda i, j: (layer, 0, j)),
                  pl.BlockSpec((TM, tn), lambda i, j: (i, j)),
                  pl.BlockSpec((None, 1, tn), lambda i, j: (i, 0, j))],
        out_specs=pl.BlockSpec((TM, tn), lambda i, j: (i, j)),
        out_shape=jax.ShapeDtypeStruct((m, n), f32),
        compiler_params=_params(),
        name="residual_matmul",
    )(y, w, h, gate_t)


def _router_kernel(v_ref, w_ref, o_ref):
    s = jnp.dot(v_ref[...], w_ref[...], preferred_element_type=f32)
    lane = lax.broadcasted_iota(jnp.int32, s.shape, 1)
    s = jnp.where(lane < N_EXPERTS, s, NEG)
    e = jnp.exp(s - s.max(axis=-1, keepdims=True))
    o_ref[...] = e / e.sum(axis=-1, keepdims=True)


def router_affinity(v, w_router):
    t, d = v.shape
    w = jnp.pad(w_router, ((0, 0), (0, 128 - N_EXPERTS))).astype(bf16)
    return pl.pallas_call(
        _router_kernel,
        grid=(t // TM,),
        in_specs=[pl.BlockSpec((TM, d), lambda i: (i, 0)), pl.BlockSpec((d, 128), lambda i: (0, 0))],
        out_specs=pl.BlockSpec((TM, 128), lambda i: (i, 0)),
        out_shape=jax.ShapeDtypeStruct((t, 128), f32),
        compiler_params=_params(),
        name="router_affinity",
    )(v, w)


def _expert_hidden_kernel(x_ref, w1_ref, w3_ref, o_ref, w1b, w3b):
    @pl.when(pl.program_id(1) == 0)
    def _():
        w1b[...] = w1_ref[...].astype(bf16)
        w3b[...] = w3_ref[...].astype(bf16)

    x = x_ref[...]
    a = jnp.dot(x, w1b[...], preferred_element_type=f32)
    b = jnp.dot(x, w3b[...], preferred_element_type=f32)
    o_ref[...] = ((a * jax.nn.sigmoid(a)) * b).astype(o_ref.dtype)


def expert_hidden(xg, w1, w3, layer, tm):
    e, r, d = xg.shape
    f = w1.shape[-1]
    return pl.pallas_call(
        _expert_hidden_kernel,
        grid=(e, r // tm),
        in_specs=[pl.BlockSpec((None, tm, d), lambda k, i: (k, i, 0)),
                  pl.BlockSpec((None, None, d, f), lambda k, i: (layer, k, 0, 0)),
                  pl.BlockSpec((None, None, d, f), lambda k, i: (layer, k, 0, 0))],
        out_specs=pl.BlockSpec((None, tm, f), lambda k, i: (k, i, 0)),
        out_shape=jax.ShapeDtypeStruct((e, r, f), bf16),
        scratch_shapes=[pltpu.VMEM((d, f), bf16), pltpu.VMEM((d, f), bf16)],
        compiler_params=_params(),
        name="expert_hidden",
    )(xg, w1, w3)


SCATTER_UNROLL = 8


def _combine_kernel(nb, slots, idx_ref, hdn_ref, w2_ref, g_ref, h_ref, g2_ref, *rest):
    o_ref, ye_scr = rest[-2], rest[-1]
    b, e = pl.program_id(0), pl.program_id(2)

    @pl.when(e == 0)
    def _():
        o_ref[...] = jnp.zeros_like(o_ref)

    ye_scr[...] = jnp.dot(hdn_ref[...], w2_ref[...], preferred_element_type=f32) * g_ref[...]
    base0 = (e * nb + b) * slots

    for s0 in range(0, slots, SCATTER_UNROLL):
        rows = [idx_ref[base0 + s0 + k] for k in range(SCATTER_UNROLL)]
        vals = [o_ref[pl.ds(rows[k], 1), :] + ye_scr[s0 + k:s0 + k + 1, :] for k in range(SCATTER_UNROLL)]
        for k in range(SCATTER_UNROLL):
            o_ref[pl.ds(rows[k], 1), :] = vals[k]

    @pl.when(e == pl.num_programs(2) - 1)
    def _():
        o_ref[...] = h_ref[...] + g2_ref[...] * o_ref[...]


def moe_combine(idx, hdn, slot_blk0, slots, gates, w2, layer, h, g2_rows, g2_row0, row_blk0, rows_blk, dq, nb,
                prev=None):
    e, _, f = hdn.shape
    t, d = h.shape
    in_specs = [
        pl.BlockSpec((None, slots, f), lambda b, q, k, idx: (k, slot_blk0 + b, 0)),
        pl.BlockSpec((None, None, f, dq), lambda b, q, k, idx: (layer, k, 0, q)),
        pl.BlockSpec((None, slots, 1), lambda b, q, k, idx: (k, slot_blk0 + b, 0)),
        pl.BlockSpec((rows_blk, dq), lambda b, q, k, idx: (row_blk0 + b, q)),
        pl.BlockSpec((None, 1, dq), lambda b, q, k, idx: (g2_row0 + b, 0, q)),
    ]
    args = [idx, hdn, w2, gates, h, g2_rows]
    aliases = {}
    if prev is not None:
        in_specs.append(pl.BlockSpec(memory_space=pl.ANY))
        args.append(prev)
        aliases = {len(args) - 1: 0}
    return pl.pallas_call(
        functools.partial(_combine_kernel, nb, slots),
        grid_spec=pltpu.PrefetchScalarGridSpec(
            num_scalar_prefetch=1,
            grid=(nb, d // dq, e),
            in_specs=in_specs,
            out_specs=pl.BlockSpec((rows_blk, dq), lambda b, q, k, idx: (row_blk0 + b, q)),
            scratch_shapes=[pltpu.VMEM((slots, dq), f32)],
        ),
        out_shape=jax.ShapeDtypeStruct((t, d), f32),
        input_output_aliases=aliases,
        compiler_params=_params(),
        name="moe_combine",
    )(*args)


def na_bias_blocks(rpb):
    rows = SEQ // GRID_W
    n_dr, n_dc = 2 * NA_WIN_R - 1, 2 * NA_WIN_C - 1
    qc, kc = np.arange(GRID_W)[:, None], np.arange(GRID_W)[None, :]
    cs = np.clip(qc - NA_WIN_C // 2, 0, GRID_W - NA_WIN_C)
    col_ok = (kc >= cs) & (kc < cs + NA_WIN_C)
    pick = (np.arange(n_dc)[:, None, None] == (kc - qc + NA_WIN_C - 1)[None]) & col_ok[None]
    blocks = jnp.einsum('...rd,dqk->...rqk', rpb, jnp.asarray(pick, f32), precision=HI)
    blocks = jnp.where(col_ok, blocks, NEG)
    blocks = jnp.concatenate([blocks, jnp.full(blocks.shape[:-3] + (1, GRID_W, GRID_W), NEG, f32)], axis=-3)
    which = np.full((NA_NCFG, NA_GROUP_ROWS, NA_BAND_ROWS), n_dr, np.int32)
    for c, r0 in enumerate((0, 2, 4, 60, 62)):
        bs = int(np.clip(r0 - NA_WIN_R // 2, 0, rows - NA_BAND_ROWS))
        for qr in range(NA_GROUP_ROWS):
            r = r0 + qr
            rs = int(np.clip(r - NA_WIN_R // 2, 0, rows - NA_WIN_R))
            for kr in range(NA_BAND_ROWS):
                if rs <= bs + kr < rs + NA_WIN_R:
                    which[c, qr, kr] = bs + kr - r + NA_WIN_R - 1
    pairs = sorted({(int(which[c, qr, 2 * m]), int(which[c, qr, 2 * m + 1]))
                    for c in range(NA_NCFG) for qr in range(NA_GROUP_ROWS) for m in range(NA_BAND_ROWS // 2)})
    pair_of = np.array([[[pairs.index((int(which[c, qr, 2 * m]), int(which[c, qr, 2 * m + 1])))
                          for m in range(NA_BAND_ROWS // 2)] for qr in range(NA_GROUP_ROWS)]
                        for c in range(NA_NCFG)], np.int32)
    left = jnp.take(blocks, np.array([p[0] for p in pairs]), axis=-3)
    right = jnp.take(blocks, np.array([p[1] for p in pairs]), axis=-3)
    return jnp.concatenate([left, right], axis=-1), pair_of.reshape(-1)


def _dot_nt(a, b):
    return lax.dot_general(a, b, (((1,), (1,)), ((), ())), preferred_element_type=f32)


def _softmax_av(parts):
    m = parts[0][0].max(axis=-1, keepdims=True)
    for s, _ in parts[1:]:
        m = jnp.maximum(m, s.max(axis=-1, keepdims=True))
    l = None
    o = None
    for s, v in parts:
        p = jnp.exp(s - m)
        ls = p.sum(axis=-1, keepdims=True)
        os_ = jnp.dot(p.astype(bf16), v, preferred_element_type=f32)
        l = ls if l is None else l + ls
        o = os_ if o is None else o + os_
    return o / l


def _na_kernel(pair_ref, q_ref, k_ref, v_ref, kc_ref, vc_ref, tbl_ref, o_ref):
    j = pl.program_id(2)
    npairs = NA_BAND_ROWS // 2

    def bias(hh, cfg):
        rows = []
        for qr in range(NA_GROUP_ROWS):
            base = (cfg * NA_GROUP_ROWS + qr) * npairs
            rows.append(jnp.concatenate([tbl_ref[hh, pair_ref[base + m]] for m in range(npairs)], axis=1))
        return jnp.concatenate(rows, axis=0)

    scale = NA_HEADDIM ** -0.5
    lane = lax.broadcasted_iota(jnp.int32, (1, 2 * NA_HEADDIM), 1)
    first = lane < NA_HEADDIM
    kc = kc_ref[...]
    vc = vc_ref[...]
    gq = NA_GROUP_ROWS * GRID_W

    def head_q(q, hh):
        keep = first if hh == 0 else jnp.logical_not(first)
        return jnp.where(keep, q, jnp.zeros_like(q))

    @pl.when(j < LAT_BLOCKS)
    def _():
        for g in range(TQ // gq):
            r0 = j * (TQ // GRID_W) + NA_GROUP_ROWS * g
            bs = jnp.clip(r0 - NA_WIN_R // 2, 0, SEQ // GRID_W - NA_BAND_ROWS)
            cfg = jnp.where(r0 == 0, 0, jnp.where(r0 == 2, 1, jnp.where(r0 == 60, 3, jnp.where(r0 == 62, 4, 2))))
            start = pl.multiple_of(bs * GRID_W, GRID_W)
            kb = k_ref[pl.ds(start, NA_BAND_ROWS * GRID_W), :]
            vb = v_ref[pl.ds(start, NA_BAND_ROWS * GRID_W), :]
            q = q_ref[pl.ds(g * gq, gq), :]
            outs = []
            for hh in range(2):
                qm = head_q(q, hh)
                s = _dot_nt(qm, kb) * scale + bias(hh, cfg)
                sc = _dot_nt(qm, kc) * scale
                outs.append(_softmax_av([(s, vb), (sc, vc)]))
            o_ref[pl.ds(g * gq, gq), :] = jnp.where(first, outs[0], outs[1]).astype(o_ref.dtype)

    @pl.when(j == LAT_BLOCKS)
    def _():
        q = q_ref[...]
        outs = []
        for hh in range(2):
            sc = _dot_nt(head_q(q, hh), kc) * scale
            outs.append(_softmax_av([(sc, vc)]))
        o_ref[...] = jnp.where(first, outs[0], outs[1]).astype(o_ref.dtype)


def na_attention(qkv, tbl, layer, pair_of, bsz):
    t = qkv.shape[0]
    npair = NA_HEADS // 2
    ctx0 = bsz * LAT_BLOCKS

    def qrow(b, j):
        return jnp.where(j < LAT_BLOCKS, b * LAT_BLOCKS + j, ctx0 + b)

    return pl.pallas_call(
        _na_kernel,
        grid_spec=pltpu.PrefetchScalarGridSpec(
            num_scalar_prefetch=1,
            grid=(bsz, npair, LAT_BLOCKS + 1),
            in_specs=[
                pl.BlockSpec((TQ, 128), lambda b, p, j, po: (qrow(b, j), p)),
                pl.BlockSpec((SEQ, 128), lambda b, p, j, po: (b, npair + p)),
                pl.BlockSpec((SEQ, 128), lambda b, p, j, po: (b, 2 * npair + p)),
                pl.BlockSpec((CTX_LEN, 128), lambda b, p, j, po: (ctx0 + b, npair + p)),
                pl.BlockSpec((CTX_LEN, 128), lambda b, p, j, po: (ctx0 + b, 2 * npair + p)),
                pl.BlockSpec((None, 2) + tbl.shape[2:], lambda b, p, j, po: (layer, p, 0, 0, 0)),
            ],
            out_specs=pl.BlockSpec((TQ, 128), lambda b, p, j, po: (qrow(b, j), p)),
        ),
        out_shape=jax.ShapeDtypeStruct((t, NA_WIDTH), bf16),
        compiler_params=_params(),
        name="na_attention",
    )(jnp.asarray(pair_of), qkv, qkv, qkv, qkv, qkv, tbl)


def _seq_block(d, bsz, lat_blocks):
    ctx0 = bsz * lat_blocks

    def jj(s):
        return (s - 1) if d == 0 else (lat_blocks - s)

    def blk(b, s):
        return jnp.where(s == 0, ctx0 + b, b * lat_blocks + jj(s))

    return jj, blk


def _halo_specs(width, col, blk, t):
    nb8 = TQ // HALO
    return [
        pl.BlockSpec((TQ, width), lambda b, s: (blk(b, s), col)),
        pl.BlockSpec((HALO, width), lambda b, s: (jnp.maximum(blk(b, s) * nb8 - 1, 0), col)),
        pl.BlockSpec((HALO, width), lambda b, s: (jnp.minimum(blk(b, s) * nb8 + nb8, t // HALO - 1), col)),
    ]


def _centred_conv(xe, parts, cw_ref, cb_ref, has_prev, has_next):
    c0 = 0
    for x_ref, xp_ref, xn_ref in parts:
        cols = slice(c0, c0 + x_ref.shape[1])
        xe[0:HALO, cols] = jnp.where(has_prev, xp_ref[...], 0.0)
        xe[HALO:HALO + TQ, cols] = x_ref[...]
        xe[HALO + TQ:2 * HALO + TQ, cols] = jnp.where(has_next, xn_ref[...], 0.0)
        c0 += x_ref.shape[1]
    out = cb_ref[...] + cw_ref[0:1, :] * xe[HALO - 2:HALO - 2 + TQ, :]
    for k in range(1, 4):
        out = out + cw_ref[k:k + 1, :] * xe[HALO - 2 + k:HALO - 2 + k + TQ, :]
    return out


def lru_gate_weights(w_r, w_i):
    def pair(w):
        w = w.reshape(2, LRU_BLOCKS // 2, 2, LRU_BLOCK, LRU_BLOCK)
        z = jnp.zeros_like(w[:, :, 0])
        top = jnp.concatenate([w[:, :, 0], z], axis=-1)
        bot = jnp.concatenate([z, w[:, :, 1]], axis=-1)
        return jnp.concatenate([top, bot], axis=-2)
    return jnp.concatenate([pair(w_r), pair(w_i)], axis=-1).astype(bf16)


def _gelu_tanh(x):
    return 0.5 * x * (1.0 + jnp.tanh(float(np.sqrt(2.0 / np.pi)) * (x + 0.044715 * (x * x * x))))


def _lru_kernel(d, lat_blocks, *refs):
    if d == 0:
        (x_ref, xp_ref, xn_ref, cw_ref, cb_ref, wg_ref, br_ref, bi_ref, lam_ref, o_ref,
         xe, a_scr, u_scr, h_scr) = refs
    else:
        (x_ref, xp_ref, xn_ref, ag_ref, hf_ref, cw_ref, cb_ref, wg_ref, br_ref, bi_ref, lam_ref, o_ref,
         xe, a_scr, u_scr, h_scr, hs_scr) = refs
    s = pl.program_id(1)
    j = (s - 1) if d == 0 else (lat_blocks - s)
    has_prev = jnp.logical_and(s > 0, j > 0)
    has_next = jnp.logical_and(s > 0, j < lat_blocks - 1)

    @pl.when(s == 0)
    def _():
        h_scr[...] = jnp.zeros_like(h_scr)

    xc = _centred_conv(xe, [(x_ref, xp_ref, xn_ref)], cw_ref, cb_ref, has_prev, has_next)
    xcb = xc.astype(bf16)
    sp = jax.nn.softplus(-lam_ref[...])
    for p in range(LRU_WIDTH // 128):
        sl = slice(128 * p, 128 * (p + 1))
        g = jnp.dot(xcb[:, sl], wg_ref[p], preferred_element_type=f32)
        r = jax.nn.sigmoid(g[:, :128] + br_ref[:, sl])
        i = jax.nn.sigmoid(g[:, 128:] + bi_ref[:, sl])
        log_a = -LRU_C * r * sp[:, sl]
        a = jnp.exp(log_a)
        a_scr[:, sl] = a
        u_scr[:, sl] = jnp.sqrt(-jnp.tanh(log_a) * (a * a + 1.0)) * (i * xc[:, sl])

    out = o_ref if d == 0 else hs_scr
    ngrp = TQ // 8

    def body(gi, h):
        base = pl.multiple_of((gi if d == 0 else ngrp - 1 - gi) * 8, 8)
        for t in (range(8) if d == 0 else range(7, -1, -1)):
            h = a_scr[pl.ds(base + t, 1), :] * h + u_scr[pl.ds(base + t, 1), :]
            out[pl.ds(base + t, 1), :] = h
        return h

    h = lax.fori_loop(0, ngrp, body, h_scr[0:1, :])
    h_scr[0:1, :] = h
    if d == 1:
        y = (hf_ref[...] + hs_scr[...]) * _gelu_tanh(ag_ref[...])
        o_ref[...] = y.astype(o_ref.dtype)


def lru_pass(d, proj, hf, conv_w, conv_b, wg, b_r, b_i, lam, bsz, lat_blocks):
    t = proj.shape[0]
    _, blk = _seq_block(d, bsz, lat_blocks)
    row = lambda c: (lambda b, s: (blk(b, s), c))
    in_specs = _halo_specs(LRU_WIDTH, COL_AX, blk, t)
    args = [proj, proj, proj]
    if d == 1:
        in_specs += [pl.BlockSpec((TQ, LRU_WIDTH), row(COL_AG)), pl.BlockSpec((TQ, LRU_WIDTH), row(0))]
        args += [proj, hf]
    const = lambda shape: pl.BlockSpec(shape, lambda b, s: (0,) * len(shape))
    in_specs += [const((4, LRU_WIDTH)), const((1, LRU_WIDTH)), const((LRU_WIDTH // 128, 128, 256)),
                 const((1, LRU_WIDTH)), const((1, LRU_WIDTH)), const((1, LRU_WIDTH))]
    args += [conv_w, conv_b.reshape(1, -1), wg[d], b_r[d].reshape(1, -1), b_i[d].reshape(1, -1), lam[d].reshape(1, -1)]
    scratch = [pltpu.VMEM((TQ + 2 * HALO, LRU_WIDTH), f32), pltpu.VMEM((TQ, LRU_WIDTH), f32),
               pltpu.VMEM((TQ, LRU_WIDTH), f32), pltpu.VMEM((8, LRU_WIDTH), f32)]
    if d == 1:
        scratch.append(pltpu.VMEM((TQ, LRU_WIDTH), f32))
    return pl.pallas_call(
        functools.partial(_lru_kernel, d, lat_blocks),
        grid=(bsz, lat_blocks + 1),
        in_specs=in_specs,
        out_specs=pl.BlockSpec((TQ, LRU_WIDTH), row(0)),
        out_shape=jax.ShapeDtypeStruct((t, LRU_WIDTH), f32 if d == 0 else bf16),
        scratch_shapes=scratch,
        compiler_params=_params(),
        name=f"lru_pass{d}",
    )(*args)


def lru_branch(proj, conv_w, conv_b, w_r, b_r, w_i, b_i, lam, bsz, lat_blocks):
    wg = lru_gate_weights(w_r, w_i)
    hf = lru_pass(0, proj, None, conv_w, conv_b, wg, b_r, b_i, lam, bsz, lat_blocks)
    return lru_pass(1, proj, hf, conv_w, conv_b, wg, b_r, b_i, lam, bsz, lat_blocks)


def rope_tables(seq):
    pos = jnp.arange(seq)
    row = (pos // GRID_W).astype(f32)
    col = (pos % GRID_W).astype(f32)
    n_freq = SSD_STATE // 4
    freqs = ROPE_BASE ** (-jnp.arange(n_freq, dtype=f32) / n_freq)
    ang = jnp.concatenate([row[:, None] * freqs, col[:, None] * freqs], axis=-1)
    cos, sin = jnp.cos(ang), jnp.sin(ang)
    cosf = jnp.concatenate([cos, cos], axis=-1)
    sinf = jnp.concatenate([-sin, sin], axis=-1)
    cosf = jnp.concatenate([cosf, jnp.ones((TQ, SSD_STATE), f32)], axis=0)
    sinf = jnp.concatenate([sinf, jnp.zeros((TQ, SSD_STATE), f32)], axis=0)
    return cosf, sinf


def head_expand_matrix(d):
    e = np.zeros((128, SSD_INNER), np.float32)
    for h in range(SSD_HEADS):
        e[SSD_HEADS * d + h, h * SSD_HEADDIM:(h + 1) * SSD_HEADDIM] = 1.0
    return jnp.asarray(e, bf16)


def _split3(a):
    hi = a.astype(bf16)
    r = a - hi.astype(f32)
    mid = r.astype(bf16)
    return hi, mid, (r - mid.astype(f32)).astype(bf16)


def _dot_exact_rhs01(a, m01):
    return sum(jnp.dot(p, m01, preferred_element_type=f32) for p in _split3(a))


def _dot_exact_lhs01(m01, a):
    return sum(jnp.dot(m01, p, preferred_element_type=f32) for p in _split3(a))


def _ssd_kernel(d, lat_blocks, *refs):
    xs_refs, bc_refs, refs = refs[0:3], refs[3:6], refs[6:]
    if d == 0:
        (dt_ref, cos_ref, sin_ref, cw_ref, cb_ref, dtb_ref, alog_ref, e_ref,
         o_ref, xe, h_scr) = refs
    else:
        (dt_ref, cos_ref, sin_ref, z_ref, y0_ref, cw_ref, cb_ref, dtb_ref, alog_ref, e_ref,
         dsk_ref, ng_ref, o_ref, xe, h_scr, y_scr) = refs
    s = pl.program_id(1)
    j = (s - 1) if d == 0 else (lat_blocks - s)
    has_prev = jnp.logical_and(s > 0, j > 0)
    has_next = jnp.logical_and(s > 0, j < lat_blocks - 1)
    q = SSD_CHUNK

    @pl.when(s == 0)
    def _():
        h_scr[...] = jnp.zeros_like(h_scr)

    xbc = _centred_conv(xe, [xs_refs, bc_refs], cw_ref, cb_ref, has_prev, has_next)
    xbc = xbc * jax.nn.sigmoid(xbc)
    xs = xbc[:, :SSD_INNER]
    cosf, sinf = cos_ref[...], sin_ref[...]

    def rope(g, off):
        v = xbc[:, off + g * SSD_STATE: off + (g + 1) * SSD_STATE]
        return (v * cosf + pltpu.roll(v, SSD_STATE // 2, 1) * sinf).astype(bf16)

    bm = [rope(g, SSD_INNER) for g in range(SSD_GROUPS)]
    cm = [rope(g, SSD_INNER + SSD_GROUPS * SSD_STATE) for g in range(SSD_GROUPS)]
    dt = jax.nn.softplus(dt_ref[...] + dtb_ref[...])
    delta = dt * (-jnp.exp(alog_ref[...]))
    ri = lax.broadcasted_iota(jnp.int32, (q, q), 0)
    ci = lax.broadcasted_iota(jnp.int32, (q, q), 1)
    keep = (ci <= ri) if d == 0 else (ci >= ri)
    tri = jnp.where(keep, 1.0, 0.0).astype(bf16)
    lane = lax.broadcasted_iota(jnp.int32, (1, 2 * SSD_HEADDIM), 1)
    halves = (lane < SSD_HEADDIM, lane >= SSD_HEADDIM)
    e = e_ref[...]
    last = q - 1 if d == 0 else 0
    out = o_ref if d == 0 else y_scr

    for c in (range(TQ // q) if d == 0 else range(TQ // q - 1, -1, -1)):
        rows = slice(c * q, (c + 1) * q)
        at = _dot_exact_lhs01(tri, delta[rows])
        at_exp = _dot_exact_rhs01(at, e)
        dt_exp = _dot_exact_rhs01(dt[rows], e)
        tot_exp = at_exp[last:last + 1, :]
        xdt = xs[rows] * dt_exp
        xd = (xdt * jnp.exp(tot_exp - at_exp)).astype(bf16)
        eat = jnp.exp(at_exp)
        cdec = jnp.exp(tot_exp)
        at_row = at.T
        ys = []
        for g in range(SSD_GROUPS):
            bg, cg = bm[g][rows], cm[g][rows]
            cb = _dot_nt(cg, bg)
            ht = h_scr[g]
            yoff = jnp.dot(cg, ht.astype(bf16), preferred_element_type=f32) * eat[:, g * GW:(g + 1) * GW]
            for pp in range(2):
                xpair = xdt[:, g * GW + pp * 128: g * GW + (pp + 1) * 128]
                acc = yoff[:, pp * 128:(pp + 1) * 128]
                for hh in range(2):
                    li = SSD_HEADS * d + 4 * g + 2 * pp + hh
                    seg = at[:, li:li + 1] - at_row[li:li + 1, :]
                    m = (cb * jnp.exp(jnp.where(keep, seg, NEG))).astype(bf16)
                    xm = jnp.where(halves[hh], xpair, 0.0).astype(bf16)
                    acc = acc + jnp.dot(m, xm, preferred_element_type=f32)
                ys.append(acc)
            upd = lax.dot_general(bg, xd[:, g * GW:(g + 1) * GW], (((0,), (0,)), ((), ())),
                                  preferred_element_type=f32)
            h_scr[g] = cdec[:, g * GW:(g + 1) * GW] * ht + upd
        out[rows, :] = jnp.concatenate(ys, axis=-1)

    if d == 1:
        y = y0_ref[...] + y_scr[...] + dsk_ref[...] * xs
        z = z_ref[...]
        y = y * (z * jax.nn.sigmoid(z))
        y = y * lax.rsqrt(jnp.mean(y * y, axis=-1, keepdims=True) + EPS)
        o_ref[...] = (y * ng_ref[...]).astype(o_ref.dtype)


def ssd_pass(d, proj, y0, cosf, sinf, conv_w, conv_b, a_log, dt_bias, d_skip, norm_g, bsz, lat_blocks):
    t = proj.shape[0]
    jj, blk = _seq_block(d, bsz, lat_blocks)
    row = lambda c: (lambda b, s: (blk(b, s), c))
    tbl = lambda b, s: (jnp.where(s == 0, lat_blocks, jj(s)), 0)
    const = lambda shape: pl.BlockSpec(shape, lambda b, s: (0,) * len(shape))
    pad128 = lambda v: jnp.pad(v.reshape(1, -1).astype(f32), ((0, 0), (0, 128 - v.size)))
    in_specs = _halo_specs(SSD_INNER, COL_XS, blk, t) + _halo_specs(SSD_CONV_DIM - SSD_INNER, COL_BC, blk, t) + [
        pl.BlockSpec((TQ, 128), row(COL_DT)),
        pl.BlockSpec((TQ, SSD_STATE), tbl),
        pl.BlockSpec((TQ, SSD_STATE), tbl),
    ]
    args = [proj] * 7 + [cosf, sinf]
    if d == 1:
        in_specs += [pl.BlockSpec((TQ, SSD_INNER), row(COL_Z)), pl.BlockSpec((TQ, SSD_INNER), row(0))]
        args += [proj, y0]
    in_specs += [const((4, SSD_CONV_DIM)), const((1, SSD_CONV_DIM)), const((1, 128)), const((1, 128)),
                 const((128, SSD_INNER))]
    args += [conv_w, conv_b.reshape(1, -1), pad128(dt_bias), pad128(a_log), head_expand_matrix(d)]
    scratch = [pltpu.VMEM((TQ + 2 * HALO, SSD_CONV_DIM), f32), pltpu.VMEM((SSD_GROUPS, SSD_STATE, GW), f32)]
    if d == 1:
        in_specs += [const((1, SSD_INNER)), const((1, SSD_INNER))]
        args += [jnp.repeat(d_skip, SSD_HEADDIM).reshape(1, -1), norm_g.reshape(1, -1)]
        scratch.append(pltpu.VMEM((TQ, SSD_INNER), f32))
    return pl.pallas_call(
        functools.partial(_ssd_kernel, d, lat_blocks),
        grid=(bsz, lat_blocks + 1),
        in_specs=in_specs,
        out_specs=pl.BlockSpec((TQ, SSD_INNER), row(0)),
        out_shape=jax.ShapeDtypeStruct((t, SSD_INNER), f32 if d == 0 else bf16),
        scratch_shapes=scratch,
        compiler_params=_params(),
        name=f"ssd_pass{d}",
    )(*args)


def ssd_branch(proj, cosf, sinf, conv_w, conv_b, a_log, dt_bias, d_skip, norm_g, bsz, lat_blocks):
    y0 = ssd_pass(0, proj, None, cosf, sinf, conv_w, conv_b, a_log, dt_bias, d_skip, norm_g, bsz, lat_blocks)
    return ssd_pass(1, proj, y0, cosf, sinf, conv_w, conv_b, a_log, dt_bias, d_skip, norm_g, bsz, lat_blocks)


def _tile_rows(mod_rows, bsz):
    idx = np.concatenate([np.repeat(np.arange(bsz), SEQ // TM), np.full(bsz * CTX_LEN // TM, bsz)])
    return mod_rows[idx][:, None, :]


def _expert_choice_moe(h, v, w_router, w1, w3, w2, layer, g2_rows, bsz):
    t, d = v.shape
    aff = router_affinity(v, w_router)[:, :N_EXPERTS]
    nl = bsz * SEQ

    def choose(a, length):
        cap = CAPACITY_FACTOR * length // N_EXPERTS
        g, idx = lax.top_k(jnp.swapaxes(a.reshape(bsz, length, N_EXPERTS), 1, 2), cap)
        return jnp.swapaxes(g, 0, 1), jnp.swapaxes(idx, 0, 1), cap

    g_l, i_l, cap_l = choose(aff[:nl], SEQ)
    g_c, i_c, cap_c = choose(aff[nl:], CTX_LEN)
    boff = jnp.arange(bsz)[None, :, None]
    ctx_rows = (i_c + boff * CTX_LEN).reshape(N_EXPERTS, -1)
    rows = jnp.concatenate([(i_l + boff * SEQ).reshape(N_EXPERTS, -1), ctx_rows + nl], axis=1)
    gates = jnp.concatenate([g_l.reshape(N_EXPERTS, -1), g_c.reshape(N_EXPERTS, -1)], axis=1)[..., None]
    r = rows.shape[1]
    xg = jnp.take(v, rows.reshape(-1), axis=0).reshape(N_EXPERTS, r, d)
    hdn = expert_hidden(xg, w1, w3, layer, r // 4)
    ctx_slots, ctx_tile = bsz * cap_c, bsz * CTX_LEN
    assert (bsz * cap_l) % ctx_slots == 0 and nl % ctx_tile == 0
    out = moe_combine(i_l.reshape(-1), hdn, 0, cap_l, gates, w2, layer, h, g2_rows, 0, 0, SEQ, TN, bsz)
    return moe_combine(ctx_rows.reshape(-1), hdn, bsz * cap_l // ctx_slots, ctx_slots, gates, w2, layer, h, g2_rows,
                       bsz, nl // ctx_tile, ctx_tile, d, 1, prev=out)


def kernel(x, c, ctx, c_ctx, w_ada, b_ada, norm_mix, norm_ffn, w_in, lru_conv_w, lru_conv_b, lru_w_r, lru_b_r,
           lru_w_i, lru_b_i, lru_lambda, ssd_conv_w, ssd_conv_b, ssd_a_log, ssd_dt_bias, ssd_d, ssd_norm, na_rpb,
           w_branch_lru, w_branch_ssd, w_branch_na, w_out, w_router, w1, w3, w2, norm_final):
    bsz = x.shape[0]
    assert x.shape[1:] == (SEQ, D_MODEL) and ctx.shape[1:] == (CTX_LEN, D_MODEL) and bsz * CTX_LEN == TM
    d = D_MODEL
    h = jnp.concatenate([x.reshape(bsz * SEQ, d), ctx.reshape(bsz * CTX_LEN, d)], axis=0)
    t = h.shape[0]

    cond = jnp.concatenate([c, c_ctx[None, :], jnp.zeros((8 - bsz - 1, d), f32)], axis=0)
    mod = ada_modulation(cond, w_ada, b_ada)
    cosf, sinf = rope_tables(SEQ)

    assert DT_END + DT_PAD == W_QKV0 and w_in.shape[-1] == DT_END + 3 * NA_WIDTH + 3 * d
    w_proj = relayout_proj_weights(w_in)
    qkv_blocks = 3 * NA_WIDTH // TN
    pa, pb, pc, wo, w2b = (w.astype(bf16) for w in (w_branch_lru, w_branch_ssd, w_branch_na, w_out, w2))
    na_tbl, na_pair_of = na_bias_blocks(na_rpb)

    for l in range(DEPTH):
        mods = jnp.split(mod[l], 6, axis=-1)
        sh1, sc1, g1, sh2, sc2 = (_tile_rows(m, bsz) for m in mods[:5])
        u = norm_modulate(h, norm_mix[l], sc1, sh1, bf16)
        proj = matmul_bf16(u, w_proj, l, N_F32, lambda j: jnp.where(j < W_QKV0 // TN, j, j + qkv_blocks), f32)
        qkv = matmul_bf16(u, w_proj, l, 3 * NA_WIDTH, lambda j: j + W_QKV0 // TN, bf16)
        ya = lru_branch(proj, lru_conv_w[l], lru_conv_b[l], lru_w_r[l], lru_b_r[l], lru_w_i[l], lru_b_i[l],
                        lru_lambda[l], bsz, LAT_BLOCKS)
        yb = ssd_branch(proj, cosf, sinf, ssd_conv_w[l], ssd_conv_b[l], ssd_a_log[l], ssd_dt_bias[l], ssd_d[l],
                        ssd_norm[l], bsz, LAT_BLOCKS)
        yc = na_attention(qkv, na_tbl, l, na_pair_of, bsz)
        y = branch_merge(ya, yb, yc, proj, pa, pb, pc, l, COL_G)
        h = residual_matmul(y, wo, l, h, g1)
        v = norm_modulate(h, norm_ffn[l], sc2, sh2, bf16)
        h = _expert_choice_moe(h, v, w_router[l], w1, w3, w2b, l, mods[5][:, None, :], bsz)

    zeros = jnp.zeros((t // TM, 1, d), f32)
    out = norm_modulate(h, norm_final, zeros, zeros, f32)
    return out[:bsz * SEQ].reshape(bsz, SEQ, d)
```

```python
import functools

import jax
import jax.numpy as jnp
import numpy as np
from jax import lax
from jax.experimental import pallas as pl
from jax.experimental.pallas import tpu as pltpu

D_MODEL = 2048
SEQ = 4096
CTX_LEN = 256
DEPTH = 4
GRID_W = 64
EPS = 1e-6
ROPE_BASE = 10000.0
LRU_WIDTH = 1024
LRU_BLOCKS = 16
LRU_BLOCK = LRU_WIDTH // LRU_BLOCKS
LRU_C = 8.0
SSD_INNER = 1024
SSD_HEADDIM = 64
SSD_HEADS = SSD_INNER // SSD_HEADDIM
SSD_GROUPS = 4
SSD_STATE = 128
SSD_CHUNK = 128
SSD_CONV_DIM = SSD_INNER + 2 * SSD_GROUPS * SSD_STATE
NA_HEADS = 16
NA_HEADDIM = 64
NA_WIDTH = NA_HEADS * NA_HEADDIM
NA_WIN_R = 8
NA_WIN_C = 16
N_EXPERTS = 16
EXPERT_FF = 1024
CAPACITY_FACTOR = 2

V7X_VMEM_LIMIT = 56 * 1024 * 1024
TM = 1024
TQ = 256
HALO = 8
LAT_BLOCKS = SEQ // TQ
NEG = -1e30
HI = lax.Precision.HIGHEST
bf16 = jnp.bfloat16
f32 = jnp.float32

TN = 512
DT_END = 2 * LRU_WIDTH + SSD_INNER + SSD_CONV_DIM + 2 * SSD_HEADS
DT_PAD = 480
COL_AX, COL_AG, COL_Z, COL_XS, COL_BC = 0, 1, 2, 3, 4
COL_DT = 40
W_QKV0 = 5632
COL_G = 5632
N_F32 = COL_G + 3 * D_MODEL
GW = SSD_INNER // SSD_GROUPS

NA_GROUP_ROWS = 2
NA_BAND_ROWS = 10
NA_NCFG = 5


def _params():
    return pltpu.CompilerParams(vmem_limit_bytes=V7X_VMEM_LIMIT)


def _ada_kernel(x_ref, w_ref, b_ref, o_ref):
    c = x_ref[...]
    x = (c * jax.nn.sigmoid(c)).astype(bf16)
    o_ref[...] = jnp.dot(x, w_ref[...].astype(bf16), preferred_element_type=f32) + b_ref[...]


def ada_modulation(cond, w_ada, b_ada):
    depth, d, n = w_ada.shape
    tn = 1024
    return pl.pallas_call(
        _ada_kernel,
        grid=(depth, n // tn),
        in_specs=[pl.BlockSpec((8, d), lambda l, j: (0, 0)),
                  pl.BlockSpec((None, d, tn), lambda l, j: (l, 0, j)),
                  pl.BlockSpec((None, 1, tn), lambda l, j: (l, 0, j))],
        out_specs=pl.BlockSpec((None, 8, tn), lambda l, j: (l, 0, j)),
        out_shape=jax.ShapeDtypeStruct((depth, 8, n), f32),
        compiler_params=_params(),
        name="ada_modulation",
    )(cond, w_ada, b_ada.reshape(depth, 1, n))


def _norm_mod_kernel(h_ref, g_ref, sc_ref, sh_ref, o_ref):
    x = h_ref[...]
    y = x * lax.rsqrt(jnp.mean(x * x, axis=-1, keepdims=True) + EPS)
    o_ref[...] = ((y * g_ref[...]) * (1.0 + sc_ref[...]) + sh_ref[...]).astype(o_ref.dtype)


def norm_modulate(h, g, sc_t, sh_t, out_dtype):
    t, d = h.shape
    tm = min(512, TM)
    per = TM // tm
    return pl.pallas_call(
        _norm_mod_kernel,
        grid=(t // tm,),
        in_specs=[pl.BlockSpec((tm, d), lambda i: (i, 0)),
                  pl.BlockSpec((1, d), lambda i: (0, 0)),
                  pl.BlockSpec((None, 1, d), lambda i: (i // per, 0, 0)),
                  pl.BlockSpec((None, 1, d), lambda i: (i // per, 0, 0))],
        out_specs=pl.BlockSpec((tm, d), lambda i: (i, 0)),
        out_shape=jax.ShapeDtypeStruct((t, d), out_dtype),
        compiler_params=_params(),
        name="norm_modulate",
    )(h, g.reshape(1, d), sc_t, sh_t)


def _relayout_kernel(shift, a_ref, b_ref, o_ref):
    j = pl.program_id(2)
    split = DT_END // TN

    @pl.when(j < split)
    def _():
        o_ref[...] = a_ref[...].astype(o_ref.dtype)

    @pl.when(j == split)
    def _():
        lane = lax.broadcasted_iota(jnp.int32, a_ref.shape, 1)
        o_ref[...] = jnp.where(lane < shift, a_ref[...], 0.0).astype(o_ref.dtype)

    @pl.when(j > split)
    def _():
        cat = jnp.concatenate([a_ref[...], b_ref[...]], axis=1)
        o_ref[...] = cat[:, shift:shift + TN].astype(o_ref.dtype)


def relayout_proj_weights(w_in):
    depth, d, n = w_in.shape
    shift = DT_END % TN
    assert shift + DT_PAD == TN and shift <= 128
    split = DT_END // TN
    tr = d
    return pl.pallas_call(
        functools.partial(_relayout_kernel, shift),
        grid=(depth, d // tr, (n + DT_PAD) // TN),
        in_specs=[pl.BlockSpec((None, tr, TN), lambda l, i, j: (l, i, jnp.where(j <= split, j, j - 1))),
                  pl.BlockSpec((None, tr, 128), lambda l, i, j: (l, i, jnp.where(j <= split, 0, (TN // 128) * j)))],
        out_specs=pl.BlockSpec((None, tr, TN), lambda l, i, j: (l, i, j)),
        out_shape=jax.ShapeDtypeStruct((depth, d, n + DT_PAD), bf16),
        compiler_params=_params(),
        name="relayout_proj_weights",
    )(w_in, w_in)


def _mm_kernel(x_ref, w_ref, o_ref):
    o_ref[...] = jnp.dot(x_ref[...], w_ref[...], preferred_element_type=f32).astype(o_ref.dtype)


def matmul_bf16(x, w, layer, n_out, w_block, out_dtype):
    m, k = x.shape
    return pl.pallas_call(
        _mm_kernel,
        grid=(m // TM, n_out // TN),
        in_specs=[pl.BlockSpec((TM, k), lambda i, j: (i, 0)),
                  pl.BlockSpec((None, k, TN), lambda i, j: (layer, 0, w_block(j)))],
        out_specs=pl.BlockSpec((TM, TN), lambda i, j: (i, j)),
        out_shape=jax.ShapeDtypeStruct((m, n_out), out_dtype),
        compiler_params=_params(),
        name="matmul_bf16",
    )(x, w)


def _merge_kernel(ya_ref, yb_ref, yc_ref, ga_ref, gb_ref, gc_ref, pa_ref, pb_ref, pc_ref, o_ref):
    acc = jax.nn.sigmoid(ga_ref[...]) * jnp.dot(ya_ref[...], pa_ref[...], preferred_element_type=f32)
    acc = acc + jax.nn.sigmoid(gb_ref[...]) * jnp.dot(yb_ref[...], pb_ref[...], preferred_element_type=f32)
    acc = acc + jax.nn.sigmoid(gc_ref[...]) * jnp.dot(yc_ref[...], pc_ref[...], preferred_element_type=f32)
    o_ref[...] = acc.astype(o_ref.dtype)


def branch_merge(ya, yb, yc, proj, pa, pb, pc, layer, g_col0):
    t, k = ya.shape
    n = pa.shape[-1]
    tn = 512
    gb0 = g_col0 // tn
    nj = n // tn
    xs = pl.BlockSpec((TM, k), lambda i, j: (i, 0))
    ws = pl.BlockSpec((None, k, tn), lambda i, j: (layer, 0, j))
    gs = lambda q: pl.BlockSpec((TM, tn), lambda i, j: (i, gb0 + q * nj + j))
    return pl.pallas_call(
        _merge_kernel,
        grid=(t // TM, nj),
        in_specs=[xs, xs, xs, gs(0), gs(1), gs(2), ws, ws, ws],
        out_specs=pl.BlockSpec((TM, tn), lambda i, j: (i, j)),
        out_shape=jax.ShapeDtypeStruct((t, n), bf16),
        compiler_params=_params(),
        name="branch_merge",
    )(ya, yb, yc, proj, proj, proj, pa, pb, pc)


def _resid_mm_kernel(y_ref, w_ref, h_ref, g_ref, o_ref):
    o_ref[...] = h_ref[...] + g_ref[...] * jnp.dot(y_ref[...], w_ref[...], preferred_element_type=f32)


def residual_matmul(y, w, layer, h, gate_t):
    m, k = y.shape
    n = w.shape[-1]
    tn = 512
    return pl.pallas_call(
        _resid_mm_kernel,
        grid=(m // TM, n // tn),
        in_specs=[pl.BlockSpec((TM, k), lambda i, j: (i, 0)),
                  pl.BlockSpec((None, k, tn), lambda i, j: (layer, 0, j)),
                  pl.BlockSpec((TM, tn), lambda i, j: (i, j)),
                  pl.BlockSpec((None, 1, tn), lambda i, j: (i, 0, j))],
        out_specs=pl.BlockSpec((TM, tn), lambda i, j: (i, j)),
        out_shape=jax.ShapeDtypeStruct((m, n), f32),
        compiler_params=_params(),
        name="residual_matmul",
    )(y, w, h, gate_t)


def _router_kernel(v_ref, w_ref, o_ref):
    s = jnp.dot(v_ref[...], w_ref[...], preferred_element_type=f32)
    lane = lax.broadcasted_iota(jnp.int32, s.shape, 1)
    s = jnp.where(lane < N_EXPERTS, s, NEG)
    e = jnp.exp(s - s.max(axis=-1, keepdims=True))
    o_ref[...] = e / e.sum(axis=-1, keepdims=True)


def router_affinity(v, w_router):
    t, d = v.shape
    w = jnp.pad(w_router, ((0, 0), (0, 128 - N_EXPERTS))).astype(bf16)
    return pl.pallas_call(
        _router_kernel,
        grid=(t // TM,),
        in_specs=[pl.BlockSpec((TM, d), lambda i: (i, 0)), pl.BlockSpec((d, 128), lambda i: (0, 0))],
        out_specs=pl.BlockSpec((TM, 128), lambda i: (i, 0)),
        out_shape=jax.ShapeDtypeStruct((t, 128), f32),
        compiler_params=_params(),
        name="router_affinity",
    )(v, w)


def _expert_hidden_kernel(x_ref, w1_ref, w3_ref, o_ref, w1b, w3b):
    @pl.when(pl.program_id(1) == 0)
    def _():
        w1b[...] = w1_ref[...].astype(bf16)
        w3b[...] = w3_ref[...].astype(bf16)

    x = x_ref[...]
    a = jnp.dot(x, w1b[...], preferred_element_type=f32)
    b = jnp.dot(x, w3b[...], preferred_element_type=f32)
    o_ref[...] = ((a * jax.nn.sigmoid(a)) * b).astype(o_ref.dtype)


def expert_hidden(xg, w1, w3, layer, tm):
    e, r, d = xg.shape
    f = w1.shape[-1]
    return pl.pallas_call(
        _expert_hidden_kernel,
        grid=(e, r // tm),
        in_specs=[pl.BlockSpec((None, tm, d), lambda k, i: (k, i, 0)),
                  pl.BlockSpec((None, None, d, f), lambda k, i: (layer, k, 0, 0)),
                  pl.BlockSpec((None, None, d, f), lambda k, i: (layer, k, 0, 0))],
        out_specs=pl.BlockSpec((None, tm, f), lambda k, i: (k, i, 0)),
        out_shape=jax.ShapeDtypeStruct((e, r, f), bf16),
        scratch_shapes=[pltpu.VMEM((d, f), bf16), pltpu.VMEM((d, f), bf16)],
        compiler_params=_params(),
        name="expert_hidden",
    )(xg, w1, w3)


SCATTER_UNROLL = 8


def _combine_kernel(nb, slots, idx_ref, hdn_ref, w2_ref, g_ref, h_ref, g2_ref, *rest):
    o_ref, ye_scr = rest[-2], rest[-1]
    b, e = pl.program_id(0), pl.program_id(2)

    @pl.when(e == 0)
    def _():
        o_ref[...] = jnp.zeros_like(o_ref)

    ye_scr[...] = jnp.dot(hdn_ref[...], w2_ref[...], preferred_element_type=f32) * g_ref[...]
    base0 = (e * nb + b) * slots

    for s0 in range(0, slots, SCATTER_UNROLL):
        rows = [idx_ref[base0 + s0 + k] for k in range(SCATTER_UNROLL)]
        vals = [o_ref[pl.ds(rows[k], 1), :] + ye_scr[s0 + k:s0 + k + 1, :] for k in range(SCATTER_UNROLL)]
        for k in range(SCATTER_UNROLL):
            o_ref[pl.ds(rows[k], 1), :] = vals[k]

    @pl.when(e == pl.num_programs(2) - 1)
    def _():
        o_ref[...] = h_ref[...] + g2_ref[...] * o_ref[...]


def moe_combine(idx, hdn, slot_blk0, slots, gates, w2, layer, h, g2_rows, g2_row0, row_blk0, rows_blk, dq, nb,
                prev=None):
    e, _, f = hdn.shape
    t, d = h.shape
    in_specs = [
        pl.BlockSpec((None, slots, f), lambda b, q, k, idx: (k, slot_blk0 + b, 0)),
        pl.BlockSpec((None, None, f, dq), lambda b, q, k, idx: (layer, k, 0, q)),
        pl.BlockSpec((None, slots, 1), lambda b, q, k, idx: (k, slot_blk0 + b, 0)),
        pl.BlockSpec((rows_blk, dq), lambda b, q, k, idx: (row_blk0 + b, q)),
        pl.BlockSpec((None, 1, dq), lambda b, q, k, idx: (g2_row0 + b, 0, q)),
    ]
    args = [idx, hdn, w2, gates, h, g2_rows]
    aliases = {}
    if prev is not None:
        in_specs.append(pl.BlockSpec(memory_space=pl.ANY))
        args.append(prev)
        aliases = {len(args) - 1: 0}
    return pl.pallas_call(
        functools.partial(_combine_kernel, nb, slots),
        grid_spec=pltpu.PrefetchScalarGridSpec(
            num_scalar_prefetch=1,
            grid=(nb, d // dq, e),
            in_specs=in_specs,
            out_specs=pl.BlockSpec((rows_blk, dq), lambda b, q, k, idx: (row_blk0 + b, q)),
            scratch_shapes=[pltpu.VMEM((slots, dq), f32)],
        ),
        out_shape=jax.ShapeDtypeStruct((t, d), f32),
        input_output_aliases=aliases,
        compiler_params=_params(),
        name="moe_combine",
    )(*args)


def na_bias_blocks(rpb):
    rows = SEQ // GRID_W
    n_dr, n_dc = 2 * NA_WIN_R - 1, 2 * NA_WIN_C - 1
    qc, kc = np.arange(GRID_W)[:, None], np.arange(GRID_W)[None, :]
    cs = np.clip(qc - NA_WIN_C // 2, 0, GRID_W - NA_WIN_C)
    col_ok = (kc >= cs) & (kc < cs + NA_WIN_C)
    pick = (np.arange(n_dc)[:, None, None] == (kc - qc + NA_WIN_C - 1)[None]) & col_ok[None]
    blocks = jnp.einsum('...rd,dqk->...rqk', rpb, jnp.asarray(pick, f32), precision=HI)
    blocks = jnp.where(col_ok, blocks, NEG)
    blocks = jnp.concatenate([blocks, jnp.full(blocks.shape[:-3] + (1, GRID_W, GRID_W), NEG, f32)], axis=-3)
    which = np.full((NA_NCFG, NA_GROUP_ROWS, NA_BAND_ROWS), n_dr, np.int32)
    for c, r0 in enumerate((0, 2, 4, 60, 62)):
        bs = int(np.clip(r0 - NA_WIN_R // 2, 0, rows - NA_BAND_ROWS))
        for qr in range(NA_GROUP_ROWS):
            r = r0 + qr
            rs = int(np.clip(r - NA_WIN_R // 2, 0, rows - NA_WIN_R))
            for kr in range(NA_BAND_ROWS):
                if rs <= bs + kr < rs + NA_WIN_R:
                    which[c, qr, kr] = bs + kr - r + NA_WIN_R - 1
    pairs = sorted({(int(which[c, qr, 2 * m]), int(which[c, qr, 2 * m + 1]))
                    for c in range(NA_NCFG) for qr in range(NA_GROUP_ROWS) for m in range(NA_BAND_ROWS // 2)})
    pair_of = np.array([[[pairs.index((int(which[c, qr, 2 * m]), int(which[c, qr, 2 * m + 1])))
                          for m in range(NA_BAND_ROWS // 2)] for qr in range(NA_GROUP_ROWS)]
                        for c in range(NA_NCFG)], np.int32)
    left = jnp.take(blocks, np.array([p[0] for p in pairs]), axis=-3)
    right = jnp.take(blocks, np.array([p[1] for p in pairs]), axis=-3)
    return jnp.concatenate([left, right], axis=-1), pair_of.reshape(-1)


def _dot_nt(a, b):
    return lax.dot_general(a, b, (((1,), (1,)), ((), ())), preferred_element_type=f32)


def _softmax_av(parts):
    m = parts[0][0].max(axis=-1, keepdims=True)
    for s, _ in parts[1:]:
        m = jnp.maximum(m, s.max(axis=-1, keepdims=True))
    l = None
    o = None
    for s, v in parts:
        p = jnp.exp(s - m)
        ls = p.sum(axis=-1, keepdims=True)
        os_ = jnp.dot(p.astype(bf16), v, preferred_element_type=f32)
        l = ls if l is None else l + ls
        o = os_ if o is None else o + os_
    return o / l


def _na_kernel(pair_ref, q_ref, k_ref, v_ref, kc_ref, vc_ref, tbl_ref, o_ref):
    j = pl.program_id(2)
    npairs = NA_BAND_ROWS // 2

    def bias(hh, cfg):
        rows = []
        for qr in range(NA_GROUP_ROWS):
            base = (cfg * NA_GROUP_ROWS + qr) * npairs
            rows.append(jnp.concatenate([tbl_ref[hh, pair_ref[base + m]] for m in range(npairs)], axis=1))
        return jnp.concatenate(rows, axis=0)

    scale = NA_HEADDIM ** -0.5
    assert scale == 2.0 ** round(np.log2(scale))
    lane = lax.broadcasted_iota(jnp.int32, (1, 2 * NA_HEADDIM), 1)
    first = lane < NA_HEADDIM
    kc = kc_ref[...]
    vc = vc_ref[...]
    gq = NA_GROUP_ROWS * GRID_W

    def head_q(q, hh):
        keep = first if hh == 0 else jnp.logical_not(first)
        return jnp.where(keep, q * scale, jnp.zeros_like(q))

    @pl.when(j < LAT_BLOCKS)
    def _():
        for g in range(TQ // gq):
            r0 = j * (TQ // GRID_W) + NA_GROUP_ROWS * g
            bs = jnp.clip(r0 - NA_WIN_R // 2, 0, SEQ // GRID_W - NA_BAND_ROWS)
            cfg = jnp.where(r0 == 0, 0, jnp.where(r0 == 2, 1, jnp.where(r0 == 60, 3, jnp.where(r0 == 62, 4, 2))))
            start = pl.multiple_of(bs * GRID_W, GRID_W)
            kb = k_ref[pl.ds(start, NA_BAND_ROWS * GRID_W), :]
            vb = v_ref[pl.ds(start, NA_BAND_ROWS * GRID_W), :]
            q = q_ref[pl.ds(g * gq, gq), :]
            q2 = jnp.concatenate([head_q(q, 0), head_q(q, 1)], axis=0)
            s = _dot_nt(q2, kb) + jnp.concatenate([bias(0, cfg), bias(1, cfg)], axis=0)
            sc = _dot_nt(q2, kc)
            o2 = _softmax_av([(s, vb), (sc, vc)])
            o_ref[pl.ds(g * gq, gq), :] = jnp.where(first, o2[:gq], o2[gq:]).astype(o_ref.dtype)

    @pl.when(j == LAT_BLOCKS)
    def _():
        q = q_ref[...]
        q2 = jnp.concatenate([head_q(q, 0), head_q(q, 1)], axis=0)
        o2 = _softmax_av([(_dot_nt(q2, kc), vc)])
        o_ref[...] = jnp.where(first, o2[:CTX_LEN], o2[CTX_LEN:]).astype(o_ref.dtype)


def na_attention(qkv, tbl, layer, pair_of, bsz):
    t = qkv.shape[0]
    npair = NA_HEADS // 2
    ctx0 = bsz * LAT_BLOCKS

    def qrow(b, j):
        return jnp.where(j < LAT_BLOCKS, b * LAT_BLOCKS + j, ctx0 + b)

    return pl.pallas_call(
        _na_kernel,
        grid_spec=pltpu.PrefetchScalarGridSpec(
            num_scalar_prefetch=1,
            grid=(bsz, npair, LAT_BLOCKS + 1),
            in_specs=[
                pl.BlockSpec((TQ, 128), lambda b, p, j, po: (qrow(b, j), p)),
                pl.BlockSpec((SEQ, 128), lambda b, p, j, po: (b, npair + p)),
                pl.BlockSpec((SEQ, 128), lambda b, p, j, po: (b, 2 * npair + p)),
                pl.BlockSpec((CTX_LEN, 128), lambda b, p, j, po: (ctx0 + b, npair + p)),
                pl.BlockSpec((CTX_LEN, 128), lambda b, p, j, po: (ctx0 + b, 2 * npair + p)),
                pl.BlockSpec((None, 2) + tbl.shape[2:], lambda b, p, j, po: (layer, p, 0, 0, 0)),
            ],
            out_specs=pl.BlockSpec((TQ, 128), lambda b, p, j, po: (qrow(b, j), p)),
        ),
        out_shape=jax.ShapeDtypeStruct((t, NA_WIDTH), bf16),
        compiler_params=_params(),
        name="na_attention",
    )(jnp.asarray(pair_of), qkv, qkv, qkv, qkv, qkv, tbl)


def _seq_block(d, bsz, lat_blocks):
    ctx0 = bsz * lat_blocks

    def jj(s):
        return (s - 1) if d == 0 else (lat_blocks - s)

    def blk(b, s):
        return jnp.where(s == 0, ctx0 + b, b * lat_blocks + jj(s))

    return jj, blk


def _halo_specs(width, col, blk, t):
    nb8 = TQ // HALO
    return [
        pl.BlockSpec((TQ, width), lambda b, s: (blk(b, s), col)),
        pl.BlockSpec((HALO, width), lambda b, s: (jnp.maximum(blk(b, s) * nb8 - 1, 0), col)),
        pl.BlockSpec((HALO, width), lambda b, s: (jnp.minimum(blk(b, s) * nb8 + nb8, t // HALO - 1), col)),
    ]


def _centred_conv(xe, parts, cw_ref, cb_ref, has_prev, has_next):
    c0 = 0
    for x_ref, xp_ref, xn_ref in parts:
        cols = slice(c0, c0 + x_ref.shape[1])
        xe[0:HALO, cols] = jnp.where(has_prev, xp_ref[...], 0.0)
        xe[HALO:HALO + TQ, cols] = x_ref[...]
        xe[HALO + TQ:2 * HALO + TQ, cols] = jnp.where(has_next, xn_ref[...], 0.0)
        c0 += x_ref.shape[1]
    xv = xe[...]
    n = xv.shape[0]
    out = cb_ref[...]
    for k in range(4):
        shifted = xv if k == 2 else pltpu.roll(xv, (2 - k) % n, 0)
        out = out + cw_ref[k:k + 1, :] * shifted[HALO:HALO + TQ, :]
    return out


def lru_gate_weights(w_r, w_i):
    def pair(w):
        w = w.reshape(2, LRU_BLOCKS // 2, 2, LRU_BLOCK, LRU_BLOCK)
        z = jnp.zeros_like(w[:, :, 0])
        top = jnp.concatenate([w[:, :, 0], z], axis=-1)
        bot = jnp.concatenate([z, w[:, :, 1]], axis=-1)
        return jnp.concatenate([top, bot], axis=-2)
    return jnp.concatenate([pair(w_r), pair(w_i)], axis=-1).astype(bf16)


def _gelu_tanh(x):
    return 0.5 * x * (1.0 + jnp.tanh(float(np.sqrt(2.0 / np.pi)) * (x + 0.044715 * (x * x * x))))


def _lru_kernel(d, lat_blocks, *refs):
    if d == 0:
        (x_ref, xp_ref, xn_ref, cw_ref, cb_ref, wg_ref, br_ref, bi_ref, lam_ref, o_ref,
         xe, a_scr, u_scr, h_scr) = refs
    else:
        (x_ref, xp_ref, xn_ref, ag_ref, hf_ref, cw_ref, cb_ref, wg_ref, br_ref, bi_ref, lam_ref, o_ref,
         xe, a_scr, u_scr, h_scr, hs_scr) = refs
    s = pl.program_id(1)
    j = (s - 1) if d == 0 else (lat_blocks - s)
    has_prev = jnp.logical_and(s > 0, j > 0)
    has_next = jnp.logical_and(s > 0, j < lat_blocks - 1)

    @pl.when(s == 0)
    def _():
        h_scr[...] = jnp.zeros_like(h_scr)

    xc = _centred_conv(xe, [(x_ref, xp_ref, xn_ref)], cw_ref, cb_ref, has_prev, has_next)
    xcb = xc.astype(bf16)
    sp = jax.nn.softplus(-lam_ref[...])
    for p in range(LRU_WIDTH // 128):
        sl = slice(128 * p, 128 * (p + 1))
        g = jnp.dot(xcb[:, sl], wg_ref[p], preferred_element_type=f32)
        r = jax.nn.sigmoid(g[:, :128] + br_ref[:, sl])
        i = jax.nn.sigmoid(g[:, 128:] + bi_ref[:, sl])
        log_a = -LRU_C * r * sp[:, sl]
        a = jnp.exp(log_a)
        a_scr[:, sl] = a
        u_scr[:, sl] = jnp.sqrt(-jnp.tanh(log_a) * (a * a + 1.0)) * (i * xc[:, sl])

    out = o_ref if d == 0 else hs_scr
    ngrp = TQ // 8

    def body(gi, h):
        base = pl.multiple_of((gi if d == 0 else ngrp - 1 - gi) * 8, 8)
        for t in (range(8) if d == 0 else range(7, -1, -1)):
            h = a_scr[pl.ds(base + t, 1), :] * h + u_scr[pl.ds(base + t, 1), :]
            out[pl.ds(base + t, 1), :] = h
        return h

    h = lax.fori_loop(0, ngrp, body, h_scr[0:1, :])
    h_scr[0:1, :] = h
    if d == 1:
        y = (hf_ref[...] + hs_scr[...]) * _gelu_tanh(ag_ref[...])
        o_ref[...] = y.astype(o_ref.dtype)


def lru_pass(d, proj, hf, conv_w, conv_b, wg, b_r, b_i, lam, bsz, lat_blocks):
    t = proj.shape[0]
    _, blk = _seq_block(d, bsz, lat_blocks)
    row = lambda c: (lambda b, s: (blk(b, s), c))
    in_specs = _halo_specs(LRU_WIDTH, COL_AX, blk, t)
    args = [proj, proj, proj]
    if d == 1:
        in_specs += [pl.BlockSpec((TQ, LRU_WIDTH), row(COL_AG)), pl.BlockSpec((TQ, LRU_WIDTH), row(0))]
        args += [proj, hf]
    const = lambda shape: pl.BlockSpec(shape, lambda b, s: (0,) * len(shape))
    in_specs += [const((4, LRU_WIDTH)), const((1, LRU_WIDTH)), const((LRU_WIDTH // 128, 128, 256)),
                 const((1, LRU_WIDTH)), const((1, LRU_WIDTH)), const((1, LRU_WIDTH))]
    args += [conv_w, conv_b.reshape(1, -1), wg[d], b_r[d].reshape(1, -1), b_i[d].reshape(1, -1), lam[d].reshape(1, -1)]
    scratch = [pltpu.VMEM((TQ + 2 * HALO, LRU_WIDTH), f32), pltpu.VMEM((TQ, LRU_WIDTH), f32),
               pltpu.VMEM((TQ, LRU_WIDTH), f32), pltpu.VMEM((8, LRU_WIDTH), f32)]
    if d == 1:
        scratch.append(pltpu.VMEM((TQ, LRU_WIDTH), f32))
    return pl.pallas_call(
        functools.partial(_lru_kernel, d, lat_blocks),
        grid=(bsz, lat_blocks + 1),
        in_specs=in_specs,
        out_specs=pl.BlockSpec((TQ, LRU_WIDTH), row(0)),
        out_shape=jax.ShapeDtypeStruct((t, LRU_WIDTH), f32 if d == 0 else bf16),
        scratch_shapes=scratch,
        compiler_params=_params(),
        name=f"lru_pass{d}",
    )(*args)


def lru_branch(proj, conv_w, conv_b, w_r, b_r, w_i, b_i, lam, bsz, lat_blocks):
    wg = lru_gate_weights(w_r, w_i)
    hf = lru_pass(0, proj, None, conv_w, conv_b, wg, b_r, b_i, lam, bsz, lat_blocks)
    return lru_pass(1, proj, hf, conv_w, conv_b, wg, b_r, b_i, lam, bsz, lat_blocks)


def rope_tables(seq):
    pos = jnp.arange(seq)
    row = (pos // GRID_W).astype(f32)
    col = (pos % GRID_W).astype(f32)
    n_freq = SSD_STATE // 4
    freqs = ROPE_BASE ** (-jnp.arange(n_freq, dtype=f32) / n_freq)
    ang = jnp.concatenate([row[:, None] * freqs, col[:, None] * freqs], axis=-1)
    cos, sin = jnp.cos(ang), jnp.sin(ang)
    cosf = jnp.concatenate([cos, cos], axis=-1)
    sinf = jnp.concatenate([-sin, sin], axis=-1)
    cosf = jnp.concatenate([cosf, jnp.ones((TQ, SSD_STATE), f32)], axis=0)
    sinf = jnp.concatenate([sinf, jnp.zeros((TQ, SSD_STATE), f32)], axis=0)
    return cosf, sinf


def head_expand_matrix(d):
    e = np.zeros((128, SSD_INNER), np.float32)
    for h in range(SSD_HEADS):
        e[SSD_HEADS * d + h, h * SSD_HEADDIM:(h + 1) * SSD_HEADDIM] = 1.0
    return jnp.asarray(e, bf16)


def _split3(a):
    hi = a.astype(bf16)
    r = a - hi.astype(f32)
    mid = r.astype(bf16)
    return hi, mid, (r - mid.astype(f32)).astype(bf16)


def _dot_exact_rhs01(a, m01):
    return sum(jnp.dot(p, m01, preferred_element_type=f32) for p in _split3(a))


def _dot_exact_lhs01(m01, a):
    return sum(jnp.dot(m01, p, preferred_element_type=f32) for p in _split3(a))


def _ssd_kernel(d, lat_blocks, *refs):
    xs_refs, bc_refs, refs = refs[0:3], refs[3:6], refs[6:]
    if d == 0:
        (dt_ref, cos_ref, sin_ref, cw_ref, cb_ref, dtb_ref, alog_ref, e_ref,
         o_ref, xe, h_scr) = refs
    else:
        (dt_ref, cos_ref, sin_ref, z_ref, y0_ref, cw_ref, cb_ref, dtb_ref, alog_ref, e_ref,
         dsk_ref, ng_ref, o_ref, xe, h_scr, y_scr) = refs
    s = pl.program_id(1)
    j = (s - 1) if d == 0 else (lat_blocks - s)
    has_prev = jnp.logical_and(s > 0, j > 0)
    has_next = jnp.logical_and(s > 0, j < lat_blocks - 1)
    q = SSD_CHUNK

    @pl.when(s == 0)
    def _():
        h_scr[...] = jnp.zeros_like(h_scr)

    xbc = _centred_conv(xe, [xs_refs, bc_refs], cw_ref, cb_ref, has_prev, has_next)
    xbc = xbc * jax.nn.sigmoid(xbc)
    xs = xbc[:, :SSD_INNER]
    cosf, sinf = cos_ref[...], sin_ref[...]

    def rope(g, off):
        v = xbc[:, off + g * SSD_STATE: off + (g + 1) * SSD_STATE]
        return (v * cosf + pltpu.roll(v, SSD_STATE // 2, 1) * sinf).astype(bf16)

    bm = [rope(g, SSD_INNER) for g in range(SSD_GROUPS)]
    cm = [rope(g, SSD_INNER + SSD_GROUPS * SSD_STATE) for g in range(SSD_GROUPS)]
    dt = jax.nn.softplus(dt_ref[...] + dtb_ref[...])
    delta = dt * (-jnp.exp(alog_ref[...]))
    ri = lax.broadcasted_iota(jnp.int32, (q, q), 0)
    ci = lax.broadcasted_iota(jnp.int32, (q, q), 1)
    keep = (ci <= ri) if d == 0 else (ci >= ri)
    tri = jnp.where(keep, 1.0, 0.0).astype(bf16)
    lane = lax.broadcasted_iota(jnp.int32, (1, 2 * SSD_HEADDIM), 1)
    halves = (lane < SSD_HEADDIM, lane >= SSD_HEADDIM)
    e = e_ref[...]
    last = q - 1 if d == 0 else 0
    out = o_ref if d == 0 else y_scr

    for c in (range(TQ // q) if d == 0 else range(TQ // q - 1, -1, -1)):
        rows = slice(c * q, (c + 1) * q)
        at = _dot_exact_lhs01(tri, delta[rows])
        at_exp = _dot_exact_rhs01(at, e)
        dt_exp = _dot_exact_rhs01(dt[rows], e)
        tot_exp = at_exp[last:last + 1, :]
        xdt = xs[rows] * dt_exp
        xd = (xdt * jnp.exp(tot_exp - at_exp)).astype(bf16)
        eat = jnp.exp(at_exp)
        cdec = jnp.exp(tot_exp)
        at_row = at.T
        ys = []
        for g in range(SSD_GROUPS):
            bg, cg = bm[g][rows], cm[g][rows]
            cb = _dot_nt(cg, bg)
            ht = h_scr[g]
            yoff = jnp.dot(cg, ht.astype(bf16), preferred_element_type=f32) * eat[:, g * GW:(g + 1) * GW]
            for pp in range(2):
                xpair = xdt[:, g * GW + pp * 128: g * GW + (pp + 1) * 128]
                acc = yoff[:, pp * 128:(pp + 1) * 128]
                for hh in range(2):
                    li = SSD_HEADS * d + 4 * g + 2 * pp + hh
                    seg = at[:, li:li + 1] - at_row[li:li + 1, :]
                    m = (cb * jnp.exp(jnp.where(keep, seg, NEG))).astype(bf16)
                    xm = jnp.where(halves[hh], xpair, 0.0).astype(bf16)
                    acc = acc + jnp.dot(m, xm, preferred_element_type=f32)
                ys.append(acc)
            upd = lax.dot_general(bg, xd[:, g * GW:(g + 1) * GW], (((0,), (0,)), ((), ())),
                                  preferred_element_type=f32)
            h_scr[g] = cdec[:, g * GW:(g + 1) * GW] * ht + upd
        out[rows, :] = jnp.concatenate(ys, axis=-1)

    if d == 1:
        y = y0_ref[...] + y_scr[...] + dsk_ref[...] * xs
        z = z_ref[...]
        y = y * (z * jax.nn.sigmoid(z))
        y = y * lax.rsqrt(jnp.mean(y * y, axis=-1, keepdims=True) + EPS)
        o_ref[...] = (y * ng_ref[...]).astype(o_ref.dtype)


def ssd_pass(d, proj, y0, cosf, sinf, conv_w, conv_b, a_log, dt_bias, d_skip, norm_g, bsz, lat_blocks):
    t = proj.shape[0]
    jj, blk = _seq_block(d, bsz, lat_blocks)
    row = lambda c: (lambda b, s: (blk(b, s), c))
    tbl = lambda b, s: (jnp.where(s == 0, lat_blocks, jj(s)), 0)
    const = lambda shape: pl.BlockSpec(shape, lambda b, s: (0,) * len(shape))
    pad128 = lambda v: jnp.pad(v.reshape(1, -1).astype(f32), ((0, 0), (0, 128 - v.size)))
    in_specs = _halo_specs(SSD_INNER, COL_XS, blk, t) + _halo_specs(SSD_CONV_DIM - SSD_INNER, COL_BC, blk, t) + [
        pl.BlockSpec((TQ, 128), row(COL_DT)),
        pl.BlockSpec((TQ, SSD_STATE), tbl),
        pl.BlockSpec((TQ, SSD_STATE), tbl),
    ]
    args = [proj] * 7 + [cosf, sinf]
    if d == 1:
        in_specs += [pl.BlockSpec((TQ, SSD_INNER), row(COL_Z)), pl.BlockSpec((TQ, SSD_INNER), row(0))]
        args += [proj, y0]
    in_specs += [const((4, SSD_CONV_DIM)), const((1, SSD_CONV_DIM)), const((1, 128)), const((1, 128)),
                 const((128, SSD_INNER))]
    args += [conv_w, conv_b.reshape(1, -1), pad128(dt_bias), pad128(a_log), head_expand_matrix(d)]
    scratch = [pltpu.VMEM((TQ + 2 * HALO, SSD_CONV_DIM), f32), pltpu.VMEM((SSD_GROUPS, SSD_STATE, GW), f32)]
    if d == 1:
        in_specs += [const((1, SSD_INNER)), const((1, SSD_INNER))]
        args += [jnp.repeat(d_skip, SSD_HEADDIM).reshape(1, -1), norm_g.reshape(1, -1)]
        scratch.append(pltpu.VMEM((TQ, SSD_INNER), f32))
    return pl.pallas_call(
        functools.partial(_ssd_kernel, d, lat_blocks),
        grid=(bsz, lat_blocks + 1),
        in_specs=in_specs,
        out_specs=pl.BlockSpec((TQ, SSD_INNER), row(0)),
        out_shape=jax.ShapeDtypeStruct((t, SSD_INNER), f32 if d == 0 else bf16),
        scratch_shapes=scratch,
        compiler_params=_params(),
        name=f"ssd_pass{d}",
    )(*args)


def ssd_branch(proj, cosf, sinf, conv_w, conv_b, a_log, dt_bias, d_skip, norm_g, bsz, lat_blocks):
    y0 = ssd_pass(0, proj, None, cosf, sinf, conv_w, conv_b, a_log, dt_bias, d_skip, norm_g, bsz, lat_blocks)
    return ssd_pass(1, proj, y0, cosf, sinf, conv_w, conv_b, a_log, dt_bias, d_skip, norm_g, bsz, lat_blocks)


def _tile_rows(mod_rows, bsz):
    idx = np.concatenate([np.repeat(np.arange(bsz), SEQ // TM), np.full(bsz * CTX_LEN // TM, bsz)])
    return mod_rows[idx][:, None, :]


def _expert_choice_moe(h, v, w_router, w1, w3, w2, layer, g2_rows, bsz):
    t, d = v.shape
    aff = router_affinity(v, w_router)[:, :N_EXPERTS]
    nl = bsz * SEQ

    def choose(a, length):
        cap = CAPACITY_FACTOR * length // N_EXPERTS
        g, idx = lax.top_k(jnp.swapaxes(a.reshape(bsz, length, N_EXPERTS), 1, 2), cap)
        return jnp.swapaxes(g, 0, 1), jnp.swapaxes(idx, 0, 1), cap

    g_l, i_l, cap_l = choose(aff[:nl], SEQ)
    g_c, i_c, cap_c = choose(aff[nl:], CTX_LEN)
    boff = jnp.arange(bsz)[None, :, None]
    ctx_rows = (i_c + boff * CTX_LEN).reshape(N_EXPERTS, -1)
    rows = jnp.concatenate([(i_l + boff * SEQ).reshape(N_EXPERTS, -1), ctx_rows + nl], axis=1)
    gates = jnp.concatenate([g_l.reshape(N_EXPERTS, -1), g_c.reshape(N_EXPERTS, -1)], axis=1)[..., None]
    r = rows.shape[1]
    xg = jnp.take(v, rows.reshape(-1), axis=0).reshape(N_EXPERTS, r, d)
    hdn = expert_hidden(xg, w1, w3, layer, r // 4)
    ctx_slots, ctx_tile = bsz * cap_c, bsz * CTX_LEN
    assert (bsz * cap_l) % ctx_slots == 0 and nl % ctx_tile == 0
    out = moe_combine(i_l.reshape(-1), hdn, 0, cap_l, gates, w2, layer, h, g2_rows, 0, 0, SEQ, TN, bsz)
    return moe_combine(ctx_rows.reshape(-1), hdn, bsz * cap_l // ctx_slots, ctx_slots, gates, w2, layer, h, g2_rows,
                       bsz, nl // ctx_tile, ctx_tile, d, 1, prev=out)


def kernel(x, c, ctx, c_ctx, w_ada, b_ada, norm_mix, norm_ffn, w_in, lru_conv_w, lru_conv_b, lru_w_r, lru_b_r,
           lru_w_i, lru_b_i, lru_lambda, ssd_conv_w, ssd_conv_b, ssd_a_log, ssd_dt_bias, ssd_d, ssd_norm, na_rpb,
           w_branch_lru, w_branch_ssd, w_branch_na, w_out, w_router, w1, w3, w2, norm_final):
    bsz = x.shape[0]
    assert x.shape[1:] == (SEQ, D_MODEL) and ctx.shape[1:] == (CTX_LEN, D_MODEL) and bsz * CTX_LEN == TM
    d = D_MODEL
    h = jnp.concatenate([x.reshape(bsz * SEQ, d), ctx.reshape(bsz * CTX_LEN, d)], axis=0)
    t = h.shape[0]

    cond = jnp.concatenate([c, c_ctx[None, :], jnp.zeros((8 - bsz - 1, d), f32)], axis=0)
    mod = ada_modulation(cond, w_ada, b_ada)
    cosf, sinf = rope_tables(SEQ)

    assert DT_END + DT_PAD == W_QKV0 and w_in.shape[-1] == DT_END + 3 * NA_WIDTH + 3 * d
    w_proj = relayout_proj_weights(w_in)
    qkv_blocks = 3 * NA_WIDTH // TN
    pa, pb, pc, wo, w2b = (w.astype(bf16) for w in (w_branch_lru, w_branch_ssd, w_branch_na, w_out, w2))
    na_tbl, na_pair_of = na_bias_blocks(na_rpb)

    for l in range(DEPTH):
        mods = jnp.split(mod[l], 6, axis=-1)
        sh1, sc1, g1, sh2, sc2 = (_tile_rows(m, bsz) for m in mods[:5])
        u = norm_modulate(h, norm_mix[l], sc1, sh1, bf16)
        proj = matmul_bf16(u, w_proj, l, N_F32, lambda j: jnp.where(j < W_QKV0 // TN, j, j + qkv_blocks), f32)
        qkv = matmul_bf16(u, w_proj, l, 3 * NA_WIDTH, lambda j: j + W_QKV0 // TN, bf16)
        ya = lru_branch(proj, lru_conv_w[l], lru_conv_b[l], lru_w_r[l], lru_b_r[l], lru_w_i[l], lru_b_i[l],
                        lru_lambda[l], bsz, LAT_BLOCKS)
        yb = ssd_branch(proj, cosf, sinf, ssd_conv_w[l], ssd_conv_b[l], ssd_a_log[l], ssd_dt_bias[l], ssd_d[l],
                        ssd_norm[l], bsz, LAT_BLOCKS)
        yc = na_attention(qkv, na_tbl, l, na_pair_of, bsz)
        y = branch_merge(ya, yb, yc, proj, pa, pb, pc, l, COL_G)
        h = residual_matmul(y, wo, l, h, g1)
        v = norm_modulate(h, norm_ffn[l], sc2, sh2, bf16)
        h = _expert_choice_moe(h, v, w_router[l], w1, w3, w2b, l, mods[5][:, None, :], bsz)

    zeros = jnp.zeros((t // TM, 1, d), f32)
    out = norm_modulate(h, norm_final, zeros, zeros, f32)
    return out[:bsz * SEQ].reshape(bsz, SEQ, d)
```

```python
import functools

import jax
import jax.numpy as jnp
import numpy as np
from jax import lax
from jax.experimental import pallas as pl
from jax.experimental.pallas import tpu as pltpu

D_MODEL = 2048
SEQ = 4096
CTX_LEN = 256
DEPTH = 4
GRID_W = 64
EPS = 1e-6
ROPE_BASE = 10000.0
LRU_WIDTH = 1024
LRU_BLOCKS = 16
LRU_BLOCK = LRU_WIDTH // LRU_BLOCKS
LRU_C = 8.0
SSD_INNER = 1024
SSD_HEADDIM = 64
SSD_HEADS = SSD_INNER // SSD_HEADDIM
SSD_GROUPS = 4
SSD_STATE = 128
SSD_CHUNK = 128
SSD_CONV_DIM = SSD_INNER + 2 * SSD_GROUPS * SSD_STATE
NA_HEADS = 16
NA_HEADDIM = 64
NA_WIDTH = NA_HEADS * NA_HEADDIM
NA_WIN_R = 8
NA_WIN_C = 16
N_EXPERTS = 16
EXPERT_FF = 1024
CAPACITY_FACTOR = 2

V7X_VMEM_LIMIT = 56 * 1024 * 1024
TM = 1024
MM_MAX_ROWS = 2304
TQ = 256
HALO = 8
LAT_BLOCKS = SEQ // TQ
NEG = -1e30
HI = lax.Precision.HIGHEST
bf16 = jnp.bfloat16
f32 = jnp.float32

TN = 512
DT_END = 2 * LRU_WIDTH + SSD_INNER + SSD_CONV_DIM + 2 * SSD_HEADS
DT_PAD = 480
COL_AX, COL_AG, COL_Z, COL_XS, COL_BC = 0, 1, 2, 3, 4
COL_DT = 40
W_QKV0 = 5632
COL_G = 5632
N_F32 = COL_G + 3 * D_MODEL
GW = SSD_INNER // SSD_GROUPS

NA_GROUP_ROWS = 2
NA_BAND_ROWS = 10
NA_NCFG = 5


def _params():
    return pltpu.CompilerParams(vmem_limit_bytes=V7X_VMEM_LIMIT)


def _ada_kernel(x_ref, w_ref, b_ref, o_ref):
    c = x_ref[...]
    x = (c * jax.nn.sigmoid(c)).astype(bf16)
    o_ref[...] = jnp.dot(x, w_ref[...].astype(bf16), preferred_element_type=f32) + b_ref[...]


def ada_modulation(cond, w_ada, b_ada):
    depth, d, n = w_ada.shape
    tn = 1024
    return pl.pallas_call(
        _ada_kernel,
        grid=(depth, n // tn),
        in_specs=[pl.BlockSpec((8, d), lambda l, j: (0, 0)),
                  pl.BlockSpec((None, d, tn), lambda l, j: (l, 0, j)),
                  pl.BlockSpec((None, 1, tn), lambda l, j: (l, 0, j))],
        out_specs=pl.BlockSpec((None, 8, tn), lambda l, j: (l, 0, j)),
        out_shape=jax.ShapeDtypeStruct((depth, 8, n), f32),
        compiler_params=_params(),
        name="ada_modulation",
    )(cond, w_ada, b_ada.reshape(depth, 1, n))


def _norm_mod_kernel(h_ref, g_ref, sc_ref, sh_ref, o_ref):
    x = h_ref[...]
    y = x * lax.rsqrt(jnp.mean(x * x, axis=-1, keepdims=True) + EPS)
    o_ref[...] = ((y * g_ref[...]) * (1.0 + sc_ref[...]) + sh_ref[...]).astype(o_ref.dtype)


def norm_modulate(h, g, sc_t, sh_t, out_dtype):
    t, d = h.shape
    tm = min(512, TM)
    per = TM // tm
    return pl.pallas_call(
        _norm_mod_kernel,
        grid=(t // tm,),
        in_specs=[pl.BlockSpec((tm, d), lambda i: (i, 0)),
                  pl.BlockSpec((1, d), lambda i: (0, 0)),
                  pl.BlockSpec((None, 1, d), lambda i: (i // per, 0, 0)),
                  pl.BlockSpec((None, 1, d), lambda i: (i // per, 0, 0))],
        out_specs=pl.BlockSpec((tm, d), lambda i: (i, 0)),
        out_shape=jax.ShapeDtypeStruct((t, d), out_dtype),
        compiler_params=_params(),
        name="norm_modulate",
    )(h, g.reshape(1, d), sc_t, sh_t)


def _relayout_kernel(shift, a_ref, b_ref, o_ref):
    j = pl.program_id(1)
    split = DT_END // TN

    @pl.when(j < split)
    def _():
        o_ref[...] = a_ref[...].astype(o_ref.dtype)

    @pl.when(j == split)
    def _():
        row = lax.broadcasted_iota(jnp.int32, a_ref.shape, 0)
        o_ref[...] = jnp.where(row < shift, a_ref[...], 0.0).astype(o_ref.dtype)

    @pl.when(j > split)
    def _():
        o_ref[0:TN - shift, :] = a_ref[shift:TN, :].astype(o_ref.dtype)
        o_ref[TN - shift:TN, :] = b_ref[...].astype(o_ref.dtype)


def relayout_proj_weights(w_t):
    depth, n, d = w_t.shape
    shift = DT_END % TN
    assert shift + DT_PAD == TN and shift % 16 == 0 and n % shift == 0
    split = DT_END // TN
    return pl.pallas_call(
        functools.partial(_relayout_kernel, shift),
        grid=(depth, (n + DT_PAD) // TN),
        in_specs=[pl.BlockSpec((None, TN, d), lambda l, j: (l, jnp.where(j <= split, j, j - 1), 0)),
                  pl.BlockSpec((None, shift, d), lambda l, j: (l, jnp.where(j <= split, 0, (TN // shift) * j), 0))],
        out_specs=pl.BlockSpec((None, TN, d), lambda l, j: (l, j, 0)),
        out_shape=jax.ShapeDtypeStruct((depth, n + DT_PAD, d), bf16),
        compiler_params=_params(),
        name="relayout_proj_weights",
    )(w_t, w_t)


def _mm_kernel(x_ref, w_ref, o_ref):
    o_ref[...] = _dot_nt(x_ref[...], w_ref[...]).astype(o_ref.dtype)


def matmul_bf16(x, w, layer, n_out, w_block, out_dtype):
    m, k = x.shape
    tm = max(c for c in range(16, MM_MAX_ROWS + 1, 16) if m % c == 0)
    return pl.pallas_call(
        _mm_kernel,
        grid=(m // tm, n_out // TN),
        in_specs=[pl.BlockSpec((tm, k), lambda i, j: (i, 0)),
                  pl.BlockSpec((None, TN, k), lambda i, j: (layer, w_block(j), 0))],
        out_specs=pl.BlockSpec((tm, TN), lambda i, j: (i, j)),
        out_shape=jax.ShapeDtypeStruct((m, n_out), out_dtype),
        compiler_params=_params(),
        name="matmul_bf16",
    )(x, w)


def _merge_kernel(ya_ref, yb_ref, yc_ref, ga_ref, gb_ref, gc_ref, pa_ref, pb_ref, pc_ref, o_ref):
    acc = jax.nn.sigmoid(ga_ref[...]) * jnp.dot(ya_ref[...], pa_ref[...], preferred_element_type=f32)
    acc = acc + jax.nn.sigmoid(gb_ref[...]) * jnp.dot(yb_ref[...], pb_ref[...], preferred_element_type=f32)
    acc = acc + jax.nn.sigmoid(gc_ref[...]) * jnp.dot(yc_ref[...], pc_ref[...], preferred_element_type=f32)
    o_ref[...] = acc.astype(o_ref.dtype)


def branch_merge(ya, yb, yc, proj, pa, pb, pc, layer, g_col0):
    t, k = ya.shape
    n = pa.shape[-1]
    tn = 512
    gb0 = g_col0 // tn
    nj = n // tn
    xs = pl.BlockSpec((TM, k), lambda i, j: (i, 0))
    ws = pl.BlockSpec((None, k, tn), lambda i, j: (layer, 0, j))
    gs = lambda q: pl.BlockSpec((TM, tn), lambda i, j: (i, gb0 + q * nj + j))
    return pl.pallas_call(
        _merge_kernel,
        grid=(t // TM, nj),
        in_specs=[xs, xs, xs, gs(0), gs(1), gs(2), ws, ws, ws],
        out_specs=pl.BlockSpec((TM, tn), lambda i, j: (i, j)),
        out_shape=jax.ShapeDtypeStruct((t, n), bf16),
        compiler_params=_params(),
        name="branch_merge",
    )(ya, yb, yc, proj, proj, proj, pa, pb, pc)


def _resid_mm_kernel(y_ref, w_ref, h_ref, g_ref, o_ref):
    o_ref[...] = h_ref[...] + g_ref[...] * jnp.dot(y_ref[...], w_ref[...], preferred_element_type=f32)


def residual_matmul(y, w, layer, h, gate_t):
    m, k = y.shape
    n = w.shape[-1]
    tn = 512
    return pl.pallas_call(
        _resid_mm_kernel,
        grid=(m // TM, n // tn),
        in_specs=[pl.BlockSpec((TM, k), lambda i, j: (i, 0)),
                  pl.BlockSpec((None, k, tn), lambda i, j: (layer, 0, j)),
                  pl.BlockSpec((TM, tn), lambda i, j: (i, j)),
                  pl.BlockSpec((None, 1, tn), lambda i, j: (i, 0, j))],
        out_specs=pl.BlockSpec((TM, tn), lambda i, j: (i, j)),
        out_shape=jax.ShapeDtypeStruct((m, n), f32),
        compiler_params=_params(),
        name="residual_matmul",
    )(y, w, h, gate_t)


def _router_kernel(v_ref, w_ref, o_ref):
    s = jnp.dot(v_ref[...], w_ref[...], preferred_element_type=f32)
    lane = lax.broadcasted_iota(jnp.int32, s.shape, 1)
    s = jnp.where(lane < N_EXPERTS, s, NEG)
    e = jnp.exp(s - s.max(axis=-1, keepdims=True))
    o_ref[...] = e / e.sum(axis=-1, keepdims=True)


def router_affinity(v, w_router):
    t, d = v.shape
    w = jnp.pad(w_router, ((0, 0), (0, 128 - N_EXPERTS))).astype(bf16)
    return pl.pallas_call(
        _router_kernel,
        grid=(t // TM,),
        in_specs=[pl.BlockSpec((TM, d), lambda i: (i, 0)), pl.BlockSpec((d, 128), lambda i: (0, 0))],
        out_specs=pl.BlockSpec((TM, 128), lambda i: (i, 0)),
        out_shape=jax.ShapeDtypeStruct((t, 128), f32),
        compiler_params=_params(),
        name="router_affinity",
    )(v, w)


def _expert_hidden_kernel(x_ref, w1_ref, w3_ref, o_ref, w1b, w3b):
    @pl.when(pl.program_id(1) == 0)
    def _():
        w1b[...] = w1_ref[...].astype(bf16)
        w3b[...] = w3_ref[...].astype(bf16)

    x = x_ref[...]
    a = jnp.dot(x, w1b[...], preferred_element_type=f32)
    b = jnp.dot(x, w3b[...], preferred_element_type=f32)
    o_ref[...] = ((a * jax.nn.sigmoid(a)) * b).astype(o_ref.dtype)


def expert_hidden(xg, w1, w3, layer, tm):
    e, r, d = xg.shape
    f = w1.shape[-1]
    return pl.pallas_call(
        _expert_hidden_kernel,
        grid=(e, r // tm),
        in_specs=[pl.BlockSpec((None, tm, d), lambda k, i: (k, i, 0)),
                  pl.BlockSpec((None, None, d, f), lambda k, i: (layer, k, 0, 0)),
                  pl.BlockSpec((None, None, d, f), lambda k, i: (layer, k, 0, 0))],
        out_specs=pl.BlockSpec((None, tm, f), lambda k, i: (k, i, 0)),
        out_shape=jax.ShapeDtypeStruct((e, r, f), bf16),
        scratch_shapes=[pltpu.VMEM((d, f), bf16), pltpu.VMEM((d, f), bf16)],
        compiler_params=_params(),
        name="expert_hidden",
    )(xg, w1, w3)


SCATTER_UNROLL = 8


def _combine_kernel(nb, slots, idx_ref, hdn_ref, w2_ref, g_ref, h_ref, g2_ref, *rest):
    o_ref, ye_scr = rest[-2], rest[-1]
    b, e = pl.program_id(0), pl.program_id(2)

    @pl.when(e == 0)
    def _():
        o_ref[...] = jnp.zeros_like(o_ref)

    ye_scr[...] = jnp.dot(hdn_ref[...], w2_ref[...].astype(bf16), preferred_element_type=f32) * g_ref[...]
    base0 = (e * nb + b) * slots

    for s0 in range(0, slots, SCATTER_UNROLL):
        rows = [idx_ref[base0 + s0 + k] for k in range(SCATTER_UNROLL)]
        vals = [o_ref[pl.ds(rows[k], 1), :] + ye_scr[s0 + k:s0 + k + 1, :] for k in range(SCATTER_UNROLL)]
        for k in range(SCATTER_UNROLL):
            o_ref[pl.ds(rows[k], 1), :] = vals[k]

    @pl.when(e == pl.num_programs(2) - 1)
    def _():
        o_ref[...] = h_ref[...] + g2_ref[...] * o_ref[...]


def moe_combine(idx, hdn, slot_blk0, slots, gates, w2, layer, h, g2_rows, g2_row0, row_blk0, rows_blk, dq, nb,
                prev=None):
    e, _, f = hdn.shape
    t, d = h.shape
    in_specs = [
        pl.BlockSpec((None, slots, f), lambda b, q, k, idx: (k, slot_blk0 + b, 0)),
        pl.BlockSpec((None, None, f, dq), lambda b, q, k, idx: (layer, k, 0, q)),
        pl.BlockSpec((None, slots, 1), lambda b, q, k, idx: (k, slot_blk0 + b, 0)),
        pl.BlockSpec((rows_blk, dq), lambda b, q, k, idx: (row_blk0 + b, q)),
        pl.BlockSpec((None, 1, dq), lambda b, q, k, idx: (g2_row0 + b, 0, q)),
    ]
    args = [idx, hdn, w2, gates, h, g2_rows]
    aliases = {}
    if prev is not None:
        in_specs.append(pl.BlockSpec(memory_space=pl.ANY))
        args.append(prev)
        aliases = {len(args) - 1: 0}
    return pl.pallas_call(
        functools.partial(_combine_kernel, nb, slots),
        grid_spec=pltpu.PrefetchScalarGridSpec(
            num_scalar_prefetch=1,
            grid=(nb, d // dq, e),
            in_specs=in_specs,
            out_specs=pl.BlockSpec((rows_blk, dq), lambda b, q, k, idx: (row_blk0 + b, q)),
            scratch_shapes=[pltpu.VMEM((slots, dq), f32)],
        ),
        out_shape=jax.ShapeDtypeStruct((t, d), f32),
        input_output_aliases=aliases,
        compiler_params=_params(),
        name="moe_combine",
    )(*args)


def na_bias_blocks(rpb):
    rows = SEQ // GRID_W
    n_dr, n_dc = 2 * NA_WIN_R - 1, 2 * NA_WIN_C - 1
    qc, kc = np.arange(GRID_W)[:, None], np.arange(GRID_W)[None, :]
    cs = np.clip(qc - NA_WIN_C // 2, 0, GRID_W - NA_WIN_C)
    col_ok = (kc >= cs) & (kc < cs + NA_WIN_C)
    pick = (np.arange(n_dc)[:, None, None] == (kc - qc + NA_WIN_C - 1)[None]) & col_ok[None]
    blocks = jnp.einsum('...rd,dqk->...rqk', rpb, jnp.asarray(pick, f32), precision=HI)
    blocks = jnp.where(col_ok, blocks, NEG)
    blocks = jnp.concatenate([blocks, jnp.full(blocks.shape[:-3] + (1, GRID_W, GRID_W), NEG, f32)], axis=-3)
    which = np.full((NA_NCFG, NA_GROUP_ROWS, NA_BAND_ROWS), n_dr, np.int32)
    for c, r0 in enumerate((0, 2, 4, 60, 62)):
        bs = int(np.clip(r0 - NA_WIN_R // 2, 0, rows - NA_BAND_ROWS))
        for qr in range(NA_GROUP_ROWS):
            r = r0 + qr
            rs = int(np.clip(r - NA_WIN_R // 2, 0, rows - NA_WIN_R))
            for kr in range(NA_BAND_ROWS):
                if rs <= bs + kr < rs + NA_WIN_R:
                    which[c, qr, kr] = bs + kr - r + NA_WIN_R - 1
    pairs = sorted({(int(which[c, qr, 2 * m]), int(which[c, qr, 2 * m + 1]))
                    for c in range(NA_NCFG) for qr in range(NA_GROUP_ROWS) for m in range(NA_BAND_ROWS // 2)})
    pair_of = np.array([[[pairs.index((int(which[c, qr, 2 * m]), int(which[c, qr, 2 * m + 1])))
                          for m in range(NA_BAND_ROWS // 2)] for qr in range(NA_GROUP_ROWS)]
                        for c in range(NA_NCFG)], np.int32)
    left = jnp.take(blocks, np.array([p[0] for p in pairs]), axis=-3)
    right = jnp.take(blocks, np.array([p[1] for p in pairs]), axis=-3)
    return jnp.concatenate([left, right], axis=-1), pair_of.reshape(-1)


def _dot_nt(a, b):
    return lax.dot_general(a, b, (((1,), (1,)), ((), ())), preferred_element_type=f32)


def _softmax_av(parts):
    m = parts[0][0].max(axis=-1, keepdims=True)
    for s, _ in parts[1:]:
        m = jnp.maximum(m, s.max(axis=-1, keepdims=True))
    l = None
    o = None
    for s, v in parts:
        p = jnp.exp(s - m)
        ls = p.sum(axis=-1, keepdims=True)
        os_ = jnp.dot(p.astype(bf16), v, preferred_element_type=f32)
        l = ls if l is None else l + ls
        o = os_ if o is None else o + os_
    return o / l


def _na_kernel(pair_ref, q_ref, k_ref, v_ref, kc_ref, vc_ref, tbl_ref, o_ref):
    j = pl.program_id(2)
    npairs = NA_BAND_ROWS // 2

    def bias(hh, cfg):
        rows = []
        for qr in range(NA_GROUP_ROWS):
            base = (cfg * NA_GROUP_ROWS + qr) * npairs
            rows.append(jnp.concatenate([tbl_ref[hh, pair_ref[base + m]] for m in range(npairs)], axis=1))
        return jnp.concatenate(rows, axis=0)

    scale = NA_HEADDIM ** -0.5
    assert scale == 2.0 ** round(np.log2(scale))
    lane = lax.broadcasted_iota(jnp.int32, (1, 2 * NA_HEADDIM), 1)
    first = lane < NA_HEADDIM
    kc = kc_ref[...]
    vc = vc_ref[...]
    gq = NA_GROUP_ROWS * GRID_W

    def head_q(q, hh):
        keep = first if hh == 0 else jnp.logical_not(first)
        return jnp.where(keep, q * scale, jnp.zeros_like(q))

    @pl.when(j < LAT_BLOCKS)
    def _():
        ngroups = TQ // gq
        q2 = []
        for g in range(ngroups):
            q = q_ref[pl.ds(g * gq, gq), :]
            q2.append(jnp.concatenate([head_q(q, 0), head_q(q, 1)], axis=0))
        sc_all = _dot_nt(jnp.concatenate(q2, axis=0), kc)
        o_win, p_ctx, denom = [], [], []
        for g in range(ngroups):
            r0 = j * (TQ // GRID_W) + NA_GROUP_ROWS * g
            bs = jnp.clip(r0 - NA_WIN_R // 2, 0, SEQ // GRID_W - NA_BAND_ROWS)
            cfg = jnp.where(r0 == 0, 0, jnp.where(r0 == 2, 1, jnp.where(r0 == 60, 3, jnp.where(r0 == 62, 4, 2))))
            start = pl.multiple_of(bs * GRID_W, GRID_W)
            kb = k_ref[pl.ds(start, NA_BAND_ROWS * GRID_W), :]
            vb = v_ref[pl.ds(start, NA_BAND_ROWS * GRID_W), :]
            s = _dot_nt(q2[g], kb) + jnp.concatenate([bias(0, cfg), bias(1, cfg)], axis=0)
            sc = sc_all[2 * gq * g:2 * gq * (g + 1)]
            m = jnp.maximum(s.max(axis=-1, keepdims=True), sc.max(axis=-1, keepdims=True))
            p, pc = jnp.exp(s - m), jnp.exp(sc - m)
            denom.append(p.sum(axis=-1, keepdims=True) + pc.sum(axis=-1, keepdims=True))
            o_win.append(jnp.dot(p.astype(bf16), vb, preferred_element_type=f32))
            p_ctx.append(pc.astype(bf16))
        o_ctx = jnp.dot(jnp.concatenate(p_ctx, axis=0), vc, preferred_element_type=f32)
        for g in range(ngroups):
            o2 = (o_win[g] + o_ctx[2 * gq * g:2 * gq * (g + 1)]) / denom[g]
            o_ref[pl.ds(g * gq, gq), :] = jnp.where(first, o2[:gq], o2[gq:]).astype(o_ref.dtype)

    @pl.when(j == LAT_BLOCKS)
    def _():
        q = q_ref[...]
        q2 = jnp.concatenate([head_q(q, 0), head_q(q, 1)], axis=0)
        o2 = _softmax_av([(_dot_nt(q2, kc), vc)])
        o_ref[...] = jnp.where(first, o2[:CTX_LEN], o2[CTX_LEN:]).astype(o_ref.dtype)


def na_attention(qkv, tbl, layer, pair_of, bsz):
    t = qkv.shape[0]
    npair = NA_HEADS // 2
    ctx0 = bsz * LAT_BLOCKS

    def qrow(b, j):
        return jnp.where(j < LAT_BLOCKS, b * LAT_BLOCKS + j, ctx0 + b)

    return pl.pallas_call(
        _na_kernel,
        grid_spec=pltpu.PrefetchScalarGridSpec(
            num_scalar_prefetch=1,
            grid=(bsz, npair, LAT_BLOCKS + 1),
            in_specs=[
                pl.BlockSpec((TQ, 128), lambda b, p, j, po: (qrow(b, j), p)),
                pl.BlockSpec((SEQ, 128), lambda b, p, j, po: (b, npair + p)),
                pl.BlockSpec((SEQ, 128), lambda b, p, j, po: (b, 2 * npair + p)),
                pl.BlockSpec((CTX_LEN, 128), lambda b, p, j, po: (ctx0 + b, npair + p)),
                pl.BlockSpec((CTX_LEN, 128), lambda b, p, j, po: (ctx0 + b, 2 * npair + p)),
                pl.BlockSpec((None, 2) + tbl.shape[2:], lambda b, p, j, po: (layer, p, 0, 0, 0)),
            ],
            out_specs=pl.BlockSpec((TQ, 128), lambda b, p, j, po: (qrow(b, j), p)),
        ),
        out_shape=jax.ShapeDtypeStruct((t, NA_WIDTH), bf16),
        compiler_params=_params(),
        name="na_attention",
    )(jnp.asarray(pair_of), qkv, qkv, qkv, qkv, qkv, tbl)


def _seq_block(d, bsz, lat_blocks):
    ctx0 = bsz * lat_blocks

    def jj(s):
        return (s - 1) if d == 0 else (lat_blocks - s)

    def blk(b, s):
        return jnp.where(s == 0, ctx0 + b, b * lat_blocks + jj(s))

    return jj, blk


def _halo_specs(width, col, blk, t):
    nb8 = TQ // HALO
    return [
        pl.BlockSpec((TQ, width), lambda b, s: (blk(b, s), col)),
        pl.BlockSpec((HALO, width), lambda b, s: (jnp.maximum(blk(b, s) * nb8 - 1, 0), col)),
        pl.BlockSpec((HALO, width), lambda b, s: (jnp.minimum(blk(b, s) * nb8 + nb8, t // HALO - 1), col)),
    ]


def _fill_halo(xe, parts, has_prev, has_next):
    c0 = 0
    for x_ref, xp_ref, xn_ref in parts:
        cols = slice(c0, c0 + x_ref.shape[1])
        xe[0:HALO, cols] = jnp.where(has_prev, xp_ref[...], 0.0)
        xe[HALO:HALO + TQ, cols] = x_ref[...]
        xe[HALO + TQ:2 * HALO + TQ, cols] = jnp.where(has_next, xn_ref[...], 0.0)
        c0 += x_ref.shape[1]


def _centred_conv(xe, cw_ref, cb_ref, cols):
    xv = xe[:, cols]
    n = xv.shape[0]
    out = cb_ref[:, cols]
    for k in range(4):
        shifted = xv if k == 2 else pltpu.roll(xv, (2 - k) % n, 0)
        out = out + cw_ref[k:k + 1, cols] * shifted[HALO:HALO + TQ, :]
    return out


def lru_gate_weights(w_r, w_i):
    def pair(w):
        w = w.reshape(2, LRU_BLOCKS // 2, 2, LRU_BLOCK, LRU_BLOCK)
        z = jnp.zeros_like(w[:, :, 0])
        top = jnp.concatenate([w[:, :, 0], z], axis=-1)
        bot = jnp.concatenate([z, w[:, :, 1]], axis=-1)
        return jnp.concatenate([top, bot], axis=-2)
    return jnp.concatenate([pair(w_r), pair(w_i)], axis=-1).astype(bf16)


def _gelu_tanh(x):
    return 0.5 * x * (1.0 + jnp.tanh(float(np.sqrt(2.0 / np.pi)) * (x + 0.044715 * (x * x * x))))


def _lru_kernel(d, lat_blocks, *refs):
    if d == 0:
        (x_ref, xp_ref, xn_ref, cw_ref, cb_ref, wg_ref, br_ref, bi_ref, lam_ref, o_ref,
         xe, a_scr, u_scr, h_scr) = refs
    else:
        (x_ref, xp_ref, xn_ref, ag_ref, hf_ref, cw_ref, cb_ref, wg_ref, br_ref, bi_ref, lam_ref, o_ref,
         xe, a_scr, u_scr, h_scr, hs_scr) = refs
    s = pl.program_id(1)
    j = (s - 1) if d == 0 else (lat_blocks - s)
    has_prev = jnp.logical_and(s > 0, j > 0)
    has_next = jnp.logical_and(s > 0, j < lat_blocks - 1)

    @pl.when(s == 0)
    def _():
        h_scr[...] = jnp.zeros_like(h_scr)

    _fill_halo(xe, [(x_ref, xp_ref, xn_ref)], has_prev, has_next)
    sp = jax.nn.softplus(-lam_ref[...])
    for p in range(LRU_WIDTH // 128):
        sl = slice(128 * p, 128 * (p + 1))
        xc = _centred_conv(xe, cw_ref, cb_ref, sl)
        g = jnp.dot(xc.astype(bf16), wg_ref[p], preferred_element_type=f32)
        r = jax.nn.sigmoid(g[:, :128] + br_ref[:, sl])
        i = jax.nn.sigmoid(g[:, 128:] + bi_ref[:, sl])
        log_a = -LRU_C * r * sp[:, sl]
        a = jnp.exp(log_a)
        a_scr[:, sl] = a
        u_scr[:, sl] = jnp.sqrt(-jnp.tanh(log_a) * (a * a + 1.0)) * (i * xc)

    out = o_ref if d == 0 else hs_scr
    ngrp = TQ // 8

    def body(gi, h):
        base = pl.multiple_of((gi if d == 0 else ngrp - 1 - gi) * 8, 8)
        for t in (range(8) if d == 0 else range(7, -1, -1)):
            h = a_scr[pl.ds(base + t, 1), :] * h + u_scr[pl.ds(base + t, 1), :]
            out[pl.ds(base + t, 1), :] = h
        return h

    h = lax.fori_loop(0, ngrp, body, h_scr[0:1, :])
    h_scr[0:1, :] = h
    if d == 1:
        y = (hf_ref[...] + hs_scr[...]) * _gelu_tanh(ag_ref[...])
        o_ref[...] = y.astype(o_ref.dtype)


def lru_pass(d, proj, hf, conv_w, conv_b, wg, b_r, b_i, lam, bsz, lat_blocks):
    t = proj.shape[0]
    _, blk = _seq_block(d, bsz, lat_blocks)
    row = lambda c: (lambda b, s: (blk(b, s), c))
    in_specs = _halo_specs(LRU_WIDTH, COL_AX, blk, t)
    args = [proj, proj, proj]
    if d == 1:
        in_specs += [pl.BlockSpec((TQ, LRU_WIDTH), row(COL_AG)), pl.BlockSpec((TQ, LRU_WIDTH), row(0))]
        args += [proj, hf]
    const = lambda shape: pl.BlockSpec(shape, lambda b, s: (0,) * len(shape))
    in_specs += [const((4, LRU_WIDTH)), const((1, LRU_WIDTH)), const((LRU_WIDTH // 128, 128, 256)),
                 const((1, LRU_WIDTH)), const((1, LRU_WIDTH)), const((1, LRU_WIDTH))]
    args += [conv_w, conv_b.reshape(1, -1), wg[d], b_r[d].reshape(1, -1), b_i[d].reshape(1, -1), lam[d].reshape(1, -1)]
    scratch = [pltpu.VMEM((TQ + 2 * HALO, LRU_WIDTH), f32), pltpu.VMEM((TQ, LRU_WIDTH), f32),
               pltpu.VMEM((TQ, LRU_WIDTH), f32), pltpu.VMEM((8, LRU_WIDTH), f32)]
    if d == 1:
        scratch.append(pltpu.VMEM((TQ, LRU_WIDTH), f32))
    return pl.pallas_call(
        functools.partial(_lru_kernel, d, lat_blocks),
        grid=(bsz, lat_blocks + 1),
        in_specs=in_specs,
        out_specs=pl.BlockSpec((TQ, LRU_WIDTH), row(0)),
        out_shape=jax.ShapeDtypeStruct((t, LRU_WIDTH), f32 if d == 0 else bf16),
        scratch_shapes=scratch,
        compiler_params=_params(),
        name=f"lru_pass{d}",
    )(*args)


def lru_branch(proj, conv_w, conv_b, w_r, b_r, w_i, b_i, lam, bsz, lat_blocks):
    wg = lru_gate_weights(w_r, w_i)
    hf = lru_pass(0, proj, None, conv_w, conv_b, wg, b_r, b_i, lam, bsz, lat_blocks)
    return lru_pass(1, proj, hf, conv_w, conv_b, wg, b_r, b_i, lam, bsz, lat_blocks)


def rope_tables(seq):
    pos = jnp.arange(seq)
    row = (pos // GRID_W).astype(f32)
    col = (pos % GRID_W).astype(f32)
    n_freq = SSD_STATE // 4
    freqs = ROPE_BASE ** (-jnp.arange(n_freq, dtype=f32) / n_freq)
    ang = jnp.concatenate([row[:, None] * freqs, col[:, None] * freqs], axis=-1)
    cos, sin = jnp.cos(ang), jnp.sin(ang)
    cosf = jnp.concatenate([cos, cos], axis=-1)
    sinf = jnp.concatenate([-sin, sin], axis=-1)
    cosf = jnp.concatenate([cosf, jnp.ones((TQ, SSD_STATE), f32)], axis=0)
    sinf = jnp.concatenate([sinf, jnp.zeros((TQ, SSD_STATE), f32)], axis=0)
    return cosf, sinf


def head_expand_matrix(d):
    e = np.zeros((128, SSD_INNER), np.float32)
    for h in range(SSD_HEADS):
        e[SSD_HEADS * d + h, h * SSD_HEADDIM:(h + 1) * SSD_HEADDIM] = 1.0
    return jnp.asarray(e, bf16)


def _split3(a):
    hi = a.astype(bf16)
    r = a - hi.astype(f32)
    mid = r.astype(bf16)
    return hi, mid, (r - mid.astype(f32)).astype(bf16)


def _dot_exact_rhs01(a, m01):
    return sum(jnp.dot(p, m01, preferred_element_type=f32) for p in _split3(a))


def _dot_exact_lhs01(m01, a):
    return sum(jnp.dot(m01, p, preferred_element_type=f32) for p in _split3(a))


def _ssd_kernel(d, lat_blocks, *refs):
    xs_refs, bc_refs, refs = refs[0:3], refs[3:6], refs[6:]
    if d == 0:
        (dt_ref, cos_ref, sin_ref, cw_ref, cb_ref, dtb_ref, alog_ref, e_ref,
         o_ref, xe, h_scr) = refs
    else:
        (dt_ref, cos_ref, sin_ref, z_ref, y0_ref, cw_ref, cb_ref, dtb_ref, alog_ref, e_ref,
         dsk_ref, ng_ref, o_ref, xe, h_scr, y_scr) = refs
    s = pl.program_id(1)
    j = (s - 1) if d == 0 else (lat_blocks - s)
    has_prev = jnp.logical_and(s > 0, j > 0)
    has_next = jnp.logical_and(s > 0, j < lat_blocks - 1)
    q = SSD_CHUNK

    @pl.when(s == 0)
    def _():
        h_scr[...] = jnp.zeros_like(h_scr)

    _fill_halo(xe, [xs_refs, bc_refs], has_prev, has_next)
    xbc = _centred_conv(xe, cw_ref, cb_ref, slice(None))
    xbc = xbc * jax.nn.sigmoid(xbc)
    xs = xbc[:, :SSD_INNER]
    cosf, sinf = cos_ref[...], sin_ref[...]

    def rope(g, off):
        v = xbc[:, off + g * SSD_STATE: off + (g + 1) * SSD_STATE]
        return (v * cosf + pltpu.roll(v, SSD_STATE // 2, 1) * sinf).astype(bf16)

    bm = [rope(g, SSD_INNER) for g in range(SSD_GROUPS)]
    cm = [rope(g, SSD_INNER + SSD_GROUPS * SSD_STATE) for g in range(SSD_GROUPS)]
    dt = jax.nn.softplus(dt_ref[...] + dtb_ref[...])
    delta = dt * (-jnp.exp(alog_ref[...]))
    ri = lax.broadcasted_iota(jnp.int32, (q, q), 0)
    ci = lax.broadcasted_iota(jnp.int32, (q, q), 1)
    keep = (ci <= ri) if d == 0 else (ci >= ri)
    tri = jnp.where(keep, 1.0, 0.0).astype(bf16)
    lane = lax.broadcasted_iota(jnp.int32, (1, 2 * SSD_HEADDIM), 1)
    halves = (lane < SSD_HEADDIM, lane >= SSD_HEADDIM)
    e = e_ref[...]
    last = q - 1 if d == 0 else 0
    out = o_ref if d == 0 else y_scr

    for c in (range(TQ // q) if d == 0 else range(TQ // q - 1, -1, -1)):
        rows = slice(c * q, (c + 1) * q)
        at = _dot_exact_lhs01(tri, delta[rows])
        at_exp = _dot_exact_rhs01(at, e)
        dt_exp = _dot_exact_rhs01(dt[rows], e)
        tot_exp = at_exp[last:last + 1, :]
        xdt = xs[rows] * dt_exp
        xd = (xdt * jnp.exp(tot_exp - at_exp)).astype(bf16)
        eat = jnp.exp(at_exp)
        cdec = jnp.exp(tot_exp)
        at_row = at.T
        ys = []
        for g in range(SSD_GROUPS):
            bg, cg = bm[g][rows], cm[g][rows]
            cb = _dot_nt(cg, bg)
            ht = h_scr[g]
            yoff = jnp.dot(cg, ht.astype(bf16), preferred_element_type=f32) * eat[:, g * GW:(g + 1) * GW]
            for pp in range(2):
                xpair = xdt[:, g * GW + pp * 128: g * GW + (pp + 1) * 128]
                acc = yoff[:, pp * 128:(pp + 1) * 128]
                for hh in range(2):
                    li = SSD_HEADS * d + 4 * g + 2 * pp + hh
                    seg = at[:, li:li + 1] - at_row[li:li + 1, :]
                    m = (cb * jnp.exp(jnp.where(keep, seg, NEG))).astype(bf16)
                    xm = jnp.where(halves[hh], xpair, 0.0).astype(bf16)
                    acc = acc + jnp.dot(m, xm, preferred_element_type=f32)
                ys.append(acc)
            upd = lax.dot_general(bg, xd[:, g * GW:(g + 1) * GW], (((0,), (0,)), ((), ())),
                                  preferred_element_type=f32)
            h_scr[g] = cdec[:, g * GW:(g + 1) * GW] * ht + upd
        out[rows, :] = jnp.concatenate(ys, axis=-1)

    if d == 1:
        y = y0_ref[...] + y_scr[...] + dsk_ref[...] * xs
        z = z_ref[...]
        y = y * (z * jax.nn.sigmoid(z))
        y = y * lax.rsqrt(jnp.mean(y * y, axis=-1, keepdims=True) + EPS)
        o_ref[...] = (y * ng_ref[...]).astype(o_ref.dtype)


def ssd_pass(d, proj, y0, cosf, sinf, conv_w, conv_b, a_log, dt_bias, d_skip, norm_g, bsz, lat_blocks):
    t = proj.shape[0]
    jj, blk = _seq_block(d, bsz, lat_blocks)
    row = lambda c: (lambda b, s: (blk(b, s), c))
    tbl = lambda b, s: (jnp.where(s == 0, lat_blocks, jj(s)), 0)
    const = lambda shape: pl.BlockSpec(shape, lambda b, s: (0,) * len(shape))
    pad128 = lambda v: jnp.pad(v.reshape(1, -1).astype(f32), ((0, 0), (0, 128 - v.size)))
    in_specs = _halo_specs(SSD_INNER, COL_XS, blk, t) + _halo_specs(SSD_CONV_DIM - SSD_INNER, COL_BC, blk, t) + [
        pl.BlockSpec((TQ, 128), row(COL_DT)),
        pl.BlockSpec((TQ, SSD_STATE), tbl),
        pl.BlockSpec((TQ, SSD_STATE), tbl),
    ]
    args = [proj] * 7 + [cosf, sinf]
    if d == 1:
        in_specs += [pl.BlockSpec((TQ, SSD_INNER), row(COL_Z)), pl.BlockSpec((TQ, SSD_INNER), row(0))]
        args += [proj, y0]
    in_specs += [const((4, SSD_CONV_DIM)), const((1, SSD_CONV_DIM)), const((1, 128)), const((1, 128)),
                 const((128, SSD_INNER))]
    args += [conv_w, conv_b.reshape(1, -1), pad128(dt_bias), pad128(a_log), head_expand_matrix(d)]
    scratch = [pltpu.VMEM((TQ + 2 * HALO, SSD_CONV_DIM), f32), pltpu.VMEM((SSD_GROUPS, SSD_STATE, GW), f32)]
    if d == 1:
        in_specs += [const((1, SSD_INNER)), const((1, SSD_INNER))]
        args += [jnp.repeat(d_skip, SSD_HEADDIM).reshape(1, -1), norm_g.reshape(1, -1)]
        scratch.append(pltpu.VMEM((TQ, SSD_INNER), f32))
    return pl.pallas_call(
        functools.partial(_ssd_kernel, d, lat_blocks),
        grid=(bsz, lat_blocks + 1),
        in_specs=in_specs,
        out_specs=pl.BlockSpec((TQ, SSD_INNER), row(0)),
        out_shape=jax.ShapeDtypeStruct((t, SSD_INNER), f32 if d == 0 else bf16),
        scratch_shapes=scratch,
        compiler_params=_params(),
        name=f"ssd_pass{d}",
    )(*args)


def ssd_branch(proj, cosf, sinf, conv_w, conv_b, a_log, dt_bias, d_skip, norm_g, bsz, lat_blocks):
    y0 = ssd_pass(0, proj, None, cosf, sinf, conv_w, conv_b, a_log, dt_bias, d_skip, norm_g, bsz, lat_blocks)
    return ssd_pass(1, proj, y0, cosf, sinf, conv_w, conv_b, a_log, dt_bias, d_skip, norm_g, bsz, lat_blocks)


def _tile_rows(mod_rows, bsz):
    idx = np.concatenate([np.repeat(np.arange(bsz), SEQ // TM), np.full(bsz * CTX_LEN // TM, bsz)])
    return mod_rows[idx][:, None, :]


def _expert_choice_moe(h, v, w_router, w1, w3, w2, layer, g2_rows, bsz):
    t, d = v.shape
    aff = router_affinity(v, w_router)[:, :N_EXPERTS]
    nl = bsz * SEQ

    def choose(a, length):
        cap = CAPACITY_FACTOR * length // N_EXPERTS
        g, idx = lax.top_k(jnp.swapaxes(a.reshape(bsz, length, N_EXPERTS), 1, 2), cap)
        return jnp.swapaxes(g, 0, 1), jnp.swapaxes(idx, 0, 1), cap

    g_l, i_l, cap_l = choose(aff[:nl], SEQ)
    g_c, i_c, cap_c = choose(aff[nl:], CTX_LEN)
    boff = jnp.arange(bsz)[None, :, None]
    ctx_rows = (i_c + boff * CTX_LEN).reshape(N_EXPERTS, -1)
    rows = jnp.concatenate([(i_l + boff * SEQ).reshape(N_EXPERTS, -1), ctx_rows + nl], axis=1)
    gates = jnp.concatenate([g_l.reshape(N_EXPERTS, -1), g_c.reshape(N_EXPERTS, -1)], axis=1)[..., None]
    r = rows.shape[1]
    xg = jnp.take(v, rows.reshape(-1), axis=0).reshape(N_EXPERTS, r, d)
    hdn = expert_hidden(xg, w1, w3, layer, r // 4)
    ctx_slots, ctx_tile = bsz * cap_c, bsz * CTX_LEN
    assert (bsz * cap_l) % ctx_slots == 0 and nl % ctx_tile == 0
    out = moe_combine(i_l.reshape(-1), hdn, 0, cap_l, gates, w2, layer, h, g2_rows, 0, 0, SEQ, TN, bsz)
    return moe_combine(ctx_rows.reshape(-1), hdn, bsz * cap_l // ctx_slots, ctx_slots, gates, w2, layer, h, g2_rows,
                       bsz, nl // ctx_tile, ctx_tile, d, 1, prev=out)


def kernel(x, c, ctx, c_ctx, w_ada, b_ada, norm_mix, norm_ffn, w_in, lru_conv_w, lru_conv_b, lru_w_r, lru_b_r,
           lru_w_i, lru_b_i, lru_lambda, ssd_conv_w, ssd_conv_b, ssd_a_log, ssd_dt_bias, ssd_d, ssd_norm, na_rpb,
           w_branch_lru, w_branch_ssd, w_branch_na, w_out, w_router, w1, w3, w2, norm_final):
    bsz = x.shape[0]
    assert x.shape[1:] == (SEQ, D_MODEL) and ctx.shape[1:] == (CTX_LEN, D_MODEL) and bsz * CTX_LEN == TM
    d = D_MODEL
    h = jnp.concatenate([x.reshape(bsz * SEQ, d), ctx.reshape(bsz * CTX_LEN, d)], axis=0)
    t = h.shape[0]

    cond = jnp.concatenate([c, c_ctx[None, :], jnp.zeros((8 - bsz - 1, d), f32)], axis=0)
    mod = ada_modulation(cond, w_ada, b_ada)
    cosf, sinf = rope_tables(SEQ)

    assert DT_END + DT_PAD == W_QKV0 and w_in.shape[-1] == DT_END + 3 * NA_WIDTH + 3 * d
    w_proj = relayout_proj_weights(jnp.swapaxes(w_in, 1, 2))
    qkv_blocks = 3 * NA_WIDTH // TN
    pa, pb, pc, wo = (w.astype(bf16) for w in (w_branch_lru, w_branch_ssd, w_branch_na, w_out))
    na_tbl, na_pair_of = na_bias_blocks(na_rpb)

    for l in range(DEPTH):
        mods = jnp.split(mod[l], 6, axis=-1)
        sh1, sc1, g1, sh2, sc2 = (_tile_rows(m, bsz) for m in mods[:5])
        u = norm_modulate(h, norm_mix[l], sc1, sh1, bf16)
        proj = matmul_bf16(u, w_proj, l, N_F32, lambda j: jnp.where(j < W_QKV0 // TN, j, j + qkv_blocks), f32)
        qkv = matmul_bf16(u, w_proj, l, 3 * NA_WIDTH, lambda j: j + W_QKV0 // TN, bf16)
        ya = lru_branch(proj, lru_conv_w[l], lru_conv_b[l], lru_w_r[l], lru_b_r[l], lru_w_i[l], lru_b_i[l],
                        lru_lambda[l], bsz, LAT_BLOCKS)
        yb = ssd_branch(proj, cosf, sinf, ssd_conv_w[l], ssd_conv_b[l], ssd_a_log[l], ssd_dt_bias[l], ssd_d[l],
                        ssd_norm[l], bsz, LAT_BLOCKS)
        yc = na_attention(qkv, na_tbl, l, na_pair_of, bsz)
        y = branch_merge(ya, yb, yc, proj, pa, pb, pc, l, COL_G)
        h = residual_matmul(y, wo, l, h, g1)
        v = norm_modulate(h, norm_ffn[l], sc2, sh2, bf16)
        h = _expert_choice_moe(h, v, w_router[l], w1, w3, w2, l, mods[5][:, None, :], bsz)

    zeros = jnp.zeros((t // TM, 1, d), f32)
    out = norm_modulate(h, norm_final, zeros, zeros, f32)
    return out[:bsz * SEQ].reshape(bsz, SEQ, d)
```

```python
import functools

import jax
import jax.numpy as jnp
import numpy as np
from jax import lax
from jax.experimental import pallas as pl
from jax.experimental.pallas import tpu as pltpu

D_MODEL = 2048
SEQ = 4096
CTX_LEN = 256
DEPTH = 4
GRID_W = 64
EPS = 1e-6
ROPE_BASE = 10000.0
LRU_WIDTH = 1024
LRU_BLOCKS = 16
LRU_BLOCK = LRU_WIDTH // LRU_BLOCKS
LRU_C = 8.0
SSD_INNER = 1024
SSD_HEADDIM = 64
SSD_HEADS = SSD_INNER // SSD_HEADDIM
SSD_GROUPS = 4
SSD_STATE = 128
SSD_CHUNK = 128
SSD_CONV_DIM = SSD_INNER + 2 * SSD_GROUPS * SSD_STATE
NA_HEADS = 16
NA_HEADDIM = 64
NA_WIDTH = NA_HEADS * NA_HEADDIM
NA_WIN_R = 8
NA_WIN_C = 16
N_EXPERTS = 16
EXPERT_FF = 1024
CAPACITY_FACTOR = 2

V7X_VMEM_LIMIT = 56 * 1024 * 1024
TM = 1024
MM_MAX_ROWS = 2304
TQ = 256
HALO = 8
LAT_BLOCKS = SEQ // TQ
NEG = -1e30
HI = lax.Precision.HIGHEST
bf16 = jnp.bfloat16
f32 = jnp.float32

TN = 512
DT_END = 2 * LRU_WIDTH + SSD_INNER + SSD_CONV_DIM + 2 * SSD_HEADS
DT_PAD = 480
COL_AX, COL_AG, COL_Z, COL_XS, COL_BC = 0, 1, 2, 3, 4
COL_DT = 40
W_QKV0 = 5632
COL_G = 5632
N_F32 = COL_G + 3 * D_MODEL
GW = SSD_INNER // SSD_GROUPS

NA_GROUP_ROWS = 2
NA_BAND_ROWS = 10
NA_NCFG = 5


def _params():
    return pltpu.CompilerParams(vmem_limit_bytes=V7X_VMEM_LIMIT)


def _ada_kernel(x_ref, w_ref, b_ref, o_ref):
    c = x_ref[...]
    x = (c * jax.nn.sigmoid(c)).astype(bf16)
    o_ref[...] = jnp.dot(x, w_ref[...].astype(bf16), preferred_element_type=f32) + b_ref[...]


def ada_modulation(cond, w_ada, b_ada):
    depth, d, n = w_ada.shape
    tn = 1024
    return pl.pallas_call(
        _ada_kernel,
        grid=(depth, n // tn),
        in_specs=[pl.BlockSpec((8, d), lambda l, j: (0, 0)),
                  pl.BlockSpec((None, d, tn), lambda l, j: (l, 0, j)),
                  pl.BlockSpec((None, 1, tn), lambda l, j: (l, 0, j))],
        out_specs=pl.BlockSpec((None, 8, tn), lambda l, j: (l, 0, j)),
        out_shape=jax.ShapeDtypeStruct((depth, 8, n), f32),
        compiler_params=_params(),
        name="ada_modulation",
    )(cond, w_ada, b_ada.reshape(depth, 1, n))


def _norm_mod_kernel(h_ref, g_ref, sc_ref, sh_ref, o_ref):
    x = h_ref[...]
    y = x * lax.rsqrt(jnp.mean(x * x, axis=-1, keepdims=True) + EPS)
    o_ref[...] = ((y * g_ref[...]) * (1.0 + sc_ref[...]) + sh_ref[...]).astype(o_ref.dtype)


def norm_modulate(h, g, sc_t, sh_t, out_dtype):
    t, d = h.shape
    tm = min(512, TM)
    per = TM // tm
    return pl.pallas_call(
        _norm_mod_kernel,
        grid=(t // tm,),
        in_specs=[pl.BlockSpec((tm, d), lambda i: (i, 0)),
                  pl.BlockSpec((1, d), lambda i: (0, 0)),
                  pl.BlockSpec((None, 1, d), lambda i: (i // per, 0, 0)),
                  pl.BlockSpec((None, 1, d), lambda i: (i // per, 0, 0))],
        out_specs=pl.BlockSpec((tm, d), lambda i: (i, 0)),
        out_shape=jax.ShapeDtypeStruct((t, d), out_dtype),
        compiler_params=_params(),
        name="norm_modulate",
    )(h, g.reshape(1, d), sc_t, sh_t)


def _relayout_kernel(shift, a_ref, b_ref, o_ref):
    j = pl.program_id(1)
    split = DT_END // TN

    @pl.when(j < split)
    def _():
        o_ref[...] = a_ref[...].astype(o_ref.dtype)

    @pl.when(j == split)
    def _():
        row = lax.broadcasted_iota(jnp.int32, a_ref.shape, 0)
        o_ref[...] = jnp.where(row < shift, a_ref[...], 0.0).astype(o_ref.dtype)

    @pl.when(j > split)
    def _():
        o_ref[0:TN - shift, :] = a_ref[shift:TN, :].astype(o_ref.dtype)
        o_ref[TN - shift:TN, :] = b_ref[...].astype(o_ref.dtype)


def relayout_proj_weights(w_t):
    depth, n, d = w_t.shape
    shift = DT_END % TN
    assert shift + DT_PAD == TN and shift % 16 == 0 and n % shift == 0
    split = DT_END // TN
    return pl.pallas_call(
        functools.partial(_relayout_kernel, shift),
        grid=(depth, (n + DT_PAD) // TN),
        in_specs=[pl.BlockSpec((None, TN, d), lambda l, j: (l, jnp.where(j <= split, j, j - 1), 0)),
                  pl.BlockSpec((None, shift, d), lambda l, j: (l, jnp.where(j <= split, 0, (TN // shift) * j), 0))],
        out_specs=pl.BlockSpec((None, TN, d), lambda l, j: (l, j, 0)),
        out_shape=jax.ShapeDtypeStruct((depth, n + DT_PAD, d), bf16),
        compiler_params=_params(),
        name="relayout_proj_weights",
    )(w_t, w_t)


def _mm_kernel(x_ref, w_ref, o_ref):
    o_ref[...] = _dot_nt(x_ref[...], w_ref[...]).astype(o_ref.dtype)


def matmul_bf16(x, w, layer, n_out, w_block, out_dtype):
    m, k = x.shape
    tm = max(c for c in range(16, MM_MAX_ROWS + 1, 16) if m % c == 0)
    return pl.pallas_call(
        _mm_kernel,
        grid=(m // tm, n_out // TN),
        in_specs=[pl.BlockSpec((tm, k), lambda i, j: (i, 0)),
                  pl.BlockSpec((None, TN, k), lambda i, j: (layer, w_block(j), 0))],
        out_specs=pl.BlockSpec((tm, TN), lambda i, j: (i, j)),
        out_shape=jax.ShapeDtypeStruct((m, n_out), out_dtype),
        compiler_params=_params(),
        name="matmul_bf16",
    )(x, w)


def _merge_kernel(ya_ref, yb_ref, yc_ref, ga_ref, gb_ref, gc_ref, pa_ref, pb_ref, pc_ref, o_ref):
    acc = jax.nn.sigmoid(ga_ref[...]) * jnp.dot(ya_ref[...], pa_ref[...], preferred_element_type=f32)
    acc = acc + jax.nn.sigmoid(gb_ref[...]) * jnp.dot(yb_ref[...], pb_ref[...], preferred_element_type=f32)
    acc = acc + jax.nn.sigmoid(gc_ref[...]) * jnp.dot(yc_ref[...], pc_ref[...], preferred_element_type=f32)
    o_ref[...] = acc.astype(o_ref.dtype)


def branch_merge(ya, yb, yc, proj, pa, pb, pc, layer, g_col0):
    t, k = ya.shape
    n = pa.shape[-1]
    tn = 512
    gb0 = g_col0 // tn
    nj = n // tn
    xs = pl.BlockSpec((TM, k), lambda i, j: (i, 0))
    ws = pl.BlockSpec((None, k, tn), lambda i, j: (layer, 0, j))
    gs = lambda q: pl.BlockSpec((TM, tn), lambda i, j: (i, gb0 + q * nj + j))
    return pl.pallas_call(
        _merge_kernel,
        grid=(t // TM, nj),
        in_specs=[xs, xs, xs, gs(0), gs(1), gs(2), ws, ws, ws],
        out_specs=pl.BlockSpec((TM, tn), lambda i, j: (i, j)),
        out_shape=jax.ShapeDtypeStruct((t, n), bf16),
        compiler_params=_params(),
        name="branch_merge",
    )(ya, yb, yc, proj, proj, proj, pa, pb, pc)


def _resid_norm_router_kernel(y_ref, w_ref, h_ref, g1_ref, ng_ref, sc_ref, sh_ref, wr_ref, ho_ref, v_ref, aff_ref):
    hn = h_ref[...] + g1_ref[...] * jnp.dot(y_ref[...], w_ref[...], preferred_element_type=f32)
    ho_ref[...] = hn
    yn = hn * lax.rsqrt(jnp.mean(hn * hn, axis=-1, keepdims=True) + EPS)
    v = ((yn * ng_ref[...]) * (1.0 + sc_ref[...]) + sh_ref[...]).astype(bf16)
    v_ref[...] = v
    s = jnp.dot(v, wr_ref[...], preferred_element_type=f32)
    lane = lax.broadcasted_iota(jnp.int32, s.shape, 1)
    s = jnp.where(lane < N_EXPERTS, s, NEG)
    e = jnp.exp(s - s.max(axis=-1, keepdims=True))
    aff_ref[...] = e / e.sum(axis=-1, keepdims=True)


def residual_norm_router(y, w, layer, h, g1_t, norm_g, sc_t, sh_t, w_router):
    t, d = h.shape
    tm = TM // 2
    wr = jnp.pad(w_router, ((0, 0), (0, 128 - N_EXPERTS))).astype(bf16)
    rows = pl.BlockSpec((tm, d), lambda i: (i, 0))
    per_tile = pl.BlockSpec((None, 1, d), lambda i: (i * tm // TM, 0, 0))
    return pl.pallas_call(
        _resid_norm_router_kernel,
        grid=(t // tm,),
        in_specs=[rows, pl.BlockSpec((None, d, d), lambda i: (layer, 0, 0)), rows, per_tile,
                  pl.BlockSpec((1, d), lambda i: (0, 0)), per_tile, per_tile,
                  pl.BlockSpec((d, 128), lambda i: (0, 0))],
        out_specs=[rows, rows, pl.BlockSpec((tm, 128), lambda i: (i, 0))],
        out_shape=[jax.ShapeDtypeStruct((t, d), f32), jax.ShapeDtypeStruct((t, d), bf16),
                   jax.ShapeDtypeStruct((t, 128), f32)],
        compiler_params=_params(),
        name="residual_norm_router",
    )(y, w, h, g1_t, norm_g.reshape(1, d), sc_t, sh_t, wr)


def _expert_hidden_kernel(x_ref, w1_ref, w3_ref, o_ref, w1b, w3b):
    @pl.when(pl.program_id(1) == 0)
    def _():
        w1b[...] = w1_ref[...].astype(bf16)
        w3b[...] = w3_ref[...].astype(bf16)

    x = x_ref[...]
    a = jnp.dot(x, w1b[...], preferred_element_type=f32)
    b = jnp.dot(x, w3b[...], preferred_element_type=f32)
    o_ref[...] = ((a * jax.nn.sigmoid(a)) * b).astype(o_ref.dtype)


def expert_hidden(xg, w1, w3, layer, tm):
    e, r, d = xg.shape
    f = w1.shape[-1]
    return pl.pallas_call(
        _expert_hidden_kernel,
        grid=(e, r // tm),
        in_specs=[pl.BlockSpec((None, tm, d), lambda k, i: (k, i, 0)),
                  pl.BlockSpec((None, None, d, f), lambda k, i: (layer, k, 0, 0)),
                  pl.BlockSpec((None, None, d, f), lambda k, i: (layer, k, 0, 0))],
        out_specs=pl.BlockSpec((None, tm, f), lambda k, i: (k, i, 0)),
        out_shape=jax.ShapeDtypeStruct((e, r, f), bf16),
        scratch_shapes=[pltpu.VMEM((d, f), bf16), pltpu.VMEM((d, f), bf16)],
        compiler_params=_params(),
        name="expert_hidden",
    )(xg, w1, w3)


SCATTER_UNROLL = 8
SCATTER_PART = 128


def _combine_kernel(nb, slots, idx_ref, hdn_ref, w2_ref, g_ref, h_ref, g2_ref, *rest):
    o_ref, ye_scr = rest[-2], rest[-1]
    b, e = pl.program_id(0), pl.program_id(2)

    @pl.when(e == 0)
    def _():
        o_ref[...] = jnp.zeros_like(o_ref)

    w2 = w2_ref[...].astype(bf16)
    base0 = (e * nb + b) * slots
    part = min(slots, SCATTER_PART)

    for p0 in range(0, slots, part):
        ye_scr[p0:p0 + part, :] = (jnp.dot(hdn_ref[p0:p0 + part, :], w2, preferred_element_type=f32)
                                   * g_ref[p0:p0 + part, :])
        for s0 in range(p0, p0 + part, SCATTER_UNROLL):
            rows = [idx_ref[base0 + s0 + k] for k in range(SCATTER_UNROLL)]
            vals = [o_ref[pl.ds(rows[k], 1), :] + ye_scr[s0 + k:s0 + k + 1, :] for k in range(SCATTER_UNROLL)]
            for k in range(SCATTER_UNROLL):
                o_ref[pl.ds(rows[k], 1), :] = vals[k]

    @pl.when(e == pl.num_programs(2) - 1)
    def _():
        o_ref[...] = h_ref[...] + g2_ref[...] * o_ref[...]


def moe_combine(idx, hdn, slot_blk0, slots, gates, w2, layer, h, g2_rows, g2_row0, row_blk0, rows_blk, dq, nb):
    e, _, f = hdn.shape
    t, d = h.shape
    in_specs = [
        pl.BlockSpec((None, slots, f), lambda b, q, k, idx: (k, slot_blk0 + b, 0)),
        pl.BlockSpec((None, None, f, dq), lambda b, q, k, idx: (layer, k, 0, q)),
        pl.BlockSpec((None, slots, 1), lambda b, q, k, idx: (k, slot_blk0 + b, 0)),
        pl.BlockSpec((rows_blk, dq), lambda b, q, k, idx: (row_blk0 + b, q)),
        pl.BlockSpec((None, 1, dq), lambda b, q, k, idx: (g2_row0 + b, 0, q)),
    ]
    args = [idx, hdn, w2, gates, h, g2_rows]
    aliases = {4: 0}
    return pl.pallas_call(
        functools.partial(_combine_kernel, nb, slots),
        grid_spec=pltpu.PrefetchScalarGridSpec(
            num_scalar_prefetch=1,
            grid=(nb, d // dq, e),
            in_specs=in_specs,
            out_specs=pl.BlockSpec((rows_blk, dq), lambda b, q, k, idx: (row_blk0 + b, q)),
            scratch_shapes=[pltpu.VMEM((slots, dq), f32)],
        ),
        out_shape=jax.ShapeDtypeStruct((t, d), f32),
        input_output_aliases=aliases,
        compiler_params=_params(),
        name="moe_combine",
    )(*args)


def na_bias_blocks(rpb):
    rows = SEQ // GRID_W
    n_dr, n_dc = 2 * NA_WIN_R - 1, 2 * NA_WIN_C - 1
    qc, kc = np.arange(GRID_W)[:, None], np.arange(GRID_W)[None, :]
    cs = np.clip(qc - NA_WIN_C // 2, 0, GRID_W - NA_WIN_C)
    col_ok = (kc >= cs) & (kc < cs + NA_WIN_C)
    pick = (np.arange(n_dc)[:, None, None] == (kc - qc + NA_WIN_C - 1)[None]) & col_ok[None]
    blocks = jnp.einsum('...rd,dqk->...rqk', rpb, jnp.asarray(pick, f32), precision=HI)
    blocks = jnp.where(col_ok, blocks, NEG)
    blocks = jnp.concatenate([blocks, jnp.full(blocks.shape[:-3] + (1, GRID_W, GRID_W), NEG, f32)], axis=-3)
    which = np.full((NA_NCFG, NA_GROUP_ROWS, NA_BAND_ROWS), n_dr, np.int32)
    for c, r0 in enumerate((0, 2, 4, 60, 62)):
        bs = int(np.clip(r0 - NA_WIN_R // 2, 0, rows - NA_BAND_ROWS))
        for qr in range(NA_GROUP_ROWS):
            r = r0 + qr
            rs = int(np.clip(r - NA_WIN_R // 2, 0, rows - NA_WIN_R))
            for kr in range(NA_BAND_ROWS):
                if rs <= bs + kr < rs + NA_WIN_R:
                    which[c, qr, kr] = bs + kr - r + NA_WIN_R - 1
    pairs = sorted({(int(which[c, qr, 2 * m]), int(which[c, qr, 2 * m + 1]))
                    for c in range(NA_NCFG) for qr in range(NA_GROUP_ROWS) for m in range(NA_BAND_ROWS // 2)})
    pair_of = np.array([[[pairs.index((int(which[c, qr, 2 * m]), int(which[c, qr, 2 * m + 1])))
                          for m in range(NA_BAND_ROWS // 2)] for qr in range(NA_GROUP_ROWS)]
                        for c in range(NA_NCFG)], np.int32)
    left = jnp.take(blocks, np.array([p[0] for p in pairs]), axis=-3)
    right = jnp.take(blocks, np.array([p[1] for p in pairs]), axis=-3)
    return jnp.concatenate([left, right], axis=-1), pair_of.reshape(-1)


def _dot_nt(a, b):
    return lax.dot_general(a, b, (((1,), (1,)), ((), ())), preferred_element_type=f32)


def _softmax_av(parts):
    m = parts[0][0].max(axis=-1, keepdims=True)
    for s, _ in parts[1:]:
        m = jnp.maximum(m, s.max(axis=-1, keepdims=True))
    l = None
    o = None
    for s, v in parts:
        p = jnp.exp(s - m)
        ls = p.sum(axis=-1, keepdims=True)
        os_ = jnp.dot(p.astype(bf16), v, preferred_element_type=f32)
        l = ls if l is None else l + ls
        o = os_ if o is None else o + os_
    return o / l


def _na_kernel(pair_ref, q_ref, k_ref, v_ref, kc_ref, vc_ref, tbl_ref, o_ref):
    j = pl.program_id(2)
    npairs = NA_BAND_ROWS // 2

    def bias(hh, cfg):
        rows = []
        for qr in range(NA_GROUP_ROWS):
            base = (cfg * NA_GROUP_ROWS + qr) * npairs
            rows.append(jnp.concatenate([tbl_ref[hh, pair_ref[base + m]] for m in range(npairs)], axis=1))
        return jnp.concatenate(rows, axis=0)

    scale = NA_HEADDIM ** -0.5
    assert scale == 2.0 ** round(np.log2(scale))
    lane = lax.broadcasted_iota(jnp.int32, (1, 2 * NA_HEADDIM), 1)
    first = lane < NA_HEADDIM
    kc = kc_ref[...]
    vc = vc_ref[...]
    gq = NA_GROUP_ROWS * GRID_W

    def head_q(q, hh):
        keep = first if hh == 0 else jnp.logical_not(first)
        return jnp.where(keep, q * scale, jnp.zeros_like(q))

    @pl.when(j < LAT_BLOCKS)
    def _():
        ngroups = TQ // gq
        q2 = []
        for g in range(ngroups):
            q = q_ref[pl.ds(g * gq, gq), :]
            q2.append(jnp.concatenate([head_q(q, 0), head_q(q, 1)], axis=0))
        sc_all = _dot_nt(jnp.concatenate(q2, axis=0), kc)
        o_win, p_ctx, denom = [], [], []
        for g in range(ngroups):
            r0 = j * (TQ // GRID_W) + NA_GROUP_ROWS * g
            bs = jnp.clip(r0 - NA_WIN_R // 2, 0, SEQ // GRID_W - NA_BAND_ROWS)
            cfg = jnp.where(r0 == 0, 0, jnp.where(r0 == 2, 1, jnp.where(r0 == 60, 3, jnp.where(r0 == 62, 4, 2))))
            start = pl.multiple_of(bs * GRID_W, GRID_W)
            kb = k_ref[pl.ds(start, NA_BAND_ROWS * GRID_W), :]
            vb = v_ref[pl.ds(start, NA_BAND_ROWS * GRID_W), :]
            s = _dot_nt(q2[g], kb) + jnp.concatenate([bias(0, cfg), bias(1, cfg)], axis=0)
            sc = sc_all[2 * gq * g:2 * gq * (g + 1)]
            m = jnp.maximum(s.max(axis=-1, keepdims=True), sc.max(axis=-1, keepdims=True))
            p, pc = jnp.exp(s - m), jnp.exp(sc - m)
            denom.append(p.sum(axis=-1, keepdims=True) + pc.sum(axis=-1, keepdims=True))
            o_win.append(jnp.dot(p.astype(bf16), vb, preferred_element_type=f32))
            p_ctx.append(pc.astype(bf16))
        o_ctx = jnp.dot(jnp.concatenate(p_ctx, axis=0), vc, preferred_element_type=f32)
        for g in range(ngroups):
            o2 = (o_win[g] + o_ctx[2 * gq * g:2 * gq * (g + 1)]) / denom[g]
            o_ref[pl.ds(g * gq, gq), :] = jnp.where(first, o2[:gq], o2[gq:]).astype(o_ref.dtype)

    @pl.when(j == LAT_BLOCKS)
    def _():
        q = q_ref[...]
        q2 = jnp.concatenate([head_q(q, 0), head_q(q, 1)], axis=0)
        o2 = _softmax_av([(_dot_nt(q2, kc), vc)])
        o_ref[...] = jnp.where(first, o2[:CTX_LEN], o2[CTX_LEN:]).astype(o_ref.dtype)


def na_attention(qkv, tbl, layer, pair_of, bsz):
    t = qkv.shape[0]
    npair = NA_HEADS // 2
    ctx0 = bsz * LAT_BLOCKS

    def qrow(b, j):
        return jnp.where(j < LAT_BLOCKS, b * LAT_BLOCKS + j, ctx0 + b)

    return pl.pallas_call(
        _na_kernel,
        grid_spec=pltpu.PrefetchScalarGridSpec(
            num_scalar_prefetch=1,
            grid=(bsz, npair, LAT_BLOCKS + 1),
            in_specs=[
                pl.BlockSpec((TQ, 128), lambda b, p, j, po: (qrow(b, j), p)),
                pl.BlockSpec((SEQ, 128), lambda b, p, j, po: (b, npair + p)),
                pl.BlockSpec((SEQ, 128), lambda b, p, j, po: (b, 2 * npair + p)),
                pl.BlockSpec((CTX_LEN, 128), lambda b, p, j, po: (ctx0 + b, npair + p)),
                pl.BlockSpec((CTX_LEN, 128), lambda b, p, j, po: (ctx0 + b, 2 * npair + p)),
                pl.BlockSpec((None, 2) + tbl.shape[2:], lambda b, p, j, po: (layer, p, 0, 0, 0)),
            ],
            out_specs=pl.BlockSpec((TQ, 128), lambda b, p, j, po: (qrow(b, j), p)),
        ),
        out_shape=jax.ShapeDtypeStruct((t, NA_WIDTH), bf16),
        compiler_params=_params(),
        name="na_attention",
    )(jnp.asarray(pair_of), qkv, qkv, qkv, qkv, qkv, tbl)


def _seq_block(d, bsz, lat_blocks):
    ctx0 = bsz * lat_blocks

    def jj(s):
        return (s - 1) if d == 0 else (lat_blocks - s)

    def blk(b, s):
        return jnp.where(s == 0, ctx0 + b, b * lat_blocks + jj(s))

    return jj, blk


def _halo_specs(width, col, blk, t):
    nb8 = TQ // HALO
    return [
        pl.BlockSpec((TQ, width), lambda b, s: (blk(b, s), col)),
        pl.BlockSpec((HALO, width), lambda b, s: (jnp.maximum(blk(b, s) * nb8 - 1, 0), col)),
        pl.BlockSpec((HALO, width), lambda b, s: (jnp.minimum(blk(b, s) * nb8 + nb8, t // HALO - 1), col)),
    ]


def _fill_halo(xe, parts, has_prev, has_next):
    c0 = 0
    for x_ref, xp_ref, xn_ref in parts:
        cols = slice(c0, c0 + x_ref.shape[1])
        xe[0:HALO, cols] = jnp.where(has_prev, xp_ref[...], 0.0)
        xe[HALO:HALO + TQ, cols] = x_ref[...]
        xe[HALO + TQ:2 * HALO + TQ, cols] = jnp.where(has_next, xn_ref[...], 0.0)
        c0 += x_ref.shape[1]


def _centred_conv(xe, cw_ref, cb_ref, cols):
    xv = xe[:, cols]
    n = xv.shape[0]
    out = cb_ref[:, cols]
    for k in range(4):
        shifted = xv if k == 2 else pltpu.roll(xv, (2 - k) % n, 0)
        out = out + cw_ref[k:k + 1, cols] * shifted[HALO:HALO + TQ, :]
    return out


def lru_gate_weights(w_r, w_i):
    def pair(w):
        w = w.reshape(2, LRU_BLOCKS // 2, 2, LRU_BLOCK, LRU_BLOCK)
        z = jnp.zeros_like(w[:, :, 0])
        top = jnp.concatenate([w[:, :, 0], z], axis=-1)
        bot = jnp.concatenate([z, w[:, :, 1]], axis=-1)
        return jnp.concatenate([top, bot], axis=-2)
    return jnp.concatenate([pair(w_r), pair(w_i)], axis=-1).astype(bf16)


def _gelu_tanh(x):
    return 0.5 * x * (1.0 + jnp.tanh(float(np.sqrt(2.0 / np.pi)) * (x + 0.044715 * (x * x * x))))


def _lru_kernel(d, lat_blocks, *refs):
    if d == 0:
        (x_ref, xp_ref, xn_ref, cw_ref, cb_ref, wg_ref, br_ref, bi_ref, lam_ref, o_ref,
         xe, a_scr, u_scr, h_scr) = refs
    else:
        (x_ref, xp_ref, xn_ref, ag_ref, hf_ref, cw_ref, cb_ref, wg_ref, br_ref, bi_ref, lam_ref, o_ref,
         xe, a_scr, u_scr, h_scr, hs_scr) = refs
    s = pl.program_id(1)
    j = (s - 1) if d == 0 else (lat_blocks - s)
    has_prev = jnp.logical_and(s > 0, j > 0)
    has_next = jnp.logical_and(s > 0, j < lat_blocks - 1)

    @pl.when(s == 0)
    def _():
        h_scr[...] = jnp.zeros_like(h_scr)

    _fill_halo(xe, [(x_ref, xp_ref, xn_ref)], has_prev, has_next)
    sp = jax.nn.softplus(-lam_ref[...])
    for p in range(LRU_WIDTH // 128):
        sl = slice(128 * p, 128 * (p + 1))
        xc = _centred_conv(xe, cw_ref, cb_ref, sl)
        g = jnp.dot(xc.astype(bf16), wg_ref[p], preferred_element_type=f32)
        r = jax.nn.sigmoid(g[:, :128] + br_ref[:, sl])
        i = jax.nn.sigmoid(g[:, 128:] + bi_ref[:, sl])
        log_a = -LRU_C * r * sp[:, sl]
        a = jnp.exp(log_a)
        a_scr[:, sl] = a
        u_scr[:, sl] = jnp.sqrt(-jnp.tanh(log_a) * (a * a + 1.0)) * (i * xc)

    out = o_ref if d == 0 else hs_scr
    ngrp = TQ // 8

    def body(gi, h):
        base = pl.multiple_of((gi if d == 0 else ngrp - 1 - gi) * 8, 8)
        for t in (range(8) if d == 0 else range(7, -1, -1)):
            h = a_scr[pl.ds(base + t, 1), :] * h + u_scr[pl.ds(base + t, 1), :]
            out[pl.ds(base + t, 1), :] = h
        return h

    h = lax.fori_loop(0, ngrp, body, h_scr[0:1, :])
    h_scr[0:1, :] = h
    if d == 1:
        y = (hf_ref[...] + hs_scr[...]) * _gelu_tanh(ag_ref[...])
        o_ref[...] = y.astype(o_ref.dtype)


def lru_pass(d, proj, hf, conv_w, conv_b, wg, b_r, b_i, lam, bsz, lat_blocks):
    t = proj.shape[0]
    _, blk = _seq_block(d, bsz, lat_blocks)
    row = lambda c: (lambda b, s: (blk(b, s), c))
    in_specs = _halo_specs(LRU_WIDTH, COL_AX, blk, t)
    args = [proj, proj, proj]
    if d == 1:
        in_specs += [pl.BlockSpec((TQ, LRU_WIDTH), row(COL_AG)), pl.BlockSpec((TQ, LRU_WIDTH), row(0))]
        args += [proj, hf]
    const = lambda shape: pl.BlockSpec(shape, lambda b, s: (0,) * len(shape))
    in_specs += [const((4, LRU_WIDTH)), const((1, LRU_WIDTH)), const((LRU_WIDTH // 128, 128, 256)),
                 const((1, LRU_WIDTH)), const((1, LRU_WIDTH)), const((1, LRU_WIDTH))]
    args += [conv_w, conv_b.reshape(1, -1), wg[d], b_r[d].reshape(1, -1), b_i[d].reshape(1, -1), lam[d].reshape(1, -1)]
    scratch = [pltpu.VMEM((TQ + 2 * HALO, LRU_WIDTH), f32), pltpu.VMEM((TQ, LRU_WIDTH), f32),
               pltpu.VMEM((TQ, LRU_WIDTH), f32), pltpu.VMEM((8, LRU_WIDTH), f32)]
    if d == 1:
        scratch.append(pltpu.VMEM((TQ, LRU_WIDTH), f32))
    return pl.pallas_call(
        functools.partial(_lru_kernel, d, lat_blocks),
        grid=(bsz, lat_blocks + 1),
        in_specs=in_specs,
        out_specs=pl.BlockSpec((TQ, LRU_WIDTH), row(0)),
        out_shape=jax.ShapeDtypeStruct((t, LRU_WIDTH), f32 if d == 0 else bf16),
        scratch_shapes=scratch,
        compiler_params=_params(),
        name=f"lru_pass{d}",
    )(*args)


def lru_branch(proj, conv_w, conv_b, w_r, b_r, w_i, b_i, lam, bsz, lat_blocks):
    wg = lru_gate_weights(w_r, w_i)
    hf = lru_pass(0, proj, None, conv_w, conv_b, wg, b_r, b_i, lam, bsz, lat_blocks)
    return lru_pass(1, proj, hf, conv_w, conv_b, wg, b_r, b_i, lam, bsz, lat_blocks)


def rope_tables(seq):
    pos = jnp.arange(seq)
    row = (pos // GRID_W).astype(f32)
    col = (pos % GRID_W).astype(f32)
    n_freq = SSD_STATE // 4
    freqs = ROPE_BASE ** (-jnp.arange(n_freq, dtype=f32) / n_freq)
    ang = jnp.concatenate([row[:, None] * freqs, col[:, None] * freqs], axis=-1)
    cos, sin = jnp.cos(ang), jnp.sin(ang)
    cosf = jnp.concatenate([cos, cos], axis=-1)
    sinf = jnp.concatenate([-sin, sin], axis=-1)
    cosf = jnp.concatenate([cosf, jnp.ones((TQ, SSD_STATE), f32)], axis=0)
    sinf = jnp.concatenate([sinf, jnp.zeros((TQ, SSD_STATE), f32)], axis=0)
    return cosf, sinf


def head_expand_matrix(d):
    e = np.zeros((128, SSD_INNER), np.float32)
    for h in range(SSD_HEADS):
        e[SSD_HEADS * d + h, h * SSD_HEADDIM:(h + 1) * SSD_HEADDIM] = 1.0
    return jnp.asarray(e, bf16)


def _split3(a):
    hi = a.astype(bf16)
    r = a - hi.astype(f32)
    mid = r.astype(bf16)
    return hi, mid, (r - mid.astype(f32)).astype(bf16)


def _dot_exact_rhs01(a, m01):
    return sum(jnp.dot(p, m01, preferred_element_type=f32) for p in _split3(a))


def _dot_exact_lhs01(m01, a):
    return sum(jnp.dot(m01, p, preferred_element_type=f32) for p in _split3(a))


def _ssd_kernel(d, lat_blocks, *refs):
    xs_refs, bc_refs, refs = refs[0:3], refs[3:6], refs[6:]
    if d == 0:
        (dt_ref, cos_ref, sin_ref, cw_ref, cb_ref, dtb_ref, alog_ref, e_ref,
         o_ref, xe, h_scr) = refs
    else:
        (dt_ref, cos_ref, sin_ref, z_ref, y0_ref, cw_ref, cb_ref, dtb_ref, alog_ref, e_ref,
         dsk_ref, ng_ref, o_ref, xe, h_scr, y_scr) = refs
    s = pl.program_id(1)
    j = (s - 1) if d == 0 else (lat_blocks - s)
    has_prev = jnp.logical_and(s > 0, j > 0)
    has_next = jnp.logical_and(s > 0, j < lat_blocks - 1)
    q = SSD_CHUNK

    @pl.when(s == 0)
    def _():
        h_scr[...] = jnp.zeros_like(h_scr)

    _fill_halo(xe, [xs_refs, bc_refs], has_prev, has_next)
    xbc = _centred_conv(xe, cw_ref, cb_ref, slice(None))
    xbc = xbc * jax.nn.sigmoid(xbc)
    xs = xbc[:, :SSD_INNER]
    cosf, sinf = cos_ref[...], sin_ref[...]

    def rope(g, off):
        v = xbc[:, off + g * SSD_STATE: off + (g + 1) * SSD_STATE]
        return (v * cosf + pltpu.roll(v, SSD_STATE // 2, 1) * sinf).astype(bf16)

    bm = [rope(g, SSD_INNER) for g in range(SSD_GROUPS)]
    cm = [rope(g, SSD_INNER + SSD_GROUPS * SSD_STATE) for g in range(SSD_GROUPS)]
    dt = jax.nn.softplus(dt_ref[...] + dtb_ref[...])
    delta = dt * (-jnp.exp(alog_ref[...]))
    ri = lax.broadcasted_iota(jnp.int32, (q, q), 0)
    ci = lax.broadcasted_iota(jnp.int32, (q, q), 1)
    keep = (ci <= ri) if d == 0 else (ci >= ri)
    tri = jnp.where(keep, 1.0, 0.0).astype(bf16)
    lane = lax.broadcasted_iota(jnp.int32, (1, 2 * SSD_HEADDIM), 1)
    halves = (lane < SSD_HEADDIM, lane >= SSD_HEADDIM)
    e = e_ref[...]
    last = q - 1 if d == 0 else 0
    out = o_ref if d == 0 else y_scr

    for c in (range(TQ // q) if d == 0 else range(TQ // q - 1, -1, -1)):
        rows = slice(c * q, (c + 1) * q)
        at = _dot_exact_lhs01(tri, delta[rows])
        at_exp = _dot_exact_rhs01(at, e)
        dt_exp = _dot_exact_rhs01(dt[rows], e)
        tot_exp = at_exp[last:last + 1, :]
        xdt = xs[rows] * dt_exp
        xd = (xdt * jnp.exp(tot_exp - at_exp)).astype(bf16)
        eat = jnp.exp(at_exp)
        cdec = jnp.exp(tot_exp)
        at_row = at.T
        ys = []
        for g in range(SSD_GROUPS):
            bg, cg = bm[g][rows], cm[g][rows]
            cb = _dot_nt(cg, bg)
            ht = h_scr[g]
            yoff = jnp.dot(cg, ht.astype(bf16), preferred_element_type=f32) * eat[:, g * GW:(g + 1) * GW]
            for pp in range(2):
                xpair = xdt[:, g * GW + pp * 128: g * GW + (pp + 1) * 128]
                acc = yoff[:, pp * 128:(pp + 1) * 128]
                for hh in range(2):
                    li = SSD_HEADS * d + 4 * g + 2 * pp + hh
                    seg = at[:, li:li + 1] - at_row[li:li + 1, :]
                    m = (cb * jnp.exp(jnp.where(keep, seg, NEG))).astype(bf16)
                    xm = jnp.where(halves[hh], xpair, 0.0).astype(bf16)
                    acc = acc + jnp.dot(m, xm, preferred_element_type=f32)
                ys.append(acc)
            upd = lax.dot_general(bg, xd[:, g * GW:(g + 1) * GW], (((0,), (0,)), ((), ())),
                                  preferred_element_type=f32)
            h_scr[g] = cdec[:, g * GW:(g + 1) * GW] * ht + upd
        out[rows, :] = jnp.concatenate(ys, axis=-1)

    if d == 1:
        y = y0_ref[...] + y_scr[...] + dsk_ref[...] * xs
        z = z_ref[...]
        y = y * (z * jax.nn.sigmoid(z))
        y = y * lax.rsqrt(jnp.mean(y * y, axis=-1, keepdims=True) + EPS)
        o_ref[...] = (y * ng_ref[...]).astype(o_ref.dtype)


def ssd_pass(d, proj, y0, cosf, sinf, conv_w, conv_b, a_log, dt_bias, d_skip, norm_g, bsz, lat_blocks):
    t = proj.shape[0]
    jj, blk = _seq_block(d, bsz, lat_blocks)
    row = lambda c: (lambda b, s: (blk(b, s), c))
    tbl = lambda b, s: (jnp.where(s == 0, lat_blocks, jj(s)), 0)
    const = lambda shape: pl.BlockSpec(shape, lambda b, s: (0,) * len(shape))
    pad128 = lambda v: jnp.pad(v.reshape(1, -1).astype(f32), ((0, 0), (0, 128 - v.size)))
    in_specs = _halo_specs(SSD_INNER, COL_XS, blk, t) + _halo_specs(SSD_CONV_DIM - SSD_INNER, COL_BC, blk, t) + [
        pl.BlockSpec((TQ, 128), row(COL_DT)),
        pl.BlockSpec((TQ, SSD_STATE), tbl),
        pl.BlockSpec((TQ, SSD_STATE), tbl),
    ]
    args = [proj] * 7 + [cosf, sinf]
    if d == 1:
        in_specs += [pl.BlockSpec((TQ, SSD_INNER), row(COL_Z)), pl.BlockSpec((TQ, SSD_INNER), row(0))]
        args += [proj, y0]
    in_specs += [const((4, SSD_CONV_DIM)), const((1, SSD_CONV_DIM)), const((1, 128)), const((1, 128)),
                 const((128, SSD_INNER))]
    args += [conv_w, conv_b.reshape(1, -1), pad128(dt_bias), pad128(a_log), head_expand_matrix(d)]
    scratch = [pltpu.VMEM((TQ + 2 * HALO, SSD_CONV_DIM), f32), pltpu.VMEM((SSD_GROUPS, SSD_STATE, GW), f32)]
    if d == 1:
        in_specs += [const((1, SSD_INNER)), const((1, SSD_INNER))]
        args += [jnp.repeat(d_skip, SSD_HEADDIM).reshape(1, -1), norm_g.reshape(1, -1)]
        scratch.append(pltpu.VMEM((TQ, SSD_INNER), f32))
    return pl.pallas_call(
        functools.partial(_ssd_kernel, d, lat_blocks),
        grid=(bsz, lat_blocks + 1),
        in_specs=in_specs,
        out_specs=pl.BlockSpec((TQ, SSD_INNER), row(0)),
        out_shape=jax.ShapeDtypeStruct((t, SSD_INNER), f32 if d == 0 else bf16),
        scratch_shapes=scratch,
        compiler_params=_params(),
        name=f"ssd_pass{d}",
    )(*args)


def ssd_branch(proj, cosf, sinf, conv_w, conv_b, a_log, dt_bias, d_skip, norm_g, bsz, lat_blocks):
    y0 = ssd_pass(0, proj, None, cosf, sinf, conv_w, conv_b, a_log, dt_bias, d_skip, norm_g, bsz, lat_blocks)
    return ssd_pass(1, proj, y0, cosf, sinf, conv_w, conv_b, a_log, dt_bias, d_skip, norm_g, bsz, lat_blocks)


def _tile_rows(mod_rows, bsz):
    idx = np.concatenate([np.repeat(np.arange(bsz), SEQ // TM), np.full(bsz * CTX_LEN // TM, bsz)])
    return mod_rows[idx][:, None, :]


def _expert_choice_moe(h, v, aff, w1, w3, w2, layer, g2_rows, bsz):
    t, d = v.shape
    aff = aff[:, :N_EXPERTS]
    nl = bsz * SEQ

    def choose(a, length):
        cap = CAPACITY_FACTOR * length // N_EXPERTS
        g, idx = lax.top_k(jnp.swapaxes(a.reshape(bsz, length, N_EXPERTS), 1, 2), cap)
        return jnp.swapaxes(g, 0, 1), jnp.swapaxes(idx, 0, 1), cap

    g_l, i_l, cap_l = choose(aff[:nl], SEQ)
    g_c, i_c, cap_c = choose(aff[nl:], CTX_LEN)
    boff = jnp.arange(bsz)[None, :, None]
    ctx_rows = (i_c + boff * CTX_LEN).reshape(N_EXPERTS, -1)
    rows = jnp.concatenate([(i_l + boff * SEQ).reshape(N_EXPERTS, -1), ctx_rows + nl], axis=1)
    gates = jnp.concatenate([g_l.reshape(N_EXPERTS, -1), g_c.reshape(N_EXPERTS, -1)], axis=1)[..., None]
    r = rows.shape[1]
    xg = jnp.take(v, rows.reshape(-1), axis=0).reshape(N_EXPERTS, r, d)
    hdn = expert_hidden(xg, w1, w3, layer, r // 4)
    ctx_slots, ctx_tile = bsz * cap_c, bsz * CTX_LEN
    assert (bsz * cap_l) % ctx_slots == 0 and nl % ctx_tile == 0
    h = moe_combine(i_l.reshape(-1), hdn, 0, cap_l, gates, w2, layer, h, g2_rows, 0, 0, SEQ, TN, bsz)
    return moe_combine(ctx_rows.reshape(-1), hdn, bsz * cap_l // ctx_slots, ctx_slots, gates, w2, layer, h, g2_rows,
                       bsz, nl // ctx_tile, ctx_tile, d, 1)


def kernel(x, c, ctx, c_ctx, w_ada, b_ada, norm_mix, norm_ffn, w_in, lru_conv_w, lru_conv_b, lru_w_r, lru_b_r,
           lru_w_i, lru_b_i, lru_lambda, ssd_conv_w, ssd_conv_b, ssd_a_log, ssd_dt_bias, ssd_d, ssd_norm, na_rpb,
           w_branch_lru, w_branch_ssd, w_branch_na, w_out, w_router, w1, w3, w2, norm_final):
    bsz = x.shape[0]
    assert x.shape[1:] == (SEQ, D_MODEL) and ctx.shape[1:] == (CTX_LEN, D_MODEL) and bsz * CTX_LEN == TM
    d = D_MODEL
    h = jnp.concatenate([x.reshape(bsz * SEQ, d), ctx.reshape(bsz * CTX_LEN, d)], axis=0)
    t = h.shape[0]

    cond = jnp.concatenate([c, c_ctx[None, :], jnp.zeros((8 - bsz - 1, d), f32)], axis=0)
    mod = ada_modulation(cond, w_ada, b_ada)
    cosf, sinf = rope_tables(SEQ)

    assert DT_END + DT_PAD == W_QKV0 and w_in.shape[-1] == DT_END + 3 * NA_WIDTH + 3 * d
    w_proj = relayout_proj_weights(jnp.swapaxes(w_in, 1, 2))
    qkv_blocks = 3 * NA_WIDTH // TN
    pa, pb, pc, wo = (w.astype(bf16) for w in (w_branch_lru, w_branch_ssd, w_branch_na, w_out))
    na_tbl, na_pair_of = na_bias_blocks(na_rpb)

    for l in range(DEPTH):
        mods = jnp.split(mod[l], 6, axis=-1)
        sh1, sc1, g1, sh2, sc2 = (_tile_rows(m, bsz) for m in mods[:5])
        u = norm_modulate(h, norm_mix[l], sc1, sh1, bf16)
        proj = matmul_bf16(u, w_proj, l, N_F32, lambda j: jnp.where(j < W_QKV0 // TN, j, j + qkv_blocks), f32)
        qkv = matmul_bf16(u, w_proj, l, 3 * NA_WIDTH, lambda j: j + W_QKV0 // TN, bf16)
        ya = lru_branch(proj, lru_conv_w[l], lru_conv_b[l], lru_w_r[l], lru_b_r[l], lru_w_i[l], lru_b_i[l],
                        lru_lambda[l], bsz, LAT_BLOCKS)
        yb = ssd_branch(proj, cosf, sinf, ssd_conv_w[l], ssd_conv_b[l], ssd_a_log[l], ssd_dt_bias[l], ssd_d[l],
                        ssd_norm[l], bsz, LAT_BLOCKS)
        yc = na_attention(qkv, na_tbl, l, na_pair_of, bsz)
        y = branch_merge(ya, yb, yc, proj, pa, pb, pc, l, COL_G)
        h, v, aff = residual_norm_router(y, wo, l, h, g1, norm_ffn[l], sc2, sh2, w_router[l])
        h = _expert_choice_moe(h, v, aff, w1, w3, w2, l, mods[5][:, None, :], bsz)

    zeros = jnp.zeros((t // TM, 1, d), f32)
    out = norm_modulate(h, norm_final, zeros, zeros, f32)
    return out[:bsz * SEQ].reshape(bsz, SEQ, d)
```

```python
import functools

import jax
import jax.numpy as jnp
import numpy as np
from jax import lax
from jax.experimental import pallas as pl
from jax.experimental.pallas import tpu as pltpu

D_MODEL = 2048
SEQ = 4096
CTX_LEN = 256
DEPTH = 4
GRID_W = 64
EPS = 1e-6
ROPE_BASE = 10000.0
LRU_WIDTH = 1024
LRU_BLOCKS = 16
LRU_BLOCK = LRU_WIDTH // LRU_BLOCKS
LRU_C = 8.0
SSD_INNER = 1024
SSD_HEADDIM = 64
SSD_HEADS = SSD_INNER // SSD_HEADDIM
SSD_GROUPS = 4
SSD_STATE = 128
SSD_CHUNK = 128
SSD_CONV_DIM = SSD_INNER + 2 * SSD_GROUPS * SSD_STATE
NA_HEADS = 16
NA_HEADDIM = 64
NA_WIDTH = NA_HEADS * NA_HEADDIM
NA_WIN_R = 8
NA_WIN_C = 16
N_EXPERTS = 16
EXPERT_FF = 1024
CAPACITY_FACTOR = 2

V7X_VMEM_LIMIT = 56 * 1024 * 1024
TM = 1024
MM_MAX_ROWS = 2304
TQ = 256
HALO = 8
LAT_BLOCKS = SEQ // TQ
NEG = -1e30
HI = lax.Precision.HIGHEST
bf16 = jnp.bfloat16
f32 = jnp.float32

TN = 512
DT_END = 2 * LRU_WIDTH + SSD_INNER + SSD_CONV_DIM + 2 * SSD_HEADS
DT_PAD = 480
COL_AX, COL_AG, COL_Z, COL_XS, COL_BC = 0, 1, 2, 3, 4
COL_DT = 40
W_QKV0 = 5632
COL_G = 5632
N_F32 = COL_G + 3 * D_MODEL
GW = SSD_INNER // SSD_GROUPS

NA_GROUP_ROWS = 2
NA_BAND_ROWS = 10
NA_NCFG = 5


def _params():
    return pltpu.CompilerParams(vmem_limit_bytes=V7X_VMEM_LIMIT)


def _ada_kernel(x_ref, w_ref, b_ref, o_ref):
    c = x_ref[...]
    x = (c * jax.nn.sigmoid(c)).astype(bf16)
    o_ref[...] = jnp.dot(x, w_ref[...].astype(bf16), preferred_element_type=f32) + b_ref[...]


def ada_modulation(cond, w_ada, b_ada):
    depth, d, n = w_ada.shape
    tn = 1024
    return pl.pallas_call(
        _ada_kernel,
        grid=(depth, n // tn),
        in_specs=[pl.BlockSpec((8, d), lambda l, j: (0, 0)),
                  pl.BlockSpec((None, d, tn), lambda l, j: (l, 0, j)),
                  pl.BlockSpec((None, 1, tn), lambda l, j: (l, 0, j))],
        out_specs=pl.BlockSpec((None, 8, tn), lambda l, j: (l, 0, j)),
        out_shape=jax.ShapeDtypeStruct((depth, 8, n), f32),
        compiler_params=_params(),
        name="ada_modulation",
    )(cond, w_ada, b_ada.reshape(depth, 1, n))


def _norm_mod_kernel(h_ref, g_ref, sc_ref, sh_ref, o_ref):
    x = h_ref[...]
    y = x * lax.rsqrt(jnp.mean(x * x, axis=-1, keepdims=True) + EPS)
    o_ref[...] = ((y * g_ref[...]) * (1.0 + sc_ref[...]) + sh_ref[...]).astype(o_ref.dtype)


def norm_modulate(h, g, sc_t, sh_t, out_dtype, rows=None):
    t, d = h.shape
    t = t if rows is None else rows
    tm = min(512, TM)
    per = TM // tm
    return pl.pallas_call(
        _norm_mod_kernel,
        grid=(t // tm,),
        in_specs=[pl.BlockSpec((tm, d), lambda i: (i, 0)),
                  pl.BlockSpec((1, d), lambda i: (0, 0)),
                  pl.BlockSpec((None, 1, d), lambda i: (i // per, 0, 0)),
                  pl.BlockSpec((None, 1, d), lambda i: (i // per, 0, 0))],
        out_specs=pl.BlockSpec((tm, d), lambda i: (i, 0)),
        out_shape=jax.ShapeDtypeStruct((t, d), out_dtype),
        compiler_params=_params(),
        name="norm_modulate",
    )(h, g.reshape(1, d), sc_t, sh_t)


def _relayout_kernel(shift, a_ref, b_ref, o_ref):
    j = pl.program_id(1)
    split = DT_END // TN

    @pl.when(j < split)
    def _():
        o_ref[...] = a_ref[...].astype(o_ref.dtype)

    @pl.when(j == split)
    def _():
        row = lax.broadcasted_iota(jnp.int32, a_ref.shape, 0)
        o_ref[...] = jnp.where(row < shift, a_ref[...], 0.0).astype(o_ref.dtype)

    @pl.when(j > split)
    def _():
        o_ref[0:TN - shift, :] = a_ref[shift:TN, :].astype(o_ref.dtype)
        o_ref[TN - shift:TN, :] = b_ref[...].astype(o_ref.dtype)


def relayout_proj_weights(w_t):
    depth, n, d = w_t.shape
    shift = DT_END % TN
    assert shift + DT_PAD == TN and shift % 16 == 0 and n % shift == 0
    split = DT_END // TN
    return pl.pallas_call(
        functools.partial(_relayout_kernel, shift),
        grid=(depth, (n + DT_PAD) // TN),
        in_specs=[pl.BlockSpec((None, TN, d), lambda l, j: (l, jnp.where(j <= split, j, j - 1), 0)),
                  pl.BlockSpec((None, shift, d), lambda l, j: (l, jnp.where(j <= split, 0, (TN // shift) * j), 0))],
        out_specs=pl.BlockSpec((None, TN, d), lambda l, j: (l, j, 0)),
        out_shape=jax.ShapeDtypeStruct((depth, n + DT_PAD, d), bf16),
        compiler_params=_params(),
        name="relayout_proj_weights",
    )(w_t, w_t)


def _mm_kernel(x_ref, w_ref, o_ref):
    o_ref[...] = _dot_nt(x_ref[...], w_ref[...]).astype(o_ref.dtype)


def matmul_bf16(x, w, layer, n_out, w_block, out_dtype):
    m, k = x.shape
    tm = max(c for c in range(16, MM_MAX_ROWS + 1, 16) if m % c == 0)
    return pl.pallas_call(
        _mm_kernel,
        grid=(m // tm, n_out // TN),
        in_specs=[pl.BlockSpec((tm, k), lambda i, j: (i, 0)),
                  pl.BlockSpec((None, TN, k), lambda i, j: (layer, w_block(j), 0))],
        out_specs=pl.BlockSpec((tm, TN), lambda i, j: (i, j)),
        out_shape=jax.ShapeDtypeStruct((m, n_out), out_dtype),
        compiler_params=_params(),
        name="matmul_bf16",
    )(x, w)


def _merge_kernel(ya_ref, yb_ref, yc_ref, ga_ref, gb_ref, gc_ref, pa_ref, pb_ref, pc_ref, o_ref):
    acc = jax.nn.sigmoid(ga_ref[...]) * jnp.dot(ya_ref[...], pa_ref[...], preferred_element_type=f32)
    acc = acc + jax.nn.sigmoid(gb_ref[...]) * jnp.dot(yb_ref[...], pb_ref[...], preferred_element_type=f32)
    acc = acc + jax.nn.sigmoid(gc_ref[...]) * jnp.dot(yc_ref[...], pc_ref[...], preferred_element_type=f32)
    o_ref[...] = acc.astype(o_ref.dtype)


def branch_merge(ya, yb, yc, proj, pa, pb, pc, layer, g_col0):
    t, k = ya.shape
    n = pa.shape[-1]
    tn = 512
    gb0 = g_col0 // tn
    nj = n // tn
    xs = pl.BlockSpec((TM, k), lambda i, j: (i, 0))
    ws = pl.BlockSpec((None, k, tn), lambda i, j: (layer, 0, j))
    gs = lambda q: pl.BlockSpec((TM, tn), lambda i, j: (i, gb0 + q * nj + j))
    return pl.pallas_call(
        _merge_kernel,
        grid=(t // TM, nj),
        in_specs=[xs, xs, xs, gs(0), gs(1), gs(2), ws, ws, ws],
        out_specs=pl.BlockSpec((TM, tn), lambda i, j: (i, j)),
        out_shape=jax.ShapeDtypeStruct((t, n), bf16),
        compiler_params=_params(),
        name="branch_merge",
    )(ya, yb, yc, proj, proj, proj, pa, pb, pc)


def _resid_norm_router_kernel(y_ref, w_ref, h_ref, g1_ref, ng_ref, sc_ref, sh_ref, wr_ref, ho_ref, v_ref, aff_ref):
    hn = h_ref[...] + g1_ref[...] * jnp.dot(y_ref[...], w_ref[...], preferred_element_type=f32)
    ho_ref[...] = hn
    yn = hn * lax.rsqrt(jnp.mean(hn * hn, axis=-1, keepdims=True) + EPS)
    v = ((yn * ng_ref[...]) * (1.0 + sc_ref[...]) + sh_ref[...]).astype(bf16)
    v_ref[...] = v
    s = jnp.dot(v, wr_ref[...], preferred_element_type=f32)
    lane = lax.broadcasted_iota(jnp.int32, s.shape, 1)
    s = jnp.where(lane < N_EXPERTS, s, NEG)
    e = jnp.exp(s - s.max(axis=-1, keepdims=True))
    aff_ref[...] = e / e.sum(axis=-1, keepdims=True)


def residual_norm_router(y, w, layer, h, g1_t, norm_g, sc_t, sh_t, w_router):
    t, d = h.shape
    tm = TM // 2
    wr = jnp.pad(w_router, ((0, 0), (0, 128 - N_EXPERTS))).astype(bf16)
    rows = pl.BlockSpec((tm, d), lambda i: (i, 0))
    per_tile = pl.BlockSpec((None, 1, d), lambda i: (i * tm // TM, 0, 0))
    return pl.pallas_call(
        _resid_norm_router_kernel,
        grid=(t // tm,),
        in_specs=[rows, pl.BlockSpec((None, d, d), lambda i: (layer, 0, 0)), rows, per_tile,
                  pl.BlockSpec((1, d), lambda i: (0, 0)), per_tile, per_tile,
                  pl.BlockSpec((d, 128), lambda i: (0, 0))],
        out_specs=[rows, rows, pl.BlockSpec((tm, 128), lambda i: (i, 0))],
        out_shape=[jax.ShapeDtypeStruct((t, d), f32), jax.ShapeDtypeStruct((t, d), bf16),
                   jax.ShapeDtypeStruct((t, 128), f32)],
        compiler_params=_params(),
        name="residual_norm_router",
    )(y, w, h, g1_t, norm_g.reshape(1, d), sc_t, sh_t, wr)


def _expert_hidden_kernel(x_ref, w1_ref, w3_ref, o_ref, w1b, w3b):
    @pl.when(pl.program_id(1) == 0)
    def _():
        w1b[...] = w1_ref[...].astype(bf16)
        w3b[...] = w3_ref[...].astype(bf16)

    x = x_ref[...]
    a = jnp.dot(x, w1b[...], preferred_element_type=f32)
    b = jnp.dot(x, w3b[...], preferred_element_type=f32)
    o_ref[...] = ((a * jax.nn.sigmoid(a)) * b).astype(o_ref.dtype)


def expert_hidden(xg, w1, w3, layer, tm):
    e, r, d = xg.shape
    f = w1.shape[-1]
    return pl.pallas_call(
        _expert_hidden_kernel,
        grid=(e, r // tm),
        in_specs=[pl.BlockSpec((None, tm, d), lambda k, i: (k, i, 0)),
                  pl.BlockSpec((None, None, d, f), lambda k, i: (layer, k, 0, 0)),
                  pl.BlockSpec((None, None, d, f), lambda k, i: (layer, k, 0, 0))],
        out_specs=pl.BlockSpec((None, tm, f), lambda k, i: (k, i, 0)),
        out_shape=jax.ShapeDtypeStruct((e, r, f), bf16),
        scratch_shapes=[pltpu.VMEM((d, f), bf16), pltpu.VMEM((d, f), bf16)],
        compiler_params=_params(),
        name="expert_hidden",
    )(xg, w1, w3)


SCATTER_UNROLL = 8
SCATTER_PART = 512


def _combine_kernel(nb, slots, idx_ref, hdn_ref, w2_ref, g_ref, h_ref, g2_ref, *rest):
    o_ref, ye_scr = rest[-2], rest[-1]
    b, e = pl.program_id(0), pl.program_id(2)

    @pl.when(e == 0)
    def _():
        o_ref[...] = jnp.zeros_like(o_ref)

    w2 = w2_ref[...].astype(bf16)
    base0 = (e * nb + b) * slots
    part = min(slots, SCATTER_PART)

    for p0 in range(0, slots, part):
        ye_scr[p0:p0 + part, :] = (jnp.dot(hdn_ref[p0:p0 + part, :], w2, preferred_element_type=f32)
                                   * g_ref[p0:p0 + part, :])
        for s0 in range(p0, p0 + part, SCATTER_UNROLL):
            rows = [idx_ref[base0 + s0 + k] for k in range(SCATTER_UNROLL)]
            vals = [o_ref[pl.ds(rows[k], 1), :] + ye_scr[s0 + k:s0 + k + 1, :] for k in range(SCATTER_UNROLL)]
            for k in range(SCATTER_UNROLL):
                o_ref[pl.ds(rows[k], 1), :] = vals[k]

    @pl.when(e == pl.num_programs(2) - 1)
    def _():
        o_ref[...] = h_ref[...] + g2_ref[...] * o_ref[...]


def moe_combine(idx, hdn, slot_blk0, slots, gates, w2, layer, h, g2_rows, g2_row0, row_blk0, rows_blk, dq, nb):
    e, _, f = hdn.shape
    t, d = h.shape
    in_specs = [
        pl.BlockSpec((None, slots, f), lambda b, q, k, idx: (k, slot_blk0 + b, 0)),
        pl.BlockSpec((None, None, f, dq), lambda b, q, k, idx: (layer, k, 0, q)),
        pl.BlockSpec((None, slots, 1), lambda b, q, k, idx: (k, slot_blk0 + b, 0)),
        pl.BlockSpec((rows_blk, dq), lambda b, q, k, idx: (row_blk0 + b, q)),
        pl.BlockSpec((None, 1, dq), lambda b, q, k, idx: (g2_row0 + b, 0, q)),
    ]
    args = [idx, hdn, w2, gates, h, g2_rows]
    aliases = {4: 0}
    return pl.pallas_call(
        functools.partial(_combine_kernel, nb, slots),
        grid_spec=pltpu.PrefetchScalarGridSpec(
            num_scalar_prefetch=1,
            grid=(nb, d // dq, e),
            in_specs=in_specs,
            out_specs=pl.BlockSpec((rows_blk, dq), lambda b, q, k, idx: (row_blk0 + b, q)),
            scratch_shapes=[pltpu.VMEM((slots, dq), f32)],
        ),
        out_shape=jax.ShapeDtypeStruct((t, d), f32),
        input_output_aliases=aliases,
        compiler_params=_params(),
        name="moe_combine",
    )(*args)


def na_bias_blocks(rpb):
    rows = SEQ // GRID_W
    n_dr, n_dc = 2 * NA_WIN_R - 1, 2 * NA_WIN_C - 1
    qc, kc = np.arange(GRID_W)[:, None], np.arange(GRID_W)[None, :]
    cs = np.clip(qc - NA_WIN_C // 2, 0, GRID_W - NA_WIN_C)
    col_ok = (kc >= cs) & (kc < cs + NA_WIN_C)
    pick = (np.arange(n_dc)[:, None, None] == (kc - qc + NA_WIN_C - 1)[None]) & col_ok[None]
    blocks = jnp.einsum('...rd,dqk->...rqk', rpb, jnp.asarray(pick, f32), precision=HI)
    blocks = jnp.where(col_ok, blocks, NEG)
    blocks = jnp.concatenate([blocks, jnp.full(blocks.shape[:-3] + (1, GRID_W, GRID_W), NEG, f32)], axis=-3)
    which = np.full((NA_NCFG, NA_GROUP_ROWS, NA_BAND_ROWS), n_dr, np.int32)
    for c, r0 in enumerate((0, 2, 4, 60, 62)):
        bs = int(np.clip(r0 - NA_WIN_R // 2, 0, rows - NA_BAND_ROWS))
        for qr in range(NA_GROUP_ROWS):
            r = r0 + qr
            rs = int(np.clip(r - NA_WIN_R // 2, 0, rows - NA_WIN_R))
            for kr in range(NA_BAND_ROWS):
                if rs <= bs + kr < rs + NA_WIN_R:
                    which[c, qr, kr] = bs + kr - r + NA_WIN_R - 1
    pairs = sorted({(int(which[c, qr, 2 * m]), int(which[c, qr, 2 * m + 1]))
                    for c in range(NA_NCFG) for qr in range(NA_GROUP_ROWS) for m in range(NA_BAND_ROWS // 2)})
    pair_of = np.array([[[pairs.index((int(which[c, qr, 2 * m]), int(which[c, qr, 2 * m + 1])))
                          for m in range(NA_BAND_ROWS // 2)] for qr in range(NA_GROUP_ROWS)]
                        for c in range(NA_NCFG)], np.int32)
    left = jnp.take(blocks, np.array([p[0] for p in pairs]), axis=-3)
    right = jnp.take(blocks, np.array([p[1] for p in pairs]), axis=-3)
    return jnp.concatenate([left, right], axis=-1), pair_of.reshape(-1)


def _dot_nt(a, b):
    return lax.dot_general(a, b, (((1,), (1,)), ((), ())), preferred_element_type=f32)


def _softmax_av(parts):
    m = parts[0][0].max(axis=-1, keepdims=True)
    for s, _ in parts[1:]:
        m = jnp.maximum(m, s.max(axis=-1, keepdims=True))
    l = None
    o = None
    for s, v in parts:
        p = jnp.exp(s - m)
        ls = p.sum(axis=-1, keepdims=True)
        os_ = jnp.dot(p.astype(bf16), v, preferred_element_type=f32)
        l = ls if l is None else l + ls
        o = os_ if o is None else o + os_
    return o / l


def _na_kernel(pair_ref, q_ref, k_ref, v_ref, kc_ref, vc_ref, tbl_ref, o_ref):
    j = pl.program_id(2)
    npairs = NA_BAND_ROWS // 2

    def bias(hh, cfg):
        rows = []
        for qr in range(NA_GROUP_ROWS):
            base = (cfg * NA_GROUP_ROWS + qr) * npairs
            rows.append(jnp.concatenate([tbl_ref[hh, pair_ref[base + m]] for m in range(npairs)], axis=1))
        return jnp.concatenate(rows, axis=0)

    scale = NA_HEADDIM ** -0.5
    assert scale == 2.0 ** round(np.log2(scale))
    lane = lax.broadcasted_iota(jnp.int32, (1, 2 * NA_HEADDIM), 1)
    first = lane < NA_HEADDIM
    kc = kc_ref[...]
    vc = vc_ref[...]
    gq = NA_GROUP_ROWS * GRID_W

    def head_q(q, hh):
        keep = first if hh == 0 else jnp.logical_not(first)
        return jnp.where(keep, q * scale, jnp.zeros_like(q))

    @pl.when(j < LAT_BLOCKS)
    def _():
        ngroups = TQ // gq
        q2 = []
        for g in range(ngroups):
            q = q_ref[pl.ds(g * gq, gq), :]
            q2.append(jnp.concatenate([head_q(q, 0), head_q(q, 1)], axis=0))
        sc_all = _dot_nt(jnp.concatenate(q2, axis=0), kc)
        o_win, p_ctx, denom = [], [], []
        for g in range(ngroups):
            r0 = j * (TQ // GRID_W) + NA_GROUP_ROWS * g
            bs = jnp.clip(r0 - NA_WIN_R // 2, 0, SEQ // GRID_W - NA_BAND_ROWS)
            cfg = jnp.where(r0 == 0, 0, jnp.where(r0 == 2, 1, jnp.where(r0 == 60, 3, jnp.where(r0 == 62, 4, 2))))
            start = pl.multiple_of(bs * GRID_W, GRID_W)
            kb = k_ref[pl.ds(start, NA_BAND_ROWS * GRID_W), :]
            vb = v_ref[pl.ds(start, NA_BAND_ROWS * GRID_W), :]
            s = _dot_nt(q2[g], kb) + jnp.concatenate([bias(0, cfg), bias(1, cfg)], axis=0)
            sc = sc_all[2 * gq * g:2 * gq * (g + 1)]
            m = jnp.maximum(s.max(axis=-1, keepdims=True), sc.max(axis=-1, keepdims=True))
            p, pc = jnp.exp(s - m), jnp.exp(sc - m)
            denom.append(p.sum(axis=-1, keepdims=True) + pc.sum(axis=-1, keepdims=True))
            o_win.append(jnp.dot(p.astype(bf16), vb, preferred_element_type=f32))
            p_ctx.append(pc.astype(bf16))
        o_ctx = jnp.dot(jnp.concatenate(p_ctx, axis=0), vc, preferred_element_type=f32)
        for g in range(ngroups):
            o2 = (o_win[g] + o_ctx[2 * gq * g:2 * gq * (g + 1)]) / denom[g]
            o_ref[pl.ds(g * gq, gq), :] = jnp.where(first, o2[:gq], o2[gq:]).astype(o_ref.dtype)

    @pl.when(j == LAT_BLOCKS)
    def _():
        q = q_ref[...]
        q2 = jnp.concatenate([head_q(q, 0), head_q(q, 1)], axis=0)
        o2 = _softmax_av([(_dot_nt(q2, kc), vc)])
        o_ref[...] = jnp.where(first, o2[:CTX_LEN], o2[CTX_LEN:]).astype(o_ref.dtype)


def na_attention(qkv, tbl, layer, pair_of, bsz):
    t = qkv.shape[0]
    npair = NA_HEADS // 2
    ctx0 = bsz * LAT_BLOCKS

    def qrow(b, j):
        return jnp.where(j < LAT_BLOCKS, b * LAT_BLOCKS + j, ctx0 + b)

    return pl.pallas_call(
        _na_kernel,
        grid_spec=pltpu.PrefetchScalarGridSpec(
            num_scalar_prefetch=1,
            grid=(bsz, npair, LAT_BLOCKS + 1),
            in_specs=[
                pl.BlockSpec((TQ, 128), lambda b, p, j, po: (qrow(b, j), p)),
                pl.BlockSpec((SEQ, 128), lambda b, p, j, po: (b, npair + p)),
                pl.BlockSpec((SEQ, 128), lambda b, p, j, po: (b, 2 * npair + p)),
                pl.BlockSpec((CTX_LEN, 128), lambda b, p, j, po: (ctx0 + b, npair + p)),
                pl.BlockSpec((CTX_LEN, 128), lambda b, p, j, po: (ctx0 + b, 2 * npair + p)),
                pl.BlockSpec((None, 2) + tbl.shape[2:], lambda b, p, j, po: (layer, p, 0, 0, 0)),
            ],
            out_specs=pl.BlockSpec((TQ, 128), lambda b, p, j, po: (qrow(b, j), p)),
        ),
        out_shape=jax.ShapeDtypeStruct((t, NA_WIDTH), bf16),
        compiler_params=_params(),
        name="na_attention",
    )(jnp.asarray(pair_of), qkv, qkv, qkv, qkv, qkv, tbl)


def _seq_block(d, bsz, lat_blocks):
    ctx0 = bsz * lat_blocks

    def jj(s):
        return (s - 1) if d == 0 else (lat_blocks - s)

    def blk(b, s):
        return jnp.where(s == 0, ctx0 + b, b * lat_blocks + jj(s))

    return jj, blk


def _halo_specs(width, col, blk, t):
    nb8 = TQ // HALO
    return [
        pl.BlockSpec((TQ, width), lambda b, s: (blk(b, s), col)),
        pl.BlockSpec((HALO, width), lambda b, s: (jnp.maximum(blk(b, s) * nb8 - 1, 0), col)),
        pl.BlockSpec((HALO, width), lambda b, s: (jnp.minimum(blk(b, s) * nb8 + nb8, t // HALO - 1), col)),
    ]


def _fill_halo(xe, parts, has_prev, has_next):
    c0 = 0
    for x_ref, xp_ref, xn_ref in parts:
        cols = slice(c0, c0 + x_ref.shape[1])
        xe[0:HALO, cols] = jnp.where(has_prev, xp_ref[...], 0.0)
        xe[HALO:HALO + TQ, cols] = x_ref[...]
        xe[HALO + TQ:2 * HALO + TQ, cols] = jnp.where(has_next, xn_ref[...], 0.0)
        c0 += x_ref.shape[1]


def _centred_conv(xe, cw_ref, cb_ref, cols):
    xv = xe[:, cols]
    n = xv.shape[0]
    out = cb_ref[:, cols]
    for k in range(4):
        shifted = xv if k == 2 else pltpu.roll(xv, (2 - k) % n, 0)
        out = out + cw_ref[k:k + 1, cols] * shifted[HALO:HALO + TQ, :]
    return out


def lru_gate_weights(w_r, w_i):
    def pair(w):
        w = w.reshape(2, LRU_BLOCKS // 2, 2, LRU_BLOCK, LRU_BLOCK)
        z = jnp.zeros_like(w[:, :, 0])
        top = jnp.concatenate([w[:, :, 0], z], axis=-1)
        bot = jnp.concatenate([z, w[:, :, 1]], axis=-1)
        return jnp.concatenate([top, bot], axis=-2)
    return jnp.concatenate([pair(w_r), pair(w_i)], axis=-1).astype(bf16)


def _gelu_tanh(x):
    return 0.5 * x * (1.0 + jnp.tanh(float(np.sqrt(2.0 / np.pi)) * (x + 0.044715 * (x * x * x))))


def _lru_kernel(d, lat_blocks, *refs):
    if d == 0:
        (x_ref, xp_ref, xn_ref, cw_ref, cb_ref, wg_ref, br_ref, bi_ref, lam_ref, o_ref,
         xe, a_scr, u_scr, h_scr) = refs
    else:
        (x_ref, xp_ref, xn_ref, ag_ref, hf_ref, cw_ref, cb_ref, wg_ref, br_ref, bi_ref, lam_ref, o_ref,
         xe, a_scr, u_scr, h_scr, hs_scr) = refs
    s = pl.program_id(1)
    j = (s - 1) if d == 0 else (lat_blocks - s)
    has_prev = jnp.logical_and(s > 0, j > 0)
    has_next = jnp.logical_and(s > 0, j < lat_blocks - 1)

    @pl.when(s == 0)
    def _():
        h_scr[...] = jnp.zeros_like(h_scr)

    _fill_halo(xe, [(x_ref, xp_ref, xn_ref)], has_prev, has_next)
    sp = jax.nn.softplus(-lam_ref[...])
    for p in range(LRU_WIDTH // 128):
        sl = slice(128 * p, 128 * (p + 1))
        xc = _centred_conv(xe, cw_ref, cb_ref, sl)
        g = jnp.dot(xc.astype(bf16), wg_ref[p], preferred_element_type=f32)
        r = jax.nn.sigmoid(g[:, :128] + br_ref[:, sl])
        i = jax.nn.sigmoid(g[:, 128:] + bi_ref[:, sl])
        log_a = -LRU_C * r * sp[:, sl]
        a = jnp.exp(log_a)
        a_scr[:, sl] = a
        u_scr[:, sl] = jnp.sqrt(-jnp.tanh(log_a) * (a * a + 1.0)) * (i * xc)

    out = o_ref if d == 0 else hs_scr
    h = h_scr[0:1, :]
    for t in (range(TQ) if d == 0 else range(TQ - 1, -1, -1)):
        h = a_scr[t:t + 1, :] * h + u_scr[t:t + 1, :]
        out[t:t + 1, :] = h
    h_scr[0:1, :] = h
    if d == 1:
        y = (hf_ref[...] + hs_scr[...]) * _gelu_tanh(ag_ref[...])
        o_ref[...] = y.astype(o_ref.dtype)


def lru_pass(d, proj, hf, conv_w, conv_b, wg, b_r, b_i, lam, bsz, lat_blocks):
    t = proj.shape[0]
    _, blk = _seq_block(d, bsz, lat_blocks)
    row = lambda c: (lambda b, s: (blk(b, s), c))
    in_specs = _halo_specs(LRU_WIDTH, COL_AX, blk, t)
    args = [proj, proj, proj]
    if d == 1:
        in_specs += [pl.BlockSpec((TQ, LRU_WIDTH), row(COL_AG)), pl.BlockSpec((TQ, LRU_WIDTH), row(0))]
        args += [proj, hf]
    const = lambda shape: pl.BlockSpec(shape, lambda b, s: (0,) * len(shape))
    in_specs += [const((4, LRU_WIDTH)), const((1, LRU_WIDTH)), const((LRU_WIDTH // 128, 128, 256)),
                 const((1, LRU_WIDTH)), const((1, LRU_WIDTH)), const((1, LRU_WIDTH))]
    args += [conv_w, conv_b.reshape(1, -1), wg[d], b_r[d].reshape(1, -1), b_i[d].reshape(1, -1), lam[d].reshape(1, -1)]
    scratch = [pltpu.VMEM((TQ + 2 * HALO, LRU_WIDTH), f32), pltpu.VMEM((TQ, LRU_WIDTH), f32),
               pltpu.VMEM((TQ, LRU_WIDTH), f32), pltpu.VMEM((8, LRU_WIDTH), f32)]
    if d == 1:
        scratch.append(pltpu.VMEM((TQ, LRU_WIDTH), f32))
    return pl.pallas_call(
        functools.partial(_lru_kernel, d, lat_blocks),
        grid=(bsz, lat_blocks + 1),
        in_specs=in_specs,
        out_specs=pl.BlockSpec((TQ, LRU_WIDTH), row(0)),
        out_shape=jax.ShapeDtypeStruct((t, LRU_WIDTH), f32 if d == 0 else bf16),
        scratch_shapes=scratch,
        compiler_params=_params(),
        name=f"lru_pass{d}",
    )(*args)


def lru_branch(proj, conv_w, conv_b, w_r, b_r, w_i, b_i, lam, bsz, lat_blocks):
    wg = lru_gate_weights(w_r, w_i)
    hf = lru_pass(0, proj, None, conv_w, conv_b, wg, b_r, b_i, lam, bsz, lat_blocks)
    return lru_pass(1, proj, hf, conv_w, conv_b, wg, b_r, b_i, lam, bsz, lat_blocks)


def rope_tables(seq):
    pos = jnp.arange(seq)
    row = (pos // GRID_W).astype(f32)
    col = (pos % GRID_W).astype(f32)
    n_freq = SSD_STATE // 4
    freqs = ROPE_BASE ** (-jnp.arange(n_freq, dtype=f32) / n_freq)
    ang = jnp.concatenate([row[:, None] * freqs, col[:, None] * freqs], axis=-1)
    cos, sin = jnp.cos(ang), jnp.sin(ang)
    cosf = jnp.concatenate([cos, cos], axis=-1)
    sinf = jnp.concatenate([-sin, sin], axis=-1)
    cosf = jnp.concatenate([cosf, jnp.ones((TQ, SSD_STATE), f32)], axis=0)
    sinf = jnp.concatenate([sinf, jnp.zeros((TQ, SSD_STATE), f32)], axis=0)
    return cosf, sinf


def head_expand_matrix(d):
    e = np.zeros((128, SSD_INNER), np.float32)
    for h in range(SSD_HEADS):
        e[SSD_HEADS * d + h, h * SSD_HEADDIM:(h + 1) * SSD_HEADDIM] = 1.0
    return jnp.asarray(e, bf16)


def _split3(a):
    hi = a.astype(bf16)
    r = a - hi.astype(f32)
    mid = r.astype(bf16)
    return hi, mid, (r - mid.astype(f32)).astype(bf16)


def _dot_exact_rhs01(a, m01):
    return sum(jnp.dot(p, m01, preferred_element_type=f32) for p in _split3(a))


def _dot_exact_lhs01(m01, a):
    return sum(jnp.dot(m01, p, preferred_element_type=f32) for p in _split3(a))


def _ssd_kernel(d, lat_blocks, *refs):
    xs_refs, bc_refs, refs = refs[0:3], refs[3:6], refs[6:]
    if d == 0:
        (dt_ref, cos_ref, sin_ref, cw_ref, cb_ref, dtb_ref, alog_ref, e_ref,
         o_ref, xe, h_scr) = refs
    else:
        (dt_ref, cos_ref, sin_ref, z_ref, y0_ref, cw_ref, cb_ref, dtb_ref, alog_ref, e_ref,
         dsk_ref, ng_ref, o_ref, xe, h_scr, y_scr) = refs
    s = pl.program_id(1)
    j = (s - 1) if d == 0 else (lat_blocks - s)
    has_prev = jnp.logical_and(s > 0, j > 0)
    has_next = jnp.logical_and(s > 0, j < lat_blocks - 1)
    q = SSD_CHUNK

    @pl.when(s == 0)
    def _():
        h_scr[...] = jnp.zeros_like(h_scr)

    _fill_halo(xe, [xs_refs, bc_refs], has_prev, has_next)
    xbc = _centred_conv(xe, cw_ref, cb_ref, slice(None))
    xbc = xbc * jax.nn.sigmoid(xbc)
    xs = xbc[:, :SSD_INNER]
    cosf, sinf = cos_ref[...], sin_ref[...]

    def rope(g, off):
        v = xbc[:, off + g * SSD_STATE: off + (g + 1) * SSD_STATE]
        return (v * cosf + pltpu.roll(v, SSD_STATE // 2, 1) * sinf).astype(bf16)

    bm = [rope(g, SSD_INNER) for g in range(SSD_GROUPS)]
    cm = [rope(g, SSD_INNER + SSD_GROUPS * SSD_STATE) for g in range(SSD_GROUPS)]
    dt = jax.nn.softplus(dt_ref[...] + dtb_ref[...])
    delta = dt * (-jnp.exp(alog_ref[...]))
    ri = lax.broadcasted_iota(jnp.int32, (q, q), 0)
    ci = lax.broadcasted_iota(jnp.int32, (q, q), 1)
    keep = (ci <= ri) if d == 0 else (ci >= ri)
    tri = jnp.where(keep, 1.0, 0.0).astype(bf16)
    lane = lax.broadcasted_iota(jnp.int32, (1, 2 * SSD_HEADDIM), 1)
    halves = (lane < SSD_HEADDIM, lane >= SSD_HEADDIM)
    e = e_ref[...]
    last = q - 1 if d == 0 else 0
    out = o_ref if d == 0 else y_scr

    for c in (range(TQ // q) if d == 0 else range(TQ // q - 1, -1, -1)):
        rows = slice(c * q, (c + 1) * q)
        at = _dot_exact_lhs01(tri, delta[rows])
        at_exp = _dot_exact_rhs01(at, e)
        dt_exp = _dot_exact_rhs01(dt[rows], e)
        tot_exp = at_exp[last:last + 1, :]
        xdt = xs[rows] * dt_exp
        xd = (xdt * jnp.exp(tot_exp - at_exp)).astype(bf16)
        eat = jnp.exp(at_exp)
        cdec = jnp.exp(tot_exp)
        at_row = at.T
        ys = []
        for g in range(SSD_GROUPS):
            bg, cg = bm[g][rows], cm[g][rows]
            cb = _dot_nt(cg, bg)
            ht = h_scr[g]
            yoff = jnp.dot(cg, ht.astype(bf16), preferred_element_type=f32) * eat[:, g * GW:(g + 1) * GW]
            for pp in range(2):
                xpair = xdt[:, g * GW + pp * 128: g * GW + (pp + 1) * 128]
                acc = yoff[:, pp * 128:(pp + 1) * 128]
                for hh in range(2):
                    li = SSD_HEADS * d + 4 * g + 2 * pp + hh
                    seg = at[:, li:li + 1] - at_row[li:li + 1, :]
                    m = (cb * jnp.exp(jnp.where(keep, seg, NEG))).astype(bf16)
                    xm = jnp.where(halves[hh], xpair, 0.0).astype(bf16)
                    acc = acc + jnp.dot(m, xm, preferred_element_type=f32)
                ys.append(acc)
            upd = lax.dot_general(bg, xd[:, g * GW:(g + 1) * GW], (((0,), (0,)), ((), ())),
                                  preferred_element_type=f32)
            h_scr[g] = cdec[:, g * GW:(g + 1) * GW] * ht + upd
        out[rows, :] = jnp.concatenate(ys, axis=-1)

    if d == 1:
        y = y0_ref[...] + y_scr[...] + dsk_ref[...] * xs
        z = z_ref[...]
        y = y * (z * jax.nn.sigmoid(z))
        y = y * lax.rsqrt(jnp.mean(y * y, axis=-1, keepdims=True) + EPS)
        o_ref[...] = (y * ng_ref[...]).astype(o_ref.dtype)


def ssd_pass(d, proj, y0, cosf, sinf, conv_w, conv_b, a_log, dt_bias, d_skip, norm_g, bsz, lat_blocks):
    t = proj.shape[0]
    jj, blk = _seq_block(d, bsz, lat_blocks)
    row = lambda c: (lambda b, s: (blk(b, s), c))
    tbl = lambda b, s: (jnp.where(s == 0, lat_blocks, jj(s)), 0)
    const = lambda shape: pl.BlockSpec(shape, lambda b, s: (0,) * len(shape))
    pad128 = lambda v: jnp.pad(v.reshape(1, -1).astype(f32), ((0, 0), (0, 128 - v.size)))
    in_specs = _halo_specs(SSD_INNER, COL_XS, blk, t) + _halo_specs(SSD_CONV_DIM - SSD_INNER, COL_BC, blk, t) + [
        pl.BlockSpec((TQ, 128), row(COL_DT)),
        pl.BlockSpec((TQ, SSD_STATE), tbl),
        pl.BlockSpec((TQ, SSD_STATE), tbl),
    ]
    args = [proj] * 7 + [cosf, sinf]
    if d == 1:
        in_specs += [pl.BlockSpec((TQ, SSD_INNER), row(COL_Z)), pl.BlockSpec((TQ, SSD_INNER), row(0))]
        args += [proj, y0]
    in_specs += [const((4, SSD_CONV_DIM)), const((1, SSD_CONV_DIM)), const((1, 128)), const((1, 128)),
                 const((128, SSD_INNER))]
    args += [conv_w, conv_b.reshape(1, -1), pad128(dt_bias), pad128(a_log), head_expand_matrix(d)]
    scratch = [pltpu.VMEM((TQ + 2 * HALO, SSD_CONV_DIM), f32), pltpu.VMEM((SSD_GROUPS, SSD_STATE, GW), f32)]
    if d == 1:
        in_specs += [const((1, SSD_INNER)), const((1, SSD_INNER))]
        args += [jnp.repeat(d_skip, SSD_HEADDIM).reshape(1, -1), norm_g.reshape(1, -1)]
        scratch.append(pltpu.VMEM((TQ, SSD_INNER), f32))
    return pl.pallas_call(
        functools.partial(_ssd_kernel, d, lat_blocks),
        grid=(bsz, lat_blocks + 1),
        in_specs=in_specs,
        out_specs=pl.BlockSpec((TQ, SSD_INNER), row(0)),
        out_shape=jax.ShapeDtypeStruct((t, SSD_INNER), f32 if d == 0 else bf16),
        scratch_shapes=scratch,
        compiler_params=_params(),
        name=f"ssd_pass{d}",
    )(*args)


def ssd_branch(proj, cosf, sinf, conv_w, conv_b, a_log, dt_bias, d_skip, norm_g, bsz, lat_blocks):
    y0 = ssd_pass(0, proj, None, cosf, sinf, conv_w, conv_b, a_log, dt_bias, d_skip, norm_g, bsz, lat_blocks)
    return ssd_pass(1, proj, y0, cosf, sinf, conv_w, conv_b, a_log, dt_bias, d_skip, norm_g, bsz, lat_blocks)


def _tile_rows(mod_rows, bsz):
    idx = np.concatenate([np.repeat(np.arange(bsz), SEQ // TM), np.full(bsz * CTX_LEN // TM, bsz)])
    return mod_rows[idx][:, None, :]


def _expert_choice_moe(h, v, aff, w1, w3, w2, layer, g2_rows, bsz):
    t, d = v.shape
    aff = aff[:, :N_EXPERTS]
    nl = bsz * SEQ

    def choose(a, length):
        cap = CAPACITY_FACTOR * length // N_EXPERTS
        g, idx = lax.top_k(jnp.swapaxes(a.reshape(bsz, length, N_EXPERTS), 1, 2), cap)
        return jnp.swapaxes(g, 0, 1), jnp.swapaxes(idx, 0, 1), cap

    g_l, i_l, cap_l = choose(aff[:nl], SEQ)
    g_c, i_c, cap_c = choose(aff[nl:], CTX_LEN)
    boff = jnp.arange(bsz)[None, :, None]
    ctx_rows = (i_c + boff * CTX_LEN).reshape(N_EXPERTS, -1)
    rows = jnp.concatenate([(i_l + boff * SEQ).reshape(N_EXPERTS, -1), ctx_rows + nl], axis=1)
    gates = jnp.concatenate([g_l.reshape(N_EXPERTS, -1), g_c.reshape(N_EXPERTS, -1)], axis=1)[..., None]
    r = rows.shape[1]
    xg = jnp.take(v, rows.reshape(-1), axis=0).reshape(N_EXPERTS, r, d)
    hdn = expert_hidden(xg, w1, w3, layer, r // 4)
    ctx_slots, ctx_tile = bsz * cap_c, bsz * CTX_LEN
    assert (bsz * cap_l) % ctx_slots == 0 and nl % ctx_tile == 0
    h = moe_combine(i_l.reshape(-1), hdn, 0, cap_l, gates, w2, layer, h, g2_rows, 0, 0, SEQ, TN, bsz)
    return moe_combine(ctx_rows.reshape(-1), hdn, bsz * cap_l // ctx_slots, ctx_slots, gates, w2, layer, h, g2_rows,
                       bsz, nl // ctx_tile, ctx_tile, d, 1)


def kernel(x, c, ctx, c_ctx, w_ada, b_ada, norm_mix, norm_ffn, w_in, lru_conv_w, lru_conv_b, lru_w_r, lru_b_r,
           lru_w_i, lru_b_i, lru_lambda, ssd_conv_w, ssd_conv_b, ssd_a_log, ssd_dt_bias, ssd_d, ssd_norm, na_rpb,
           w_branch_lru, w_branch_ssd, w_branch_na, w_out, w_router, w1, w3, w2, norm_final):
    bsz = x.shape[0]
    assert x.shape[1:] == (SEQ, D_MODEL) and ctx.shape[1:] == (CTX_LEN, D_MODEL) and bsz * CTX_LEN == TM
    d = D_MODEL
    h = jnp.concatenate([x.reshape(bsz * SEQ, d), ctx.reshape(bsz * CTX_LEN, d)], axis=0)
    t = h.shape[0]

    cond = jnp.concatenate([c, c_ctx[None, :], jnp.zeros((8 - bsz - 1, d), f32)], axis=0)
    mod = ada_modulation(cond, w_ada, b_ada)
    cosf, sinf = rope_tables(SEQ)

    assert DT_END + DT_PAD == W_QKV0 and w_in.shape[-1] == DT_END + 3 * NA_WIDTH + 3 * d
    w_proj = relayout_proj_weights(jnp.swapaxes(w_in, 1, 2))
    qkv_blocks = 3 * NA_WIDTH // TN
    pa, pb, pc, wo = (w.astype(bf16) for w in (w_branch_lru, w_branch_ssd, w_branch_na, w_out))
    na_tbl, na_pair_of = na_bias_blocks(na_rpb)

    for l in range(DEPTH):
        mods = jnp.split(mod[l], 6, axis=-1)
        sh1, sc1, g1, sh2, sc2 = (_tile_rows(m, bsz) for m in mods[:5])
        u = norm_modulate(h, norm_mix[l], sc1, sh1, bf16)
        proj = matmul_bf16(u, w_proj, l, N_F32, lambda j: jnp.where(j < W_QKV0 // TN, j, j + qkv_blocks), f32)
        qkv = matmul_bf16(u, w_proj, l, 3 * NA_WIDTH, lambda j: j + W_QKV0 // TN, bf16)
        ya = lru_branch(proj, lru_conv_w[l], lru_conv_b[l], lru_w_r[l], lru_b_r[l], lru_w_i[l], lru_b_i[l],
                        lru_lambda[l], bsz, LAT_BLOCKS)
        yb = ssd_branch(proj, cosf, sinf, ssd_conv_w[l], ssd_conv_b[l], ssd_a_log[l], ssd_dt_bias[l], ssd_d[l],
                        ssd_norm[l], bsz, LAT_BLOCKS)
        yc = na_attention(qkv, na_tbl, l, na_pair_of, bsz)
        y = branch_merge(ya, yb, yc, proj, pa, pb, pc, l, COL_G)
        h, v, aff = residual_norm_router(y, wo, l, h, g1, norm_ffn[l], sc2, sh2, w_router[l])
        h = _expert_choice_moe(h, v, aff, w1, w3, w2, l, mods[5][:, None, :], bsz)

    zeros = jnp.zeros((t // TM, 1, d), f32)
    out = norm_modulate(h, norm_final, zeros, zeros, f32, rows=bsz * SEQ)
    return out.reshape(bsz, SEQ, d)
```

```python
import functools

import jax
import jax.numpy as jnp
import numpy as np
from jax import lax
from jax.experimental import pallas as pl
from jax.experimental.pallas import tpu as pltpu

D_MODEL = 2048
SEQ = 4096
CTX_LEN = 256
DEPTH = 4
GRID_W = 64
EPS = 1e-6
ROPE_BASE = 10000.0
LRU_WIDTH = 1024
LRU_BLOCKS = 16
LRU_BLOCK = LRU_WIDTH // LRU_BLOCKS
LRU_C = 8.0
SSD_INNER = 1024
SSD_HEADDIM = 64
SSD_HEADS = SSD_INNER // SSD_HEADDIM
SSD_GROUPS = 4
SSD_STATE = 128
SSD_CHUNK = 128
SSD_CONV_DIM = SSD_INNER + 2 * SSD_GROUPS * SSD_STATE
NA_HEADS = 16
NA_HEADDIM = 64
NA_WIDTH = NA_HEADS * NA_HEADDIM
NA_WIN_R = 8
NA_WIN_C = 16
N_EXPERTS = 16
EXPERT_FF = 1024
CAPACITY_FACTOR = 2

V7X_VMEM_LIMIT = 56 * 1024 * 1024
TM = 1024
MM_MAX_ROWS = 2304
TQ = 256
HALO = 8
LAT_BLOCKS = SEQ // TQ
NEG = -1e30
HI = lax.Precision.HIGHEST
bf16 = jnp.bfloat16
f32 = jnp.float32

TN = 512
DT_END = 2 * LRU_WIDTH + SSD_INNER + SSD_CONV_DIM + 2 * SSD_HEADS
DT_PAD = 480
COL_AX, COL_AG, COL_Z, COL_XS, COL_BC = 0, 1, 2, 3, 4
COL_DT = 40
W_QKV0 = 5632
COL_G = 5632
N_F32 = COL_G + 3 * D_MODEL
GW = SSD_INNER // SSD_GROUPS

NA_GROUP_ROWS = 2
NA_BAND_ROWS = 10
NA_NCFG = 5


def _params():
    return pltpu.CompilerParams(vmem_limit_bytes=V7X_VMEM_LIMIT)


def _ada_kernel(x_ref, w_ref, b_ref, o_ref):
    c = x_ref[...]
    x = (c * jax.nn.sigmoid(c)).astype(bf16)
    o_ref[...] = jnp.dot(x, w_ref[...].astype(bf16), preferred_element_type=f32) + b_ref[...]


def ada_modulation(cond, w_ada, b_ada):
    depth, d, n = w_ada.shape
    tn = 1024
    return pl.pallas_call(
        _ada_kernel,
        grid=(depth, n // tn),
        in_specs=[pl.BlockSpec((8, d), lambda l, j: (0, 0)),
                  pl.BlockSpec((None, d, tn), lambda l, j: (l, 0, j)),
                  pl.BlockSpec((None, 1, tn), lambda l, j: (l, 0, j))],
        out_specs=pl.BlockSpec((None, 8, tn), lambda l, j: (l, 0, j)),
        out_shape=jax.ShapeDtypeStruct((depth, 8, n), f32),
        compiler_params=_params(),
        name="ada_modulation",
    )(cond, w_ada, b_ada.reshape(depth, 1, n))


def _norm_mod_kernel(h_ref, g_ref, sc_ref, sh_ref, o_ref):
    x = h_ref[...]
    y = x * lax.rsqrt(jnp.mean(x * x, axis=-1, keepdims=True) + EPS)
    o_ref[...] = ((y * g_ref[...]) * (1.0 + sc_ref[...]) + sh_ref[...]).astype(o_ref.dtype)


def norm_modulate(h, g, sc_t, sh_t, out_dtype, rows=None):
    t, d = h.shape
    t = t if rows is None else rows
    tm = min(512, TM)
    per = TM // tm
    return pl.pallas_call(
        _norm_mod_kernel,
        grid=(t // tm,),
        in_specs=[pl.BlockSpec((tm, d), lambda i: (i, 0)),
                  pl.BlockSpec((1, d), lambda i: (0, 0)),
                  pl.BlockSpec((None, 1, d), lambda i: (i // per, 0, 0)),
                  pl.BlockSpec((None, 1, d), lambda i: (i // per, 0, 0))],
        out_specs=pl.BlockSpec((tm, d), lambda i: (i, 0)),
        out_shape=jax.ShapeDtypeStruct((t, d), out_dtype),
        compiler_params=_params(),
        name="norm_modulate",
    )(h, g.reshape(1, d), sc_t, sh_t)


def _relayout_kernel(shift, a_ref, b_ref, o_ref):
    j = pl.program_id(1)
    split = DT_END // TN

    @pl.when(j < split)
    def _():
        o_ref[...] = a_ref[...].astype(o_ref.dtype)

    @pl.when(j == split)
    def _():
        row = lax.broadcasted_iota(jnp.int32, a_ref.shape, 0)
        o_ref[...] = jnp.where(row < shift, a_ref[...], 0.0).astype(o_ref.dtype)

    @pl.when(j > split)
    def _():
        o_ref[0:TN - shift, :] = a_ref[shift:TN, :].astype(o_ref.dtype)
        o_ref[TN - shift:TN, :] = b_ref[...].astype(o_ref.dtype)


def relayout_proj_weights(w_t):
    depth, n, d = w_t.shape
    shift = DT_END % TN
    assert shift + DT_PAD == TN and shift % 16 == 0 and n % shift == 0
    split = DT_END // TN
    return pl.pallas_call(
        functools.partial(_relayout_kernel, shift),
        grid=(depth, (n + DT_PAD) // TN),
        in_specs=[pl.BlockSpec((None, TN, d), lambda l, j: (l, jnp.where(j <= split, j, j - 1), 0)),
                  pl.BlockSpec((None, shift, d), lambda l, j: (l, jnp.where(j <= split, 0, (TN // shift) * j), 0))],
        out_specs=pl.BlockSpec((None, TN, d), lambda l, j: (l, j, 0)),
        out_shape=jax.ShapeDtypeStruct((depth, n + DT_PAD, d), bf16),
        compiler_params=_params(),
        name="relayout_proj_weights",
    )(w_t, w_t)


def _mm_kernel(x_ref, w_ref, o_ref):
    o_ref[...] = _dot_nt(x_ref[...], w_ref[...]).astype(o_ref.dtype)


def matmul_bf16(x, w, layer, n_out, w_block, out_dtype):
    m, k = x.shape
    tm = max(c for c in range(16, MM_MAX_ROWS + 1, 16) if m % c == 0)
    return pl.pallas_call(
        _mm_kernel,
        grid=(m // tm, n_out // TN),
        in_specs=[pl.BlockSpec((tm, k), lambda i, j: (i, 0)),
                  pl.BlockSpec((None, TN, k), lambda i, j: (layer, w_block(j), 0))],
        out_specs=pl.BlockSpec((tm, TN), lambda i, j: (i, j)),
        out_shape=jax.ShapeDtypeStruct((m, n_out), out_dtype),
        compiler_params=_params(),
        name="matmul_bf16",
    )(x, w)


def _merge_kernel(ya_ref, yb_ref, yc_ref, ga_ref, gb_ref, gc_ref, pa_ref, pb_ref, pc_ref, o_ref):
    acc = jax.nn.sigmoid(ga_ref[...]) * jnp.dot(ya_ref[...], pa_ref[...], preferred_element_type=f32)
    acc = acc + jax.nn.sigmoid(gb_ref[...]) * jnp.dot(yb_ref[...], pb_ref[...], preferred_element_type=f32)
    acc = acc + jax.nn.sigmoid(gc_ref[...]) * jnp.dot(yc_ref[...], pc_ref[...], preferred_element_type=f32)
    o_ref[...] = acc.astype(o_ref.dtype)


def branch_merge(ya, yb, yc, proj, pa, pb, pc, layer, g_col0):
    t, k = ya.shape
    n = pa.shape[-1]
    tn = 512
    gb0 = g_col0 // tn
    nj = n // tn
    xs = pl.BlockSpec((TM, k), lambda i, j: (i, 0))
    ws = pl.BlockSpec((None, k, tn), lambda i, j: (layer, 0, j))
    gs = lambda q: pl.BlockSpec((TM, tn), lambda i, j: (i, gb0 + q * nj + j))
    return pl.pallas_call(
        _merge_kernel,
        grid=(t // TM, nj),
        in_specs=[xs, xs, xs, gs(0), gs(1), gs(2), ws, ws, ws],
        out_specs=pl.BlockSpec((TM, tn), lambda i, j: (i, j)),
        out_shape=jax.ShapeDtypeStruct((t, n), bf16),
        compiler_params=_params(),
        name="branch_merge",
    )(ya, yb, yc, proj, proj, proj, pa, pb, pc)


def _resid_norm_router_kernel(y_ref, w_ref, h_ref, g1_ref, ng_ref, sc_ref, sh_ref, wr_ref, ho_ref, v_ref, aff_ref):
    hn = h_ref[...] + g1_ref[...] * jnp.dot(y_ref[...], w_ref[...], preferred_element_type=f32)
    ho_ref[...] = hn
    yn = hn * lax.rsqrt(jnp.mean(hn * hn, axis=-1, keepdims=True) + EPS)
    v = ((yn * ng_ref[...]) * (1.0 + sc_ref[...]) + sh_ref[...]).astype(bf16)
    v_ref[...] = v
    s = jnp.dot(v, wr_ref[...], preferred_element_type=f32)
    lane = lax.broadcasted_iota(jnp.int32, s.shape, 1)
    s = jnp.where(lane < N_EXPERTS, s, NEG)
    e = jnp.exp(s - s.max(axis=-1, keepdims=True))
    aff_ref[...] = e / e.sum(axis=-1, keepdims=True)


def residual_norm_router(y, w, layer, h, g1_t, norm_g, sc_t, sh_t, w_router):
    t, d = h.shape
    tm = TM // 2
    wr = jnp.pad(w_router, ((0, 0), (0, 128 - N_EXPERTS))).astype(bf16)
    rows = pl.BlockSpec((tm, d), lambda i: (i, 0))
    per_tile = pl.BlockSpec((None, 1, d), lambda i: (i * tm // TM, 0, 0))
    return pl.pallas_call(
        _resid_norm_router_kernel,
        grid=(t // tm,),
        in_specs=[rows, pl.BlockSpec((None, d, d), lambda i: (layer, 0, 0)), rows, per_tile,
                  pl.BlockSpec((1, d), lambda i: (0, 0)), per_tile, per_tile,
                  pl.BlockSpec((d, 128), lambda i: (0, 0))],
        out_specs=[rows, rows, pl.BlockSpec((tm, 128), lambda i: (i, 0))],
        out_shape=[jax.ShapeDtypeStruct((t, d), f32), jax.ShapeDtypeStruct((t, d), bf16),
                   jax.ShapeDtypeStruct((t, 128), f32)],
        compiler_params=_params(),
        name="residual_norm_router",
    )(y, w, h, g1_t, norm_g.reshape(1, d), sc_t, sh_t, wr)


def _expert_hidden_kernel(x_ref, w1_ref, w3_ref, o_ref, w1b, w3b):
    @pl.when(pl.program_id(1) == 0)
    def _():
        w1b[...] = w1_ref[...].astype(bf16)
        w3b[...] = w3_ref[...].astype(bf16)

    x = x_ref[...]
    a = jnp.dot(x, w1b[...], preferred_element_type=f32)
    b = jnp.dot(x, w3b[...], preferred_element_type=f32)
    o_ref[...] = ((a * jax.nn.sigmoid(a)) * b).astype(o_ref.dtype)


def expert_hidden(xg, w1, w3, layer, tm):
    e, r, d = xg.shape
    f = w1.shape[-1]
    return pl.pallas_call(
        _expert_hidden_kernel,
        grid=(e, r // tm),
        in_specs=[pl.BlockSpec((None, tm, d), lambda k, i: (k, i, 0)),
                  pl.BlockSpec((None, None, d, f), lambda k, i: (layer, k, 0, 0)),
                  pl.BlockSpec((None, None, d, f), lambda k, i: (layer, k, 0, 0))],
        out_specs=pl.BlockSpec((None, tm, f), lambda k, i: (k, i, 0)),
        out_shape=jax.ShapeDtypeStruct((e, r, f), bf16),
        scratch_shapes=[pltpu.VMEM((d, f), bf16), pltpu.VMEM((d, f), bf16)],
        compiler_params=_params(),
        name="expert_hidden",
    )(xg, w1, w3)


SCATTER_UNROLL = 8
SCATTER_PART = 512


def _combine_kernel(nb, slots, idx_ref, hdn_ref, w2_ref, g_ref, h_ref, g2_ref, *rest):
    o_ref, ye_scr = rest[-2], rest[-1]
    b, e = pl.program_id(0), pl.program_id(2)

    @pl.when(e == 0)
    def _():
        o_ref[...] = jnp.zeros_like(o_ref)

    w2 = w2_ref[...].astype(bf16)
    base0 = (e * nb + b) * slots
    part = min(slots, SCATTER_PART)

    for p0 in range(0, slots, part):
        ye_scr[p0:p0 + part, :] = (jnp.dot(hdn_ref[p0:p0 + part, :], w2, preferred_element_type=f32)
                                   * g_ref[p0:p0 + part, :])
        for s0 in range(p0, p0 + part, SCATTER_UNROLL):
            rows = [idx_ref[base0 + s0 + k] for k in range(SCATTER_UNROLL)]
            vals = [o_ref[pl.ds(rows[k], 1), :] + ye_scr[s0 + k:s0 + k + 1, :] for k in range(SCATTER_UNROLL)]
            for k in range(SCATTER_UNROLL):
                o_ref[pl.ds(rows[k], 1), :] = vals[k]

    @pl.when(e == pl.num_programs(2) - 1)
    def _():
        o_ref[...] = h_ref[...] + g2_ref[...] * o_ref[...]


def moe_combine(idx, hdn, slot_blk0, slots, gates, w2, layer, h, g2_rows, g2_row0, row_blk0, rows_blk, dq, nb):
    e, _, f = hdn.shape
    t, d = h.shape
    in_specs = [
        pl.BlockSpec((None, slots, f), lambda b, q, k, idx: (k, slot_blk0 + b, 0)),
        pl.BlockSpec((None, None, f, dq), lambda b, q, k, idx: (layer, k, 0, q)),
        pl.BlockSpec((None, slots, 1), lambda b, q, k, idx: (k, slot_blk0 + b, 0)),
        pl.BlockSpec((rows_blk, dq), lambda b, q, k, idx: (row_blk0 + b, q)),
        pl.BlockSpec((None, 1, dq), lambda b, q, k, idx: (g2_row0 + b, 0, q)),
    ]
    args = [idx, hdn, w2, gates, h, g2_rows]
    aliases = {4: 0}
    return pl.pallas_call(
        functools.partial(_combine_kernel, nb, slots),
        grid_spec=pltpu.PrefetchScalarGridSpec(
            num_scalar_prefetch=1,
            grid=(nb, d // dq, e),
            in_specs=in_specs,
            out_specs=pl.BlockSpec((rows_blk, dq), lambda b, q, k, idx: (row_blk0 + b, q)),
            scratch_shapes=[pltpu.VMEM((slots, dq), f32)],
        ),
        out_shape=jax.ShapeDtypeStruct((t, d), f32),
        input_output_aliases=aliases,
        compiler_params=_params(),
        name="moe_combine",
    )(*args)


def na_bias_blocks(rpb):
    rows = SEQ // GRID_W
    n_dr, n_dc = 2 * NA_WIN_R - 1, 2 * NA_WIN_C - 1
    qc, kc = np.arange(GRID_W)[:, None], np.arange(GRID_W)[None, :]
    cs = np.clip(qc - NA_WIN_C // 2, 0, GRID_W - NA_WIN_C)
    col_ok = (kc >= cs) & (kc < cs + NA_WIN_C)
    pick = (np.arange(n_dc)[:, None, None] == (kc - qc + NA_WIN_C - 1)[None]) & col_ok[None]
    blocks = jnp.einsum('...rd,dqk->...rqk', rpb, jnp.asarray(pick, f32), precision=HI)
    blocks = jnp.where(col_ok, blocks, NEG)
    blocks = jnp.concatenate([blocks, jnp.full(blocks.shape[:-3] + (1, GRID_W, GRID_W), NEG, f32)], axis=-3)
    which = np.full((NA_NCFG, NA_GROUP_ROWS, NA_BAND_ROWS), n_dr, np.int32)
    for c, r0 in enumerate((0, 2, 4, 60, 62)):
        bs = int(np.clip(r0 - NA_WIN_R // 2, 0, rows - NA_BAND_ROWS))
        for qr in range(NA_GROUP_ROWS):
            r = r0 + qr
            rs = int(np.clip(r - NA_WIN_R // 2, 0, rows - NA_WIN_R))
            for kr in range(NA_BAND_ROWS):
                if rs <= bs + kr < rs + NA_WIN_R:
                    which[c, qr, kr] = bs + kr - r + NA_WIN_R - 1
    pairs = sorted({(int(which[c, qr, 2 * m]), int(which[c, qr, 2 * m + 1]))
                    for c in range(NA_NCFG) for qr in range(NA_GROUP_ROWS) for m in range(NA_BAND_ROWS // 2)})
    pair_of = np.array([[[pairs.index((int(which[c, qr, 2 * m]), int(which[c, qr, 2 * m + 1])))
                          for m in range(NA_BAND_ROWS // 2)] for qr in range(NA_GROUP_ROWS)]
                        for c in range(NA_NCFG)], np.int32)
    left = jnp.take(blocks, np.array([p[0] for p in pairs]), axis=-3)
    right = jnp.take(blocks, np.array([p[1] for p in pairs]), axis=-3)
    return jnp.concatenate([left, right], axis=-1), pair_of.reshape(-1)


def _dot_nt(a, b):
    return lax.dot_general(a, b, (((1,), (1,)), ((), ())), preferred_element_type=f32)


def _softmax_av(parts):
    m = parts[0][0].max(axis=-1, keepdims=True)
    for s, _ in parts[1:]:
        m = jnp.maximum(m, s.max(axis=-1, keepdims=True))
    l = None
    o = None
    for s, v in parts:
        p = jnp.exp(s - m)
        ls = p.sum(axis=-1, keepdims=True)
        os_ = jnp.dot(p.astype(bf16), v, preferred_element_type=f32)
        l = ls if l is None else l + ls
        o = os_ if o is None else o + os_
    return o / l


def _na_kernel(pair_ref, q_ref, k_ref, v_ref, kc_ref, vc_ref, tbl_ref, o_ref):
    j = pl.program_id(2)
    npairs = NA_BAND_ROWS // 2

    def bias(hh, cfg):
        rows = []
        for qr in range(NA_GROUP_ROWS):
            base = (cfg * NA_GROUP_ROWS + qr) * npairs
            rows.append(jnp.concatenate([tbl_ref[hh, pair_ref[base + m]] for m in range(npairs)], axis=1))
        return jnp.concatenate(rows, axis=0)

    scale = NA_HEADDIM ** -0.5
    assert scale == 2.0 ** round(np.log2(scale))
    lane = lax.broadcasted_iota(jnp.int32, (1, 2 * NA_HEADDIM), 1)
    first = lane < NA_HEADDIM
    kc = kc_ref[...]
    vc = vc_ref[...]
    gq = NA_GROUP_ROWS * GRID_W

    def head_q(q, hh):
        keep = first if hh == 0 else jnp.logical_not(first)
        return jnp.where(keep, q * scale, jnp.zeros_like(q))

    @pl.when(j < LAT_BLOCKS)
    def _():
        ngroups = TQ // gq
        q2 = []
        for g in range(ngroups):
            q = q_ref[pl.ds(g * gq, gq), :]
            q2.append(jnp.concatenate([head_q(q, 0), head_q(q, 1)], axis=0))
        sc_all = _dot_nt(jnp.concatenate(q2, axis=0), kc)
        o_win, p_ctx, denom = [], [], []
        for g in range(ngroups):
            r0 = j * (TQ // GRID_W) + NA_GROUP_ROWS * g
            bs = jnp.clip(r0 - NA_WIN_R // 2, 0, SEQ // GRID_W - NA_BAND_ROWS)
            cfg = jnp.where(r0 == 0, 0, jnp.where(r0 == 2, 1, jnp.where(r0 == 60, 3, jnp.where(r0 == 62, 4, 2))))
            start = pl.multiple_of(bs * GRID_W, GRID_W)
            kb = k_ref[pl.ds(start, NA_BAND_ROWS * GRID_W), :]
            vb = v_ref[pl.ds(start, NA_BAND_ROWS * GRID_W), :]
            s = _dot_nt(q2[g], kb) + jnp.concatenate([bias(0, cfg), bias(1, cfg)], axis=0)
            sc = sc_all[2 * gq * g:2 * gq * (g + 1)]
            m = jnp.maximum(s.max(axis=-1, keepdims=True), sc.max(axis=-1, keepdims=True))
            p, pc = jnp.exp(s - m), jnp.exp(sc - m)
            denom.append(p.sum(axis=-1, keepdims=True) + pc.sum(axis=-1, keepdims=True))
            o_win.append(jnp.dot(p.astype(bf16), vb, preferred_element_type=f32))
            p_ctx.append(pc.astype(bf16))
        o_ctx = jnp.dot(jnp.concatenate(p_ctx, axis=0), vc, preferred_element_type=f32)
        for g in range(ngroups):
            o2 = (o_win[g] + o_ctx[2 * gq * g:2 * gq * (g + 1)]) / denom[g]
            o_ref[pl.ds(g * gq, gq), :] = jnp.where(first, o2[:gq], o2[gq:]).astype(o_ref.dtype)

    @pl.when(j == LAT_BLOCKS)
    def _():
        q = q_ref[...]
        q2 = jnp.concatenate([head_q(q, 0), head_q(q, 1)], axis=0)
        o2 = _softmax_av([(_dot_nt(q2, kc), vc)])
        o_ref[...] = jnp.where(first, o2[:CTX_LEN], o2[CTX_LEN:]).astype(o_ref.dtype)


def na_attention(qkv, tbl, layer, pair_of, bsz):
    t = qkv.shape[0]
    npair = NA_HEADS // 2
    ctx0 = bsz * LAT_BLOCKS

    def qrow(b, j):
        return jnp.where(j < LAT_BLOCKS, b * LAT_BLOCKS + j, ctx0 + b)

    return pl.pallas_call(
        _na_kernel,
        grid_spec=pltpu.PrefetchScalarGridSpec(
            num_scalar_prefetch=1,
            grid=(bsz, npair, LAT_BLOCKS + 1),
            in_specs=[
                pl.BlockSpec((TQ, 128), lambda b, p, j, po: (qrow(b, j), p)),
                pl.BlockSpec((SEQ, 128), lambda b, p, j, po: (b, npair + p)),
                pl.BlockSpec((SEQ, 128), lambda b, p, j, po: (b, 2 * npair + p)),
                pl.BlockSpec((CTX_LEN, 128), lambda b, p, j, po: (ctx0 + b, npair + p)),
                pl.BlockSpec((CTX_LEN, 128), lambda b, p, j, po: (ctx0 + b, 2 * npair + p)),
                pl.BlockSpec((None, 2) + tbl.shape[2:], lambda b, p, j, po: (layer, p, 0, 0, 0)),
            ],
            out_specs=pl.BlockSpec((TQ, 128), lambda b, p, j, po: (qrow(b, j), p)),
        ),
        out_shape=jax.ShapeDtypeStruct((t, NA_WIDTH), bf16),
        compiler_params=_params(),
        name="na_attention",
    )(jnp.asarray(pair_of), qkv, qkv, qkv, qkv, qkv, tbl)


def _seq_block(d, bsz, lat_blocks):
    ctx0 = bsz * lat_blocks

    def jj(s):
        return (s - 1) if d == 0 else (lat_blocks - s)

    def blk(b, s):
        return jnp.where(s == 0, ctx0 + b, b * lat_blocks + jj(s))

    return jj, blk


def _halo_specs(width, col, blk, t):
    nb8 = TQ // HALO
    return [
        pl.BlockSpec((TQ, width), lambda b, s: (blk(b, s), col)),
        pl.BlockSpec((HALO, width), lambda b, s: (jnp.maximum(blk(b, s) * nb8 - 1, 0), col)),
        pl.BlockSpec((HALO, width), lambda b, s: (jnp.minimum(blk(b, s) * nb8 + nb8, t // HALO - 1), col)),
    ]


def _fill_halo(xe, parts, has_prev, has_next):
    c0 = 0
    for x_ref, xp_ref, xn_ref in parts:
        cols = slice(c0, c0 + x_ref.shape[1])
        xe[0:HALO, cols] = jnp.where(has_prev, xp_ref[...], 0.0)
        xe[HALO:HALO + TQ, cols] = x_ref[...]
        xe[HALO + TQ:2 * HALO + TQ, cols] = jnp.where(has_next, xn_ref[...], 0.0)
        c0 += x_ref.shape[1]


def _centred_conv(xe, cw_ref, cb_ref, cols):
    xv = xe[:, cols]
    n = xv.shape[0]
    out = cb_ref[:, cols]
    for k in range(4):
        shifted = xv if k == 2 else pltpu.roll(xv, (2 - k) % n, 0)
        out = out + cw_ref[k:k + 1, cols] * shifted[HALO:HALO + TQ, :]
    return out


def lru_gate_weights(w_r, w_i):
    def pair(w):
        w = w.reshape(2, LRU_BLOCKS // 2, 2, LRU_BLOCK, LRU_BLOCK)
        z = jnp.zeros_like(w[:, :, 0])
        top = jnp.concatenate([w[:, :, 0], z], axis=-1)
        bot = jnp.concatenate([z, w[:, :, 1]], axis=-1)
        return jnp.concatenate([top, bot], axis=-2)
    return jnp.concatenate([pair(w_r), pair(w_i)], axis=-1).astype(bf16)


def _gelu_tanh(x):
    return 0.5 * x * (1.0 + jnp.tanh(float(np.sqrt(2.0 / np.pi)) * (x + 0.044715 * (x * x * x))))


def _lru_kernel(d, lat_blocks, *refs):
    if d == 0:
        (x_ref, xp_ref, xn_ref, cw_ref, cb_ref, wg_ref, br_ref, bi_ref, lam_ref, o_ref,
         xe, a_scr, u_scr, h_scr) = refs
    else:
        (x_ref, xp_ref, xn_ref, ag_ref, hf_ref, cw_ref, cb_ref, wg_ref, br_ref, bi_ref, lam_ref, o_ref,
         xe, a_scr, u_scr, h_scr, hs_scr) = refs
    s = pl.program_id(1)
    j = (s - 1) if d == 0 else (lat_blocks - s)
    has_prev = jnp.logical_and(s > 0, j > 0)
    has_next = jnp.logical_and(s > 0, j < lat_blocks - 1)

    @pl.when(s == 0)
    def _():
        h_scr[...] = jnp.zeros_like(h_scr)

    _fill_halo(xe, [(x_ref, xp_ref, xn_ref)], has_prev, has_next)
    sp = jax.nn.softplus(-lam_ref[...])
    for p in range(LRU_WIDTH // 128):
        sl = slice(128 * p, 128 * (p + 1))
        xc = _centred_conv(xe, cw_ref, cb_ref, sl)
        g = jnp.dot(xc.astype(bf16), wg_ref[p], preferred_element_type=f32)
        r = jax.nn.sigmoid(g[:, :128] + br_ref[:, sl])
        i = jax.nn.sigmoid(g[:, 128:] + bi_ref[:, sl])
        log_a = -LRU_C * r * sp[:, sl]
        a = jnp.exp(log_a)
        a_scr[:, sl] = a
        u_scr[:, sl] = jnp.sqrt(-jnp.tanh(log_a) * (a * a + 1.0)) * (i * xc)

    out = o_ref if d == 0 else hs_scr
    h = h_scr[0:1, :]
    for t in (range(TQ) if d == 0 else range(TQ - 1, -1, -1)):
        h = a_scr[t:t + 1, :] * h + u_scr[t:t + 1, :]
        out[t:t + 1, :] = h
    h_scr[0:1, :] = h
    if d == 1:
        y = (hf_ref[...] + hs_scr[...]) * _gelu_tanh(ag_ref[...])
        o_ref[...] = y.astype(o_ref.dtype)


def lru_pass(d, proj, hf, conv_w, conv_b, wg, b_r, b_i, lam, bsz, lat_blocks):
    t = proj.shape[0]
    _, blk = _seq_block(d, bsz, lat_blocks)
    row = lambda c: (lambda b, s: (blk(b, s), c))
    in_specs = _halo_specs(LRU_WIDTH, COL_AX, blk, t)
    args = [proj, proj, proj]
    if d == 1:
        in_specs += [pl.BlockSpec((TQ, LRU_WIDTH), row(COL_AG)), pl.BlockSpec((TQ, LRU_WIDTH), row(0))]
        args += [proj, hf]
    const = lambda shape: pl.BlockSpec(shape, lambda b, s: (0,) * len(shape))
    in_specs += [const((4, LRU_WIDTH)), const((1, LRU_WIDTH)), const((LRU_WIDTH // 128, 128, 256)),
                 const((1, LRU_WIDTH)), const((1, LRU_WIDTH)), const((1, LRU_WIDTH))]
    args += [conv_w, conv_b.reshape(1, -1), wg[d], b_r[d].reshape(1, -1), b_i[d].reshape(1, -1), lam[d].reshape(1, -1)]
    scratch = [pltpu.VMEM((TQ + 2 * HALO, LRU_WIDTH), f32), pltpu.VMEM((TQ, LRU_WIDTH), f32),
               pltpu.VMEM((TQ, LRU_WIDTH), f32), pltpu.VMEM((8, LRU_WIDTH), f32)]
    if d == 1:
        scratch.append(pltpu.VMEM((TQ, LRU_WIDTH), f32))
    return pl.pallas_call(
        functools.partial(_lru_kernel, d, lat_blocks),
        grid=(bsz, lat_blocks + 1),
        in_specs=in_specs,
        out_specs=pl.BlockSpec((TQ, LRU_WIDTH), row(0)),
        out_shape=jax.ShapeDtypeStruct((t, LRU_WIDTH), f32 if d == 0 else bf16),
        scratch_shapes=scratch,
        compiler_params=_params(),
        name=f"lru_pass{d}",
    )(*args)


def lru_branch(proj, conv_w, conv_b, w_r, b_r, w_i, b_i, lam, bsz, lat_blocks):
    wg = lru_gate_weights(w_r, w_i)
    hf = lru_pass(0, proj, None, conv_w, conv_b, wg, b_r, b_i, lam, bsz, lat_blocks)
    return lru_pass(1, proj, hf, conv_w, conv_b, wg, b_r, b_i, lam, bsz, lat_blocks)


def rope_tables(seq):
    pos = jnp.arange(seq)
    row = (pos // GRID_W).astype(f32)
    col = (pos % GRID_W).astype(f32)
    n_freq = SSD_STATE // 4
    freqs = ROPE_BASE ** (-jnp.arange(n_freq, dtype=f32) / n_freq)
    ang = jnp.concatenate([row[:, None] * freqs, col[:, None] * freqs], axis=-1)
    cos, sin = jnp.cos(ang), jnp.sin(ang)
    cosf = jnp.concatenate([cos, cos], axis=-1)
    sinf = jnp.concatenate([-sin, sin], axis=-1)
    cosf = jnp.concatenate([cosf, jnp.ones((TQ, SSD_STATE), f32)], axis=0)
    sinf = jnp.concatenate([sinf, jnp.zeros((TQ, SSD_STATE), f32)], axis=0)
    return cosf, sinf


def head_expand_matrix(d):
    e = np.zeros((128, SSD_INNER), np.float32)
    for h in range(SSD_HEADS):
        e[SSD_HEADS * d + h, h * SSD_HEADDIM:(h + 1) * SSD_HEADDIM] = 1.0
    return jnp.asarray(e, bf16)


def _split3(a):
    hi = a.astype(bf16)
    r = a - hi.astype(f32)
    mid = r.astype(bf16)
    return hi, mid, (r - mid.astype(f32)).astype(bf16)


def _dot_exact_rhs01(a, m01):
    return sum(jnp.dot(p, m01, preferred_element_type=f32) for p in _split3(a))


def _dot_exact_lhs01(m01, a):
    return sum(jnp.dot(m01, p, preferred_element_type=f32) for p in _split3(a))


def _ssd_kernel(d, lat_blocks, *refs):
    xs_refs, bc_refs, refs = refs[0:3], refs[3:6], refs[6:]
    if d == 0:
        (dt_ref, cos_ref, sin_ref, cw_ref, cb_ref, dtb_ref, alog_ref, e_ref,
         o_ref, xe, h_scr) = refs
    else:
        (dt_ref, cos_ref, sin_ref, z_ref, y0_ref, cw_ref, cb_ref, dtb_ref, alog_ref, e_ref,
         dsk_ref, ng_ref, o_ref, xe, h_scr, y_scr) = refs
    s = pl.program_id(1)
    j = (s - 1) if d == 0 else (lat_blocks - s)
    has_prev = jnp.logical_and(s > 0, j > 0)
    has_next = jnp.logical_and(s > 0, j < lat_blocks - 1)
    q = SSD_CHUNK

    @pl.when(s == 0)
    def _():
        h_scr[...] = jnp.zeros_like(h_scr)

    _fill_halo(xe, [xs_refs, bc_refs], has_prev, has_next)
    xbc = _centred_conv(xe, cw_ref, cb_ref, slice(None))
    xbc = xbc * jax.nn.sigmoid(xbc)
    xs = xbc[:, :SSD_INNER]
    cosf, sinf = cos_ref[...], sin_ref[...]

    def rope(g, off):
        v = xbc[:, off + g * SSD_STATE: off + (g + 1) * SSD_STATE]
        return (v * cosf + pltpu.roll(v, SSD_STATE // 2, 1) * sinf).astype(bf16)

    bm = [rope(g, SSD_INNER) for g in range(SSD_GROUPS)]
    cm = [rope(g, SSD_INNER + SSD_GROUPS * SSD_STATE) for g in range(SSD_GROUPS)]
    dt = jax.nn.softplus(dt_ref[...] + dtb_ref[...])
    delta = dt * (-jnp.exp(alog_ref[...]))
    ri = lax.broadcasted_iota(jnp.int32, (q, q), 0)
    ci = lax.broadcasted_iota(jnp.int32, (q, q), 1)
    keep = (ci <= ri) if d == 0 else (ci >= ri)
    tri = jnp.where(keep, 1.0, 0.0).astype(bf16)
    lane = lax.broadcasted_iota(jnp.int32, (1, 2 * SSD_HEADDIM), 1)
    halves = (lane < SSD_HEADDIM, lane >= SSD_HEADDIM)
    e = e_ref[...]
    last = q - 1 if d == 0 else 0
    out = o_ref if d == 0 else y_scr

    for c in (range(TQ // q) if d == 0 else range(TQ // q - 1, -1, -1)):
        rows = slice(c * q, (c + 1) * q)
        at = _dot_exact_lhs01(tri, delta[rows])
        at_exp = _dot_exact_rhs01(at, e)
        dt_exp = _dot_exact_rhs01(dt[rows], e)
        tot_exp = at_exp[last:last + 1, :]
        xdt = xs[rows] * dt_exp
        xd = (xdt * jnp.exp(tot_exp - at_exp)).astype(bf16)
        eat = jnp.exp(at_exp)
        cdec = jnp.exp(tot_exp)
        at_row = at.T
        ys = []
        for g in range(SSD_GROUPS):
            bg, cg = bm[g][rows], cm[g][rows]
            cb = _dot_nt(cg, bg)
            ht = h_scr[g]
            yoff = jnp.dot(cg, ht.astype(bf16), preferred_element_type=f32) * eat[:, g * GW:(g + 1) * GW]
            for pp in range(2):
                xpair = xdt[:, g * GW + pp * 128: g * GW + (pp + 1) * 128]
                acc = yoff[:, pp * 128:(pp + 1) * 128]
                for hh in range(2):
                    li = SSD_HEADS * d + 4 * g + 2 * pp + hh
                    seg = at[:, li:li + 1] - at_row[li:li + 1, :]
                    m = (cb * jnp.exp(jnp.where(keep, seg, NEG))).astype(bf16)
                    xm = jnp.where(halves[hh], xpair, 0.0).astype(bf16)
                    acc = acc + jnp.dot(m, xm, preferred_element_type=f32)
                ys.append(acc)
            upd = lax.dot_general(bg, xd[:, g * GW:(g + 1) * GW], (((0,), (0,)), ((), ())),
                                  preferred_element_type=f32)
            h_scr[g] = cdec[:, g * GW:(g + 1) * GW] * ht + upd
        out[rows, :] = jnp.concatenate(ys, axis=-1)

    if d == 1:
        y = y0_ref[...] + y_scr[...] + dsk_ref[...] * xs
        z = z_ref[...]
        y = y * (z * jax.nn.sigmoid(z))
        y = y * lax.rsqrt(jnp.mean(y * y, axis=-1, keepdims=True) + EPS)
        o_ref[...] = (y * ng_ref[...]).astype(o_ref.dtype)


def ssd_pass(d, proj, y0, cosf, sinf, conv_w, conv_b, a_log, dt_bias, d_skip, norm_g, bsz, lat_blocks):
    t = proj.shape[0]
    jj, blk = _seq_block(d, bsz, lat_blocks)
    row = lambda c: (lambda b, s: (blk(b, s), c))
    tbl = lambda b, s: (jnp.where(s == 0, lat_blocks, jj(s)), 0)
    const = lambda shape: pl.BlockSpec(shape, lambda b, s: (0,) * len(shape))
    pad128 = lambda v: jnp.pad(v.reshape(1, -1).astype(f32), ((0, 0), (0, 128 - v.size)))
    in_specs = _halo_specs(SSD_INNER, COL_XS, blk, t) + _halo_specs(SSD_CONV_DIM - SSD_INNER, COL_BC, blk, t) + [
        pl.BlockSpec((TQ, 128), row(COL_DT)),
        pl.BlockSpec((TQ, SSD_STATE), tbl),
        pl.BlockSpec((TQ, SSD_STATE), tbl),
    ]
    args = [proj] * 7 + [cosf, sinf]
    if d == 1:
        in_specs += [pl.BlockSpec((TQ, SSD_INNER), row(COL_Z)), pl.BlockSpec((TQ, SSD_INNER), row(0))]
        args += [proj, y0]
    in_specs += [const((4, SSD_CONV_DIM)), const((1, SSD_CONV_DIM)), const((1, 128)), const((1, 128)),
                 const((128, SSD_INNER))]
    args += [conv_w, conv_b.reshape(1, -1), pad128(dt_bias), pad128(a_log), head_expand_matrix(d)]
    scratch = [pltpu.VMEM((TQ + 2 * HALO, SSD_CONV_DIM), f32), pltpu.VMEM((SSD_GROUPS, SSD_STATE, GW), f32)]
    if d == 1:
        in_specs += [const((1, SSD_INNER)), const((1, SSD_INNER))]
        args += [jnp.repeat(d_skip, SSD_HEADDIM).reshape(1, -1), norm_g.reshape(1, -1)]
        scratch.append(pltpu.VMEM((TQ, SSD_INNER), f32))
    return pl.pallas_call(
        functools.partial(_ssd_kernel, d, lat_blocks),
        grid=(bsz, lat_blocks + 1),
        in_specs=in_specs,
        out_specs=pl.BlockSpec((TQ, SSD_INNER), row(0)),
        out_shape=jax.ShapeDtypeStruct((t, SSD_INNER), f32 if d == 0 else bf16),
        scratch_shapes=scratch,
        compiler_params=_params(),
        name=f"ssd_pass{d}",
    )(*args)


def ssd_branch(proj, cosf, sinf, conv_w, conv_b, a_log, dt_bias, d_skip, norm_g, bsz, lat_blocks):
    y0 = ssd_pass(0, proj, None, cosf, sinf, conv_w, conv_b, a_log, dt_bias, d_skip, norm_g, bsz, lat_blocks)
    return ssd_pass(1, proj, y0, cosf, sinf, conv_w, conv_b, a_log, dt_bias, d_skip, norm_g, bsz, lat_blocks)


def _tile_rows(mod_rows, bsz):
    idx = np.concatenate([np.repeat(np.arange(bsz), SEQ // TM), np.full(bsz * CTX_LEN // TM, bsz)])
    return mod_rows[idx][:, None, :]


def _expert_choice_moe(h, v, aff, w1, w3, w2, layer, g2_rows, bsz):
    t, d = v.shape
    aff = aff[:, :N_EXPERTS]
    nl = bsz * SEQ

    def choose(a, length):
        cap = CAPACITY_FACTOR * length // N_EXPERTS
        g, idx = lax.top_k(jnp.swapaxes(a.reshape(bsz, length, N_EXPERTS), 1, 2), cap)
        return jnp.swapaxes(g, 0, 1), jnp.swapaxes(idx, 0, 1), cap

    g_l, i_l, cap_l = choose(aff[:nl], SEQ)
    g_c, i_c, cap_c = choose(aff[nl:], CTX_LEN)
    boff = jnp.arange(bsz)[None, :, None]
    ctx_rows = (i_c + boff * CTX_LEN).reshape(N_EXPERTS, -1)
    rows = jnp.concatenate([(i_l + boff * SEQ).reshape(N_EXPERTS, -1), ctx_rows + nl], axis=1)
    gates = jnp.concatenate([g_l.reshape(N_EXPERTS, -1), g_c.reshape(N_EXPERTS, -1)], axis=1)[..., None]
    r = rows.shape[1]
    xg = v.at[rows.reshape(-1)].get(mode="promise_in_bounds").reshape(N_EXPERTS, r, d)
    hdn = expert_hidden(xg, w1, w3, layer, r // 4)
    ctx_slots, ctx_tile = bsz * cap_c, bsz * CTX_LEN
    assert (bsz * cap_l) % ctx_slots == 0 and nl % ctx_tile == 0
    h = moe_combine(i_l.reshape(-1), hdn, 0, cap_l, gates, w2, layer, h, g2_rows, 0, 0, SEQ, TN, bsz)
    return moe_combine(ctx_rows.reshape(-1), hdn, bsz * cap_l // ctx_slots, ctx_slots, gates, w2, layer, h, g2_rows,
                       bsz, nl // ctx_tile, ctx_tile, d, 1)


def kernel(x, c, ctx, c_ctx, w_ada, b_ada, norm_mix, norm_ffn, w_in, lru_conv_w, lru_conv_b, lru_w_r, lru_b_r,
           lru_w_i, lru_b_i, lru_lambda, ssd_conv_w, ssd_conv_b, ssd_a_log, ssd_dt_bias, ssd_d, ssd_norm, na_rpb,
           w_branch_lru, w_branch_ssd, w_branch_na, w_out, w_router, w1, w3, w2, norm_final):
    bsz = x.shape[0]
    assert x.shape[1:] == (SEQ, D_MODEL) and ctx.shape[1:] == (CTX_LEN, D_MODEL) and bsz * CTX_LEN == TM
    d = D_MODEL
    h = jnp.concatenate([x.reshape(bsz * SEQ, d), ctx.reshape(bsz * CTX_LEN, d)], axis=0)
    t = h.shape[0]

    cond = jnp.concatenate([c, c_ctx[None, :], jnp.zeros((8 - bsz - 1, d), f32)], axis=0)
    mod = ada_modulation(cond, w_ada, b_ada)
    cosf, sinf = rope_tables(SEQ)

    assert DT_END + DT_PAD == W_QKV0 and w_in.shape[-1] == DT_END + 3 * NA_WIDTH + 3 * d
    w_proj = relayout_proj_weights(jnp.swapaxes(w_in, 1, 2))
    qkv_blocks = 3 * NA_WIDTH // TN
    pa, pb, pc, wo = (w.astype(bf16) for w in (w_branch_lru, w_branch_ssd, w_branch_na, w_out))
    na_tbl, na_pair_of = na_bias_blocks(na_rpb)

    for l in range(DEPTH):
        mods = jnp.split(mod[l], 6, axis=-1)
        sh1, sc1, g1, sh2, sc2 = (_tile_rows(m, bsz) for m in mods[:5])
        u = norm_modulate(h, norm_mix[l], sc1, sh1, bf16)
        proj = matmul_bf16(u, w_proj, l, N_F32, lambda j: jnp.where(j < W_QKV0 // TN, j, j + qkv_blocks), f32)
        qkv = matmul_bf16(u, w_proj, l, 3 * NA_WIDTH, lambda j: j + W_QKV0 // TN, bf16)
        ya = lru_branch(proj, lru_conv_w[l], lru_conv_b[l], lru_w_r[l], lru_b_r[l], lru_w_i[l], lru_b_i[l],
                        lru_lambda[l], bsz, LAT_BLOCKS)
        yb = ssd_branch(proj, cosf, sinf, ssd_conv_w[l], ssd_conv_b[l], ssd_a_log[l], ssd_dt_bias[l], ssd_d[l],
                        ssd_norm[l], bsz, LAT_BLOCKS)
        yc = na_attention(qkv, na_tbl, l, na_pair_of, bsz)
        y = branch_merge(ya, yb, yc, proj, pa, pb, pc, l, COL_G)
        h, v, aff = residual_norm_router(y, wo, l, h, g1, norm_ffn[l], sc2, sh2, w_router[l])
        h = _expert_choice_moe(h, v, aff, w1, w3, w2, l, mods[5][:, None, :], bsz)

    zeros = jnp.zeros((t // TM, 1, d), f32)
    out = norm_modulate(h, norm_final, zeros, zeros, f32, rows=bsz * SEQ)
    return out.reshape(bsz, SEQ, d)
```

```python
import functools

import jax
import jax.numpy as jnp
import numpy as np
from jax import lax
from jax.experimental import pallas as pl
from jax.experimental.pallas import tpu as pltpu

D_MODEL = 2048
SEQ = 4096
CTX_LEN = 256
DEPTH = 4
GRID_W = 64
EPS = 1e-6
ROPE_BASE = 10000.0
LRU_WIDTH = 1024
LRU_BLOCKS = 16
LRU_BLOCK = LRU_WIDTH // LRU_BLOCKS
LRU_C = 8.0
SSD_INNER = 1024
SSD_HEADDIM = 64
SSD_HEADS = SSD_INNER // SSD_HEADDIM
SSD_GROUPS = 4
SSD_STATE = 128
SSD_CHUNK = 128
SSD_CONV_DIM = SSD_INNER + 2 * SSD_GROUPS * SSD_STATE
NA_HEADS = 16
NA_HEADDIM = 64
NA_WIDTH = NA_HEADS * NA_HEADDIM
NA_WIN_R = 8
NA_WIN_C = 16
N_EXPERTS = 16
EXPERT_FF = 1024
CAPACITY_FACTOR = 2

V7X_VMEM_LIMIT = 56 * 1024 * 1024
TM = 1024
MM_MAX_ROWS = 2304
TQ = 256
HALO = 8
LAT_BLOCKS = SEQ // TQ
NEG = -1e30
HI = lax.Precision.HIGHEST
bf16 = jnp.bfloat16
f32 = jnp.float32

TN = 512
DT_END = 2 * LRU_WIDTH + SSD_INNER + SSD_CONV_DIM + 2 * SSD_HEADS
DT_PAD = 480
COL_AX, COL_AG, COL_Z, COL_XS, COL_BC = 0, 1, 2, 3, 4
COL_DT = 40
W_QKV0 = 5632
COL_G = 5632
N_F32 = COL_G + 3 * D_MODEL
GW = SSD_INNER // SSD_GROUPS

NA_GROUP_ROWS = 2
NA_BAND_ROWS = 10
NA_NCFG = 5


def _params():
    return pltpu.CompilerParams(vmem_limit_bytes=V7X_VMEM_LIMIT)


def _ada_kernel(x_ref, w_ref, b_ref, o_ref):
    c = x_ref[...]
    x = (c * jax.nn.sigmoid(c)).astype(bf16)
    o_ref[...] = jnp.dot(x, w_ref[...].astype(bf16), preferred_element_type=f32) + b_ref[...]


def ada_modulation(cond, w_ada, b_ada):
    depth, d, n = w_ada.shape
    tn = 1024
    return pl.pallas_call(
        _ada_kernel,
        grid=(depth, n // tn),
        in_specs=[pl.BlockSpec((8, d), lambda l, j: (0, 0)),
                  pl.BlockSpec((None, d, tn), lambda l, j: (l, 0, j)),
                  pl.BlockSpec((None, 1, tn), lambda l, j: (l, 0, j))],
        out_specs=pl.BlockSpec((None, 8, tn), lambda l, j: (l, 0, j)),
        out_shape=jax.ShapeDtypeStruct((depth, 8, n), f32),
        compiler_params=_params(),
        name="ada_modulation",
    )(cond, w_ada, b_ada.reshape(depth, 1, n))


def _stream_rows(h, tm):
    if not isinstance(h, tuple):
        return [pl.BlockSpec((tm, h.shape[1]), lambda i: (i, 0))], [h], (lambda refs, i: refs[0][...]), h.shape[0]
    lat, ctx = h
    d = lat.shape[1]
    n_lat = lat.shape[0] // tm
    assert lat.shape[0] % tm == 0 and ctx.shape[0] % tm == 0
    specs = [pl.BlockSpec((tm, d), lambda i: (jnp.minimum(i, n_lat - 1), 0)),
             pl.BlockSpec((tm, d), lambda i: (jnp.maximum(i - n_lat, 0), 0))]
    return specs, [lat, ctx], (lambda refs, i: jnp.where(i < n_lat, refs[0][...], refs[1][...])), lat.shape[0] + ctx.shape[0]


def _norm_mod_kernel(read_h, nh, *refs):
    g_ref, sc_ref, sh_ref, o_ref = refs[nh:]
    x = read_h(refs[:nh], pl.program_id(0))
    y = x * lax.rsqrt(jnp.mean(x * x, axis=-1, keepdims=True) + EPS)
    o_ref[...] = ((y * g_ref[...]) * (1.0 + sc_ref[...]) + sh_ref[...]).astype(o_ref.dtype)


def norm_modulate(h, g, sc_t, sh_t, out_dtype, rows=None):
    tm = min(512, TM)
    per = TM // tm
    h_specs, h_args, read_h, t = _stream_rows(h, tm)
    d = h_args[0].shape[1]
    t = t if rows is None else rows
    return pl.pallas_call(
        functools.partial(_norm_mod_kernel, read_h, len(h_args)),
        grid=(t // tm,),
        in_specs=h_specs + [pl.BlockSpec((1, d), lambda i: (0, 0)),
                            pl.BlockSpec((None, 1, d), lambda i: (i // per, 0, 0)),
                            pl.BlockSpec((None, 1, d), lambda i: (i // per, 0, 0))],
        out_specs=pl.BlockSpec((tm, d), lambda i: (i, 0)),
        out_shape=jax.ShapeDtypeStruct((t, d), out_dtype),
        compiler_params=_params(),
        name="norm_modulate",
    )(*h_args, g.reshape(1, d), sc_t, sh_t)


def _relayout_kernel(shift, a_ref, b_ref, o_ref):
    j = pl.program_id(1)
    split = DT_END // TN

    @pl.when(j < split)
    def _():
        o_ref[...] = a_ref[...].astype(o_ref.dtype)

    @pl.when(j == split)
    def _():
        row = lax.broadcasted_iota(jnp.int32, a_ref.shape, 0)
        o_ref[...] = jnp.where(row < shift, a_ref[...], 0.0).astype(o_ref.dtype)

    @pl.when(j > split)
    def _():
        o_ref[0:TN - shift, :] = a_ref[shift:TN, :].astype(o_ref.dtype)
        o_ref[TN - shift:TN, :] = b_ref[...].astype(o_ref.dtype)


def relayout_proj_weights(w_t):
    depth, n, d = w_t.shape
    shift = DT_END % TN
    assert shift + DT_PAD == TN and shift % 16 == 0 and n % shift == 0
    split = DT_END // TN
    return pl.pallas_call(
        functools.partial(_relayout_kernel, shift),
        grid=(depth, (n + DT_PAD) // TN),
        in_specs=[pl.BlockSpec((None, TN, d), lambda l, j: (l, jnp.where(j <= split, j, j - 1), 0)),
                  pl.BlockSpec((None, shift, d), lambda l, j: (l, jnp.where(j <= split, 0, (TN // shift) * j), 0))],
        out_specs=pl.BlockSpec((None, TN, d), lambda l, j: (l, j, 0)),
        out_shape=jax.ShapeDtypeStruct((depth, n + DT_PAD, d), bf16),
        compiler_params=_params(),
        name="relayout_proj_weights",
    )(w_t, w_t)


def _mm_kernel(x_ref, w_ref, o_ref):
    o_ref[...] = _dot_nt(x_ref[...], w_ref[...]).astype(o_ref.dtype)


def matmul_bf16(x, w, layer, n_out, w_block, out_dtype):
    m, k = x.shape
    tm = max(c for c in range(16, MM_MAX_ROWS + 1, 16) if m % c == 0)
    return pl.pallas_call(
        _mm_kernel,
        grid=(m // tm, n_out // TN),
        in_specs=[pl.BlockSpec((tm, k), lambda i, j: (i, 0)),
                  pl.BlockSpec((None, TN, k), lambda i, j: (layer, w_block(j), 0))],
        out_specs=pl.BlockSpec((tm, TN), lambda i, j: (i, j)),
        out_shape=jax.ShapeDtypeStruct((m, n_out), out_dtype),
        compiler_params=_params(),
        name="matmul_bf16",
    )(x, w)


def _merge_kernel(ya_ref, yb_ref, yc_ref, ga_ref, gb_ref, gc_ref, pa_ref, pb_ref, pc_ref, o_ref):
    acc = jax.nn.sigmoid(ga_ref[...]) * jnp.dot(ya_ref[...], pa_ref[...], preferred_element_type=f32)
    acc = acc + jax.nn.sigmoid(gb_ref[...]) * jnp.dot(yb_ref[...], pb_ref[...], preferred_element_type=f32)
    acc = acc + jax.nn.sigmoid(gc_ref[...]) * jnp.dot(yc_ref[...], pc_ref[...], preferred_element_type=f32)
    o_ref[...] = acc.astype(o_ref.dtype)


def branch_merge(ya, yb, yc, proj, pa, pb, pc, layer, g_col0):
    t, k = ya.shape
    n = pa.shape[-1]
    tn = 512
    gb0 = g_col0 // tn
    nj = n // tn
    xs = pl.BlockSpec((TM, k), lambda i, j: (i, 0))
    ws = pl.BlockSpec((None, k, tn), lambda i, j: (layer, 0, j))
    gs = lambda q: pl.BlockSpec((TM, tn), lambda i, j: (i, gb0 + q * nj + j))
    return pl.pallas_call(
        _merge_kernel,
        grid=(t // TM, nj),
        in_specs=[xs, xs, xs, gs(0), gs(1), gs(2), ws, ws, ws],
        out_specs=pl.BlockSpec((TM, tn), lambda i, j: (i, j)),
        out_shape=jax.ShapeDtypeStruct((t, n), bf16),
        compiler_params=_params(),
        name="branch_merge",
    )(ya, yb, yc, proj, proj, proj, pa, pb, pc)


def _resid_norm_router_kernel(read_h, nh, *refs):
    y_ref, w_ref, g1_ref, ng_ref, sc_ref, sh_ref, wr_ref, ho_ref, v_ref, aff_ref = refs[nh:]
    h = read_h(refs[:nh], pl.program_id(0))
    hn = h + g1_ref[...] * jnp.dot(y_ref[...], w_ref[...], preferred_element_type=f32)
    ho_ref[...] = hn
    yn = hn * lax.rsqrt(jnp.mean(hn * hn, axis=-1, keepdims=True) + EPS)
    v = ((yn * ng_ref[...]) * (1.0 + sc_ref[...]) + sh_ref[...]).astype(bf16)
    v_ref[...] = v
    s = jnp.dot(v, wr_ref[...], preferred_element_type=f32)
    lane = lax.broadcasted_iota(jnp.int32, s.shape, 1)
    s = jnp.where(lane < N_EXPERTS, s, NEG)
    e = jnp.exp(s - s.max(axis=-1, keepdims=True))
    aff_ref[...] = e / e.sum(axis=-1, keepdims=True)


def residual_norm_router(y, w, layer, h, g1_t, norm_g, sc_t, sh_t, w_router):
    tm = TM // 2
    h_specs, h_args, read_h, t = _stream_rows(h, tm)
    d = h_args[0].shape[1]
    wr = jnp.pad(w_router, ((0, 0), (0, 128 - N_EXPERTS))).astype(bf16)
    rows = pl.BlockSpec((tm, d), lambda i: (i, 0))
    per_tile = pl.BlockSpec((None, 1, d), lambda i: (i * tm // TM, 0, 0))
    return pl.pallas_call(
        functools.partial(_resid_norm_router_kernel, read_h, len(h_args)),
        grid=(t // tm,),
        in_specs=h_specs + [rows, pl.BlockSpec((None, d, d), lambda i: (layer, 0, 0)), per_tile,
                            pl.BlockSpec((1, d), lambda i: (0, 0)), per_tile, per_tile,
                            pl.BlockSpec((d, 128), lambda i: (0, 0))],
        out_specs=[rows, rows, pl.BlockSpec((tm, 128), lambda i: (i, 0))],
        out_shape=[jax.ShapeDtypeStruct((t, d), f32), jax.ShapeDtypeStruct((t, d), bf16),
                   jax.ShapeDtypeStruct((t, 128), f32)],
        compiler_params=_params(),
        name="residual_norm_router",
    )(*h_args, y, w, g1_t, norm_g.reshape(1, d), sc_t, sh_t, wr)


def _expert_hidden_kernel(x_ref, w1_ref, w3_ref, o_ref, w1b, w3b):
    @pl.when(pl.program_id(1) == 0)
    def _():
        w1b[...] = w1_ref[...].astype(bf16)
        w3b[...] = w3_ref[...].astype(bf16)

    x = x_ref[...]
    a = jnp.dot(x, w1b[...], preferred_element_type=f32)
    b = jnp.dot(x, w3b[...], preferred_element_type=f32)
    o_ref[...] = ((a * jax.nn.sigmoid(a)) * b).astype(o_ref.dtype)


def expert_hidden(xg, w1, w3, layer, tm):
    e, r, d = xg.shape
    f = w1.shape[-1]
    return pl.pallas_call(
        _expert_hidden_kernel,
        grid=(e, r // tm),
        in_specs=[pl.BlockSpec((None, tm, d), lambda k, i: (k, i, 0)),
                  pl.BlockSpec((None, None, d, f), lambda k, i: (layer, k, 0, 0)),
                  pl.BlockSpec((None, None, d, f), lambda k, i: (layer, k, 0, 0))],
        out_specs=pl.BlockSpec((None, tm, f), lambda k, i: (k, i, 0)),
        out_shape=jax.ShapeDtypeStruct((e, r, f), bf16),
        scratch_shapes=[pltpu.VMEM((d, f), bf16), pltpu.VMEM((d, f), bf16)],
        compiler_params=_params(),
        name="expert_hidden",
    )(xg, w1, w3)


SCATTER_UNROLL = 8
SCATTER_PART = 512


def _combine_kernel(nb, slots, idx_ref, hdn_ref, w2_ref, g_ref, h_ref, g2_ref, *rest):
    o_ref, ye_scr = rest[-2], rest[-1]
    b, e = pl.program_id(0), pl.program_id(2)

    @pl.when(e == 0)
    def _():
        o_ref[...] = jnp.zeros_like(o_ref)

    w2 = w2_ref[...].astype(bf16)
    base0 = (e * nb + b) * slots
    part = min(slots, SCATTER_PART)

    for p0 in range(0, slots, part):
        ye_scr[p0:p0 + part, :] = (jnp.dot(hdn_ref[p0:p0 + part, :], w2, preferred_element_type=f32)
                                   * g_ref[p0:p0 + part, :])
        for s0 in range(p0, p0 + part, SCATTER_UNROLL):
            rows = [idx_ref[base0 + s0 + k] for k in range(SCATTER_UNROLL)]
            vals = [o_ref[pl.ds(rows[k], 1), :] + ye_scr[s0 + k:s0 + k + 1, :] for k in range(SCATTER_UNROLL)]
            for k in range(SCATTER_UNROLL):
                o_ref[pl.ds(rows[k], 1), :] = vals[k]

    @pl.when(e == pl.num_programs(2) - 1)
    def _():
        o_ref[...] = h_ref[...] + g2_ref[...] * o_ref[...]


def moe_combine(idx, hdn, slot_blk0, slots, gates, w2, layer, h, g2_rows, g2_row0, row_blk0, rows_blk, dq, nb):
    e, _, f = hdn.shape
    t, d = h.shape
    in_specs = [
        pl.BlockSpec((None, slots, f), lambda b, q, k, idx: (k, slot_blk0 + b, 0)),
        pl.BlockSpec((None, None, f, dq), lambda b, q, k, idx: (layer, k, 0, q)),
        pl.BlockSpec((None, slots, 1), lambda b, q, k, idx: (k, slot_blk0 + b, 0)),
        pl.BlockSpec((rows_blk, dq), lambda b, q, k, idx: (row_blk0 + b, q)),
        pl.BlockSpec((None, 1, dq), lambda b, q, k, idx: (g2_row0 + b, 0, q)),
    ]
    args = [idx, hdn, w2, gates, h, g2_rows]
    aliases = {4: 0}
    return pl.pallas_call(
        functools.partial(_combine_kernel, nb, slots),
        grid_spec=pltpu.PrefetchScalarGridSpec(
            num_scalar_prefetch=1,
            grid=(nb, d // dq, e),
            in_specs=in_specs,
            out_specs=pl.BlockSpec((rows_blk, dq), lambda b, q, k, idx: (row_blk0 + b, q)),
            scratch_shapes=[pltpu.VMEM((slots, dq), f32)],
        ),
        out_shape=jax.ShapeDtypeStruct((t, d), f32),
        input_output_aliases=aliases,
        compiler_params=_params(),
        name="moe_combine",
    )(*args)


def na_bias_blocks(rpb):
    rows = SEQ // GRID_W
    n_dr, n_dc = 2 * NA_WIN_R - 1, 2 * NA_WIN_C - 1
    qc, kc = np.arange(GRID_W)[:, None], np.arange(GRID_W)[None, :]
    cs = np.clip(qc - NA_WIN_C // 2, 0, GRID_W - NA_WIN_C)
    col_ok = (kc >= cs) & (kc < cs + NA_WIN_C)
    pick = (np.arange(n_dc)[:, None, None] == (kc - qc + NA_WIN_C - 1)[None]) & col_ok[None]
    blocks = jnp.einsum('...rd,dqk->...rqk', rpb, jnp.asarray(pick, f32), precision=HI)
    blocks = jnp.where(col_ok, blocks, NEG)
    blocks = jnp.concatenate([blocks, jnp.full(blocks.shape[:-3] + (1, GRID_W, GRID_W), NEG, f32)], axis=-3)
    which = np.full((NA_NCFG, NA_GROUP_ROWS, NA_BAND_ROWS), n_dr, np.int32)
    for c, r0 in enumerate((0, 2, 4, 60, 62)):
        bs = int(np.clip(r0 - NA_WIN_R // 2, 0, rows - NA_BAND_ROWS))
        for qr in range(NA_GROUP_ROWS):
            r = r0 + qr
            rs = int(np.clip(r - NA_WIN_R // 2, 0, rows - NA_WIN_R))
            for kr in range(NA_BAND_ROWS):
                if rs <= bs + kr < rs + NA_WIN_R:
                    which[c, qr, kr] = bs + kr - r + NA_WIN_R - 1
    pairs = sorted({(int(which[c, qr, 2 * m]), int(which[c, qr, 2 * m + 1]))
                    for c in range(NA_NCFG) for qr in range(NA_GROUP_ROWS) for m in range(NA_BAND_ROWS // 2)})
    pair_of = np.array([[[pairs.index((int(which[c, qr, 2 * m]), int(which[c, qr, 2 * m + 1])))
                          for m in range(NA_BAND_ROWS // 2)] for qr in range(NA_GROUP_ROWS)]
                        for c in range(NA_NCFG)], np.int32)
    left = jnp.take(blocks, np.array([p[0] for p in pairs]), axis=-3)
    right = jnp.take(blocks, np.array([p[1] for p in pairs]), axis=-3)
    return jnp.concatenate([left, right], axis=-1), pair_of.reshape(-1)


def _dot_nt(a, b):
    return lax.dot_general(a, b, (((1,), (1,)), ((), ())), preferred_element_type=f32)


def _softmax_av(parts):
    m = parts[0][0].max(axis=-1, keepdims=True)
    for s, _ in parts[1:]:
        m = jnp.maximum(m, s.max(axis=-1, keepdims=True))
    l = None
    o = None
    for s, v in parts:
        p = jnp.exp(s - m)
        ls = p.sum(axis=-1, keepdims=True)
        os_ = jnp.dot(p.astype(bf16), v, preferred_element_type=f32)
        l = ls if l is None else l + ls
        o = os_ if o is None else o + os_
    return o / l


def _na_kernel(pair_ref, q_ref, k_ref, v_ref, kc_ref, vc_ref, tbl_ref, o_ref):
    j = pl.program_id(2)
    npairs = NA_BAND_ROWS // 2

    def bias(hh, cfg):
        rows = []
        for qr in range(NA_GROUP_ROWS):
            base = (cfg * NA_GROUP_ROWS + qr) * npairs
            rows.append(jnp.concatenate([tbl_ref[hh, pair_ref[base + m]] for m in range(npairs)], axis=1))
        return jnp.concatenate(rows, axis=0)

    scale = NA_HEADDIM ** -0.5
    assert scale == 2.0 ** round(np.log2(scale))
    lane = lax.broadcasted_iota(jnp.int32, (1, 2 * NA_HEADDIM), 1)
    first = lane < NA_HEADDIM
    kc = kc_ref[...]
    vc = vc_ref[...]
    gq = NA_GROUP_ROWS * GRID_W

    def head_q(q, hh):
        keep = first if hh == 0 else jnp.logical_not(first)
        return jnp.where(keep, q * scale, jnp.zeros_like(q))

    @pl.when(j < LAT_BLOCKS)
    def _():
        ngroups = TQ // gq
        q2 = []
        for g in range(ngroups):
            q = q_ref[pl.ds(g * gq, gq), :]
            q2.append(jnp.concatenate([head_q(q, 0), head_q(q, 1)], axis=0))
        sc_all = _dot_nt(jnp.concatenate(q2, axis=0), kc)
        o_win, p_ctx, denom = [], [], []
        for g in range(ngroups):
            r0 = j * (TQ // GRID_W) + NA_GROUP_ROWS * g
            bs = jnp.clip(r0 - NA_WIN_R // 2, 0, SEQ // GRID_W - NA_BAND_ROWS)
            cfg = jnp.where(r0 == 0, 0, jnp.where(r0 == 2, 1, jnp.where(r0 == 60, 3, jnp.where(r0 == 62, 4, 2))))
            start = pl.multiple_of(bs * GRID_W, GRID_W)
            kb = k_ref[pl.ds(start, NA_BAND_ROWS * GRID_W), :]
            vb = v_ref[pl.ds(start, NA_BAND_ROWS * GRID_W), :]
            s = _dot_nt(q2[g], kb) + jnp.concatenate([bias(0, cfg), bias(1, cfg)], axis=0)
            sc = sc_all[2 * gq * g:2 * gq * (g + 1)]
            m = jnp.maximum(s.max(axis=-1, keepdims=True), sc.max(axis=-1, keepdims=True))
            p, pc = jnp.exp(s - m), jnp.exp(sc - m)
            denom.append(p.sum(axis=-1, keepdims=True) + pc.sum(axis=-1, keepdims=True))
            o_win.append(jnp.dot(p.astype(bf16), vb, preferred_element_type=f32))
            p_ctx.append(pc.astype(bf16))
        o_ctx = jnp.dot(jnp.concatenate(p_ctx, axis=0), vc, preferred_element_type=f32)
        for g in range(ngroups):
            o2 = (o_win[g] + o_ctx[2 * gq * g:2 * gq * (g + 1)]) / denom[g]
            o_ref[pl.ds(g * gq, gq), :] = jnp.where(first, o2[:gq], o2[gq:]).astype(o_ref.dtype)

    @pl.when(j == LAT_BLOCKS)
    def _():
        q = q_ref[...]
        q2 = jnp.concatenate([head_q(q, 0), head_q(q, 1)], axis=0)
        o2 = _softmax_av([(_dot_nt(q2, kc), vc)])
        o_ref[...] = jnp.where(first, o2[:CTX_LEN], o2[CTX_LEN:]).astype(o_ref.dtype)


def na_attention(qkv, tbl, layer, pair_of, bsz):
    t = qkv.shape[0]
    npair = NA_HEADS // 2
    ctx0 = bsz * LAT_BLOCKS

    def qrow(b, j):
        return jnp.where(j < LAT_BLOCKS, b * LAT_BLOCKS + j, ctx0 + b)

    return pl.pallas_call(
        _na_kernel,
        grid_spec=pltpu.PrefetchScalarGridSpec(
            num_scalar_prefetch=1,
            grid=(bsz, npair, LAT_BLOCKS + 1),
            in_specs=[
                pl.BlockSpec((TQ, 128), lambda b, p, j, po: (qrow(b, j), p)),
                pl.BlockSpec((SEQ, 128), lambda b, p, j, po: (b, npair + p)),
                pl.BlockSpec((SEQ, 128), lambda b, p, j, po: (b, 2 * npair + p)),
                pl.BlockSpec((CTX_LEN, 128), lambda b, p, j, po: (ctx0 + b, npair + p)),
                pl.BlockSpec((CTX_LEN, 128), lambda b, p, j, po: (ctx0 + b, 2 * npair + p)),
                pl.BlockSpec((None, 2) + tbl.shape[2:], lambda b, p, j, po: (layer, p, 0, 0, 0)),
            ],
            out_specs=pl.BlockSpec((TQ, 128), lambda b, p, j, po: (qrow(b, j), p)),
        ),
        out_shape=jax.ShapeDtypeStruct((t, NA_WIDTH), bf16),
        compiler_params=_params(),
        name="na_attention",
    )(jnp.asarray(pair_of), qkv, qkv, qkv, qkv, qkv, tbl)


def _seq_block(d, bsz, lat_blocks):
    ctx0 = bsz * lat_blocks

    def jj(s):
        return (s - 1) if d == 0 else (lat_blocks - s)

    def blk(b, s):
        return jnp.where(s == 0, ctx0 + b, b * lat_blocks + jj(s))

    return jj, blk


def _halo_specs(width, col, blk, t):
    nb8 = TQ // HALO
    return [
        pl.BlockSpec((TQ, width), lambda b, s: (blk(b, s), col)),
        pl.BlockSpec((HALO, width), lambda b, s: (jnp.maximum(blk(b, s) * nb8 - 1, 0), col)),
        pl.BlockSpec((HALO, width), lambda b, s: (jnp.minimum(blk(b, s) * nb8 + nb8, t // HALO - 1), col)),
    ]


def _fill_halo(xe, parts, has_prev, has_next):
    c0 = 0
    for x_ref, xp_ref, xn_ref in parts:
        cols = slice(c0, c0 + x_ref.shape[1])
        xe[0:HALO, cols] = jnp.where(has_prev, xp_ref[...], 0.0)
        xe[HALO:HALO + TQ, cols] = x_ref[...]
        xe[HALO + TQ:2 * HALO + TQ, cols] = jnp.where(has_next, xn_ref[...], 0.0)
        c0 += x_ref.shape[1]


def _centred_conv(xe, cw_ref, cb_ref, cols):
    xv = xe[:, cols]
    n = xv.shape[0]
    out = cb_ref[:, cols]
    for k in range(4):
        shifted = xv if k == 2 else pltpu.roll(xv, (2 - k) % n, 0)
        out = out + cw_ref[k:k + 1, cols] * shifted[HALO:HALO + TQ, :]
    return out


def lru_gate_weights(w_r, w_i):
    def pair(w):
        w = w.reshape(2, LRU_BLOCKS // 2, 2, LRU_BLOCK, LRU_BLOCK)
        z = jnp.zeros_like(w[:, :, 0])
        top = jnp.concatenate([w[:, :, 0], z], axis=-1)
        bot = jnp.concatenate([z, w[:, :, 1]], axis=-1)
        return jnp.concatenate([top, bot], axis=-2)
    return jnp.concatenate([pair(w_r), pair(w_i)], axis=-1).astype(bf16)


def _gelu_tanh(x):
    return 0.5 * x * (1.0 + jnp.tanh(float(np.sqrt(2.0 / np.pi)) * (x + 0.044715 * (x * x * x))))


def _lru_kernel(d, lat_blocks, *refs):
    if d == 0:
        (x_ref, xp_ref, xn_ref, cw_ref, cb_ref, wg_ref, br_ref, bi_ref, lam_ref, o_ref,
         xe, a_scr, u_scr, h_scr) = refs
    else:
        (x_ref, xp_ref, xn_ref, ag_ref, hf_ref, cw_ref, cb_ref, wg_ref, br_ref, bi_ref, lam_ref, o_ref,
         xe, a_scr, u_scr, h_scr, hs_scr) = refs
    s = pl.program_id(1)
    j = (s - 1) if d == 0 else (lat_blocks - s)
    has_prev = jnp.logical_and(s > 0, j > 0)
    has_next = jnp.logical_and(s > 0, j < lat_blocks - 1)

    @pl.when(s == 0)
    def _():
        h_scr[...] = jnp.zeros_like(h_scr)

    _fill_halo(xe, [(x_ref, xp_ref, xn_ref)], has_prev, has_next)
    sp = jax.nn.softplus(-lam_ref[...])
    for p in range(LRU_WIDTH // 128):
        sl = slice(128 * p, 128 * (p + 1))
        xc = _centred_conv(xe, cw_ref, cb_ref, sl)
        g = jnp.dot(xc.astype(bf16), wg_ref[p], preferred_element_type=f32)
        r = jax.nn.sigmoid(g[:, :128] + br_ref[:, sl])
        i = jax.nn.sigmoid(g[:, 128:] + bi_ref[:, sl])
        log_a = -LRU_C * r * sp[:, sl]
        a = jnp.exp(log_a)
        a_scr[:, sl] = a
        u_scr[:, sl] = jnp.sqrt(-jnp.tanh(log_a) * (a * a + 1.0)) * (i * xc)

    out = o_ref if d == 0 else hs_scr
    h = h_scr[0:1, :]
    for t in (range(TQ) if d == 0 else range(TQ - 1, -1, -1)):
        h = a_scr[t:t + 1, :] * h + u_scr[t:t + 1, :]
        out[t:t + 1, :] = h
    h_scr[0:1, :] = h
    if d == 1:
        y = (hf_ref[...] + hs_scr[...]) * _gelu_tanh(ag_ref[...])
        o_ref[...] = y.astype(o_ref.dtype)


def lru_pass(d, proj, hf, conv_w, conv_b, wg, b_r, b_i, lam, bsz, lat_blocks):
    t = proj.shape[0]
    _, blk = _seq_block(d, bsz, lat_blocks)
    row = lambda c: (lambda b, s: (blk(b, s), c))
    in_specs = _halo_specs(LRU_WIDTH, COL_AX, blk, t)
    args = [proj, proj, proj]
    if d == 1:
        in_specs += [pl.BlockSpec((TQ, LRU_WIDTH), row(COL_AG)), pl.BlockSpec((TQ, LRU_WIDTH), row(0))]
        args += [proj, hf]
    const = lambda shape: pl.BlockSpec(shape, lambda b, s: (0,) * len(shape))
    in_specs += [const((4, LRU_WIDTH)), const((1, LRU_WIDTH)), const((LRU_WIDTH // 128, 128, 256)),
                 const((1, LRU_WIDTH)), const((1, LRU_WIDTH)), const((1, LRU_WIDTH))]
    args += [conv_w, conv_b.reshape(1, -1), wg[d], b_r[d].reshape(1, -1), b_i[d].reshape(1, -1), lam[d].reshape(1, -1)]
    scratch = [pltpu.VMEM((TQ + 2 * HALO, LRU_WIDTH), f32), pltpu.VMEM((TQ, LRU_WIDTH), f32),
               pltpu.VMEM((TQ, LRU_WIDTH), f32), pltpu.VMEM((8, LRU_WIDTH), f32)]
    if d == 1:
        scratch.append(pltpu.VMEM((TQ, LRU_WIDTH), f32))
    return pl.pallas_call(
        functools.partial(_lru_kernel, d, lat_blocks),
        grid=(bsz, lat_blocks + 1),
        in_specs=in_specs,
        out_specs=pl.BlockSpec((TQ, LRU_WIDTH), row(0)),
        out_shape=jax.ShapeDtypeStruct((t, LRU_WIDTH), f32 if d == 0 else bf16),
        scratch_shapes=scratch,
        compiler_params=_params(),
        name=f"lru_pass{d}",
    )(*args)


def lru_branch(proj, conv_w, conv_b, w_r, b_r, w_i, b_i, lam, bsz, lat_blocks):
    wg = lru_gate_weights(w_r, w_i)
    hf = lru_pass(0, proj, None, conv_w, conv_b, wg, b_r, b_i, lam, bsz, lat_blocks)
    return lru_pass(1, proj, hf, conv_w, conv_b, wg, b_r, b_i, lam, bsz, lat_blocks)


def rope_tables(seq):
    pos = jnp.arange(seq)
    row = (pos // GRID_W).astype(f32)
    col = (pos % GRID_W).astype(f32)
    n_freq = SSD_STATE // 4
    freqs = ROPE_BASE ** (-jnp.arange(n_freq, dtype=f32) / n_freq)
    ang = jnp.concatenate([row[:, None] * freqs, col[:, None] * freqs], axis=-1)
    cos, sin = jnp.cos(ang), jnp.sin(ang)
    cosf = jnp.concatenate([cos, cos], axis=-1)
    sinf = jnp.concatenate([-sin, sin], axis=-1)
    cosf = jnp.concatenate([cosf, jnp.ones((TQ, SSD_STATE), f32)], axis=0)
    sinf = jnp.concatenate([sinf, jnp.zeros((TQ, SSD_STATE), f32)], axis=0)
    return cosf, sinf


def head_expand_matrix(d):
    e = np.zeros((128, SSD_INNER), np.float32)
    for h in range(SSD_HEADS):
        e[SSD_HEADS * d + h, h * SSD_HEADDIM:(h + 1) * SSD_HEADDIM] = 1.0
    return jnp.asarray(e, bf16)


def _split3(a):
    hi = a.astype(bf16)
    r = a - hi.astype(f32)
    mid = r.astype(bf16)
    return hi, mid, (r - mid.astype(f32)).astype(bf16)


def _dot_exact_rhs01(a, m01):
    return sum(jnp.dot(p, m01, preferred_element_type=f32) for p in _split3(a))


def _dot_exact_lhs01(m01, a):
    return sum(jnp.dot(m01, p, preferred_element_type=f32) for p in _split3(a))


def _ssd_kernel(d, lat_blocks, *refs):
    xs_refs, bc_refs, refs = refs[0:3], refs[3:6], refs[6:]
    if d == 0:
        (dt_ref, cos_ref, sin_ref, cw_ref, cb_ref, dtb_ref, alog_ref, e_ref,
         o_ref, xe, h_scr) = refs
    else:
        (dt_ref, cos_ref, sin_ref, z_ref, y0_ref, cw_ref, cb_ref, dtb_ref, alog_ref, e_ref,
         dsk_ref, ng_ref, o_ref, xe, h_scr, y_scr) = refs
    s = pl.program_id(1)
    j = (s - 1) if d == 0 else (lat_blocks - s)
    has_prev = jnp.logical_and(s > 0, j > 0)
    has_next = jnp.logical_and(s > 0, j < lat_blocks - 1)
    q = SSD_CHUNK

    @pl.when(s == 0)
    def _():
        h_scr[...] = jnp.zeros_like(h_scr)

    _fill_halo(xe, [xs_refs, bc_refs], has_prev, has_next)
    xbc = _centred_conv(xe, cw_ref, cb_ref, slice(None))
    xbc = xbc * jax.nn.sigmoid(xbc)
    xs = xbc[:, :SSD_INNER]
    cosf, sinf = cos_ref[...], sin_ref[...]

    def rope(g, off):
        v = xbc[:, off + g * SSD_STATE: off + (g + 1) * SSD_STATE]
        return (v * cosf + pltpu.roll(v, SSD_STATE // 2, 1) * sinf).astype(bf16)

    bm = [rope(g, SSD_INNER) for g in range(SSD_GROUPS)]
    cm = [rope(g, SSD_INNER + SSD_GROUPS * SSD_STATE) for g in range(SSD_GROUPS)]
    dt = jax.nn.softplus(dt_ref[...] + dtb_ref[...])
    delta = dt * (-jnp.exp(alog_ref[...]))
    ri = lax.broadcasted_iota(jnp.int32, (q, q), 0)
    ci = lax.broadcasted_iota(jnp.int32, (q, q), 1)
    keep = (ci <= ri) if d == 0 else (ci >= ri)
    tri = jnp.where(keep, 1.0, 0.0).astype(bf16)
    lane = lax.broadcasted_iota(jnp.int32, (1, 2 * SSD_HEADDIM), 1)
    halves = (lane < SSD_HEADDIM, lane >= SSD_HEADDIM)
    e = e_ref[...]
    last = q - 1 if d == 0 else 0
    out = o_ref if d == 0 else y_scr

    for c in (range(TQ // q) if d == 0 else range(TQ // q - 1, -1, -1)):
        rows = slice(c * q, (c + 1) * q)
        at = _dot_exact_lhs01(tri, delta[rows])
        at_exp = _dot_exact_rhs01(at, e)
        dt_exp = _dot_exact_rhs01(dt[rows], e)
        tot_exp = at_exp[last:last + 1, :]
        xdt = xs[rows] * dt_exp
        xd = (xdt * jnp.exp(tot_exp - at_exp)).astype(bf16)
        eat = jnp.exp(at_exp)
        cdec = jnp.exp(tot_exp)
        at_row = at.T
        ys = []
        for g in range(SSD_GROUPS):
            bg, cg = bm[g][rows], cm[g][rows]
            cb = _dot_nt(cg, bg)
            ht = h_scr[g]
            yoff = jnp.dot(cg, ht.astype(bf16), preferred_element_type=f32) * eat[:, g * GW:(g + 1) * GW]
            for pp in range(2):
                xpair = xdt[:, g * GW + pp * 128: g * GW + (pp + 1) * 128]
                acc = yoff[:, pp * 128:(pp + 1) * 128]
                for hh in range(2):
                    li = SSD_HEADS * d + 4 * g + 2 * pp + hh
                    seg = at[:, li:li + 1] - at_row[li:li + 1, :]
                    m = (cb * jnp.exp(jnp.where(keep, seg, NEG))).astype(bf16)
                    xm = jnp.where(halves[hh], xpair, 0.0).astype(bf16)
                    acc = acc + jnp.dot(m, xm, preferred_element_type=f32)
                ys.append(acc)
            upd = lax.dot_general(bg, xd[:, g * GW:(g + 1) * GW], (((0,), (0,)), ((), ())),
                                  preferred_element_type=f32)
            h_scr[g] = cdec[:, g * GW:(g + 1) * GW] * ht + upd
        out[rows, :] = jnp.concatenate(ys, axis=-1)

    if d == 1:
        y = y0_ref[...] + y_scr[...] + dsk_ref[...] * xs
        z = z_ref[...]
        y = y * (z * jax.nn.sigmoid(z))
        y = y * lax.rsqrt(jnp.mean(y * y, axis=-1, keepdims=True) + EPS)
        o_ref[...] = (y * ng_ref[...]).astype(o_ref.dtype)


def ssd_pass(d, proj, y0, cosf, sinf, conv_w, conv_b, a_log, dt_bias, d_skip, norm_g, bsz, lat_blocks):
    t = proj.shape[0]
    jj, blk = _seq_block(d, bsz, lat_blocks)
    row = lambda c: (lambda b, s: (blk(b, s), c))
    tbl = lambda b, s: (jnp.where(s == 0, lat_blocks, jj(s)), 0)
    const = lambda shape: pl.BlockSpec(shape, lambda b, s: (0,) * len(shape))
    pad128 = lambda v: jnp.pad(v.reshape(1, -1).astype(f32), ((0, 0), (0, 128 - v.size)))
    in_specs = _halo_specs(SSD_INNER, COL_XS, blk, t) + _halo_specs(SSD_CONV_DIM - SSD_INNER, COL_BC, blk, t) + [
        pl.BlockSpec((TQ, 128), row(COL_DT)),
        pl.BlockSpec((TQ, SSD_STATE), tbl),
        pl.BlockSpec((TQ, SSD_STATE), tbl),
    ]
    args = [proj] * 7 + [cosf, sinf]
    if d == 1:
        in_specs += [pl.BlockSpec((TQ, SSD_INNER), row(COL_Z)), pl.BlockSpec((TQ, SSD_INNER), row(0))]
        args += [proj, y0]
    in_specs += [const((4, SSD_CONV_DIM)), const((1, SSD_CONV_DIM)), const((1, 128)), const((1, 128)),
                 const((128, SSD_INNER))]
    args += [conv_w, conv_b.reshape(1, -1), pad128(dt_bias), pad128(a_log), head_expand_matrix(d)]
    scratch = [pltpu.VMEM((TQ + 2 * HALO, SSD_CONV_DIM), f32), pltpu.VMEM((SSD_GROUPS, SSD_STATE, GW), f32)]
    if d == 1:
        in_specs += [const((1, SSD_INNER)), const((1, SSD_INNER))]
        args += [jnp.repeat(d_skip, SSD_HEADDIM).reshape(1, -1), norm_g.reshape(1, -1)]
        scratch.append(pltpu.VMEM((TQ, SSD_INNER), f32))
    return pl.pallas_call(
        functools.partial(_ssd_kernel, d, lat_blocks),
        grid=(bsz, lat_blocks + 1),
        in_specs=in_specs,
        out_specs=pl.BlockSpec((TQ, SSD_INNER), row(0)),
        out_shape=jax.ShapeDtypeStruct((t, SSD_INNER), f32 if d == 0 else bf16),
        scratch_shapes=scratch,
        compiler_params=_params(),
        name=f"ssd_pass{d}",
    )(*args)


def ssd_branch(proj, cosf, sinf, conv_w, conv_b, a_log, dt_bias, d_skip, norm_g, bsz, lat_blocks):
    y0 = ssd_pass(0, proj, None, cosf, sinf, conv_w, conv_b, a_log, dt_bias, d_skip, norm_g, bsz, lat_blocks)
    return ssd_pass(1, proj, y0, cosf, sinf, conv_w, conv_b, a_log, dt_bias, d_skip, norm_g, bsz, lat_blocks)


def _tile_rows(mod_rows, bsz):
    idx = np.concatenate([np.repeat(np.arange(bsz), SEQ // TM), np.full(bsz * CTX_LEN // TM, bsz)])
    return mod_rows[idx][:, None, :]


def _expert_choice_moe(h, v, aff, w1, w3, w2, layer, g2_rows, bsz):
    t, d = v.shape
    aff = aff[:, :N_EXPERTS]
    nl = bsz * SEQ

    def choose(a, length):
        cap = CAPACITY_FACTOR * length // N_EXPERTS
        g, idx = lax.top_k(jnp.swapaxes(a.reshape(bsz, length, N_EXPERTS), 1, 2), cap)
        return jnp.swapaxes(g, 0, 1), jnp.swapaxes(idx, 0, 1), cap

    g_l, i_l, cap_l = choose(aff[:nl], SEQ)
    g_c, i_c, cap_c = choose(aff[nl:], CTX_LEN)
    boff = jnp.arange(bsz)[None, :, None]
    ctx_rows = (i_c + boff * CTX_LEN).reshape(N_EXPERTS, -1)
    rows = jnp.concatenate([(i_l + boff * SEQ).reshape(N_EXPERTS, -1), ctx_rows + nl], axis=1)
    gates = jnp.concatenate([g_l.reshape(N_EXPERTS, -1), g_c.reshape(N_EXPERTS, -1)], axis=1)[..., None]
    r = rows.shape[1]
    xg = v.at[rows.reshape(-1)].get(mode="promise_in_bounds").reshape(N_EXPERTS, r, d)
    hdn = expert_hidden(xg, w1, w3, layer, r // 4)
    ctx_slots, ctx_tile = bsz * cap_c, bsz * CTX_LEN
    assert (bsz * cap_l) % ctx_slots == 0 and nl % ctx_tile == 0
    h = moe_combine(i_l.reshape(-1), hdn, 0, cap_l, gates, w2, layer, h, g2_rows, 0, 0, SEQ, TN, bsz)
    return moe_combine(ctx_rows.reshape(-1), hdn, bsz * cap_l // ctx_slots, ctx_slots, gates, w2, layer, h, g2_rows,
                       bsz, nl // ctx_tile, ctx_tile, d, 1)


def kernel(x, c, ctx, c_ctx, w_ada, b_ada, norm_mix, norm_ffn, w_in, lru_conv_w, lru_conv_b, lru_w_r, lru_b_r,
           lru_w_i, lru_b_i, lru_lambda, ssd_conv_w, ssd_conv_b, ssd_a_log, ssd_dt_bias, ssd_d, ssd_norm, na_rpb,
           w_branch_lru, w_branch_ssd, w_branch_na, w_out, w_router, w1, w3, w2, norm_final):
    bsz = x.shape[0]
    assert x.shape[1:] == (SEQ, D_MODEL) and ctx.shape[1:] == (CTX_LEN, D_MODEL) and bsz * CTX_LEN == TM
    d = D_MODEL
    h = (x.reshape(bsz * SEQ, d), ctx.reshape(bsz * CTX_LEN, d))
    t = bsz * (SEQ + CTX_LEN)

    cond = jnp.concatenate([c, c_ctx[None, :], jnp.zeros((8 - bsz - 1, d), f32)], axis=0)
    mod = ada_modulation(cond, w_ada, b_ada)
    cosf, sinf = rope_tables(SEQ)

    assert DT_END + DT_PAD == W_QKV0 and w_in.shape[-1] == DT_END + 3 * NA_WIDTH + 3 * d
    w_proj = relayout_proj_weights(jnp.swapaxes(w_in, 1, 2))
    qkv_blocks = 3 * NA_WIDTH // TN
    pa, pb, pc, wo = (w.astype(bf16) for w in (w_branch_lru, w_branch_ssd, w_branch_na, w_out))
    na_tbl, na_pair_of = na_bias_blocks(na_rpb)

    for l in range(DEPTH):
        mods = jnp.split(mod[l], 6, axis=-1)
        sh1, sc1, g1, sh2, sc2 = (_tile_rows(m, bsz) for m in mods[:5])
        u = norm_modulate(h, norm_mix[l], sc1, sh1, bf16)
        proj = matmul_bf16(u, w_proj, l, N_F32, lambda j: jnp.where(j < W_QKV0 // TN, j, j + qkv_blocks), f32)
        qkv = matmul_bf16(u, w_proj, l, 3 * NA_WIDTH, lambda j: j + W_QKV0 // TN, bf16)
        ya = lru_branch(proj, lru_conv_w[l], lru_conv_b[l], lru_w_r[l], lru_b_r[l], lru_w_i[l], lru_b_i[l],
                        lru_lambda[l], bsz, LAT_BLOCKS)
        yb = ssd_branch(proj, cosf, sinf, ssd_conv_w[l], ssd_conv_b[l], ssd_a_log[l], ssd_dt_bias[l], ssd_d[l],
                        ssd_norm[l], bsz, LAT_BLOCKS)
        yc = na_attention(qkv, na_tbl, l, na_pair_of, bsz)
        y = branch_merge(ya, yb, yc, proj, pa, pb, pc, l, COL_G)
        h, v, aff = residual_norm_router(y, wo, l, h, g1, norm_ffn[l], sc2, sh2, w_router[l])
        h = _expert_choice_moe(h, v, aff, w1, w3, w2, l, mods[5][:, None, :], bsz)

    zeros = jnp.zeros((t // TM, 1, d), f32)
    out = norm_modulate(h, norm_final, zeros, zeros, f32, rows=bsz * SEQ)
    return out.reshape(bsz, SEQ, d)
```

```python
import functools

import jax
import jax.numpy as jnp
import numpy as np
from jax import lax
from jax.experimental import pallas as pl
from jax.experimental.pallas import tpu as pltpu

D_MODEL = 2048
SEQ = 4096
CTX_LEN = 256
DEPTH = 4
GRID_W = 64
EPS = 1e-6
ROPE_BASE = 10000.0
LRU_WIDTH = 1024
LRU_BLOCKS = 16
LRU_BLOCK = LRU_WIDTH // LRU_BLOCKS
LRU_C = 8.0
SSD_INNER = 1024
SSD_HEADDIM = 64
SSD_HEADS = SSD_INNER // SSD_HEADDIM
SSD_GROUPS = 4
SSD_STATE = 128
SSD_CHUNK = 128
SSD_CONV_DIM = SSD_INNER + 2 * SSD_GROUPS * SSD_STATE
NA_HEADS = 16
NA_HEADDIM = 64
NA_WIDTH = NA_HEADS * NA_HEADDIM
NA_WIN_R = 8
NA_WIN_C = 16
N_EXPERTS = 16
EXPERT_FF = 1024
CAPACITY_FACTOR = 2

V7X_VMEM_LIMIT = 56 * 1024 * 1024
TM = 1024
MM_MAX_ROWS = 2304
TQ = 256
HALO = 8
LAT_BLOCKS = SEQ // TQ
NEG = -1e30
HI = lax.Precision.HIGHEST
bf16 = jnp.bfloat16
f32 = jnp.float32

TN = 512
DT_END = 2 * LRU_WIDTH + SSD_INNER + SSD_CONV_DIM + 2 * SSD_HEADS
DT_PAD = 480
COL_AX, COL_AG, COL_Z, COL_XS, COL_BC = 0, 1, 2, 3, 4
COL_DT = 40
W_QKV0 = 5632
COL_G = 5632
N_F32 = COL_G + 3 * D_MODEL
GW = SSD_INNER // SSD_GROUPS

NA_GROUP_ROWS = 2
NA_BAND_ROWS = 10
NA_NCFG = 5


def _params():
    return pltpu.CompilerParams(vmem_limit_bytes=V7X_VMEM_LIMIT)


def _ada_kernel(x_ref, w_ref, b_ref, o_ref):
    c = x_ref[...]
    x = (c * jax.nn.sigmoid(c)).astype(bf16)
    o_ref[...] = jnp.dot(x, w_ref[...].astype(bf16), preferred_element_type=f32) + b_ref[...]


def ada_modulation(cond, w_ada, b_ada):
    depth, d, n = w_ada.shape
    tn = 1024
    return pl.pallas_call(
        _ada_kernel,
        grid=(depth, n // tn),
        in_specs=[pl.BlockSpec((8, d), lambda l, j: (0, 0)),
                  pl.BlockSpec((None, d, tn), lambda l, j: (l, 0, j)),
                  pl.BlockSpec((None, 1, tn), lambda l, j: (l, 0, j))],
        out_specs=pl.BlockSpec((None, 8, tn), lambda l, j: (l, 0, j)),
        out_shape=jax.ShapeDtypeStruct((depth, 8, n), f32),
        compiler_params=_params(),
        name="ada_modulation",
    )(cond, w_ada, b_ada.reshape(depth, 1, n))


def _stream_rows(h, tm):
    if not isinstance(h, tuple):
        return [pl.BlockSpec((tm, h.shape[1]), lambda i: (i, 0))], [h], (lambda refs, i: refs[0][...]), h.shape[0]
    lat, ctx = h
    d = lat.shape[1]
    n_lat = lat.shape[0] // tm
    assert lat.shape[0] % tm == 0 and ctx.shape[0] % tm == 0
    specs = [pl.BlockSpec((tm, d), lambda i: (jnp.minimum(i, n_lat - 1), 0)),
             pl.BlockSpec((tm, d), lambda i: (jnp.maximum(i - n_lat, 0), 0))]
    return specs, [lat, ctx], (lambda refs, i: jnp.where(i < n_lat, refs[0][...], refs[1][...])), lat.shape[0] + ctx.shape[0]


def _norm_mod_kernel(read_h, nh, *refs):
    g_ref, sc_ref, sh_ref, o_ref = refs[nh:]
    x = read_h(refs[:nh], pl.program_id(0))
    y = x * lax.rsqrt(jnp.mean(x * x, axis=-1, keepdims=True) + EPS)
    o_ref[...] = ((y * g_ref[...]) * (1.0 + sc_ref[...]) + sh_ref[...]).astype(o_ref.dtype)


def norm_modulate(h, g, sc_t, sh_t, out_dtype, rows=None):
    tm = min(512, TM)
    per = TM // tm
    h_specs, h_args, read_h, t = _stream_rows(h, tm)
    d = h_args[0].shape[1]
    t = t if rows is None else rows
    return pl.pallas_call(
        functools.partial(_norm_mod_kernel, read_h, len(h_args)),
        grid=(t // tm,),
        in_specs=h_specs + [pl.BlockSpec((1, d), lambda i: (0, 0)),
                            pl.BlockSpec((None, 1, d), lambda i: (i // per, 0, 0)),
                            pl.BlockSpec((None, 1, d), lambda i: (i // per, 0, 0))],
        out_specs=pl.BlockSpec((tm, d), lambda i: (i, 0)),
        out_shape=jax.ShapeDtypeStruct((t, d), out_dtype),
        compiler_params=_params(),
        name="norm_modulate",
    )(*h_args, g.reshape(1, d), sc_t, sh_t)


def _relayout_kernel(shift, a_ref, b_ref, o_ref):
    j = pl.program_id(1)
    split = DT_END // TN

    @pl.when(j < split)
    def _():
        o_ref[...] = a_ref[...].astype(o_ref.dtype)

    @pl.when(j == split)
    def _():
        row = lax.broadcasted_iota(jnp.int32, a_ref.shape, 0)
        o_ref[...] = jnp.where(row < shift, a_ref[...], 0.0).astype(o_ref.dtype)

    @pl.when(j > split)
    def _():
        o_ref[0:TN - shift, :] = a_ref[shift:TN, :].astype(o_ref.dtype)
        o_ref[TN - shift:TN, :] = b_ref[...].astype(o_ref.dtype)


def relayout_proj_weights(w_t):
    depth, n, d = w_t.shape
    shift = DT_END % TN
    assert shift + DT_PAD == TN and shift % 16 == 0 and n % shift == 0
    split = DT_END // TN
    return pl.pallas_call(
        functools.partial(_relayout_kernel, shift),
        grid=(depth, (n + DT_PAD) // TN),
        in_specs=[pl.BlockSpec((None, TN, d), lambda l, j: (l, jnp.where(j <= split, j, j - 1), 0)),
                  pl.BlockSpec((None, shift, d), lambda l, j: (l, jnp.where(j <= split, 0, (TN // shift) * j), 0))],
        out_specs=pl.BlockSpec((None, TN, d), lambda l, j: (l, j, 0)),
        out_shape=jax.ShapeDtypeStruct((depth, n + DT_PAD, d), bf16),
        compiler_params=_params(),
        name="relayout_proj_weights",
    )(w_t, w_t)


def _mm_kernel(x_ref, w_ref, o_ref):
    o_ref[...] = _dot_nt(x_ref[...], w_ref[...]).astype(o_ref.dtype)


def matmul_bf16(x, w, layer, n_out, w_block, out_dtype):
    m, k = x.shape
    tm = max(c for c in range(16, MM_MAX_ROWS + 1, 16) if m % c == 0)
    return pl.pallas_call(
        _mm_kernel,
        grid=(m // tm, n_out // TN),
        in_specs=[pl.BlockSpec((tm, k), lambda i, j: (i, 0)),
                  pl.BlockSpec((None, TN, k), lambda i, j: (layer, w_block(j), 0))],
        out_specs=pl.BlockSpec((tm, TN), lambda i, j: (i, j)),
        out_shape=jax.ShapeDtypeStruct((m, n_out), out_dtype),
        compiler_params=_params(),
        name="matmul_bf16",
    )(x, w)


def _merge_kernel(ya_ref, yb_ref, yc_ref, ga_ref, gb_ref, gc_ref, pa_ref, pb_ref, pc_ref, o_ref):
    acc = jax.nn.sigmoid(ga_ref[...]) * jnp.dot(ya_ref[...], pa_ref[...], preferred_element_type=f32)
    acc = acc + jax.nn.sigmoid(gb_ref[...]) * jnp.dot(yb_ref[...], pb_ref[...], preferred_element_type=f32)
    acc = acc + jax.nn.sigmoid(gc_ref[...]) * jnp.dot(yc_ref[...], pc_ref[...], preferred_element_type=f32)
    o_ref[...] = acc.astype(o_ref.dtype)


def branch_merge(ya, yb, yc, proj, pa, pb, pc, layer, g_col0):
    t, k = ya.shape
    n = pa.shape[-1]
    tn = 512
    gb0 = g_col0 // tn
    nj = n // tn
    xs = pl.BlockSpec((TM, k), lambda i, j: (i, 0))
    ws = pl.BlockSpec((None, k, tn), lambda i, j: (layer, 0, j))
    gs = lambda q: pl.BlockSpec((TM, tn), lambda i, j: (i, gb0 + q * nj + j))
    return pl.pallas_call(
        _merge_kernel,
        grid=(t // TM, nj),
        in_specs=[xs, xs, xs, gs(0), gs(1), gs(2), ws, ws, ws],
        out_specs=pl.BlockSpec((TM, tn), lambda i, j: (i, j)),
        out_shape=jax.ShapeDtypeStruct((t, n), bf16),
        compiler_params=_params(),
        name="branch_merge",
    )(ya, yb, yc, proj, proj, proj, pa, pb, pc)


def _resid_norm_router_kernel(read_h, nh, *refs):
    y_ref, w_ref, g1_ref, ng_ref, sc_ref, sh_ref, wr_ref, ho_ref, v_ref, aff_ref = refs[nh:]
    h = read_h(refs[:nh], pl.program_id(0))
    hn = h + g1_ref[...] * jnp.dot(y_ref[...], w_ref[...], preferred_element_type=f32)
    ho_ref[...] = hn
    yn = hn * lax.rsqrt(jnp.mean(hn * hn, axis=-1, keepdims=True) + EPS)
    v = ((yn * ng_ref[...]) * (1.0 + sc_ref[...]) + sh_ref[...]).astype(bf16)
    v_ref[...] = v
    s = jnp.dot(v, wr_ref[...], preferred_element_type=f32)
    lane = lax.broadcasted_iota(jnp.int32, s.shape, 1)
    s = jnp.where(lane < N_EXPERTS, s, NEG)
    e = jnp.exp(s - s.max(axis=-1, keepdims=True))
    aff_ref[...] = e / e.sum(axis=-1, keepdims=True)


def residual_norm_router(y, w, layer, h, g1_t, norm_g, sc_t, sh_t, w_router):
    tm = TM // 2
    h_specs, h_args, read_h, t = _stream_rows(h, tm)
    d = h_args[0].shape[1]
    wr = jnp.pad(w_router, ((0, 0), (0, 128 - N_EXPERTS))).astype(bf16)
    rows = pl.BlockSpec((tm, d), lambda i: (i, 0))
    per_tile = pl.BlockSpec((None, 1, d), lambda i: (i * tm // TM, 0, 0))
    return pl.pallas_call(
        functools.partial(_resid_norm_router_kernel, read_h, len(h_args)),
        grid=(t // tm,),
        in_specs=h_specs + [rows, pl.BlockSpec((None, d, d), lambda i: (layer, 0, 0)), per_tile,
                            pl.BlockSpec((1, d), lambda i: (0, 0)), per_tile, per_tile,
                            pl.BlockSpec((d, 128), lambda i: (0, 0))],
        out_specs=[rows, rows, pl.BlockSpec((tm, 128), lambda i: (i, 0))],
        out_shape=[jax.ShapeDtypeStruct((t, d), f32), jax.ShapeDtypeStruct((t, d), bf16),
                   jax.ShapeDtypeStruct((t, 128), f32)],
        compiler_params=_params(),
        name="residual_norm_router",
    )(*h_args, y, w, g1_t, norm_g.reshape(1, d), sc_t, sh_t, wr)


def _expert_hidden_kernel(x_ref, w1_ref, w3_ref, o_ref, w1b, w3b):
    @pl.when(pl.program_id(1) == 0)
    def _():
        w1b[...] = w1_ref[...].astype(bf16)
        w3b[...] = w3_ref[...].astype(bf16)

    x = x_ref[...]
    a = jnp.dot(x, w1b[...], preferred_element_type=f32)
    b = jnp.dot(x, w3b[...], preferred_element_type=f32)
    o_ref[...] = ((a * jax.nn.sigmoid(a)) * b).astype(o_ref.dtype)


def expert_hidden(xg, w1, w3, layer, tm):
    e, r, d = xg.shape
    f = w1.shape[-1]
    return pl.pallas_call(
        _expert_hidden_kernel,
        grid=(e, r // tm),
        in_specs=[pl.BlockSpec((None, tm, d), lambda k, i: (k, i, 0)),
                  pl.BlockSpec((None, None, d, f), lambda k, i: (layer, k, 0, 0)),
                  pl.BlockSpec((None, None, d, f), lambda k, i: (layer, k, 0, 0))],
        out_specs=pl.BlockSpec((None, tm, f), lambda k, i: (k, i, 0)),
        out_shape=jax.ShapeDtypeStruct((e, r, f), bf16),
        scratch_shapes=[pltpu.VMEM((d, f), bf16), pltpu.VMEM((d, f), bf16)],
        compiler_params=_params(),
        name="expert_hidden",
    )(xg, w1, w3)


SCATTER_UNROLL = 8


def _combine_kernel(nb, slots, idx_ref, hdn_ref, w2_ref, g_ref, h_ref, g2_ref, *rest):
    o_ref, ye_scr = rest[-2], rest[-1]
    b, e = pl.program_id(0), pl.program_id(2)

    @pl.when(e == 0)
    def _():
        o_ref[...] = jnp.zeros_like(o_ref)

    ye_scr[...] = jnp.dot(hdn_ref[...], w2_ref[...].astype(bf16), preferred_element_type=f32) * g_ref[...]
    base0 = (e * nb + b) * slots

    for s0 in range(0, slots, SCATTER_UNROLL):
        rows = [idx_ref[base0 + s0 + k] for k in range(SCATTER_UNROLL)]
        vals = [o_ref[pl.ds(rows[k], 1), :] + ye_scr[s0 + k:s0 + k + 1, :] for k in range(SCATTER_UNROLL)]
        for k in range(SCATTER_UNROLL):
            o_ref[pl.ds(rows[k], 1), :] = vals[k]

    @pl.when(e == pl.num_programs(2) - 1)
    def _():
        o_ref[...] = h_ref[...] + g2_ref[...] * o_ref[...]


def moe_combine(idx, hdn, slot_blk0, slots, gates, w2, layer, h, g2_rows, g2_row0, row_blk0, rows_blk, dq, nb):
    e, _, f = hdn.shape
    t, d = h.shape
    in_specs = [
        pl.BlockSpec((None, slots, f), lambda b, q, k, idx: (k, slot_blk0 + b, 0)),
        pl.BlockSpec((None, None, f, dq), lambda b, q, k, idx: (layer, k, 0, q)),
        pl.BlockSpec((None, slots, 1), lambda b, q, k, idx: (k, slot_blk0 + b, 0)),
        pl.BlockSpec((rows_blk, dq), lambda b, q, k, idx: (row_blk0 + b, q)),
        pl.BlockSpec((None, 1, dq), lambda b, q, k, idx: (g2_row0 + b, 0, q)),
    ]
    args = [idx, hdn, w2, gates, h, g2_rows]
    aliases = {4: 0}
    return pl.pallas_call(
        functools.partial(_combine_kernel, nb, slots),
        grid_spec=pltpu.PrefetchScalarGridSpec(
            num_scalar_prefetch=1,
            grid=(nb, d // dq, e),
            in_specs=in_specs,
            out_specs=pl.BlockSpec((rows_blk, dq), lambda b, q, k, idx: (row_blk0 + b, q)),
            scratch_shapes=[pltpu.VMEM((slots, dq), f32)],
        ),
        out_shape=jax.ShapeDtypeStruct((t, d), f32),
        input_output_aliases=aliases,
        compiler_params=_params(),
        name="moe_combine",
    )(*args)


def na_bias_blocks(rpb):
    rows = SEQ // GRID_W
    n_dr, n_dc = 2 * NA_WIN_R - 1, 2 * NA_WIN_C - 1
    qc, kc = np.arange(GRID_W)[:, None], np.arange(GRID_W)[None, :]
    cs = np.clip(qc - NA_WIN_C // 2, 0, GRID_W - NA_WIN_C)
    col_ok = (kc >= cs) & (kc < cs + NA_WIN_C)
    pick = (np.arange(n_dc)[:, None, None] == (kc - qc + NA_WIN_C - 1)[None]) & col_ok[None]
    blocks = jnp.einsum('...rd,dqk->...rqk', rpb, jnp.asarray(pick, f32), precision=HI)
    blocks = jnp.where(col_ok, blocks, NEG)
    blocks = jnp.concatenate([blocks, jnp.full(blocks.shape[:-3] + (1, GRID_W, GRID_W), NEG, f32)], axis=-3)
    which = np.full((NA_NCFG, NA_GROUP_ROWS, NA_BAND_ROWS), n_dr, np.int32)
    for c, r0 in enumerate((0, 2, 4, 60, 62)):
        bs = int(np.clip(r0 - NA_WIN_R // 2, 0, rows - NA_BAND_ROWS))
        for qr in range(NA_GROUP_ROWS):
            r = r0 + qr
            rs = int(np.clip(r - NA_WIN_R // 2, 0, rows - NA_WIN_R))
            for kr in range(NA_BAND_ROWS):
                if rs <= bs + kr < rs + NA_WIN_R:
                    which[c, qr, kr] = bs + kr - r + NA_WIN_R - 1
    pairs = sorted({(int(which[c, qr, 2 * m]), int(which[c, qr, 2 * m + 1]))
                    for c in range(NA_NCFG) for qr in range(NA_GROUP_ROWS) for m in range(NA_BAND_ROWS // 2)})
    pair_of = np.array([[[pairs.index((int(which[c, qr, 2 * m]), int(which[c, qr, 2 * m + 1])))
                          for m in range(NA_BAND_ROWS // 2)] for qr in range(NA_GROUP_ROWS)]
                        for c in range(NA_NCFG)], np.int32)
    left = jnp.take(blocks, np.array([p[0] for p in pairs]), axis=-3)
    right = jnp.take(blocks, np.array([p[1] for p in pairs]), axis=-3)
    return jnp.concatenate([left, right], axis=-1), pair_of.reshape(-1)


def _dot_nt(a, b):
    return lax.dot_general(a, b, (((1,), (1,)), ((), ())), preferred_element_type=f32)


def _softmax_av(parts):
    m = parts[0][0].max(axis=-1, keepdims=True)
    for s, _ in parts[1:]:
        m = jnp.maximum(m, s.max(axis=-1, keepdims=True))
    l = None
    o = None
    for s, v in parts:
        p = jnp.exp(s - m)
        ls = p.sum(axis=-1, keepdims=True)
        os_ = jnp.dot(p.astype(bf16), v, preferred_element_type=f32)
        l = ls if l is None else l + ls
        o = os_ if o is None else o + os_
    return o / l


def _na_kernel(pair_ref, q_ref, k_ref, v_ref, kc_ref, vc_ref, tbl_ref, o_ref):
    j = pl.program_id(2)
    npairs = NA_BAND_ROWS // 2

    def bias(hh, cfg):
        rows = []
        for qr in range(NA_GROUP_ROWS):
            base = (cfg * NA_GROUP_ROWS + qr) * npairs
            rows.append(jnp.concatenate([tbl_ref[hh, pair_ref[base + m]] for m in range(npairs)], axis=1))
        return jnp.concatenate(rows, axis=0)

    scale = NA_HEADDIM ** -0.5
    assert scale == 2.0 ** round(np.log2(scale))
    lane = lax.broadcasted_iota(jnp.int32, (1, 2 * NA_HEADDIM), 1)
    first = lane < NA_HEADDIM
    kc = kc_ref[...]
    vc = vc_ref[...]
    gq = NA_GROUP_ROWS * GRID_W

    def head_q(q, hh):
        keep = first if hh == 0 else jnp.logical_not(first)
        return jnp.where(keep, q * scale, jnp.zeros_like(q))

    @pl.when(j < LAT_BLOCKS)
    def _():
        ngroups = TQ // gq
        q2 = []
        for g in range(ngroups):
            q = q_ref[pl.ds(g * gq, gq), :]
            q2.append(jnp.concatenate([head_q(q, 0), head_q(q, 1)], axis=0))
        sc_all = _dot_nt(jnp.concatenate(q2, axis=0), kc)
        o_win, p_ctx, denom = [], [], []
        for g in range(ngroups):
            r0 = j * (TQ // GRID_W) + NA_GROUP_ROWS * g
            bs = jnp.clip(r0 - NA_WIN_R // 2, 0, SEQ // GRID_W - NA_BAND_ROWS)
            cfg = jnp.where(r0 == 0, 0, jnp.where(r0 == 2, 1, jnp.where(r0 == 60, 3, jnp.where(r0 == 62, 4, 2))))
            start = pl.multiple_of(bs * GRID_W, GRID_W)
            kb = k_ref[pl.ds(start, NA_BAND_ROWS * GRID_W), :]
            vb = v_ref[pl.ds(start, NA_BAND_ROWS * GRID_W), :]
            s = _dot_nt(q2[g], kb) + jnp.concatenate([bias(0, cfg), bias(1, cfg)], axis=0)
            sc = sc_all[2 * gq * g:2 * gq * (g + 1)]
            m = jnp.maximum(s.max(axis=-1, keepdims=True), sc.max(axis=-1, keepdims=True))
            p, pc = jnp.exp(s - m), jnp.exp(sc - m)
            denom.append(p.sum(axis=-1, keepdims=True) + pc.sum(axis=-1, keepdims=True))
            o_win.append(jnp.dot(p.astype(bf16), vb, preferred_element_type=f32))
            p_ctx.append(pc.astype(bf16))
        o_ctx = jnp.dot(jnp.concatenate(p_ctx, axis=0), vc, preferred_element_type=f32)
        for g in range(ngroups):
            o2 = (o_win[g] + o_ctx[2 * gq * g:2 * gq * (g + 1)]) / denom[g]
            o_ref[pl.ds(g * gq, gq), :] = jnp.where(first, o2[:gq], o2[gq:]).astype(o_ref.dtype)

    @pl.when(j == LAT_BLOCKS)
    def _():
        q = q_ref[...]
        q2 = jnp.concatenate([head_q(q, 0), head_q(q, 1)], axis=0)
        o2 = _softmax_av([(_dot_nt(q2, kc), vc)])
        o_ref[...] = jnp.where(first, o2[:CTX_LEN], o2[CTX_LEN:]).astype(o_ref.dtype)


def na_attention(qkv, tbl, layer, pair_of, bsz):
    t = qkv.shape[0]
    npair = NA_HEADS // 2
    ctx0 = bsz * LAT_BLOCKS

    def qrow(b, j):
        return jnp.where(j < LAT_BLOCKS, b * LAT_BLOCKS + j, ctx0 + b)

    return pl.pallas_call(
        _na_kernel,
        grid_spec=pltpu.PrefetchScalarGridSpec(
            num_scalar_prefetch=1,
            grid=(bsz, npair, LAT_BLOCKS + 1),
            in_specs=[
                pl.BlockSpec((TQ, 128), lambda b, p, j, po: (qrow(b, j), p)),
                pl.BlockSpec((SEQ, 128), lambda b, p, j, po: (b, npair + p)),
                pl.BlockSpec((SEQ, 128), lambda b, p, j, po: (b, 2 * npair + p)),
                pl.BlockSpec((CTX_LEN, 128), lambda b, p, j, po: (ctx0 + b, npair + p)),
                pl.BlockSpec((CTX_LEN, 128), lambda b, p, j, po: (ctx0 + b, 2 * npair + p)),
                pl.BlockSpec((None, 2) + tbl.shape[2:], lambda b, p, j, po: (layer, p, 0, 0, 0)),
            ],
            out_specs=pl.BlockSpec((TQ, 128), lambda b, p, j, po: (qrow(b, j), p)),
        ),
        out_shape=jax.ShapeDtypeStruct((t, NA_WIDTH), bf16),
        compiler_params=_params(),
        name="na_attention",
    )(jnp.asarray(pair_of), qkv, qkv, qkv, qkv, qkv, tbl)


def _seq_block(d, bsz, lat_blocks):
    ctx0 = bsz * lat_blocks

    def jj(s):
        return (s - 1) if d == 0 else (lat_blocks - s)

    def blk(b, s):
        return jnp.where(s == 0, ctx0 + b, b * lat_blocks + jj(s))

    return jj, blk


def _halo_specs(width, col, blk, t):
    nb8 = TQ // HALO
    return [
        pl.BlockSpec((TQ, width), lambda b, s: (blk(b, s), col)),
        pl.BlockSpec((HALO, width), lambda b, s: (jnp.maximum(blk(b, s) * nb8 - 1, 0), col)),
        pl.BlockSpec((HALO, width), lambda b, s: (jnp.minimum(blk(b, s) * nb8 + nb8, t // HALO - 1), col)),
    ]


def _fill_halo(xe, parts, has_prev, has_next):
    c0 = 0
    for x_ref, xp_ref, xn_ref in parts:
        cols = slice(c0, c0 + x_ref.shape[1])
        xe[0:HALO, cols] = jnp.where(has_prev, xp_ref[...], 0.0)
        xe[HALO:HALO + TQ, cols] = x_ref[...]
        xe[HALO + TQ:2 * HALO + TQ, cols] = jnp.where(has_next, xn_ref[...], 0.0)
        c0 += x_ref.shape[1]


def _centred_conv(xe, cw_ref, cb_ref, cols):
    xv = xe[:, cols]
    n = xv.shape[0]
    out = cb_ref[:, cols]
    for k in range(4):
        shifted = xv if k == 2 else pltpu.roll(xv, (2 - k) % n, 0)
        out = out + cw_ref[k:k + 1, cols] * shifted[HALO:HALO + TQ, :]
    return out


def lru_gate_weights(w_r, w_i):
    def pair(w):
        w = w.reshape(2, LRU_BLOCKS // 2, 2, LRU_BLOCK, LRU_BLOCK)
        z = jnp.zeros_like(w[:, :, 0])
        top = jnp.concatenate([w[:, :, 0], z], axis=-1)
        bot = jnp.concatenate([z, w[:, :, 1]], axis=-1)
        return jnp.concatenate([top, bot], axis=-2)
    return jnp.concatenate([pair(w_r), pair(w_i)], axis=-1).astype(bf16)


def _gelu_tanh(x):
    return 0.5 * x * (1.0 + jnp.tanh(float(np.sqrt(2.0 / np.pi)) * (x + 0.044715 * (x * x * x))))


def _lru_kernel(d, lat_blocks, *refs):
    if d == 0:
        (x_ref, xp_ref, xn_ref, cw_ref, cb_ref, wg_ref, br_ref, bi_ref, lam_ref, o_ref,
         xe, a_scr, u_scr, h_scr) = refs
    else:
        (x_ref, xp_ref, xn_ref, ag_ref, hf_ref, cw_ref, cb_ref, wg_ref, br_ref, bi_ref, lam_ref, o_ref,
         xe, a_scr, u_scr, h_scr, hs_scr) = refs
    s = pl.program_id(1)
    j = (s - 1) if d == 0 else (lat_blocks - s)
    has_prev = jnp.logical_and(s > 0, j > 0)
    has_next = jnp.logical_and(s > 0, j < lat_blocks - 1)

    @pl.when(s == 0)
    def _():
        h_scr[...] = jnp.zeros_like(h_scr)

    _fill_halo(xe, [(x_ref, xp_ref, xn_ref)], has_prev, has_next)
    sp = jax.nn.softplus(-lam_ref[...])
    for p in range(LRU_WIDTH // 128):
        sl = slice(128 * p, 128 * (p + 1))
        xc = _centred_conv(xe, cw_ref, cb_ref, sl)
        g = jnp.dot(xc.astype(bf16), wg_ref[p], preferred_element_type=f32)
        r = jax.nn.sigmoid(g[:, :128] + br_ref[:, sl])
        i = jax.nn.sigmoid(g[:, 128:] + bi_ref[:, sl])
        log_a = -LRU_C * r * sp[:, sl]
        a = jnp.exp(log_a)
        a_scr[:, sl] = a
        u_scr[:, sl] = jnp.sqrt(-jnp.tanh(log_a) * (a * a + 1.0)) * (i * xc)

    out = o_ref if d == 0 else hs_scr
    h = h_scr[0:1, :]
    for t in (range(TQ) if d == 0 else range(TQ - 1, -1, -1)):
        h = a_scr[t:t + 1, :] * h + u_scr[t:t + 1, :]
        out[t:t + 1, :] = h
    h_scr[0:1, :] = h
    if d == 1:
        y = (hf_ref[...] + hs_scr[...]) * _gelu_tanh(ag_ref[...])
        o_ref[...] = y.astype(o_ref.dtype)


def lru_pass(d, proj, hf, conv_w, conv_b, wg, b_r, b_i, lam, bsz, lat_blocks):
    t = proj.shape[0]
    _, blk = _seq_block(d, bsz, lat_blocks)
    row = lambda c: (lambda b, s: (blk(b, s), c))
    in_specs = _halo_specs(LRU_WIDTH, COL_AX, blk, t)
    args = [proj, proj, proj]
    if d == 1:
        in_specs += [pl.BlockSpec((TQ, LRU_WIDTH), row(COL_AG)), pl.BlockSpec((TQ, LRU_WIDTH), row(0))]
        args += [proj, hf]
    const = lambda shape: pl.BlockSpec(shape, lambda b, s: (0,) * len(shape))
    in_specs += [const((4, LRU_WIDTH)), const((1, LRU_WIDTH)), const((LRU_WIDTH // 128, 128, 256)),
                 const((1, LRU_WIDTH)), const((1, LRU_WIDTH)), const((1, LRU_WIDTH))]
    args += [conv_w, conv_b.reshape(1, -1), wg[d], b_r[d].reshape(1, -1), b_i[d].reshape(1, -1), lam[d].reshape(1, -1)]
    scratch = [pltpu.VMEM((TQ + 2 * HALO, LRU_WIDTH), f32), pltpu.VMEM((TQ, LRU_WIDTH), f32),
               pltpu.VMEM((TQ, LRU_WIDTH), f32), pltpu.VMEM((8, LRU_WIDTH), f32)]
    if d == 1:
        scratch.append(pltpu.VMEM((TQ, LRU_WIDTH), f32))
    return pl.pallas_call(
        functools.partial(_lru_kernel, d, lat_blocks),
        grid=(bsz, lat_blocks + 1),
        in_specs=in_specs,
        out_specs=pl.BlockSpec((TQ, LRU_WIDTH), row(0)),
        out_shape=jax.ShapeDtypeStruct((t, LRU_WIDTH), f32 if d == 0 else bf16),
        scratch_shapes=scratch,
        compiler_params=_params(),
        name=f"lru_pass{d}",
    )(*args)


def lru_branch(proj, conv_w, conv_b, w_r, b_r, w_i, b_i, lam, bsz, lat_blocks):
    wg = lru_gate_weights(w_r, w_i)
    hf = lru_pass(0, proj, None, conv_w, conv_b, wg, b_r, b_i, lam, bsz, lat_blocks)
    return lru_pass(1, proj, hf, conv_w, conv_b, wg, b_r, b_i, lam, bsz, lat_blocks)


def rope_tables(seq):
    pos = jnp.arange(seq)
    row = (pos // GRID_W).astype(f32)
    col = (pos % GRID_W).astype(f32)
    n_freq = SSD_STATE // 4
    freqs = ROPE_BASE ** (-jnp.arange(n_freq, dtype=f32) / n_freq)
    ang = jnp.concatenate([row[:, None] * freqs, col[:, None] * freqs], axis=-1)
    cos, sin = jnp.cos(ang), jnp.sin(ang)
    cosf = jnp.concatenate([cos, cos], axis=-1)
    sinf = jnp.concatenate([-sin, sin], axis=-1)
    cosf = jnp.concatenate([cosf, jnp.ones((TQ, SSD_STATE), f32)], axis=0)
    sinf = jnp.concatenate([sinf, jnp.zeros((TQ, SSD_STATE), f32)], axis=0)
    return cosf, sinf


def head_expand_matrix(d):
    e = np.zeros((128, SSD_INNER), np.float32)
    for h in range(SSD_HEADS):
        e[SSD_HEADS * d + h, h * SSD_HEADDIM:(h + 1) * SSD_HEADDIM] = 1.0
    return jnp.asarray(e, bf16)


def _split3(a):
    hi = a.astype(bf16)
    r = a - hi.astype(f32)
    mid = r.astype(bf16)
    return hi, mid, (r - mid.astype(f32)).astype(bf16)


def _dot_exact_rhs01(a, m01):
    return sum(jnp.dot(p, m01, preferred_element_type=f32) for p in _split3(a))


def _dot_exact_lhs01(m01, a):
    return sum(jnp.dot(m01, p, preferred_element_type=f32) for p in _split3(a))


def _ssd_kernel(d, lat_blocks, *refs):
    xs_refs, bc_refs, refs = refs[0:3], refs[3:6], refs[6:]
    if d == 0:
        (dt_ref, cos_ref, sin_ref, cw_ref, cb_ref, dtb_ref, alog_ref, e_ref,
         o_ref, xe, h_scr) = refs
    else:
        (dt_ref, cos_ref, sin_ref, z_ref, y0_ref, cw_ref, cb_ref, dtb_ref, alog_ref, e_ref,
         dsk_ref, ng_ref, o_ref, xe, h_scr, y_scr) = refs
    s = pl.program_id(1)
    j = (s - 1) if d == 0 else (lat_blocks - s)
    has_prev = jnp.logical_and(s > 0, j > 0)
    has_next = jnp.logical_and(s > 0, j < lat_blocks - 1)
    q = SSD_CHUNK

    @pl.when(s == 0)
    def _():
        h_scr[...] = jnp.zeros_like(h_scr)

    _fill_halo(xe, [xs_refs, bc_refs], has_prev, has_next)
    xbc = _centred_conv(xe, cw_ref, cb_ref, slice(None))
    xbc = xbc * jax.nn.sigmoid(xbc)
    xs = xbc[:, :SSD_INNER]
    cosf, sinf = cos_ref[...], sin_ref[...]

    def rope(g, off):
        v = xbc[:, off + g * SSD_STATE: off + (g + 1) * SSD_STATE]
        return (v * cosf + pltpu.roll(v, SSD_STATE // 2, 1) * sinf).astype(bf16)

    bm = [rope(g, SSD_INNER) for g in range(SSD_GROUPS)]
    cm = [rope(g, SSD_INNER + SSD_GROUPS * SSD_STATE) for g in range(SSD_GROUPS)]
    dt = jax.nn.softplus(dt_ref[...] + dtb_ref[...])
    delta = dt * (-jnp.exp(alog_ref[...]))
    ri = lax.broadcasted_iota(jnp.int32, (q, q), 0)
    ci = lax.broadcasted_iota(jnp.int32, (q, q), 1)
    keep = (ci <= ri) if d == 0 else (ci >= ri)
    tri = jnp.where(keep, 1.0, 0.0).astype(bf16)
    lane = lax.broadcasted_iota(jnp.int32, (1, 2 * SSD_HEADDIM), 1)
    halves = (lane < SSD_HEADDIM, lane >= SSD_HEADDIM)
    e = e_ref[...]
    last = q - 1 if d == 0 else 0
    out = o_ref if d == 0 else y_scr

    for c in (range(TQ // q) if d == 0 else range(TQ // q - 1, -1, -1)):
        rows = slice(c * q, (c + 1) * q)
        at = _dot_exact_lhs01(tri, delta[rows])
        at_exp = _dot_exact_rhs01(at, e)
        dt_exp = _dot_exact_rhs01(dt[rows], e)
        tot_exp = at_exp[last:last + 1, :]
        xdt = xs[rows] * dt_exp
        xd = (xdt * jnp.exp(tot_exp - at_exp)).astype(bf16)
        eat = jnp.exp(at_exp)
        cdec = jnp.exp(tot_exp)
        at_row = at.T
        ys = []
        for g in range(SSD_GROUPS):
            bg, cg = bm[g][rows], cm[g][rows]
            cb = _dot_nt(cg, bg)
            ht = h_scr[g]
            yoff = jnp.dot(cg, ht.astype(bf16), preferred_element_type=f32) * eat[:, g * GW:(g + 1) * GW]
            for pp in range(2):
                xpair = xdt[:, g * GW + pp * 128: g * GW + (pp + 1) * 128]
                acc = yoff[:, pp * 128:(pp + 1) * 128]
                for hh in range(2):
                    li = SSD_HEADS * d + 4 * g + 2 * pp + hh
                    seg = at[:, li:li + 1] - at_row[li:li + 1, :]
                    m = (cb * jnp.exp(jnp.where(keep, seg, NEG))).astype(bf16)
                    xm = jnp.where(halves[hh], xpair, 0.0).astype(bf16)
                    acc = acc + jnp.dot(m, xm, preferred_element_type=f32)
                ys.append(acc)
            upd = lax.dot_general(bg, xd[:, g * GW:(g + 1) * GW], (((0,), (0,)), ((), ())),
                                  preferred_element_type=f32)
            h_scr[g] = cdec[:, g * GW:(g + 1) * GW] * ht + upd
        out[rows, :] = jnp.concatenate(ys, axis=-1)

    if d == 1:
        y = y0_ref[...] + y_scr[...] + dsk_ref[...] * xs
        z = z_ref[...]
        y = y * (z * jax.nn.sigmoid(z))
        y = y * lax.rsqrt(jnp.mean(y * y, axis=-1, keepdims=True) + EPS)
        o_ref[...] = (y * ng_ref[...]).astype(o_ref.dtype)


def ssd_pass(d, proj, y0, cosf, sinf, conv_w, conv_b, a_log, dt_bias, d_skip, norm_g, bsz, lat_blocks):
    t = proj.shape[0]
    jj, blk = _seq_block(d, bsz, lat_blocks)
    row = lambda c: (lambda b, s: (blk(b, s), c))
    tbl = lambda b, s: (jnp.where(s == 0, lat_blocks, jj(s)), 0)
    const = lambda shape: pl.BlockSpec(shape, lambda b, s: (0,) * len(shape))
    pad128 = lambda v: jnp.pad(v.reshape(1, -1).astype(f32), ((0, 0), (0, 128 - v.size)))
    in_specs = _halo_specs(SSD_INNER, COL_XS, blk, t) + _halo_specs(SSD_CONV_DIM - SSD_INNER, COL_BC, blk, t) + [
        pl.BlockSpec((TQ, 128), row(COL_DT)),
        pl.BlockSpec((TQ, SSD_STATE), tbl),
        pl.BlockSpec((TQ, SSD_STATE), tbl),
    ]
    args = [proj] * 7 + [cosf, sinf]
    if d == 1:
        in_specs += [pl.BlockSpec((TQ, SSD_INNER), row(COL_Z)), pl.BlockSpec((TQ, SSD_INNER), row(0))]
        args += [proj, y0]
    in_specs += [const((4, SSD_CONV_DIM)), const((1, SSD_CONV_DIM)), const((1, 128)), const((1, 128)),
                 const((128, SSD_INNER))]
    args += [conv_w, conv_b.reshape(1, -1), pad128(dt_bias), pad128(a_log), head_expand_matrix(d)]
    scratch = [pltpu.VMEM((TQ + 2 * HALO, SSD_CONV_DIM), f32), pltpu.VMEM((SSD_GROUPS, SSD_STATE, GW), f32)]
    if d == 1:
        in_specs += [const((1, SSD_INNER)), const((1, SSD_INNER))]
        args += [jnp.repeat(d_skip, SSD_HEADDIM).reshape(1, -1), norm_g.reshape(1, -1)]
        scratch.append(pltpu.VMEM((TQ, SSD_INNER), f32))
    return pl.pallas_call(
        functools.partial(_ssd_kernel, d, lat_blocks),
        grid=(bsz, lat_blocks + 1),
        in_specs=in_specs,
        out_specs=pl.BlockSpec((TQ, SSD_INNER), row(0)),
        out_shape=jax.ShapeDtypeStruct((t, SSD_INNER), f32 if d == 0 else bf16),
        scratch_shapes=scratch,
        compiler_params=_params(),
        name=f"ssd_pass{d}",
    )(*args)


def ssd_branch(proj, cosf, sinf, conv_w, conv_b, a_log, dt_bias, d_skip, norm_g, bsz, lat_blocks):
    y0 = ssd_pass(0, proj, None, cosf, sinf, conv_w, conv_b, a_log, dt_bias, d_skip, norm_g, bsz, lat_blocks)
    return ssd_pass(1, proj, y0, cosf, sinf, conv_w, conv_b, a_log, dt_bias, d_skip, norm_g, bsz, lat_blocks)


def _tile_rows(mod_rows, bsz):
    idx = np.concatenate([np.repeat(np.arange(bsz), SEQ // TM), np.full(bsz * CTX_LEN // TM, bsz)])
    return mod_rows[idx][:, None, :]


def _expert_choice_moe(h, v, aff, w1, w3, w2, layer, g2_rows, bsz):
    t, d = v.shape
    aff = aff[:, :N_EXPERTS]
    nl = bsz * SEQ

    def choose(a, length):
        cap = CAPACITY_FACTOR * length // N_EXPERTS
        g, idx = lax.top_k(jnp.swapaxes(a.reshape(bsz, length, N_EXPERTS), 1, 2), cap)
        return jnp.swapaxes(g, 0, 1), jnp.swapaxes(idx, 0, 1), cap

    g_l, i_l, cap_l = choose(aff[:nl], SEQ)
    g_c, i_c, cap_c = choose(aff[nl:], CTX_LEN)
    boff = jnp.arange(bsz)[None, :, None]
    ctx_rows = (i_c + boff * CTX_LEN).reshape(N_EXPERTS, -1)
    rows = jnp.concatenate([(i_l + boff * SEQ).reshape(N_EXPERTS, -1), ctx_rows + nl], axis=1)
    gates = jnp.concatenate([g_l.reshape(N_EXPERTS, -1), g_c.reshape(N_EXPERTS, -1)], axis=1)[..., None]
    r = rows.shape[1]
    xg = v.at[rows.reshape(-1)].get(mode="promise_in_bounds").reshape(N_EXPERTS, r, d)
    hdn = expert_hidden(xg, w1, w3, layer, r // 4)
    ctx_slots, ctx_tile = bsz * cap_c, bsz * CTX_LEN
    assert (bsz * cap_l) % ctx_slots == 0 and nl % ctx_tile == 0
    h = moe_combine(i_l.reshape(-1), hdn, 0, cap_l, gates, w2, layer, h, g2_rows, 0, 0, SEQ, TN, bsz)
    return moe_combine(ctx_rows.reshape(-1), hdn, bsz * cap_l // ctx_slots, ctx_slots, gates, w2, layer, h, g2_rows,
                       bsz, nl // ctx_tile, ctx_tile, d, 1)


def kernel(x, c, ctx, c_ctx, w_ada, b_ada, norm_mix, norm_ffn, w_in, lru_conv_w, lru_conv_b, lru_w_r, lru_b_r,
           lru_w_i, lru_b_i, lru_lambda, ssd_conv_w, ssd_conv_b, ssd_a_log, ssd_dt_bias, ssd_d, ssd_norm, na_rpb,
           w_branch_lru, w_branch_ssd, w_branch_na, w_out, w_router, w1, w3, w2, norm_final):
    bsz = x.shape[0]
    assert x.shape[1:] == (SEQ, D_MODEL) and ctx.shape[1:] == (CTX_LEN, D_MODEL) and bsz * CTX_LEN == TM
    d = D_MODEL
    h = (x.reshape(bsz * SEQ, d), ctx.reshape(bsz * CTX_LEN, d))
    t = bsz * (SEQ + CTX_LEN)

    cond = jnp.concatenate([c, c_ctx[None, :], jnp.zeros((8 - bsz - 1, d), f32)], axis=0)
    mod = ada_modulation(cond, w_ada, b_ada)
    cosf, sinf = rope_tables(SEQ)

    assert DT_END + DT_PAD == W_QKV0 and w_in.shape[-1] == DT_END + 3 * NA_WIDTH + 3 * d
    w_proj = relayout_proj_weights(jnp.swapaxes(w_in, 1, 2))
    qkv_blocks = 3 * NA_WIDTH // TN
    pa, pb, pc, wo = (w.astype(bf16) for w in (w_branch_lru, w_branch_ssd, w_branch_na, w_out))
    na_tbl, na_pair_of = na_bias_blocks(na_rpb)

    for l in range(DEPTH):
        mods = jnp.split(mod[l], 6, axis=-1)
        sh1, sc1, g1, sh2, sc2 = (_tile_rows(m, bsz) for m in mods[:5])
        u = norm_modulate(h, norm_mix[l], sc1, sh1, bf16)
        proj = matmul_bf16(u, w_proj, l, N_F32, lambda j: jnp.where(j < W_QKV0 // TN, j, j + qkv_blocks), f32)
        qkv = matmul_bf16(u, w_proj, l, 3 * NA_WIDTH, lambda j: j + W_QKV0 // TN, bf16)
        ya = lru_branch(proj, lru_conv_w[l], lru_conv_b[l], lru_w_r[l], lru_b_r[l], lru_w_i[l], lru_b_i[l],
                        lru_lambda[l], bsz, LAT_BLOCKS)
        yb = ssd_branch(proj, cosf, sinf, ssd_conv_w[l], ssd_conv_b[l], ssd_a_log[l], ssd_dt_bias[l], ssd_d[l],
                        ssd_norm[l], bsz, LAT_BLOCKS)
        yc = na_attention(qkv, na_tbl, l, na_pair_of, bsz)
        y = branch_merge(ya, yb, yc, proj, pa, pb, pc, l, COL_G)
        h, v, aff = residual_norm_router(y, wo, l, h, g1, norm_ffn[l], sc2, sh2, w_router[l])
        h = _expert_choice_moe(h, v, aff, w1, w3, w2, l, mods[5][:, None, :], bsz)

    zeros = jnp.zeros((t // TM, 1, d), f32)
    out = norm_modulate(h, norm_final, zeros, zeros, f32, rows=bsz * SEQ)
    return out.reshape(bsz, SEQ, d)
```

```python
import functools

import jax
import jax.numpy as jnp
import numpy as np
from jax import lax
from jax.experimental import pallas as pl
from jax.experimental.pallas import tpu as pltpu

D_MODEL = 2048
SEQ = 4096
CTX_LEN = 256
DEPTH = 4
GRID_W = 64
EPS = 1e-6
ROPE_BASE = 10000.0
LRU_WIDTH = 1024
LRU_BLOCKS = 16
LRU_BLOCK = LRU_WIDTH // LRU_BLOCKS
LRU_C = 8.0
SSD_INNER = 1024
SSD_HEADDIM = 64
SSD_HEADS = SSD_INNER // SSD_HEADDIM
SSD_GROUPS = 4
SSD_STATE = 128
SSD_CHUNK = 128
SSD_CONV_DIM = SSD_INNER + 2 * SSD_GROUPS * SSD_STATE
NA_HEADS = 16
NA_HEADDIM = 64
NA_WIDTH = NA_HEADS * NA_HEADDIM
NA_WIN_R = 8
NA_WIN_C = 16
N_EXPERTS = 16
EXPERT_FF = 1024
CAPACITY_FACTOR = 2

V7X_VMEM_LIMIT = 56 * 1024 * 1024
TM = 1024
MM_MAX_ROWS = 2304
TQ = 256
HALO = 8
LAT_BLOCKS = SEQ // TQ
NEG = -1e30
HI = lax.Precision.HIGHEST
bf16 = jnp.bfloat16
f32 = jnp.float32

TN = 512
DT_END = 2 * LRU_WIDTH + SSD_INNER + SSD_CONV_DIM + 2 * SSD_HEADS
DT_PAD = 480
COL_AX, COL_AG, COL_Z, COL_XS, COL_BC = 0, 1, 2, 3, 4
COL_DT = 40
W_QKV0 = 5632
COL_G = 5632
N_F32 = COL_G + 3 * D_MODEL
GW = SSD_INNER // SSD_GROUPS

NA_GROUP_ROWS = 2
NA_BAND_ROWS = 10
NA_NCFG = 5


def _params():
    return pltpu.CompilerParams(vmem_limit_bytes=V7X_VMEM_LIMIT)


def _ada_kernel(x_ref, w_ref, b_ref, o_ref):
    c = x_ref[...]
    x = (c * jax.nn.sigmoid(c)).astype(bf16)
    o_ref[...] = jnp.dot(x, w_ref[...].astype(bf16), preferred_element_type=f32) + b_ref[...]


def ada_modulation(cond, w_ada, b_ada):
    depth, d, n = w_ada.shape
    tn = 1024
    return pl.pallas_call(
        _ada_kernel,
        grid=(depth, n // tn),
        in_specs=[pl.BlockSpec((8, d), lambda l, j: (0, 0)),
                  pl.BlockSpec((None, d, tn), lambda l, j: (l, 0, j)),
                  pl.BlockSpec((None, 1, tn), lambda l, j: (l, 0, j))],
        out_specs=pl.BlockSpec((None, 8, tn), lambda l, j: (l, 0, j)),
        out_shape=jax.ShapeDtypeStruct((depth, 8, n), f32),
        compiler_params=_params(),
        name="ada_modulation",
    )(cond, w_ada, b_ada.reshape(depth, 1, n))


def _stream_rows(h, tm):
    if not isinstance(h, tuple):
        return [pl.BlockSpec((tm, h.shape[1]), lambda i: (i, 0))], [h], (lambda refs, i: refs[0][...]), h.shape[0]
    lat, ctx = h
    d = lat.shape[1]
    n_lat = lat.shape[0] // tm
    assert lat.shape[0] % tm == 0 and ctx.shape[0] % tm == 0
    specs = [pl.BlockSpec((tm, d), lambda i: (jnp.minimum(i, n_lat - 1), 0)),
             pl.BlockSpec((tm, d), lambda i: (jnp.maximum(i - n_lat, 0), 0))]
    return specs, [lat, ctx], (lambda refs, i: jnp.where(i < n_lat, refs[0][...], refs[1][...])), lat.shape[0] + ctx.shape[0]


def _norm_mod_kernel(read_h, nh, *refs):
    g_ref, sc_ref, sh_ref, o_ref = refs[nh:]
    x = read_h(refs[:nh], pl.program_id(0))
    y = x * lax.rsqrt(jnp.mean(x * x, axis=-1, keepdims=True) + EPS)
    o_ref[...] = ((y * g_ref[...]) * (1.0 + sc_ref[...]) + sh_ref[...]).astype(o_ref.dtype)


def norm_modulate(h, g, sc_t, sh_t, out_dtype, rows=None):
    tm = min(512, TM)
    per = TM // tm
    h_specs, h_args, read_h, t = _stream_rows(h, tm)
    d = h_args[0].shape[1]
    t = t if rows is None else rows
    return pl.pallas_call(
        functools.partial(_norm_mod_kernel, read_h, len(h_args)),
        grid=(t // tm,),
        in_specs=h_specs + [pl.BlockSpec((1, d), lambda i: (0, 0)),
                            pl.BlockSpec((None, 1, d), lambda i: (i // per, 0, 0)),
                            pl.BlockSpec((None, 1, d), lambda i: (i // per, 0, 0))],
        out_specs=pl.BlockSpec((tm, d), lambda i: (i, 0)),
        out_shape=jax.ShapeDtypeStruct((t, d), out_dtype),
        compiler_params=_params(),
        name="norm_modulate",
    )(*h_args, g.reshape(1, d), sc_t, sh_t)


def _relayout_kernel(shift, a_ref, b_ref, o_ref):
    j = pl.program_id(1)
    split = DT_END // TN

    @pl.when(j < split)
    def _():
        o_ref[...] = a_ref[...].astype(o_ref.dtype)

    @pl.when(j == split)
    def _():
        row = lax.broadcasted_iota(jnp.int32, a_ref.shape, 0)
        o_ref[...] = jnp.where(row < shift, a_ref[...], 0.0).astype(o_ref.dtype)

    @pl.when(j > split)
    def _():
        o_ref[0:TN - shift, :] = a_ref[shift:TN, :].astype(o_ref.dtype)
        o_ref[TN - shift:TN, :] = b_ref[...].astype(o_ref.dtype)


def relayout_proj_weights(w_t):
    depth, n, d = w_t.shape
    shift = DT_END % TN
    assert shift + DT_PAD == TN and shift % 16 == 0 and n % shift == 0
    split = DT_END // TN
    return pl.pallas_call(
        functools.partial(_relayout_kernel, shift),
        grid=(depth, (n + DT_PAD) // TN),
        in_specs=[pl.BlockSpec((None, TN, d), lambda l, j: (l, jnp.where(j <= split, j, j - 1), 0)),
                  pl.BlockSpec((None, shift, d), lambda l, j: (l, jnp.where(j <= split, 0, (TN // shift) * j), 0))],
        out_specs=pl.BlockSpec((None, TN, d), lambda l, j: (l, j, 0)),
        out_shape=jax.ShapeDtypeStruct((depth, n + DT_PAD, d), bf16),
        compiler_params=_params(),
        name="relayout_proj_weights",
    )(w_t, w_t)


def _mm_kernel(x_ref, w_ref, o_ref):
    o_ref[...] = _dot_nt(x_ref[...], w_ref[...]).astype(o_ref.dtype)


def matmul_bf16(x, w, layer, n_out, w_block, out_dtype):
    m, k = x.shape
    tm = max(c for c in range(16, MM_MAX_ROWS + 1, 16) if m % c == 0)
    return pl.pallas_call(
        _mm_kernel,
        grid=(m // tm, n_out // TN),
        in_specs=[pl.BlockSpec((tm, k), lambda i, j: (i, 0)),
                  pl.BlockSpec((None, TN, k), lambda i, j: (layer, w_block(j), 0))],
        out_specs=pl.BlockSpec((tm, TN), lambda i, j: (i, j)),
        out_shape=jax.ShapeDtypeStruct((m, n_out), out_dtype),
        compiler_params=_params(),
        name="matmul_bf16",
    )(x, w)


def _merge_kernel(ya_ref, yb_ref, yc_ref, ga_ref, gb_ref, gc_ref, pa_ref, pb_ref, pc_ref, o_ref):
    acc = jax.nn.sigmoid(ga_ref[...]) * jnp.dot(ya_ref[...], pa_ref[...], preferred_element_type=f32)
    acc = acc + jax.nn.sigmoid(gb_ref[...]) * jnp.dot(yb_ref[...], pb_ref[...], preferred_element_type=f32)
    acc = acc + jax.nn.sigmoid(gc_ref[...]) * jnp.dot(yc_ref[...], pc_ref[...], preferred_element_type=f32)
    o_ref[...] = acc.astype(o_ref.dtype)


def branch_merge(ya, yb, yc, proj, pa, pb, pc, layer, g_col0):
    t, k = ya.shape
    n = pa.shape[-1]
    tn = 512
    gb0 = g_col0 // tn
    nj = n // tn
    xs = pl.BlockSpec((TM, k), lambda i, j: (i, 0))
    ws = pl.BlockSpec((None, k, tn), lambda i, j: (layer, 0, j))
    gs = lambda q: pl.BlockSpec((TM, tn), lambda i, j: (i, gb0 + q * nj + j))
    return pl.pallas_call(
        _merge_kernel,
        grid=(t // TM, nj),
        in_specs=[xs, xs, xs, gs(0), gs(1), gs(2), ws, ws, ws],
        out_specs=pl.BlockSpec((TM, tn), lambda i, j: (i, j)),
        out_shape=jax.ShapeDtypeStruct((t, n), bf16),
        compiler_params=_params(),
        name="branch_merge",
    )(ya, yb, yc, proj, proj, proj, pa, pb, pc)


def _resid_norm_router_kernel(read_h, nh, *refs):
    y_ref, w_ref, g1_ref, ng_ref, sc_ref, sh_ref, wr_ref, ho_ref, v_ref, aff_ref = refs[nh:]
    h = read_h(refs[:nh], pl.program_id(0))
    hn = h + g1_ref[...] * jnp.dot(y_ref[...], w_ref[...], preferred_element_type=f32)
    ho_ref[...] = hn
    yn = hn * lax.rsqrt(jnp.mean(hn * hn, axis=-1, keepdims=True) + EPS)
    v = ((yn * ng_ref[...]) * (1.0 + sc_ref[...]) + sh_ref[...]).astype(bf16)
    v_ref[...] = v
    s = jnp.dot(v, wr_ref[...], preferred_element_type=f32)
    lane = lax.broadcasted_iota(jnp.int32, s.shape, 1)
    s = jnp.where(lane < N_EXPERTS, s, NEG)
    e = jnp.exp(s - s.max(axis=-1, keepdims=True))
    aff_ref[...] = e / e.sum(axis=-1, keepdims=True)


def residual_norm_router(y, w, layer, h, g1_t, norm_g, sc_t, sh_t, w_router):
    tm = TM // 2
    h_specs, h_args, read_h, t = _stream_rows(h, tm)
    d = h_args[0].shape[1]
    wr = jnp.pad(w_router, ((0, 0), (0, 128 - N_EXPERTS))).astype(bf16)
    rows = pl.BlockSpec((tm, d), lambda i: (i, 0))
    per_tile = pl.BlockSpec((None, 1, d), lambda i: (i * tm // TM, 0, 0))
    return pl.pallas_call(
        functools.partial(_resid_norm_router_kernel, read_h, len(h_args)),
        grid=(t // tm,),
        in_specs=h_specs + [rows, pl.BlockSpec((None, d, d), lambda i: (layer, 0, 0)), per_tile,
                            pl.BlockSpec((1, d), lambda i: (0, 0)), per_tile, per_tile,
                            pl.BlockSpec((d, 128), lambda i: (0, 0))],
        out_specs=[rows, rows, pl.BlockSpec((tm, 128), lambda i: (i, 0))],
        out_shape=[jax.ShapeDtypeStruct((t, d), f32), jax.ShapeDtypeStruct((t, d), bf16),
                   jax.ShapeDtypeStruct((t, 128), f32)],
        compiler_params=_params(),
        name="residual_norm_router",
    )(*h_args, y, w, g1_t, norm_g.reshape(1, d), sc_t, sh_t, wr)


def _expert_hidden_kernel(x_ref, w1_ref, w3_ref, o_ref):
    x = x_ref[...]
    a = jnp.dot(x, w1_ref[...].astype(bf16), preferred_element_type=f32)
    b = jnp.dot(x, w3_ref[...].astype(bf16), preferred_element_type=f32)
    o_ref[...] = ((a * jax.nn.sigmoid(a)) * b).astype(o_ref.dtype)


def expert_hidden(xg, w1, w3, layer, tm):
    del tm
    e, r, d = xg.shape
    f = w1.shape[-1]
    ft = EXPERT_FT
    return pl.pallas_call(
        _expert_hidden_kernel,
        grid=(e, f // ft),
        in_specs=[pl.BlockSpec((None, r, d), lambda k, j: (k, 0, 0)),
                  pl.BlockSpec((None, None, d, ft), lambda k, j: (layer, k, 0, j)),
                  pl.BlockSpec((None, None, d, ft), lambda k, j: (layer, k, 0, j))],
        out_specs=pl.BlockSpec((None, r, ft), lambda k, j: (k, 0, j)),
        out_shape=jax.ShapeDtypeStruct((e, r, f), bf16),
        compiler_params=_params(),
        name="expert_hidden",
    )(xg, w1, w3)


SCATTER_UNROLL = 8
EXPERT_FT = 256


def _combine_kernel(nb, slots, idx_ref, hdn_ref, w2_ref, g_ref, h_ref, g2_ref, *rest):
    o_ref, ye_scr = rest[-2], rest[-1]
    b, e = pl.program_id(0), pl.program_id(2)

    @pl.when(e == 0)
    def _():
        o_ref[...] = jnp.zeros_like(o_ref)

    ye_scr[...] = jnp.dot(hdn_ref[...], w2_ref[...].astype(bf16), preferred_element_type=f32) * g_ref[...]
    base0 = (e * nb + b) * slots

    for s0 in range(0, slots, SCATTER_UNROLL):
        rows = [idx_ref[base0 + s0 + k] for k in range(SCATTER_UNROLL)]
        vals = [o_ref[pl.ds(rows[k], 1), :] + ye_scr[s0 + k:s0 + k + 1, :] for k in range(SCATTER_UNROLL)]
        for k in range(SCATTER_UNROLL):
            o_ref[pl.ds(rows[k], 1), :] = vals[k]

    @pl.when(e == pl.num_programs(2) - 1)
    def _():
        o_ref[...] = h_ref[...] + g2_ref[...] * o_ref[...]


def moe_combine(idx, hdn, slot_blk0, slots, gates, w2, layer, h, g2_rows, g2_row0, row_blk0, rows_blk, dq, nb):
    e, _, f = hdn.shape
    t, d = h.shape
    in_specs = [
        pl.BlockSpec((None, slots, f), lambda b, q, k, idx: (k, slot_blk0 + b, 0)),
        pl.BlockSpec((None, None, f, dq), lambda b, q, k, idx: (layer, k, 0, q)),
        pl.BlockSpec((None, slots, 1), lambda b, q, k, idx: (k, slot_blk0 + b, 0)),
        pl.BlockSpec((rows_blk, dq), lambda b, q, k, idx: (row_blk0 + b, q)),
        pl.BlockSpec((None, 1, dq), lambda b, q, k, idx: (g2_row0 + b, 0, q)),
    ]
    args = [idx, hdn, w2, gates, h, g2_rows]
    aliases = {4: 0}
    return pl.pallas_call(
        functools.partial(_combine_kernel, nb, slots),
        grid_spec=pltpu.PrefetchScalarGridSpec(
            num_scalar_prefetch=1,
            grid=(nb, d // dq, e),
            in_specs=in_specs,
            out_specs=pl.BlockSpec((rows_blk, dq), lambda b, q, k, idx: (row_blk0 + b, q)),
            scratch_shapes=[pltpu.VMEM((slots, dq), f32)],
        ),
        out_shape=jax.ShapeDtypeStruct((t, d), f32),
        input_output_aliases=aliases,
        compiler_params=_params(),
        name="moe_combine",
    )(*args)


def na_bias_blocks(rpb):
    rows = SEQ // GRID_W
    n_dr, n_dc = 2 * NA_WIN_R - 1, 2 * NA_WIN_C - 1
    qc, kc = np.arange(GRID_W)[:, None], np.arange(GRID_W)[None, :]
    cs = np.clip(qc - NA_WIN_C // 2, 0, GRID_W - NA_WIN_C)
    col_ok = (kc >= cs) & (kc < cs + NA_WIN_C)
    pick = (np.arange(n_dc)[:, None, None] == (kc - qc + NA_WIN_C - 1)[None]) & col_ok[None]
    blocks = jnp.einsum('...rd,dqk->...rqk', rpb, jnp.asarray(pick, f32), precision=HI)
    blocks = jnp.where(col_ok, blocks, NEG)
    blocks = jnp.concatenate([blocks, jnp.full(blocks.shape[:-3] + (1, GRID_W, GRID_W), NEG, f32)], axis=-3)
    which = np.full((NA_NCFG, NA_GROUP_ROWS, NA_BAND_ROWS), n_dr, np.int32)
    for c, r0 in enumerate((0, 2, 4, 60, 62)):
        bs = int(np.clip(r0 - NA_WIN_R // 2, 0, rows - NA_BAND_ROWS))
        for qr in range(NA_GROUP_ROWS):
            r = r0 + qr
            rs = int(np.clip(r - NA_WIN_R // 2, 0, rows - NA_WIN_R))
            for kr in range(NA_BAND_ROWS):
                if rs <= bs + kr < rs + NA_WIN_R:
                    which[c, qr, kr] = bs + kr - r + NA_WIN_R - 1
    pairs = sorted({(int(which[c, qr, 2 * m]), int(which[c, qr, 2 * m + 1]))
                    for c in range(NA_NCFG) for qr in range(NA_GROUP_ROWS) for m in range(NA_BAND_ROWS // 2)})
    pair_of = np.array([[[pairs.index((int(which[c, qr, 2 * m]), int(which[c, qr, 2 * m + 1])))
                          for m in range(NA_BAND_ROWS // 2)] for qr in range(NA_GROUP_ROWS)]
                        for c in range(NA_NCFG)], np.int32)
    left = jnp.take(blocks, np.array([p[0] for p in pairs]), axis=-3)
    right = jnp.take(blocks, np.array([p[1] for p in pairs]), axis=-3)
    return jnp.concatenate([left, right], axis=-1), pair_of.reshape(-1)


def _dot_nt(a, b):
    return lax.dot_general(a, b, (((1,), (1,)), ((), ())), preferred_element_type=f32)


def _softmax_av(parts):
    m = parts[0][0].max(axis=-1, keepdims=True)
    for s, _ in parts[1:]:
        m = jnp.maximum(m, s.max(axis=-1, keepdims=True))
    l = None
    o = None
    for s, v in parts:
        p = jnp.exp(s - m)
        ls = p.sum(axis=-1, keepdims=True)
        os_ = jnp.dot(p.astype(bf16), v, preferred_element_type=f32)
        l = ls if l is None else l + ls
        o = os_ if o is None else o + os_
    return o / l


def _na_kernel(pair_ref, q_ref, k_ref, v_ref, kc_ref, vc_ref, tbl_ref, o_ref):
    j = pl.program_id(2)
    npairs = NA_BAND_ROWS // 2

    def bias(hh, cfg):
        rows = []
        for qr in range(NA_GROUP_ROWS):
            base = (cfg * NA_GROUP_ROWS + qr) * npairs
            rows.append(jnp.concatenate([tbl_ref[hh, pair_ref[base + m]] for m in range(npairs)], axis=1))
        return jnp.concatenate(rows, axis=0)

    scale = NA_HEADDIM ** -0.5
    assert scale == 2.0 ** round(np.log2(scale))
    lane = lax.broadcasted_iota(jnp.int32, (1, 2 * NA_HEADDIM), 1)
    first = lane < NA_HEADDIM
    kc = kc_ref[...]
    vc = vc_ref[...]
    gq = NA_GROUP_ROWS * GRID_W

    def head_q(q, hh):
        keep = first if hh == 0 else jnp.logical_not(first)
        return jnp.where(keep, q * scale, jnp.zeros_like(q))

    @pl.when(j < LAT_BLOCKS)
    def _():
        ngroups = TQ // gq
        q2 = []
        for g in range(ngroups):
            q = q_ref[pl.ds(g * gq, gq), :]
            q2.append(jnp.concatenate([head_q(q, 0), head_q(q, 1)], axis=0))
        sc_all = _dot_nt(jnp.concatenate(q2, axis=0), kc)
        o_win, p_ctx, denom = [], [], []
        for g in range(ngroups):
            r0 = j * (TQ // GRID_W) + NA_GROUP_ROWS * g
            bs = jnp.clip(r0 - NA_WIN_R // 2, 0, SEQ // GRID_W - NA_BAND_ROWS)
            cfg = jnp.where(r0 == 0, 0, jnp.where(r0 == 2, 1, jnp.where(r0 == 60, 3, jnp.where(r0 == 62, 4, 2))))
            start = pl.multiple_of(bs * GRID_W, GRID_W)
            kb = k_ref[pl.ds(start, NA_BAND_ROWS * GRID_W), :]
            vb = v_ref[pl.ds(start, NA_BAND_ROWS * GRID_W), :]
            s = _dot_nt(q2[g], kb) + jnp.concatenate([bias(0, cfg), bias(1, cfg)], axis=0)
            sc = sc_all[2 * gq * g:2 * gq * (g + 1)]
            m = jnp.maximum(s.max(axis=-1, keepdims=True), sc.max(axis=-1, keepdims=True))
            p, pc = jnp.exp(s - m), jnp.exp(sc - m)
            denom.append(p.sum(axis=-1, keepdims=True) + pc.sum(axis=-1, keepdims=True))
            o_win.append(jnp.dot(p.astype(bf16), vb, preferred_element_type=f32))
            p_ctx.append(pc.astype(bf16))
        o_ctx = jnp.dot(jnp.concatenate(p_ctx, axis=0), vc, preferred_element_type=f32)
        for g in range(ngroups):
            o2 = (o_win[g] + o_ctx[2 * gq * g:2 * gq * (g + 1)]) / denom[g]
            o_ref[pl.ds(g * gq, gq), :] = jnp.where(first, o2[:gq], o2[gq:]).astype(o_ref.dtype)

    @pl.when(j == LAT_BLOCKS)
    def _():
        q = q_ref[...]
        q2 = jnp.concatenate([head_q(q, 0), head_q(q, 1)], axis=0)
        o2 = _softmax_av([(_dot_nt(q2, kc), vc)])
        o_ref[...] = jnp.where(first, o2[:CTX_LEN], o2[CTX_LEN:]).astype(o_ref.dtype)


def na_attention(qkv, tbl, layer, pair_of, bsz):
    t = qkv.shape[0]
    npair = NA_HEADS // 2
    ctx0 = bsz * LAT_BLOCKS

    def qrow(b, j):
        return jnp.where(j < LAT_BLOCKS, b * LAT_BLOCKS + j, ctx0 + b)

    return pl.pallas_call(
        _na_kernel,
        grid_spec=pltpu.PrefetchScalarGridSpec(
            num_scalar_prefetch=1,
            grid=(bsz, npair, LAT_BLOCKS + 1),
            in_specs=[
                pl.BlockSpec((TQ, 128), lambda b, p, j, po: (qrow(b, j), p)),
                pl.BlockSpec((SEQ, 128), lambda b, p, j, po: (b, npair + p)),
                pl.BlockSpec((SEQ, 128), lambda b, p, j, po: (b, 2 * npair + p)),
                pl.BlockSpec((CTX_LEN, 128), lambda b, p, j, po: (ctx0 + b, npair + p)),
                pl.BlockSpec((CTX_LEN, 128), lambda b, p, j, po: (ctx0 + b, 2 * npair + p)),
                pl.BlockSpec((None, 2) + tbl.shape[2:], lambda b, p, j, po: (layer, p, 0, 0, 0)),
            ],
            out_specs=pl.BlockSpec((TQ, 128), lambda b, p, j, po: (qrow(b, j), p)),
        ),
        out_shape=jax.ShapeDtypeStruct((t, NA_WIDTH), bf16),
        compiler_params=_params(),
        name="na_attention",
    )(jnp.asarray(pair_of), qkv, qkv, qkv, qkv, qkv, tbl)


def _seq_block(d, bsz, lat_blocks):
    ctx0 = bsz * lat_blocks

    def jj(s):
        return (s - 1) if d == 0 else (lat_blocks - s)

    def blk(b, s):
        return jnp.where(s == 0, ctx0 + b, b * lat_blocks + jj(s))

    return jj, blk


def _halo_specs(width, col, blk, t):
    nb8 = TQ // HALO
    return [
        pl.BlockSpec((TQ, width), lambda b, s: (blk(b, s), col)),
        pl.BlockSpec((HALO, width), lambda b, s: (jnp.maximum(blk(b, s) * nb8 - 1, 0), col)),
        pl.BlockSpec((HALO, width), lambda b, s: (jnp.minimum(blk(b, s) * nb8 + nb8, t // HALO - 1), col)),
    ]


def _fill_halo(xe, parts, has_prev, has_next):
    c0 = 0
    for x_ref, xp_ref, xn_ref in parts:
        cols = slice(c0, c0 + x_ref.shape[1])
        xe[0:HALO, cols] = jnp.where(has_prev, xp_ref[...], 0.0)
        xe[HALO:HALO + TQ, cols] = x_ref[...]
        xe[HALO + TQ:2 * HALO + TQ, cols] = jnp.where(has_next, xn_ref[...], 0.0)
        c0 += x_ref.shape[1]


def _centred_conv(xe, cw_ref, cb_ref, cols):
    xv = xe[:, cols]
    n = xv.shape[0]
    out = cb_ref[:, cols]
    for k in range(4):
        shifted = xv if k == 2 else pltpu.roll(xv, (2 - k) % n, 0)
        out = out + cw_ref[k:k + 1, cols] * shifted[HALO:HALO + TQ, :]
    return out


def lru_gate_weights(w_r, w_i):
    def pair(w):
        w = w.reshape(2, LRU_BLOCKS // 2, 2, LRU_BLOCK, LRU_BLOCK)
        z = jnp.zeros_like(w[:, :, 0])
        top = jnp.concatenate([w[:, :, 0], z], axis=-1)
        bot = jnp.concatenate([z, w[:, :, 1]], axis=-1)
        return jnp.concatenate([top, bot], axis=-2)
    return jnp.concatenate([pair(w_r), pair(w_i)], axis=-1).astype(bf16)


def _gelu_tanh(x):
    return 0.5 * x * (1.0 + jnp.tanh(float(np.sqrt(2.0 / np.pi)) * (x + 0.044715 * (x * x * x))))


def _lru_kernel(d, lat_blocks, *refs):
    if d == 0:
        (x_ref, xp_ref, xn_ref, cw_ref, cb_ref, wg_ref, br_ref, bi_ref, lam_ref, o_ref,
         xe, a_scr, u_scr, h_scr) = refs
    else:
        (x_ref, xp_ref, xn_ref, ag_ref, hf_ref, cw_ref, cb_ref, wg_ref, br_ref, bi_ref, lam_ref, o_ref,
         xe, a_scr, u_scr, h_scr, hs_scr) = refs
    s = pl.program_id(1)
    j = (s - 1) if d == 0 else (lat_blocks - s)
    has_prev = jnp.logical_and(s > 0, j > 0)
    has_next = jnp.logical_and(s > 0, j < lat_blocks - 1)

    @pl.when(s == 0)
    def _():
        h_scr[...] = jnp.zeros_like(h_scr)

    _fill_halo(xe, [(x_ref, xp_ref, xn_ref)], has_prev, has_next)
    sp = jax.nn.softplus(-lam_ref[...])
    for p in range(LRU_WIDTH // 128):
        sl = slice(128 * p, 128 * (p + 1))
        xc = _centred_conv(xe, cw_ref, cb_ref, sl)
        g = jnp.dot(xc.astype(bf16), wg_ref[p], preferred_element_type=f32)
        r = jax.nn.sigmoid(g[:, :128] + br_ref[:, sl])
        i = jax.nn.sigmoid(g[:, 128:] + bi_ref[:, sl])
        log_a = -LRU_C * r * sp[:, sl]
        a = jnp.exp(log_a)
        a_scr[:, sl] = a
        u_scr[:, sl] = jnp.sqrt(-jnp.tanh(log_a) * (a * a + 1.0)) * (i * xc)

    out = o_ref if d == 0 else hs_scr
    h = h_scr[0:1, :]
    for t in (range(TQ) if d == 0 else range(TQ - 1, -1, -1)):
        h = a_scr[t:t + 1, :] * h + u_scr[t:t + 1, :]
        out[t:t + 1, :] = h
    h_scr[0:1, :] = h
    if d == 1:
        y = (hf_ref[...] + hs_scr[...]) * _gelu_tanh(ag_ref[...])
        o_ref[...] = y.astype(o_ref.dtype)


def lru_pass(d, proj, hf, conv_w, conv_b, wg, b_r, b_i, lam, bsz, lat_blocks):
    t = proj.shape[0]
    _, blk = _seq_block(d, bsz, lat_blocks)
    row = lambda c: (lambda b, s: (blk(b, s), c))
    in_specs = _halo_specs(LRU_WIDTH, COL_AX, blk, t)
    args = [proj, proj, proj]
    if d == 1:
        in_specs += [pl.BlockSpec((TQ, LRU_WIDTH), row(COL_AG)), pl.BlockSpec((TQ, LRU_WIDTH), row(0))]
        args += [proj, hf]
    const = lambda shape: pl.BlockSpec(shape, lambda b, s: (0,) * len(shape))
    in_specs += [const((4, LRU_WIDTH)), const((1, LRU_WIDTH)), const((LRU_WIDTH // 128, 128, 256)),
                 const((1, LRU_WIDTH)), const((1, LRU_WIDTH)), const((1, LRU_WIDTH))]
    args += [conv_w, conv_b.reshape(1, -1), wg[d], b_r[d].reshape(1, -1), b_i[d].reshape(1, -1), lam[d].reshape(1, -1)]
    scratch = [pltpu.VMEM((TQ + 2 * HALO, LRU_WIDTH), f32), pltpu.VMEM((TQ, LRU_WIDTH), f32),
               pltpu.VMEM((TQ, LRU_WIDTH), f32), pltpu.VMEM((8, LRU_WIDTH), f32)]
    if d == 1:
        scratch.append(pltpu.VMEM((TQ, LRU_WIDTH), f32))
    return pl.pallas_call(
        functools.partial(_lru_kernel, d, lat_blocks),
        grid=(bsz, lat_blocks + 1),
        in_specs=in_specs,
        out_specs=pl.BlockSpec((TQ, LRU_WIDTH), row(0)),
        out_shape=jax.ShapeDtypeStruct((t, LRU_WIDTH), f32 if d == 0 else bf16),
        scratch_shapes=scratch,
        compiler_params=_params(),
        name=f"lru_pass{d}",
    )(*args)


def lru_branch(proj, conv_w, conv_b, w_r, b_r, w_i, b_i, lam, bsz, lat_blocks):
    wg = lru_gate_weights(w_r, w_i)
    hf = lru_pass(0, proj, None, conv_w, conv_b, wg, b_r, b_i, lam, bsz, lat_blocks)
    return lru_pass(1, proj, hf, conv_w, conv_b, wg, b_r, b_i, lam, bsz, lat_blocks)


def rope_tables(seq):
    pos = jnp.arange(seq)
    row = (pos // GRID_W).astype(f32)
    col = (pos % GRID_W).astype(f32)
    n_freq = SSD_STATE // 4
    freqs = ROPE_BASE ** (-jnp.arange(n_freq, dtype=f32) / n_freq)
    ang = jnp.concatenate([row[:, None] * freqs, col[:, None] * freqs], axis=-1)
    cos, sin = jnp.cos(ang), jnp.sin(ang)
    cosf = jnp.concatenate([cos, cos], axis=-1)
    sinf = jnp.concatenate([-sin, sin], axis=-1)
    cosf = jnp.concatenate([cosf, jnp.ones((TQ, SSD_STATE), f32)], axis=0)
    sinf = jnp.concatenate([sinf, jnp.zeros((TQ, SSD_STATE), f32)], axis=0)
    return cosf, sinf


def head_expand_matrix(d):
    e = np.zeros((128, SSD_INNER), np.float32)
    for h in range(SSD_HEADS):
        e[SSD_HEADS * d + h, h * SSD_HEADDIM:(h + 1) * SSD_HEADDIM] = 1.0
    return jnp.asarray(e, bf16)


def _split3(a):
    hi = a.astype(bf16)
    r = a - hi.astype(f32)
    mid = r.astype(bf16)
    return hi, mid, (r - mid.astype(f32)).astype(bf16)


def _dot_exact_rhs01(a, m01):
    return sum(jnp.dot(p, m01, preferred_element_type=f32) for p in _split3(a))


def _dot_exact_lhs01(m01, a):
    return sum(jnp.dot(m01, p, preferred_element_type=f32) for p in _split3(a))


def _ssd_kernel(d, lat_blocks, *refs):
    xs_refs, bc_refs, refs = refs[0:3], refs[3:6], refs[6:]
    if d == 0:
        (dt_ref, cos_ref, sin_ref, cw_ref, cb_ref, dtb_ref, alog_ref, e_ref,
         o_ref, xe, h_scr) = refs
    else:
        (dt_ref, cos_ref, sin_ref, z_ref, y0_ref, cw_ref, cb_ref, dtb_ref, alog_ref, e_ref,
         dsk_ref, ng_ref, o_ref, xe, h_scr, y_scr) = refs
    s = pl.program_id(1)
    j = (s - 1) if d == 0 else (lat_blocks - s)
    has_prev = jnp.logical_and(s > 0, j > 0)
    has_next = jnp.logical_and(s > 0, j < lat_blocks - 1)
    q = SSD_CHUNK

    @pl.when(s == 0)
    def _():
        h_scr[...] = jnp.zeros_like(h_scr)

    _fill_halo(xe, [xs_refs, bc_refs], has_prev, has_next)
    xbc = _centred_conv(xe, cw_ref, cb_ref, slice(None))
    xbc = xbc * jax.nn.sigmoid(xbc)
    xs = xbc[:, :SSD_INNER]
    cosf, sinf = cos_ref[...], sin_ref[...]

    def rope(g, off):
        v = xbc[:, off + g * SSD_STATE: off + (g + 1) * SSD_STATE]
        return (v * cosf + pltpu.roll(v, SSD_STATE // 2, 1) * sinf).astype(bf16)

    bm = [rope(g, SSD_INNER) for g in range(SSD_GROUPS)]
    cm = [rope(g, SSD_INNER + SSD_GROUPS * SSD_STATE) for g in range(SSD_GROUPS)]
    dt = jax.nn.softplus(dt_ref[...] + dtb_ref[...])
    delta = dt * (-jnp.exp(alog_ref[...]))
    ri = lax.broadcasted_iota(jnp.int32, (q, q), 0)
    ci = lax.broadcasted_iota(jnp.int32, (q, q), 1)
    keep = (ci <= ri) if d == 0 else (ci >= ri)
    tri = jnp.where(keep, 1.0, 0.0).astype(bf16)
    lane = lax.broadcasted_iota(jnp.int32, (1, 2 * SSD_HEADDIM), 1)
    halves = (lane < SSD_HEADDIM, lane >= SSD_HEADDIM)
    e = e_ref[...]
    last = q - 1 if d == 0 else 0
    out = o_ref if d == 0 else y_scr

    for c in (range(TQ // q) if d == 0 else range(TQ // q - 1, -1, -1)):
        rows = slice(c * q, (c + 1) * q)
        at = _dot_exact_lhs01(tri, delta[rows])
        at_exp = _dot_exact_rhs01(at, e)
        dt_exp = _dot_exact_rhs01(dt[rows], e)
        tot_exp = at_exp[last:last + 1, :]
        xdt = xs[rows] * dt_exp
        xd = (xdt * jnp.exp(tot_exp - at_exp)).astype(bf16)
        eat = jnp.exp(at_exp)
        cdec = jnp.exp(tot_exp)
        at_row = at.T
        ys = []
        for g in range(SSD_GROUPS):
            bg, cg = bm[g][rows], cm[g][rows]
            cb = _dot_nt(cg, bg)
            ht = h_scr[g]
            yoff = jnp.dot(cg, ht.astype(bf16), preferred_element_type=f32) * eat[:, g * GW:(g + 1) * GW]
            for pp in range(2):
                xpair = xdt[:, g * GW + pp * 128: g * GW + (pp + 1) * 128]
                acc = yoff[:, pp * 128:(pp + 1) * 128]
                for hh in range(2):
                    li = SSD_HEADS * d + 4 * g + 2 * pp + hh
                    seg = at[:, li:li + 1] - at_row[li:li + 1, :]
                    m = (cb * jnp.exp(jnp.where(keep, seg, NEG))).astype(bf16)
                    xm = jnp.where(halves[hh], xpair, 0.0).astype(bf16)
                    acc = acc + jnp.dot(m, xm, preferred_element_type=f32)
                ys.append(acc)
            upd = lax.dot_general(bg, xd[:, g * GW:(g + 1) * GW], (((0,), (0,)), ((), ())),
                                  preferred_element_type=f32)
            h_scr[g] = cdec[:, g * GW:(g + 1) * GW] * ht + upd
        out[rows, :] = jnp.concatenate(ys, axis=-1)

    if d == 1:
        y = y0_ref[...] + y_scr[...] + dsk_ref[...] * xs
        z = z_ref[...]
        y = y * (z * jax.nn.sigmoid(z))
        y = y * lax.rsqrt(jnp.mean(y * y, axis=-1, keepdims=True) + EPS)
        o_ref[...] = (y * ng_ref[...]).astype(o_ref.dtype)


def ssd_pass(d, proj, y0, cosf, sinf, conv_w, conv_b, a_log, dt_bias, d_skip, norm_g, bsz, lat_blocks):
    t = proj.shape[0]
    jj, blk = _seq_block(d, bsz, lat_blocks)
    row = lambda c: (lambda b, s: (blk(b, s), c))
    tbl = lambda b, s: (jnp.where(s == 0, lat_blocks, jj(s)), 0)
    const = lambda shape: pl.BlockSpec(shape, lambda b, s: (0,) * len(shape))
    pad128 = lambda v: jnp.pad(v.reshape(1, -1).astype(f32), ((0, 0), (0, 128 - v.size)))
    in_specs = _halo_specs(SSD_INNER, COL_XS, blk, t) + _halo_specs(SSD_CONV_DIM - SSD_INNER, COL_BC, blk, t) + [
        pl.BlockSpec((TQ, 128), row(COL_DT)),
        pl.BlockSpec((TQ, SSD_STATE), tbl),
        pl.BlockSpec((TQ, SSD_STATE), tbl),
    ]
    args = [proj] * 7 + [cosf, sinf]
    if d == 1:
        in_specs += [pl.BlockSpec((TQ, SSD_INNER), row(COL_Z)), pl.BlockSpec((TQ, SSD_INNER), row(0))]
        args += [proj, y0]
    in_specs += [const((4, SSD_CONV_DIM)), const((1, SSD_CONV_DIM)), const((1, 128)), const((1, 128)),
                 const((128, SSD_INNER))]
    args += [conv_w, conv_b.reshape(1, -1), pad128(dt_bias), pad128(a_log), head_expand_matrix(d)]
    scratch = [pltpu.VMEM((TQ + 2 * HALO, SSD_CONV_DIM), f32), pltpu.VMEM((SSD_GROUPS, SSD_STATE, GW), f32)]
    if d == 1:
        in_specs += [const((1, SSD_INNER)), const((1, SSD_INNER))]
        args += [jnp.repeat(d_skip, SSD_HEADDIM).reshape(1, -1), norm_g.reshape(1, -1)]
        scratch.append(pltpu.VMEM((TQ, SSD_INNER), f32))
    return pl.pallas_call(
        functools.partial(_ssd_kernel, d, lat_blocks),
        grid=(bsz, lat_blocks + 1),
        in_specs=in_specs,
        out_specs=pl.BlockSpec((TQ, SSD_INNER), row(0)),
        out_shape=jax.ShapeDtypeStruct((t, SSD_INNER), f32 if d == 0 else bf16),
        scratch_shapes=scratch,
        compiler_params=_params(),
        name=f"ssd_pass{d}",
    )(*args)


def ssd_branch(proj, cosf, sinf, conv_w, conv_b, a_log, dt_bias, d_skip, norm_g, bsz, lat_blocks):
    y0 = ssd_pass(0, proj, None, cosf, sinf, conv_w, conv_b, a_log, dt_bias, d_skip, norm_g, bsz, lat_blocks)
    return ssd_pass(1, proj, y0, cosf, sinf, conv_w, conv_b, a_log, dt_bias, d_skip, norm_g, bsz, lat_blocks)


def _tile_rows(mod_rows, bsz):
    idx = np.concatenate([np.repeat(np.arange(bsz), SEQ // TM), np.full(bsz * CTX_LEN // TM, bsz)])
    return mod_rows[idx][:, None, :]


def _expert_choice_moe(h, v, aff, w1, w3, w2, layer, g2_rows, bsz):
    t, d = v.shape
    aff = aff[:, :N_EXPERTS]
    nl = bsz * SEQ

    def choose(a, length):
        cap = CAPACITY_FACTOR * length // N_EXPERTS
        g, idx = lax.top_k(jnp.swapaxes(a.reshape(bsz, length, N_EXPERTS), 1, 2), cap)
        return jnp.swapaxes(g, 0, 1), jnp.swapaxes(idx, 0, 1), cap

    g_l, i_l, cap_l = choose(aff[:nl], SEQ)
    g_c, i_c, cap_c = choose(aff[nl:], CTX_LEN)
    boff = jnp.arange(bsz)[None, :, None]
    ctx_rows = (i_c + boff * CTX_LEN).reshape(N_EXPERTS, -1)
    rows = jnp.concatenate([(i_l + boff * SEQ).reshape(N_EXPERTS, -1), ctx_rows + nl], axis=1)
    gates = jnp.concatenate([g_l.reshape(N_EXPERTS, -1), g_c.reshape(N_EXPERTS, -1)], axis=1)[..., None]
    r = rows.shape[1]
    xg = v.at[rows.reshape(-1)].get(mode="promise_in_bounds").reshape(N_EXPERTS, r, d)
    hdn = expert_hidden(xg, w1, w3, layer, r // 4)
    ctx_slots, ctx_tile = bsz * cap_c, bsz * CTX_LEN
    assert (bsz * cap_l) % ctx_slots == 0 and nl % ctx_tile == 0
    h = moe_combine(i_l.reshape(-1), hdn, 0, cap_l, gates, w2, layer, h, g2_rows, 0, 0, SEQ, TN, bsz)
    return moe_combine(ctx_rows.reshape(-1), hdn, bsz * cap_l // ctx_slots, ctx_slots, gates, w2, layer, h, g2_rows,
                       bsz, nl // ctx_tile, ctx_tile, d, 1)


def kernel(x, c, ctx, c_ctx, w_ada, b_ada, norm_mix, norm_ffn, w_in, lru_conv_w, lru_conv_b, lru_w_r, lru_b_r,
           lru_w_i, lru_b_i, lru_lambda, ssd_conv_w, ssd_conv_b, ssd_a_log, ssd_dt_bias, ssd_d, ssd_norm, na_rpb,
           w_branch_lru, w_branch_ssd, w_branch_na, w_out, w_router, w1, w3, w2, norm_final):
    bsz = x.shape[0]
    assert x.shape[1:] == (SEQ, D_MODEL) and ctx.shape[1:] == (CTX_LEN, D_MODEL) and bsz * CTX_LEN == TM
    d = D_MODEL
    h = (x.reshape(bsz * SEQ, d), ctx.reshape(bsz * CTX_LEN, d))
    t = bsz * (SEQ + CTX_LEN)

    cond = jnp.concatenate([c, c_ctx[None, :], jnp.zeros((8 - bsz - 1, d), f32)], axis=0)
    mod = ada_modulation(cond, w_ada, b_ada)
    cosf, sinf = rope_tables(SEQ)

    assert DT_END + DT_PAD == W_QKV0 and w_in.shape[-1] == DT_END + 3 * NA_WIDTH + 3 * d
    w_proj = relayout_proj_weights(jnp.swapaxes(w_in, 1, 2))
    qkv_blocks = 3 * NA_WIDTH // TN
    pa, pb, pc, wo = (w.astype(bf16) for w in (w_branch_lru, w_branch_ssd, w_branch_na, w_out))
    na_tbl, na_pair_of = na_bias_blocks(na_rpb)

    for l in range(DEPTH):
        mods = jnp.split(mod[l], 6, axis=-1)
        sh1, sc1, g1, sh2, sc2 = (_tile_rows(m, bsz) for m in mods[:5])
        u = norm_modulate(h, norm_mix[l], sc1, sh1, bf16)
        proj = matmul_bf16(u, w_proj, l, N_F32, lambda j: jnp.where(j < W_QKV0 // TN, j, j + qkv_blocks), f32)
        qkv = matmul_bf16(u, w_proj, l, 3 * NA_WIDTH, lambda j: j + W_QKV0 // TN, bf16)
        ya = lru_branch(proj, lru_conv_w[l], lru_conv_b[l], lru_w_r[l], lru_b_r[l], lru_w_i[l], lru_b_i[l],
                        lru_lambda[l], bsz, LAT_BLOCKS)
        yb = ssd_branch(proj, cosf, sinf, ssd_conv_w[l], ssd_conv_b[l], ssd_a_log[l], ssd_dt_bias[l], ssd_d[l],
                        ssd_norm[l], bsz, LAT_BLOCKS)
        yc = na_attention(qkv, na_tbl, l, na_pair_of, bsz)
        y = branch_merge(ya, yb, yc, proj, pa, pb, pc, l, COL_G)
        h, v, aff = residual_norm_router(y, wo, l, h, g1, norm_ffn[l], sc2, sh2, w_router[l])
        h = _expert_choice_moe(h, v, aff, w1, w3, w2, l, mods[5][:, None, :], bsz)

    zeros = jnp.zeros((t // TM, 1, d), f32)
    out = norm_modulate(h, norm_final, zeros, zeros, f32, rows=bsz * SEQ)
    return out.reshape(bsz, SEQ, d)
```
